```python
import math
import jax
import jax.numpy as jnp
from jax import lax
import numpy as np

D_MODEL = 1024
BATCH = 16
SEQ = 2048
DEPTH = 4

GRID_W = 64
CTX_LEN = 256
N_MOD = 6

HEAD_DIM = 64
BRANCH_WIDTH = 512
N_BRANCH = 4

WA_HEADS = 8
WA_KV_HEADS = 2
WINDOW = 128
GA_HEADS = 8
GA_KV_HEADS = 2
Q_BLOCK = 128
ROPE_THETA = 10000.0

SSD_HEADS = 8
SSD_HEAD_DIM = 64
SSD_GROUPS = 2
SSD_STATE = 128
SSD_CONV = 3
SSD_CHUNK = 128
SSD_WIDTH = SSD_HEADS * SSD_HEAD_DIM
SSD_CONV_CH = SSD_WIDTH + 2 * SSD_GROUPS * SSD_STATE

S5_GROUP = 16
S5_WIDTH = 512
S5_GROUPS = S5_WIDTH // S5_GROUP
S5_STATE = 64

N_EXPERTS = 32
N_EXPERT_GROUPS = 8
TOP_K = 2
D_EXPERT = 512
MOE_BLOCK = 128

ALPHA = (2 * DEPTH) ** 0.25
BETA = (8 * DEPTH) ** -0.25
NORM_EPS = 1e-6
F32 = jnp.float32

IN_SIZES = (
    WA_HEADS * HEAD_DIM, WA_KV_HEADS * HEAD_DIM, WA_KV_HEADS * HEAD_DIM,
    GA_HEADS * HEAD_DIM, GA_KV_HEADS * HEAD_DIM, GA_KV_HEADS * HEAD_DIM,
    SSD_WIDTH, SSD_WIDTH, SSD_GROUPS * SSD_STATE, SSD_GROUPS * SSD_STATE,
    2 * SSD_HEADS,
    S5_WIDTH,
    N_BRANCH * D_MODEL,
)
D_IN = sum(IN_SIZES)

kernel_name = 'hybrid_prefix_dit_trunk'


def layer_norm(x):
    xf = x.astype(F32)
    mu = jnp.mean(xf, axis=-1, keepdims=True)
    var = jnp.mean(jnp.square(xf - mu), axis=-1, keepdims=True)
    return ((xf - mu) * lax.rsqrt(var + NORM_EPS)).astype(x.dtype)


def post_norm(v, g, b):
    return layer_norm(v) * g + b


def rms_norm(x, w):
    xf = x.astype(F32)
    y = xf * lax.rsqrt(jnp.mean(jnp.square(xf), axis=-1, keepdims=True) + NORM_EPS)
    return y.astype(x.dtype) * w


def split_heads(t, n_heads):
    return t.reshape(t.shape[:2] + (n_heads, HEAD_DIM))


def axial_rope_tables(n_tokens):
    rows = n_tokens // GRID_W
    row = jnp.repeat(jnp.arange(rows, dtype=F32), GRID_W)
    col = jnp.tile(jnp.arange(GRID_W, dtype=F32), rows)
    axis_dim = HEAD_DIM // 2
    inv_freq = ROPE_THETA ** (-jnp.arange(0, axis_dim, 2, dtype=F32) / axis_dim)
    ang_r = row[:, None] * inv_freq
    ang_c = col[:, None] * inv_freq
    ang = jnp.concatenate([ang_r, ang_r, ang_c, ang_c], axis=-1)
    return jnp.cos(ang), jnp.sin(ang)


def apply_axial_rope(t, cos, sin):
    axis_dim = HEAD_DIM // 2
    pair = axis_dim // 2
    tf = t.astype(F32)
    parts = []
    for a in range(2):
        seg = tf[..., a * axis_dim:(a + 1) * axis_dim]
        parts += [-seg[..., pair:], seg[..., :pair]]
    rotated = jnp.concatenate(parts, axis=-1)
    return (tf * cos[None, :, None, :] + rotated * sin[None, :, None, :]).astype(t.dtype)


def window_attention(q, k, v, k_ctx, v_ctx, sink):
    bsz, seq, n_heads, dh = q.shape
    n_kv = k.shape[2]
    grp = n_heads // n_kv
    span = Q_BLOCK + 2 * WINDOW
    pad = ((0, 0), (WINDOW, WINDOW), (0, 0), (0, 0))
    k_pad = jnp.pad(k, pad)
    v_pad = jnp.pad(v, pad)
    qg = q.reshape(bsz, seq, n_kv, grp, dh)
    sink_logit = jnp.broadcast_to(sink.astype(F32).reshape(1, n_kv, grp, 1, 1), (bsz, n_kv, grp, Q_BLOCK, 1))
    scale = dh ** -0.5
    offs_q = jnp.arange(Q_BLOCK)
    offs_k = jnp.arange(span) - WINDOW

    def block(i):
        start = i * Q_BLOCK
        qb = lax.dynamic_slice_in_dim(qg, start, Q_BLOCK, axis=1)
        kb = lax.dynamic_slice_in_dim(k_pad, start, span, axis=1)
        vb = lax.dynamic_slice_in_dim(v_pad, start, span, axis=1)
        q_pos = start + offs_q
        k_pos = start + offs_k
        valid = ((jnp.abs(q_pos[:, None] - k_pos[None, :]) <= WINDOW)
                 & (k_pos >= 0)[None, :] & (k_pos < seq)[None, :])
        s_loc = jnp.einsum('bqkgd,bskd->bkgqs', qb, kb).astype(F32) * scale
        s_loc = jnp.where(valid, s_loc, -jnp.inf)
        s_ctx = jnp.einsum('bqkgd,bckd->bkgqc', qb, k_ctx).astype(F32) * scale
        p = jax.nn.softmax(jnp.concatenate([s_loc, s_ctx, sink_logit], axis=-1), axis=-1).astype(v.dtype)
        return (jnp.einsum('bkgqs,bskd->bqkgd', p[..., :span], vb)
                + jnp.einsum('bkgqc,bckd->bqkgd', p[..., span:-1], v_ctx))

    out = lax.map(block, jnp.arange(seq // Q_BLOCK))
    return jnp.moveaxis(out, 0, 1).reshape(bsz, seq, n_heads * dh)


def dense_attention(q, k, v, k_ctx, v_ctx):
    bsz, seq, n_heads, dh = q.shape
    n_kv = k.shape[2]
    grp = n_heads // n_kv
    keys = jnp.concatenate([k, k_ctx], axis=1)
    vals = jnp.concatenate([v, v_ctx], axis=1)
    qg = q.reshape(bsz, seq, n_kv, grp, dh)
    scale = dh ** -0.5

    def block(i):
        qb = lax.dynamic_slice_in_dim(qg, i * Q_BLOCK, Q_BLOCK, axis=1)
        s = jnp.einsum('bqkgd,bskd->bkgqs', qb, keys).astype(F32) * scale
        p = jax.nn.softmax(s, axis=-1).astype(vals.dtype)
        return jnp.einsum('bkgqs,bskd->bqkgd', p, vals)

    out = lax.map(block, jnp.arange(seq // Q_BLOCK))
    return jnp.moveaxis(out, 0, 1).reshape(bsz, seq, n_heads * dh)


def context_attention(q, k, v, sink=None):
    bsz, n, n_heads, dh = q.shape
    n_kv = k.shape[2]
    grp = n_heads // n_kv
    qg = q.reshape(bsz, n, n_kv, grp, dh)
    s = jnp.einsum('bqkgd,bskd->bkgqs', qg, k).astype(F32) * dh ** -0.5
    if sink is not None:
        sink_logit = jnp.broadcast_to(sink.astype(F32).reshape(1, n_kv, grp, 1, 1), (bsz, n_kv, grp, n, 1))
        s = jnp.concatenate([s, sink_logit], axis=-1)
    p = jax.nn.softmax(s, axis=-1).astype(v.dtype)
    if sink is not None:
        p = p[..., :-1]
    return jnp.einsum('bkgqs,bskd->bqkgd', p, v).reshape(bsz, n, n_heads * dh)


def depthwise_conv(x, w, b):
    k = w.shape[0]
    y = lax.conv_general_dilated(x, w[:, None, :], window_strides=(1,), padding=[(k // 2, k // 2)],
                                 dimension_numbers=('NWC', 'WIO', 'NWC'), feature_group_count=x.shape[-1])
    return y + b


def segsum_exp(cs):
    t = cs.shape[-1]
    diff = cs[..., :, None] - cs[..., None, :]
    return jnp.exp(jnp.where(jnp.tril(jnp.ones((t, t), dtype=bool)), diff, -jnp.inf))


def ssd_scan(x, dt, a, bm, cm, init_state):
    bsz, seq, n_heads, p = x.shape
    g, n = bm.shape[2], bm.shape[3]
    j = n_heads // g
    nc = seq // SSD_CHUNK
    t = SSD_CHUNK
    xd = (x.astype(F32) * dt[..., None]).reshape(bsz, nc, t, g, j, p)
    da = (dt * a).reshape(bsz, nc, t, g, j).transpose(0, 3, 4, 1, 2)
    bc = bm.astype(F32).reshape(bsz, nc, t, g, n)
    cc = cm.astype(F32).reshape(bsz, nc, t, g, n)
    a_cs = jnp.cumsum(da, axis=-1)
    cb = jnp.einsum('bclgn,bcsgn->bgcls', cc, bc)
    y_diag = jnp.einsum('bgjcls,bcsgjp->bclgjp', cb[:, :, None] * segsum_exp(a_cs), xd)
    decay_to_end = jnp.exp(a_cs[..., -1:] - a_cs)
    states = jnp.einsum('bcsgn,bgjcs,bcsgjp->bcgjpn', bc, decay_to_end, xd)
    states = jnp.concatenate([init_state.astype(F32).reshape(bsz, 1, g, j, p, n), states], axis=1)
    chunk_cs = jnp.cumsum(jnp.pad(a_cs[..., -1], ((0, 0), (0, 0), (0, 0), (1, 0))), axis=-1)
    states = jnp.einsum('bgjzc,bcgjpn->bzgjpn', segsum_exp(chunk_cs), states)
    y_off = jnp.einsum('bclgn,bcgjpn,bgjcl->bclgjp', cc, states[:, :-1], jnp.exp(a_cs))
    y = (y_diag + y_off).reshape(bsz, seq, n_heads, p)
    return y, states[:, -1].reshape(bsz, n_heads, p, n)


def maybe_flip(t, rev):
    return jnp.flip(t, axis=1) if rev else t


def ssd_prepare(x, bm, cm, dt, conv_w, conv_b, dt_bias):
    xbc = jax.nn.silu(depthwise_conv(jnp.concatenate([x, bm, cm], axis=-1), conv_w, conv_b))
    x, bm, cm = jnp.split(xbc, [SSD_WIDTH, SSD_WIDTH + SSD_GROUPS * SSD_STATE], axis=-1)
    lead = x.shape[:2]
    return (x.reshape(lead + (SSD_HEADS, SSD_HEAD_DIM)),
            bm.reshape(lead + (SSD_GROUPS, SSD_STATE)),
            cm.reshape(lead + (SSD_GROUPS, SSD_STATE)),
            jax.nn.softplus(dt.astype(F32) + dt_bias.astype(F32)))


def ssd_gate_out(y, z, norm_w):
    g = y.reshape(z.shape[:2] + (SSD_WIDTH,)) * jax.nn.silu(z.astype(F32))
    return rms_norm(g, norm_w).astype(z.dtype)


def ssd_mixer(p_c, p_l, conv_w, conv_b, dt_bias, a_log, d_skip, norm_w, need_ctx):
    xc, bc, cc, dtc = ssd_prepare(p_c[0], p_c[2], p_c[3], p_c[4], conv_w, conv_b, dt_bias)
    xl, bl, cl, dtl = ssd_prepare(p_l[0], p_l[2], p_l[3], p_l[4], conv_w, conv_b, dt_bias)
    a = -jnp.exp(a_log.astype(F32))
    d = d_skip.astype(F32)[:, None]
    y_l = d * xl.astype(F32)
    y_c = d * xc.astype(F32) if need_ctx else None
    init = jnp.zeros((xc.shape[0], SSD_HEADS, SSD_HEAD_DIM, SSD_STATE), F32)
    for direction in range(2):
        rev = direction == 1
        hs = slice(direction * SSD_HEADS, (direction + 1) * SSD_HEADS)
        yc_d, s_ctx = ssd_scan(maybe_flip(xc, rev), maybe_flip(dtc[..., hs], rev), a[direction],
                               maybe_flip(bc, rev), maybe_flip(cc, rev), init)
        yl_d, _ = ssd_scan(maybe_flip(xl, rev), maybe_flip(dtl[..., hs], rev), a[direction],
                           maybe_flip(bl, rev), maybe_flip(cl, rev), s_ctx)
        y_l = y_l + maybe_flip(yl_d, rev)
        if need_ctx:
            y_c = y_c + maybe_flip(yc_d, rev)
    out_l = ssd_gate_out(y_l, p_l[1], norm_w)
    out_c = ssd_gate_out(y_c, p_c[1], norm_w) if need_ctx else None
    return out_c, out_l


def linear_recurrence_combine(left, right):
    a_l, b_l = left
    a_r, b_r = right
    return a_r * a_l, a_r * b_l + b_r


def s5_discretise(a_re, a_im, log_dt, b_cplx):
    a = lax.complex(jnp.minimum(a_re.astype(F32), -1e-4), a_im.astype(F32))
    dt = jnp.exp(log_dt.astype(F32))[:, None]
    a_bar = jnp.exp(a * dt)
    b_bar = ((a_bar - 1.0) / a)[..., None] * b_cplx
    return a_bar, b_bar


def s5_scan(bu, a_bar, init, reverse):
    n = bu.shape[1]
    a = jnp.broadcast_to(a_bar, (1, n) + a_bar.shape)
    a_cum, s = lax.associative_scan(linear_recurrence_combine, (a, bu), reverse=reverse, axis=1)
    if init is not None:
        s = s + a_cum * init[:, None]
    return s, (s[:, 0] if reverse else s[:, -1])


def s5_glu(y, w_glu, dtype):
    y = jax.nn.gelu(y.reshape(y.shape[:2] + (S5_WIDTH,))).astype(dtype)
    a, b = jnp.split(y @ w_glu, 2, axis=-1)
    return a * jax.nn.sigmoid(b)


def s5_mixer(u_c, u_l, a_re, a_im, log_dt, b_re, b_im, c_re, c_im, d_skip, w_glu, need_ctx):
    b_cplx = lax.complex(b_re.astype(F32), b_im.astype(F32))
    c_cplx = lax.complex(c_re.astype(F32), c_im.astype(F32))
    d_grp = d_skip.astype(F32).reshape(S5_GROUPS, S5_GROUP)
    uc = u_c.astype(F32).reshape(u_c.shape[:2] + (S5_GROUPS, S5_GROUP))
    ul = u_l.astype(F32).reshape(u_l.shape[:2] + (S5_GROUPS, S5_GROUP))
    y_l = d_grp * ul
    y_c = d_grp * uc if need_ctx else None
    for direction in range(2):
        rev = direction == 1
        a_bar, b_bar = s5_discretise(a_re[direction], a_im[direction], log_dt[direction], b_cplx)
        s_c, fin_c = s5_scan(jnp.einsum('blgs,gns->blgn', uc.astype(jnp.complex64), b_bar), a_bar, None, rev)
        s_l, _ = s5_scan(jnp.einsum('blgs,gns->blgn', ul.astype(jnp.complex64), b_bar), a_bar, fin_c, rev)
        y_l = y_l + jnp.real(jnp.einsum('blgn,gsn->blgs', s_l, c_cplx))
        if need_ctx:
            y_c = y_c + jnp.real(jnp.einsum('blgn,gsn->blgs', s_c, c_cplx))
    out_l = s5_glu(y_l, w_glu, u_l.dtype)
    out_c = s5_glu(y_c, w_glu, u_c.dtype) if need_ctx else None
    return out_c, out_l


def merge(branches, gate_logits, w_branch, w_out):
    ys = jnp.stack(branches, axis=2)
    g = jax.nn.sigmoid(gate_logits.reshape(gate_logits.shape[:2] + (N_BRANCH, D_MODEL)))
    m = jnp.einsum('blnw,nwd->blnd', ys, w_branch)
    return jnp.sum(g * m, axis=2) @ w_out


def token_mixer(h_c, h_l, cos, sin, w_in, wa_sink, ga_q_norm, ga_k_norm,
                ssd_conv_w, ssd_conv_b, ssd_dt_bias, ssd_a_log, ssd_d, ssd_norm_w,
                s5_a_re, s5_a_im, s5_log_dt, s5_b_re, s5_b_im, s5_c_re, s5_c_im, s5_d, s5_w_glu,
                w_branch, w_out, need_ctx):
    cuts = [int(v) for v in np.cumsum(IN_SIZES)[:-1]]
    (aq_c, ak_c, av_c, gq_c, gk_c, gv_c, sx_c, sz_c, sb_c, sc_c, sdt_c, su_c, gate_c) = jnp.split(h_c @ w_in, cuts, axis=-1)
    (aq_l, ak_l, av_l, gq_l, gk_l, gv_l, sx_l, sz_l, sb_l, sc_l, sdt_l, su_l, gate_l) = jnp.split(h_l @ w_in, cuts, axis=-1)

    ka_c = split_heads(ak_c, WA_KV_HEADS)
    va_c = split_heads(av_c, WA_KV_HEADS)
    ya_l = window_attention(apply_axial_rope(split_heads(aq_l, WA_HEADS), cos, sin),
                            apply_axial_rope(split_heads(ak_l, WA_KV_HEADS), cos, sin),
                            split_heads(av_l, WA_KV_HEADS), ka_c, va_c, wa_sink)
    kg_c = rms_norm(split_heads(gk_c, GA_KV_HEADS), ga_k_norm)
    vg_c = split_heads(gv_c, GA_KV_HEADS)
    yg_l = dense_attention(apply_axial_rope(rms_norm(split_heads(gq_l, GA_HEADS), ga_q_norm), cos, sin),
                           apply_axial_rope(rms_norm(split_heads(gk_l, GA_KV_HEADS), ga_k_norm), cos, sin),
                           split_heads(gv_l, GA_KV_HEADS), kg_c, vg_c)
    ys_c, ys_l = ssd_mixer((sx_c, sz_c, sb_c, sc_c, sdt_c), (sx_l, sz_l, sb_l, sc_l, sdt_l),
                           ssd_conv_w, ssd_conv_b, ssd_dt_bias, ssd_a_log, ssd_d, ssd_norm_w, need_ctx)
    y5_c, y5_l = s5_mixer(su_c, su_l, s5_a_re, s5_a_im, s5_log_dt, s5_b_re, s5_b_im,
                          s5_c_re, s5_c_im, s5_d, s5_w_glu, need_ctx)
    out_l = merge((ya_l, ys_l, yg_l, y5_l), gate_l, w_branch, w_out)
    if not need_ctx:
        return None, out_l
    ya_c = context_attention(split_heads(aq_c, WA_HEADS), ka_c, va_c, wa_sink)
    yg_c = context_attention(rms_norm(split_heads(gq_c, GA_HEADS), ga_q_norm), kg_c, vg_c)
    out_c = merge((ya_c, ys_c, yg_c, y5_c), gate_c, w_branch, w_out)
    return out_c, out_l


def expert_dispatch(h, expert_idx, gate_w, w_gate, w_up, w_down):
    n_tok, d = h.shape
    n_assign = n_tok * TOP_K
    n_blocks = -(-(n_assign + N_EXPERTS * (MOE_BLOCK - 1)) // MOE_BLOCK)
    cap = n_blocks * MOE_BLOCK
    flat_e = expert_idx.reshape(-1)
    flat_t = jnp.repeat(jnp.arange(n_tok, dtype=jnp.int32), TOP_K)
    flat_w = gate_w.reshape(-1)
    order = jnp.argsort(flat_e)
    e_s, t_s, w_s = flat_e[order], flat_t[order], flat_w[order]
    counts = jnp.bincount(flat_e, length=N_EXPERTS)
    padded = (counts + MOE_BLOCK - 1) // MOE_BLOCK * MOE_BLOCK
    first = jnp.cumsum(counts) - counts
    pad_end = jnp.cumsum(padded)
    dest = (pad_end - padded)[e_s] + jnp.arange(n_assign) - first[e_s]
    slot_tok = jnp.zeros((cap,), jnp.int32).at[dest].set(t_s)
    slot_w = jnp.zeros((cap,), h.dtype).at[dest].set(w_s.astype(h.dtype))
    block_expert = jnp.minimum(jnp.searchsorted(pad_end, jnp.arange(n_blocks) * MOE_BLOCK, side='right'),
                               N_EXPERTS - 1)
    x_blocks = h[slot_tok].reshape(n_blocks, MOE_BLOCK, d)

    def expert_block(args):
        xb, e = args
        return (jax.nn.silu(xb @ w_gate[e]) * (xb @ w_up[e])) @ w_down[e]

    y = lax.map(expert_block, (x_blocks, block_expert)).reshape(cap, d)
    return jnp.zeros_like(h).at[slot_tok].add(y * slot_w[:, None])


def moe(h, router_w, router_bias, w_gate, w_up, w_down):
    n_tok = h.shape[0]
    per_group = N_EXPERTS // N_EXPERT_GROUPS
    scores = jax.nn.sigmoid((h @ router_w).astype(F32))
    sel = (scores + router_bias.astype(F32)).reshape(n_tok, N_EXPERT_GROUPS, per_group)
    group_score = jnp.sum(lax.top_k(sel, 2)[0], axis=-1)
    grp = jnp.argmax(group_score, axis=-1).astype(jnp.int32)
    in_grp = jnp.take_along_axis(sel, grp[:, None, None], axis=1)[:, 0]
    local = lax.top_k(in_grp, TOP_K)[1]
    expert_idx = grp[:, None] * per_group + local
    w = jnp.take_along_axis(scores, expert_idx, axis=1)
    w = w / jnp.sum(w, axis=-1, keepdims=True)
    return expert_dispatch(h, expert_idx, w, w_gate, w_up, w_down)


def setup_inputs(seed: int = 0) -> dict:
    key = jax.random.key(seed)
    keys = jax.random.split(key, 40)
    counter = iter(range(40))

    def nrm(shape, scale):
        return jax.random.normal(keys[next(counter)], shape, F32) * scale

    def unif(shape, lo, hi):
        return jax.random.uniform(keys[next(counter)], shape, F32, lo, hi)

    L, D = DEPTH, D_MODEL
    inputs = {}
    inputs['x'] = nrm((BATCH, SEQ, D), 1.0)
    inputs['c'] = nrm((BATCH, D), 1.0)
    inputs['ctx'] = nrm((BATCH, CTX_LEN, D), 1.0)
    inputs['c_ctx'] = nrm((D,), 1.0)
    inputs['mod_w'] = nrm((L, D, N_MOD * D), 0.5 * D ** -0.5)
    inputs['mod_b'] = nrm((L, N_MOD * D), 0.02)
    inputs['w_in'] = nrm((L, D, D_IN), D ** -0.5)
    inputs['wa_sink'] = nrm((L, WA_HEADS), 0.5)
    inputs['ga_q_norm'] = 1.0 + nrm((L, HEAD_DIM), 0.02)
    inputs['ga_k_norm'] = 1.0 + nrm((L, HEAD_DIM), 0.02)
    inputs['ssd_conv_w'] = nrm((L, SSD_CONV, SSD_CONV_CH), SSD_CONV ** -0.5)
    inputs['ssd_conv_b'] = nrm((L, SSD_CONV_CH), 0.02)
    dt0 = jnp.exp(unif((L, 2 * SSD_HEADS), math.log(1e-3), math.log(1e-1)))
    inputs['ssd_dt_bias'] = dt0 + jnp.log(-jnp.expm1(-dt0))
    inputs['ssd_a_log'] = jnp.log(unif((L, 2, SSD_HEADS), 1.0, 16.0))
    inputs['ssd_d'] = 1.0 + nrm((L, SSD_HEADS), 0.1)
    inputs['ssd_norm_w'] = 1.0 + nrm((L, SSD_WIDTH), 0.02)
    inputs['s5_a_re'] = -0.5 + nrm((L, 2, S5_GROUPS, S5_STATE), 0.01)
    inputs['s5_a_im'] = jnp.pi * jnp.arange(S5_STATE, dtype=F32) + nrm((L, 2, S5_GROUPS, S5_STATE), 0.01)
    inputs['s5_log_dt'] = unif((L, 2, S5_GROUPS), math.log(1e-3), math.log(1e-1))
    inputs['s5_b_re'] = nrm((L, S5_GROUPS, S5_STATE, S5_GROUP), (2 * S5_GROUP) ** -0.5)
    inputs['s5_b_im'] = nrm((L, S5_GROUPS, S5_STATE, S5_GROUP), (2 * S5_GROUP) ** -0.5)
    inputs['s5_c_re'] = nrm((L, S5_GROUPS, S5_GROUP, S5_STATE), S5_STATE ** -0.5)
    inputs['s5_c_im'] = nrm((L, S5_GROUPS, S5_GROUP, S5_STATE), S5_STATE ** -0.5)
    inputs['s5_d'] = nrm((L, S5_WIDTH), 1.0)
    inputs['s5_w_glu'] = nrm((L, S5_WIDTH, 2 * S5_WIDTH), S5_WIDTH ** -0.5)
    inputs['w_branch'] = nrm((L, N_BRANCH, BRANCH_WIDTH, D), BETA * BRANCH_WIDTH ** -0.5)
    inputs['w_out'] = nrm((L, D, D), BETA * D ** -0.5)
    inputs['ln1_g'] = 1.0 + nrm((L, D), 0.02)
    inputs['ln1_b'] = nrm((L, D), 0.02)
    inputs['ln2_g'] = 1.0 + nrm((L, D), 0.02)
    inputs['ln2_b'] = nrm((L, D), 0.02)
    inputs['router_w'] = nrm((D, N_EXPERTS), D ** -0.5)
    inputs['router_bias'] = nrm((N_EXPERTS,), 0.01)
    inputs['moe_w_gate'] = nrm((L, N_EXPERTS, D, D_EXPERT), D ** -0.5)
    inputs['moe_w_up'] = nrm((L, N_EXPERTS, D, D_EXPERT), D ** -0.5)
    inputs['moe_w_down'] = nrm((L, N_EXPERTS, D_EXPERT, D), BETA * D_EXPERT ** -0.5)
    return inputs


def reference(x, c, ctx, c_ctx, mod_w, mod_b, w_in, wa_sink, ga_q_norm, ga_k_norm,
              ssd_conv_w, ssd_conv_b, ssd_dt_bias, ssd_a_log, ssd_d, ssd_norm_w,
              s5_a_re, s5_a_im, s5_log_dt, s5_b_re, s5_b_im, s5_c_re, s5_c_im, s5_d, s5_w_glu,
              w_branch, w_out, ln1_g, ln1_b, ln2_g, ln2_b, router_w, router_bias,
              moe_w_gate, moe_w_up, moe_w_down):
    bsz, seq, d = x.shape
    n_ctx = ctx.shape[1]
    cos, sin = axial_rope_tables(seq)
    cond_l = jax.nn.silu(c)
    cond_c = jax.nn.silu(c_ctx)
    xc = ctx
    for layer in range(DEPTH):
        need_ctx = layer < DEPTH - 1
        shift1, scale1, gate1, shift2, scale2, gate2 = jnp.split(
            (cond_l @ mod_w[layer] + mod_b[layer])[:, None, :], N_MOD, axis=-1)
        cshift1, cscale1, cgate1, cshift2, cscale2, cgate2 = jnp.split(
            cond_c @ mod_w[layer] + mod_b[layer], N_MOD, axis=-1)

        h_l = layer_norm(x) * (1 + scale1) + shift1
        h_c = layer_norm(xc) * (1 + cscale1) + cshift1
        mix_c, mix_l = token_mixer(h_c, h_l, cos, sin, w_in[layer], wa_sink[layer], ga_q_norm[layer], ga_k_norm[layer],
                                   ssd_conv_w[layer], ssd_conv_b[layer], ssd_dt_bias[layer], ssd_a_log[layer],
                                   ssd_d[layer], ssd_norm_w[layer],
                                   s5_a_re[layer], s5_a_im[layer], s5_log_dt[layer], s5_b_re[layer], s5_b_im[layer],
                                   s5_c_re[layer], s5_c_im[layer], s5_d[layer], s5_w_glu[layer],
                                   w_branch[layer], w_out[layer], need_ctx)
        x = post_norm(ALPHA * x + gate1 * mix_l, ln1_g[layer], ln1_b[layer])

        h_l = (layer_norm(x) * (1 + scale2) + shift2).reshape(bsz * seq, d)
        if need_ctx:
            xc = post_norm(ALPHA * xc + cgate1 * mix_c, ln1_g[layer], ln1_b[layer])
            h_c = (layer_norm(xc) * (1 + cscale2) + cshift2).reshape(bsz * n_ctx, d)
            ffn = moe(jnp.concatenate([h_c, h_l], axis=0), router_w, router_bias,
                      moe_w_gate[layer], moe_w_up[layer], moe_w_down[layer])
            xc = post_norm(ALPHA * xc + cgate2 * ffn[:bsz * n_ctx].reshape(bsz, n_ctx, d), ln2_g[layer], ln2_b[layer])
            ffn_l = ffn[bsz * n_ctx:]
        else:
            ffn_l = moe(h_l, router_w, router_bias, moe_w_gate[layer], moe_w_up[layer], moe_w_down[layer])
        x = post_norm(ALPHA * x + gate2 * ffn_l.reshape(bsz, seq, d), ln2_g[layer], ln2_b[layer])
    return x
```

```python
import functools
import math

import jax
import jax.numpy as jnp
import numpy as np
from jax import lax
from jax.experimental import pallas as pl
from jax.experimental.pallas import tpu as pltpu

F32 = jnp.float32
BF16 = jnp.bfloat16

D_MODEL = 1024
DEPTH = 4
GRID_W = 64
CTX_LEN = 256
SEQ = 2048
TOK = CTX_LEN + SEQ
N_MOD = 6
HEAD_DIM = 64
N_BRANCH = 4
N_HEADS = 8
N_KV = 2
GRP = N_HEADS // N_KV
WINDOW = 128
QB = 128
ROPE_THETA = 10000.0

SSD_HEADS = 8
SSD_P = 64
SSD_GROUPS = 2
SSD_N = 128
SSD_T = 128
SSD_WIDTH = SSD_HEADS * SSD_P
SSD_CONV_CH = SSD_WIDTH + 2 * SSD_GROUPS * SSD_N
N_CHUNK = TOK // SSD_T

S5_GROUP = 16
S5_WIDTH = 512
S5_GROUPS = S5_WIDTH // S5_GROUP
S5_STATE = 64
S5_T = 16
S5_ROWW = S5_T * S5_GROUP
S5_NCH = TOK // S5_T
S5_CTX_CH = CTX_LEN // S5_T

N_EXPERTS = 32
N_EXPERT_GROUPS = 8
PER_GROUP = N_EXPERTS // N_EXPERT_GROUPS
TOP_K = 2
D_EXPERT = 512
MOE_ROWS = 256

ALPHA = (2 * DEPTH) ** 0.25
NORM_EPS = 1e-6

TM = 256
VMEM_LIMIT = 56 * 1024 * 1024

_IN_SIZES = (512, 128, 128, 512, 128, 128, 512, 512, 256, 256, 16, 512, 4096)
_IN_OFF = np.concatenate([[0], np.cumsum(_IN_SIZES)]).astype(int)
(_AQ, _AK, _AV, _GQ, _GK, _GV, _SX, _SZ, _SB, _SC, _SDT, _SU, _GATE) = range(13)

ATTN_W = 1536
CONV_W = 1024
DT_W = 256
PROJ_W = ATTN_W + CONV_W + 512 + 512 + DT_W


def _params(sem=None):
    return pltpu.CompilerParams(dimension_semantics=sem, vmem_limit_bytes=VMEM_LIMIT)


def _ln(x):
    mu = jnp.mean(x, axis=-1, keepdims=True)
    xc = x - mu
    var = jnp.mean(xc * xc, axis=-1, keepdims=True)
    return xc * lax.rsqrt(var + NORM_EPS)


def _sigmoid(x):
    return 1.0 / (1.0 + jnp.exp(-x))


def _silu(x):
    return x * _sigmoid(x)


def _bdot(a, b):
    return jnp.dot(a.astype(BF16), b.astype(BF16), preferred_element_type=F32)


def _bdot_nt(a, b):
    return lax.dot_general(a.astype(BF16), b.astype(BF16), (((1,), (1,)), ((), ())),
                           preferred_element_type=F32)


def _split3(v):
    hi = v.astype(BF16)
    r1 = v - hi.astype(F32)
    mid = r1.astype(BF16)
    lo = (r1 - mid.astype(F32)).astype(BF16)
    return hi, mid, lo


def _mod_kernel(c_ref, w_ref, b_ref, o_ref):
    c = c_ref[...]
    o_ref[...] = _bdot(_silu(c), w_ref[...]) + b_ref[...]


def _modulation(cond, mod_w, mod_b):
    n_layer = mod_w.shape[0]
    rows = cond.shape[0]
    return pl.pallas_call(
        _mod_kernel,
        out_shape=jax.ShapeDtypeStruct((n_layer, rows, N_MOD * D_MODEL), F32),
        grid=(n_layer, N_MOD),
        in_specs=[
            pl.BlockSpec((rows, D_MODEL), lambda l, n: (0, 0)),
            pl.BlockSpec((None, D_MODEL, D_MODEL), lambda l, n: (l, 0, n)),
            pl.BlockSpec((None, 1, D_MODEL), lambda l, n: (l, 0, n)),
        ],
        out_specs=pl.BlockSpec((None, rows, D_MODEL), lambda l, n: (l, 0, n)),
        compiler_params=_params(("arbitrary", "arbitrary")),
        name="modulation",
    )(cond, mod_w, mod_b.reshape(n_layer, 1, N_MOD * D_MODEL))


def _mod_part(mod_ref, k):
    return mod_ref[:, k * D_MODEL:(k + 1) * D_MODEL]


def _mod_row(n_batch):
    return lambda b, j: (jnp.where(j == 0, n_batch, b), 0, 0)


def _inproj_kernel(x_ref, mod_ref, w_ref, attn_ref, conv_ref, sz_ref, su_ref, dt_ref):
    h = _ln(x_ref[...]) * (1.0 + _mod_part(mod_ref, 1)) + _mod_part(mod_ref, 0)
    hb = h.astype(BF16)
    off = 0
    for ref in (attn_ref, conv_ref, sz_ref, su_ref, dt_ref):
        width = ref.shape[-1]
        ref[...] = jnp.dot(hb, w_ref[:, off:off + width], preferred_element_type=F32)
        off += width


def _inproj(xcat, mods, w1):
    n_batch = xcat.shape[0]
    widths = (ATTN_W, CONV_W, 512, 512, DT_W)
    return pl.pallas_call(
        _inproj_kernel,
        out_shape=[jax.ShapeDtypeStruct((n_batch, TOK, w), F32) for w in widths],
        grid=(n_batch, TOK // TM),
        in_specs=[
            pl.BlockSpec((None, TM, D_MODEL), lambda b, j: (b, j, 0)),
            pl.BlockSpec((None, 1, N_MOD * D_MODEL), _mod_row(n_batch)),
            pl.BlockSpec((D_MODEL, PROJ_W), lambda b, j: (0, 0)),
        ],
        out_specs=[pl.BlockSpec((None, TM, w), lambda b, j: (b, j, 0)) for w in widths],
        compiler_params=_params(("arbitrary", "arbitrary")),
        name="inproj",
    )(xcat, mods, w1)


def _rope(t, cos, sin):
    width = t.shape[-1]
    lane = lax.broadcasted_iota(jnp.int32, t.shape, t.ndim - 1)
    rot = jnp.where(lane % 32 < 16, -pltpu.roll(t, width - 16, t.ndim - 1), pltpu.roll(t, 16, t.ndim - 1))
    return t * cos + rot * sin


def _tile4(v):
    return jnp.concatenate([v, v, v, v], axis=1)


def _head_inv_rms(t, n_heads):
    t2 = t * t
    lane = lax.broadcasted_iota(jnp.int32, t.shape, 1)
    inv = jnp.zeros_like(t)
    for h in range(n_heads):
        ms = jnp.sum(t2[:, h * HEAD_DIM:(h + 1) * HEAD_DIM], axis=1, keepdims=True) * (1.0 / HEAD_DIM)
        inv = jnp.where(lane // HEAD_DIM == h, lax.rsqrt(ms + NORM_EPS), inv)
    return inv


def _stack_heads(qr, kv):
    parts = [qr[:, (kv * GRP + g) * HEAD_DIM:(kv * GRP + g + 1) * HEAD_DIM] for g in range(GRP)]
    return jnp.concatenate(parts, axis=0).astype(BF16)


def _store_heads(out_ref, o, kv):
    for g in range(GRP):
        h = kv * GRP + g
        out_ref[:, h * HEAD_DIM:(h + 1) * HEAD_DIM] = o[g * QB:(g + 1) * QB].astype(out_ref.dtype)


def _wattn_kernel(q_ref, k_ref, v_ref, cos_ref, sin_ref, sink_ref, out_ref, kb_ref, vb_ref):
    i = pl.program_id(1)

    @pl.when(i == 0)
    def _prep():
        kr = _rope(k_ref[...], cos_ref[...], sin_ref[...])
        v = v_ref[...]
        for kv in range(N_KV):
            kb_ref[kv] = kr[:, kv * HEAD_DIM:(kv + 1) * HEAD_DIM].astype(BF16)
            vb_ref[kv] = v[:, kv * HEAD_DIM:(kv + 1) * HEAD_DIM].astype(BF16)

    row0 = pl.multiple_of(i * QB, QB)
    cos = _tile4(cos_ref[pl.ds(row0, QB), :])
    sin = _tile4(sin_ref[pl.ds(row0, QB), :])
    qr = _rope(q_ref[...], cos, sin) * (HEAD_DIM ** -0.5)

    def attend(band_start):
        for kv in range(N_KV):
            qs = _stack_heads(qr, kv)
            sink = sink_ref[kv]
            s_ctx = _bdot_nt(qs, kb_ref[kv, 0:CTX_LEN, :])
            m = jnp.maximum(jnp.max(s_ctx, axis=1, keepdims=True), sink)
            if band_start is not None:
                start = pl.multiple_of(CTX_LEN + band_start, QB)
                s_loc = _bdot_nt(qs, kb_ref[kv, pl.ds(start, 3 * QB), :])
                q_pos = (i - CTX_LEN // QB) * QB + lax.broadcasted_iota(jnp.int32, s_loc.shape, 0) % QB
                k_pos = band_start + lax.broadcasted_iota(jnp.int32, s_loc.shape, 1)
                s_loc = jnp.where(jnp.abs(q_pos - k_pos) <= WINDOW, s_loc, -jnp.inf)
                m = jnp.maximum(m, jnp.max(s_loc, axis=1, keepdims=True))
            p_ctx = jnp.exp(s_ctx - m)
            den = jnp.sum(p_ctx, axis=1, keepdims=True) + jnp.exp(sink - m)
            o = _bdot(p_ctx, vb_ref[kv, 0:CTX_LEN, :])
            if band_start is not None:
                p_loc = jnp.exp(s_loc - m)
                den = den + jnp.sum(p_loc, axis=1, keepdims=True)
                o = o + _bdot(p_loc, vb_ref[kv, pl.ds(start, 3 * QB), :])
            _store_heads(out_ref, o / den, kv)

    @pl.when(i < CTX_LEN // QB)
    def _ctx_queries():
        attend(None)

    @pl.when(i >= CTX_LEN // QB)
    def _latent_queries():
        j = i - CTX_LEN // QB
        attend(jnp.clip((j - 1) * QB, 0, SEQ - 3 * QB))


def _gattn_kernel(q_ref, k_ref, v_ref, cos_ref, sin_ref, qw_ref, kw_ref, out_ref, kb_ref, vb_ref):
    i = pl.program_id(1)

    @pl.when(i == 0)
    def _prep():
        k = k_ref[...]
        kn = k * _head_inv_rms(k, N_KV) * kw_ref[...]
        kr = _rope(kn, cos_ref[...], sin_ref[...])
        v = v_ref[...]
        for kv in range(N_KV):
            kb_ref[kv] = kr[:, kv * HEAD_DIM:(kv + 1) * HEAD_DIM].astype(BF16)
            vb_ref[kv] = v[:, kv * HEAD_DIM:(kv + 1) * HEAD_DIM].astype(BF16)

    row0 = pl.multiple_of(i * QB, QB)
    cos = _tile4(cos_ref[pl.ds(row0, QB), :])
    sin = _tile4(sin_ref[pl.ds(row0, QB), :])
    q = q_ref[...]
    qn = q * _head_inv_rms(q, N_HEADS) * qw_ref[...]
    qr = _rope(qn, cos, sin) * (HEAD_DIM ** -0.5)

    def attend(n_keys):
        for kv in range(N_KV):
            qs = _stack_heads(qr, kv)
            s = _bdot_nt(qs, kb_ref[kv, 0:n_keys, :])
            m = jnp.max(s, axis=1, keepdims=True)
            p = jnp.exp(s - m)
            den = jnp.sum(p, axis=1, keepdims=True)
            o = _bdot(p, vb_ref[kv, 0:n_keys, :])
            _store_heads(out_ref, o / den, kv)

    @pl.when(i < CTX_LEN // QB)
    def _ctx_queries():
        attend(CTX_LEN)

    @pl.when(i >= CTX_LEN // QB)
    def _latent_queries():
        attend(TOK)


def _attention(attn, cos, sin, sink_rows, qw, kw):
    n_batch = attn.shape[0]
    grid = (n_batch, TOK // QB)
    q_spec = lambda blk: pl.BlockSpec((None, QB, 512), lambda b, i: (b, i, blk))
    kv_spec = lambda blk: pl.BlockSpec((None, TOK, 128), lambda b, i: (b, 0, blk))
    tab_spec = pl.BlockSpec((TOK, 128), lambda b, i: (0, 0))
    out_spec = pl.BlockSpec((None, QB, 512), lambda b, i: (b, i, 0))
    out_shape = jax.ShapeDtypeStruct((n_batch, TOK, 512), BF16)
    scratch = [pltpu.VMEM((N_KV, TOK, HEAD_DIM), BF16), pltpu.VMEM((N_KV, TOK, HEAD_DIM), BF16)]
    ya = pl.pallas_call(
        _wattn_kernel, out_shape=out_shape, grid=grid,
        in_specs=[q_spec(0), kv_spec(8), kv_spec(9), tab_spec, tab_spec,
                  pl.BlockSpec((N_KV, GRP * QB, 1), lambda b, i: (0, 0, 0))],
        out_specs=out_spec, scratch_shapes=scratch,
        compiler_params=_params(("arbitrary", "arbitrary")), name="window_attention",
    )(attn, attn, attn, cos, sin, sink_rows)
    yg = pl.pallas_call(
        _gattn_kernel, out_shape=out_shape, grid=grid,
        in_specs=[q_spec(1), kv_spec(10), kv_spec(11), tab_spec, tab_spec,
                  pl.BlockSpec((1, 512), lambda b, i: (0, 0)),
                  pl.BlockSpec((1, 128), lambda b, i: (0, 0))],
        out_specs=out_spec, scratch_shapes=scratch,
        compiler_params=_params(("arbitrary", "arbitrary")), name="global_attention",
    )(attn, attn, attn, cos, sin, qw, kw)
    return ya, yg


def _ssd_chunk(s):
    r = s - N_CHUNK
    back = jnp.where(r == 0, 1, jnp.where(r == 1, 0, N_CHUNK + 1 - r))
    return jnp.where(s < N_CHUNK, s, back)


def _ssd_kernel(x_ref, prev_ref, next_ref, z_ref, dt_ref, cw_ref, cb_ref, dtb_ref, a_ref, d_ref, nw_ref,
                out_ref, yf_ref, state_ref, g_ref):
    s = pl.program_id(1)
    chunk = _ssd_chunk(s)
    backward = s >= N_CHUNK

    @pl.when((s == 0) | (s == N_CHUNK))
    def _reset():
        state_ref[...] = jnp.zeros_like(state_ref)

    xin = x_ref[...]
    first = (chunk == 0) | (chunk == CTX_LEN // SSD_T)
    last = (chunk == CTX_LEN // SSD_T - 1) | (chunk == N_CHUNK - 1)
    prev = jnp.where(first, 0.0, prev_ref[7:8, :])
    nxt = jnp.where(last, 0.0, next_ref[0:1, :])
    row = lax.broadcasted_iota(jnp.int32, xin.shape, 0)
    xm1 = jnp.where(row == 0, prev, pltpu.roll(xin, 1, 0))
    xp1 = jnp.where(row == SSD_T - 1, nxt, pltpu.roll(xin, SSD_T - 1, 0))
    conv = cw_ref[0:1, :] * xm1 + cw_ref[1:2, :] * xin + cw_ref[2:3, :] * xp1 + cb_ref[...]
    xbc = _silu(conv)
    x = xbc[:, 0:SSD_WIDTH]
    bmat = [xbc[:, SSD_WIDTH + g * SSD_N:SSD_WIDTH + (g + 1) * SSD_N] for g in range(SSD_GROUPS)]
    cmat = [xbc[:, SSD_WIDTH + (SSD_GROUPS + g) * SSD_N:SSD_WIDTH + (SSD_GROUPS + g + 1) * SSD_N]
            for g in range(SSD_GROUPS)]

    dtv = dt_ref[...] + dtb_ref[...]
    dt = jnp.maximum(dtv, 0.0) + jnp.log(1.0 + jnp.exp(-jnp.abs(dtv)))
    da = dt * (-jnp.exp(a_ref[...]))
    tr = lax.broadcasted_iota(jnp.int32, (SSD_T, SSD_T), 0)
    tc = lax.broadcasted_iota(jnp.int32, (SSD_T, SSD_T), 1)
    causal = jnp.where(backward, tc - tr, tr - tc) >= 0
    tri = jnp.where(causal, 1.0, 0.0).astype(BF16)
    hi, mid, lo = _split3(da)
    cs = (jnp.dot(tri, hi, preferred_element_type=F32) + jnp.dot(tri, mid, preferred_element_type=F32)
          + jnp.dot(tri, lo, preferred_element_type=F32))
    cs_t = cs.T
    tot = jnp.where(backward, cs[0:1, :], cs[SSD_T - 1:SSD_T, :])
    ecs = jnp.exp(cs)
    dec_end = jnp.exp(tot - cs)
    etot = jnp.exp(tot)

    cb = [_bdot_nt(cmat[g], bmat[g]) for g in range(SSD_GROUPS)]
    b_t = [bmat[g].T.astype(BF16) for g in range(SSD_GROUPS)]
    c_b = [cmat[g].astype(BF16) for g in range(SSD_GROUPS)]

    row0 = pl.multiple_of(chunk * SSD_T, SSD_T)
    ssq = jnp.zeros((SSD_T, 1), F32)
    for h in range(SSD_HEADS):
        g = h // (SSD_HEADS // SSD_GROUPS)
        lanes = slice(h * SSD_P, (h + 1) * SSD_P)
        x_h = x[:, lanes]
        xd = x_h * dt[:, h:h + 1]
        seg = jnp.exp(jnp.where(causal, cs[:, h:h + 1] - cs_t[h:h + 1, :], -jnp.inf))
        y = _bdot(cb[g] * seg, xd)
        st = state_ref[h]
        y = y + ecs[:, h:h + 1] * jnp.dot(c_b[g], st.astype(BF16), preferred_element_type=F32)
        state_ref[h] = etot[:, h:h + 1] * st + jnp.dot(
            b_t[g], (xd * dec_end[:, h:h + 1]).astype(BF16), preferred_element_type=F32)

        @pl.when(jnp.logical_not(backward))
        def _keep():
            yf_ref[pl.ds(row0, SSD_T), lanes] = y

        @pl.when(backward)
        def _gate():
            ytot = yf_ref[pl.ds(row0, SSD_T), lanes] + y + d_ref[:, lanes] * x_h
            g_ref[:, lanes] = ytot * _silu(z_ref[:, lanes])

    @pl.when(backward)
    def _finish():
        gated = g_ref[...]
        ms = jnp.mean(gated * gated, axis=1, keepdims=True)
        out_ref[...] = (gated * lax.rsqrt(ms + NORM_EPS) * nw_ref[...]).astype(out_ref.dtype)


def _ssd(conv_in, sz, dt, conv_w, conv_b, dt_bias, a_log, d_exp, norm_w):
    n_batch = conv_in.shape[0]
    halo = SSD_T // 8
    n_halo = TOK // 8

    def chunk_map(b, s):
        return (b, _ssd_chunk(s), 0)

    def out_map(b, s):
        return (b, jnp.where(s < N_CHUNK, 1, _ssd_chunk(s)), 0)

    const = lambda shape: pl.BlockSpec(shape, lambda b, s: tuple(0 for _ in shape))
    dir_spec = pl.BlockSpec((None, 1, 128), lambda b, s: (s // N_CHUNK, 0, 0))
    return pl.pallas_call(
        _ssd_kernel,
        out_shape=jax.ShapeDtypeStruct((n_batch, TOK, SSD_WIDTH), BF16),
        grid=(n_batch, 2 * N_CHUNK),
        in_specs=[
            pl.BlockSpec((None, SSD_T, CONV_W), chunk_map),
            pl.BlockSpec((None, 8, CONV_W), lambda b, s: (b, jnp.maximum(_ssd_chunk(s) * halo - 1, 0), 0)),
            pl.BlockSpec((None, 8, CONV_W), lambda b, s: (b, jnp.minimum((_ssd_chunk(s) + 1) * halo, n_halo - 1), 0)),
            pl.BlockSpec((None, SSD_T, SSD_WIDTH), chunk_map),
            pl.BlockSpec((None, SSD_T, 128), lambda b, s: (b, _ssd_chunk(s), s // N_CHUNK)),
            const((3, CONV_W)), const((1, CONV_W)), dir_spec, dir_spec,
            const((1, SSD_WIDTH)), const((1, SSD_WIDTH)),
        ],
        out_specs=pl.BlockSpec((None, SSD_T, SSD_WIDTH), out_map),
        scratch_shapes=[pltpu.VMEM((TOK, SSD_WIDTH), F32),
                        pltpu.VMEM((SSD_HEADS, SSD_N, SSD_P), F32),
                        pltpu.VMEM((SSD_T, SSD_WIDTH), F32)],
        compiler_params=_params(("arbitrary", "arbitrary")),
        name="ssd",
    )(conv_in, conv_in, conv_in, sz, dt, conv_w, conv_b, dt_bias, a_log, d_exp, norm_w)


def _s5_kernel(u_ref, m_ref, hf_ref, hb_ref, gf_ref, gb_ref, a_ref, d_ref, y_ref, loc_ref, prev_ref, *, n_batch):
    u = u_ref[...]
    ub = u.astype(BF16)
    y = jnp.dot(ub, m_ref[...], preferred_element_type=F32) + d_ref[...] * u

    for direction, (h_ref, g_ref) in enumerate(((hf_ref, gf_ref), (hb_ref, gb_ref))):
        loc_ref[...] = jnp.dot(ub, h_ref[...], preferred_element_type=F32)
        a_re = a_ref[direction, 0:1, :]
        a_im = a_ref[direction, 1:2, :]

        def step(k, carry, direction=direction, a_re=a_re, a_im=a_im):
            s_re, s_im = carry
            if direction == 0:
                c = k
            else:
                c = jnp.where(k < S5_CTX_CH, S5_CTX_CH - 1 - k, S5_NCH + S5_CTX_CH - 1 - k)
            rows = pl.ds(pl.multiple_of(c * n_batch, n_batch), n_batch)
            prev_ref[rows, 0:S5_STATE] = s_re
            prev_ref[rows, S5_STATE:2 * S5_STATE] = s_im
            l_re = loc_ref[rows, 0:S5_STATE]
            l_im = loc_ref[rows, S5_STATE:2 * S5_STATE]
            return (a_re * s_re - a_im * s_im + l_re, a_re * s_im + a_im * s_re + l_im)

        zero = jnp.zeros((n_batch, S5_STATE), F32)
        lax.fori_loop(0, S5_NCH, step, (zero, zero))
        y = y + jnp.dot(prev_ref[...].astype(BF16), g_ref[...], preferred_element_type=F32)
    y_ref[...] = y


def _s5(u_g, mats, n_batch):
    m_all, h_f, h_b, g_f, g_b, a16, d_row = mats
    rows = u_g.shape[1]
    grp = lambda shape: pl.BlockSpec((None,) + shape, lambda g: (g,) + tuple(0 for _ in shape))
    return pl.pallas_call(
        functools.partial(_s5_kernel, n_batch=n_batch),
        out_shape=jax.ShapeDtypeStruct((S5_GROUPS, rows, S5_ROWW), F32),
        grid=(S5_GROUPS,),
        in_specs=[grp((rows, S5_ROWW)), grp((S5_ROWW, S5_ROWW)),
                  grp((S5_ROWW, 2 * S5_STATE)), grp((S5_ROWW, 2 * S5_STATE)),
                  grp((2 * S5_STATE, S5_ROWW)), grp((2 * S5_STATE, S5_ROWW)),
                  grp((2, 2, S5_STATE)), grp((1, S5_ROWW))],
        out_specs=grp((rows, S5_ROWW)),
        scratch_shapes=[pltpu.VMEM((rows, 2 * S5_STATE), F32), pltpu.VMEM((rows, 2 * S5_STATE), F32)],
        compiler_params=_params(("arbitrary",)),
        name="s5",
    )(u_g, m_all, h_f, h_b, g_f, g_b, a16, d_row)


def _cmul(a, b):
    return a[0] * b[0] - a[1] * b[1], a[0] * b[1] + a[1] * b[0]


def _s5_matrices(a_re, a_im, log_dt, b_re, b_im, c_re, c_im, d_skip):
    hp = lax.Precision.HIGHEST
    t = jnp.arange(S5_T + 1, dtype=F32)
    m_sum = 0.0
    h_all, g_all, a16_all = [], [], []
    c = (c_re.astype(F32), c_im.astype(F32))
    for direction in range(2):
        are = jnp.minimum(a_re[direction].astype(F32), -1e-4)
        aim = a_im[direction].astype(F32)
        dt = jnp.exp(log_dt[direction].astype(F32))[:, None]
        mag = jnp.exp(t[:, None, None] * (are * dt)[None])
        ang = t[:, None, None] * (aim * dt)[None]
        pw = (mag * jnp.cos(ang), mag * jnp.sin(ang))
        num = (pw[0][1] - 1.0, pw[1][1])
        den = are * are + aim * aim
        coef = ((num[0] * are + num[1] * aim) / den, (num[1] * are - num[0] * aim) / den)
        bbar = _cmul((coef[0][..., None], coef[1][..., None]), (b_re.astype(F32), b_im.astype(F32)))
        pb = _cmul((pw[0][:S5_T, :, :, None], pw[1][:S5_T, :, :, None]), (bbar[0][None], bbar[1][None]))
        taps = (jnp.einsum('gon,tgni->tgoi', c[0], pb[0], precision=hp)
                - jnp.einsum('gon,tgni->tgoi', c[1], pb[1], precision=hp))
        ti = jnp.arange(S5_T)
        lag = (ti[None, :] - ti[:, None]) if direction == 0 else (ti[:, None] - ti[None, :])
        k_full = taps[jnp.clip(lag, 0, S5_T - 1)]
        k_full = jnp.where((lag >= 0)[:, :, None, None, None], k_full, 0.0)
        m_sum = m_sum + k_full.transpose(2, 0, 4, 1, 3).reshape(S5_GROUPS, S5_ROWW, S5_ROWW)
        e_in = (S5_T - 1 - ti) if direction == 0 else ti
        hb = _cmul((pw[0][e_in][..., None], pw[1][e_in][..., None]), (bbar[0][None], bbar[1][None]))
        h_mat = jnp.concatenate([hb[0], hb[1]], axis=2)
        h_all.append(h_mat.transpose(1, 0, 3, 2).reshape(S5_GROUPS, S5_ROWW, 2 * S5_STATE))
        e_out = (ti + 1) if direction == 0 else (S5_T - ti)
        cp = _cmul((c[0][None], c[1][None]),
                   (pw[0][e_out][:, :, None, :], pw[1][e_out][:, :, None, :]))
        g_mat = jnp.concatenate([cp[0], -cp[1]], axis=3)
        g_all.append(g_mat.transpose(1, 3, 0, 2).reshape(S5_GROUPS, 2 * S5_STATE, S5_ROWW))
        a16_all.append(jnp.stack([pw[0][S5_T], pw[1][S5_T]], axis=1))
    d_row = jnp.tile(d_skip.astype(F32).reshape(S5_GROUPS, 1, S5_GROUP), (1, 1, S5_T))
    return (m_sum.astype(BF16), h_all[0].astype(BF16), h_all[1].astype(BF16),
            g_all[0].astype(BF16), g_all[1].astype(BF16), jnp.stack(a16_all, axis=1), d_row)


def _route(logits_t, bias):
    scores = [_sigmoid(logits_t[j * N_EXPERT_GROUPS:(j + 1) * N_EXPERT_GROUPS]) for j in range(PER_GROUP)]
    sel = [scores[j] + bias[j * N_EXPERT_GROUPS:(j + 1) * N_EXPERT_GROUPS] for j in range(PER_GROUP)]
    hi1, lo1 = jnp.maximum(sel[0], sel[1]), jnp.minimum(sel[0], sel[1])
    hi2, lo2 = jnp.maximum(sel[2], sel[3]), jnp.minimum(sel[2], sel[3])
    group_score = jnp.maximum(hi1, hi2) + jnp.maximum(jnp.minimum(hi1, hi2), jnp.maximum(lo1, lo2))
    gid = lax.broadcasted_iota(jnp.int32, group_score.shape, 0)
    best = jnp.max(group_score, axis=0, keepdims=True)
    grp = jnp.min(jnp.where(group_score == best, gid, N_EXPERT_GROUPS), axis=0, keepdims=True)
    pick = gid == grp
    v = [jnp.sum(jnp.where(pick, sel[j], 0.0), axis=0, keepdims=True) for j in range(PER_GROUP)]
    sc = [jnp.sum(jnp.where(pick, scores[j], 0.0), axis=0, keepdims=True) for j in range(PER_GROUP)]
    rank = []
    for j in range(PER_GROUP):
        r = jnp.zeros_like(grp)
        for i in range(PER_GROUP):
            if i < j:
                r = r + jnp.where(v[i] >= v[j], 1, 0)
            elif i > j:
                r = r + jnp.where(v[i] > v[j], 1, 0)
        rank.append(r)
    e, w = [], []
    for k in range(TOP_K):
        loc = sum(jnp.where(rank[j] == k, j, 0) for j in range(PER_GROUP))
        e.append((grp * PER_GROUP + loc).astype(F32))
        w.append(sum(jnp.where(rank[j] == k, sc[j], 0.0) for j in range(PER_GROUP)))
    wsum = w[0] + w[1]
    rows = [e[0], e[1], w[0] / wsum, w[1] / wsum]
    return jnp.concatenate(rows + [jnp.zeros_like(wsum)] * (8 - len(rows)), axis=0)


def _gelu_tanh(x):
    return 0.5 * x * (1.0 + jnp.tanh(math.sqrt(2.0 / math.pi) * (x + 0.044715 * (x * x * x))))


def _merge_kernel(x_ref, mod_ref, ya_ref, ys_ref, yg_ref, y5_ref, wg_ref, wb_ref, wglu_ref, wout_ref,
                  lng_ref, lnb_ref, rw_ref, rb_ref, x1_ref, h2_ref, route_ref):
    x = x_ref[...]
    hb = (_ln(x) * (1.0 + _mod_part(mod_ref, 1)) + _mod_part(mod_ref, 0)).astype(BF16)
    glu = jnp.dot(_gelu_tanh(y5_ref[...]).astype(BF16), wglu_ref[...], preferred_element_type=F32)
    y5 = (glu[:, 0:S5_WIDTH] * _sigmoid(glu[:, S5_WIDTH:2 * S5_WIDTH])).astype(BF16)
    acc = jnp.zeros(x.shape, F32)
    for n, y in enumerate((ya_ref[...], ys_ref[...], yg_ref[...], y5)):
        gate = _sigmoid(jnp.dot(hb, wg_ref[:, n * D_MODEL:(n + 1) * D_MODEL], preferred_element_type=F32))
        acc = acc + gate * jnp.dot(y, wb_ref[n], preferred_element_type=F32)
    mix = jnp.dot(acc.astype(BF16), wout_ref[...], preferred_element_type=F32)
    x1 = _ln(ALPHA * x + _mod_part(mod_ref, 2) * mix) * lng_ref[...] + lnb_ref[...]
    x1_ref[...] = x1
    h2 = (_ln(x1) * (1.0 + _mod_part(mod_ref, 4)) + _mod_part(mod_ref, 3)).astype(BF16)
    h2_ref[...] = h2
    route_ref[...] = _route(_bdot_nt(rw_ref[...], h2), rb_ref[...])


def _merge(xcat, mods, ya, ys, yg, y5, wg, wb, wglu, wout, ln_g, ln_b, rw_t, rb):
    n_batch = xcat.shape[0]
    n_tiles = TOK // TM
    tile = lambda w: pl.BlockSpec((None, TM, w), lambda b, j: (b, j, 0))
    const = lambda shape: pl.BlockSpec(shape, lambda b, j: tuple(0 for _ in shape))
    return pl.pallas_call(
        _merge_kernel,
        out_shape=[jax.ShapeDtypeStruct((n_batch, TOK, D_MODEL), F32),
                   jax.ShapeDtypeStruct((n_batch, TOK, D_MODEL), BF16),
                   jax.ShapeDtypeStruct((8, n_batch * TOK), F32)],
        grid=(n_batch, n_tiles),
        in_specs=[tile(D_MODEL), pl.BlockSpec((None, 1, N_MOD * D_MODEL), _mod_row(n_batch)),
                  tile(512), tile(512), tile(512), tile(512),
                  const((D_MODEL, N_BRANCH * D_MODEL)), const((N_BRANCH, 512, D_MODEL)),
                  const((S5_WIDTH, 2 * S5_WIDTH)), const((D_MODEL, D_MODEL)),
                  const((1, D_MODEL)), const((1, D_MODEL)),
                  const((N_EXPERTS, D_MODEL)), const((N_EXPERTS, 1))],
        out_specs=[tile(D_MODEL), tile(D_MODEL),
                   pl.BlockSpec((8, TM), lambda b, j: (0, b * n_tiles + j))],
        compiler_params=_params(("arbitrary", "arbitrary")),
        name="merge",
    )(xcat, mods, ya, ys, yg, y5, wg, wb, wglu, wout, ln_g, ln_b, rw_t, rb)


def _moe_kernel(be_ref, x_ref, w_ref, wgu_ref, wd_ref, y_ref):
    del be_ref
    gu = jnp.dot(x_ref[...], wgu_ref[...], preferred_element_type=F32)
    mid = _silu(gu[:, 0:D_EXPERT]) * gu[:, D_EXPERT:2 * D_EXPERT]
    y_ref[...] = jnp.dot(mid.astype(BF16), wd_ref[...], preferred_element_type=F32) * w_ref[...]


def _moe_experts(block_expert, xs, slot_w, wgu, wd):
    n_blocks = block_expert.shape[0]
    return pl.pallas_call(
        _moe_kernel,
        out_shape=jax.ShapeDtypeStruct((n_blocks * MOE_ROWS, D_MODEL), F32),
        grid_spec=pltpu.PrefetchScalarGridSpec(
            num_scalar_prefetch=1, grid=(n_blocks,),
            in_specs=[pl.BlockSpec((MOE_ROWS, D_MODEL), lambda i, be: (i, 0)),
                      pl.BlockSpec((MOE_ROWS, 1), lambda i, be: (i, 0)),
                      pl.BlockSpec((None, D_MODEL, 2 * D_EXPERT), lambda i, be: (be[i], 0, 0)),
                      pl.BlockSpec((None, D_EXPERT, D_MODEL), lambda i, be: (be[i], 0, 0))],
            out_specs=pl.BlockSpec((MOE_ROWS, D_MODEL), lambda i, be: (i, 0))),
        compiler_params=_params(("arbitrary",)),
        name="moe_experts",
    )(block_expert, xs, slot_w, wgu, wd)


def _final_kernel(x_ref, mod_ref, f_ref, lng_ref, lnb_ref, o_ref):
    o_ref[...] = _ln(ALPHA * x_ref[...] + _mod_part(mod_ref, 5) * f_ref[...]) * lng_ref[...] + lnb_ref[...]


def _final(x1, mods, ffn, ln_g, ln_b):
    n_batch = x1.shape[0]
    tile = pl.BlockSpec((None, TM, D_MODEL), lambda b, j: (b, j, 0))
    const = pl.BlockSpec((1, D_MODEL), lambda b, j: (0, 0))
    return pl.pallas_call(
        _final_kernel,
        out_shape=jax.ShapeDtypeStruct(x1.shape, F32),
        grid=(n_batch, TOK // TM),
        in_specs=[tile, pl.BlockSpec((None, 1, N_MOD * D_MODEL), _mod_row(n_batch)), tile, const, const],
        out_specs=tile,
        compiler_params=_params(("arbitrary", "arbitrary")),
        name="ffn_residual",
    )(x1, mods, ffn, ln_g, ln_b)


def _dispatch(route):
    n_tok = route.shape[1]
    n_assign = n_tok * TOP_K
    n_blocks = -(-(n_assign + N_EXPERTS * (MOE_ROWS - 1)) // MOE_ROWS)
    cap = n_blocks * MOE_ROWS
    flat_e = route[0:TOP_K].astype(jnp.int32).T.reshape(-1)
    flat_w = route[TOP_K:2 * TOP_K].T.reshape(-1)
    flat_t = jnp.repeat(jnp.arange(n_tok, dtype=jnp.int32), TOP_K)
    order = jnp.argsort(flat_e)
    e_s = flat_e[order]
    counts = jnp.bincount(flat_e, length=N_EXPERTS)
    padded = (counts + MOE_ROWS - 1) // MOE_ROWS * MOE_ROWS
    first = jnp.cumsum(counts) - counts
    pad_end = jnp.cumsum(padded)
    dest = ((pad_end - padded)[e_s] + jnp.arange(n_assign) - first[e_s]).astype(jnp.int32)
    slot_tok = jnp.zeros((cap,), jnp.int32).at[dest].set(flat_t[order])
    slot_w = jnp.zeros((cap,), F32).at[dest].set(flat_w[order])
    pos = jnp.zeros((n_assign,), jnp.int32).at[order].set(dest).reshape(n_tok, TOP_K)
    block_expert = jnp.minimum(jnp.searchsorted(pad_end, jnp.arange(n_blocks) * MOE_ROWS, side='right'),
                               N_EXPERTS - 1).astype(jnp.int32)
    return slot_tok, slot_w, pos, block_expert


def _col(w, part):
    return w[:, _IN_OFF[part]:_IN_OFF[part + 1]]


def _proj_weights(w_in):
    zeros = jnp.zeros((D_MODEL, 128 - SSD_HEADS), w_in.dtype)
    dt = _col(w_in, _SDT)
    parts = [_col(w_in, p) for p in (_AQ, _GQ, _AK, _AV, _GK, _GV, _SX, _SB, _SC, _SZ, _SU)]
    parts += [dt[:, 0:SSD_HEADS], zeros, dt[:, SSD_HEADS:2 * SSD_HEADS], zeros]
    return jnp.concatenate(parts, axis=1).astype(BF16), _col(w_in, _GATE).astype(BF16)


def _rope_tables():
    rows = SEQ // GRID_W
    row = jnp.repeat(jnp.arange(rows, dtype=F32), GRID_W)
    col = jnp.tile(jnp.arange(GRID_W, dtype=F32), rows)
    axis_dim = HEAD_DIM // 2
    inv_freq = ROPE_THETA ** (-jnp.arange(0, axis_dim, 2, dtype=F32) / axis_dim)
    ang_r = row[:, None] * inv_freq
    ang_c = col[:, None] * inv_freq
    ang = jnp.concatenate([ang_r, ang_r, ang_c, ang_c], axis=-1)
    cos = jnp.concatenate([jnp.ones((CTX_LEN, HEAD_DIM), F32), jnp.cos(ang)], axis=0)
    sin = jnp.concatenate([jnp.zeros((CTX_LEN, HEAD_DIM), F32), jnp.sin(ang)], axis=0)
    return jnp.tile(cos, (1, 2)), jnp.tile(sin, (1, 2))


def _pad_lanes(v, width=128):
    return jnp.pad(v, ((0, 0), (0, width - v.shape[-1])))


def kernel(x, c, ctx, c_ctx, mod_w, mod_b, w_in, wa_sink, ga_q_norm, ga_k_norm, ssd_conv_w, ssd_conv_b, ssd_dt_bias, ssd_a_log, ssd_d, ssd_norm_w, s5_a_re, s5_a_im, s5_log_dt, s5_b_re, s5_b_im, s5_c_re, s5_c_im, s5_d, s5_w_glu, w_branch, w_out, ln1_g, ln1_b, ln2_g, ln2_b, router_w, router_bias, moe_w_gate, moe_w_up, moe_w_down):
    n_batch = x.shape[0]
    n_tok = n_batch * TOK
    xcat = jnp.concatenate([ctx, x], axis=1)
    mod_rows = -(-(n_batch + 1) // 8) * 8
    cond = jnp.zeros((mod_rows, D_MODEL), F32).at[:n_batch].set(c).at[n_batch].set(c_ctx)
    mods_all = _modulation(cond, mod_w, mod_b)
    cos, sin = _rope_tables()

    perm = np.array([g * PER_GROUP + j for j in range(PER_GROUP) for g in range(N_EXPERT_GROUPS)])
    rw_t = router_w.T[perm].astype(BF16)
    rb = router_bias.astype(F32)[perm].reshape(N_EXPERTS, 1)

    for layer in range(DEPTH):
        mods = mods_all[layer].reshape(mod_rows, 1, N_MOD * D_MODEL)
        w1, wg = _proj_weights(w_in[layer])
        attn, conv_in, sz, su, dt = _inproj(xcat, mods, w1)

        sink_rows = jnp.repeat(wa_sink[layer].astype(F32).reshape(N_KV, GRP), QB, axis=1).reshape(N_KV, GRP * QB, 1)
        qw = jnp.tile(ga_q_norm[layer].astype(F32), N_HEADS).reshape(1, 512)
        kw = jnp.tile(ga_k_norm[layer].astype(F32), N_KV).reshape(1, 128)
        ya, yg = _attention(attn, cos, sin, sink_rows, qw, kw)

        dt_bias = _pad_lanes(ssd_dt_bias[layer].astype(F32).reshape(2, SSD_HEADS)).reshape(2, 1, 128)
        a_log = _pad_lanes(ssd_a_log[layer].astype(F32)).reshape(2, 1, 128)
        d_exp = jnp.repeat(ssd_d[layer].astype(F32), SSD_P).reshape(1, SSD_WIDTH)
        ys = _ssd(conv_in, sz, dt, ssd_conv_w[layer].astype(F32), ssd_conv_b[layer].astype(F32).reshape(1, CONV_W),
                  dt_bias, a_log, d_exp, ssd_norm_w[layer].astype(F32).reshape(1, SSD_WIDTH))

        mats = _s5_matrices(s5_a_re[layer], s5_a_im[layer], s5_log_dt[layer], s5_b_re[layer], s5_b_im[layer],
                            s5_c_re[layer], s5_c_im[layer], s5_d[layer])
        u_g = su.reshape(n_batch, S5_NCH, S5_T, S5_GROUPS, S5_GROUP).transpose(3, 1, 0, 2, 4)
        y5_g = _s5(u_g.reshape(S5_GROUPS, S5_NCH * n_batch, S5_ROWW), mats, n_batch)
        y5 = y5_g.reshape(S5_GROUPS, S5_NCH, n_batch, S5_T, S5_GROUP).transpose(2, 1, 3, 0, 4)
        y5 = y5.reshape(n_batch, TOK, S5_WIDTH)

        x1, h2, route = _merge(xcat, mods, ya, ys, yg, y5, wg, w_branch[layer].astype(BF16),
                               s5_w_glu[layer].astype(BF16), w_out[layer].astype(BF16),
                               ln1_g[layer].reshape(1, D_MODEL), ln1_b[layer].reshape(1, D_MODEL), rw_t, rb)

        slot_tok, slot_w, pos, block_expert = _dispatch(route)
        xs = h2.reshape(n_tok, D_MODEL)[slot_tok]
        wgu = jnp.concatenate([moe_w_gate[layer], moe_w_up[layer]], axis=-1).astype(BF16)
        y_slots = _moe_experts(block_expert, xs, slot_w.reshape(-1, 1), wgu, moe_w_down[layer].astype(BF16))
        ffn = (y_slots[pos[:, 0]] + y_slots[pos[:, 1]]).reshape(n_batch, TOK, D_MODEL)
        xcat = _final(x1, mods, ffn, ln2_g[layer].reshape(1, D_MODEL), ln2_b[layer].reshape(1, D_MODEL))
    return xcat[:, CTX_LEN:, :]
```

```python
import functools
import math

import jax
import jax.numpy as jnp
import numpy as np
from jax import lax
from jax.experimental import pallas as pl
from jax.experimental.pallas import tpu as pltpu

F32 = jnp.float32
BF16 = jnp.bfloat16

D_MODEL = 1024
DEPTH = 4
GRID_W = 64
CTX_LEN = 256
SEQ = 2048
TOK = CTX_LEN + SEQ
N_MOD = 6
HEAD_DIM = 64
N_BRANCH = 4
N_HEADS = 8
N_KV = 2
GRP = N_HEADS // N_KV
WINDOW = 128
QB = 128
ROPE_THETA = 10000.0

SSD_HEADS = 8
SSD_P = 64
SSD_GROUPS = 2
SSD_N = 128
SSD_T = 128
SSD_WIDTH = SSD_HEADS * SSD_P
SSD_CONV_CH = SSD_WIDTH + 2 * SSD_GROUPS * SSD_N
N_CHUNK = TOK // SSD_T

S5_GROUP = 16
S5_WIDTH = 512
S5_GROUPS = S5_WIDTH // S5_GROUP
S5_STATE = 64
S5_T = 16
S5_ROWW = S5_T * S5_GROUP
S5_NCH = TOK // S5_T
S5_CTX_CH = CTX_LEN // S5_T

N_EXPERTS = 32
N_EXPERT_GROUPS = 8
PER_GROUP = N_EXPERTS // N_EXPERT_GROUPS
TOP_K = 2
D_EXPERT = 512
MOE_ROWS = 256

ALPHA = (2 * DEPTH) ** 0.25
NORM_EPS = 1e-6

TM = 256
VMEM_LIMIT = 56 * 1024 * 1024

_IN_SIZES = (512, 128, 128, 512, 128, 128, 512, 512, 256, 256, 16, 512, 4096)
_IN_OFF = np.concatenate([[0], np.cumsum(_IN_SIZES)]).astype(int)
(_AQ, _AK, _AV, _GQ, _GK, _GV, _SX, _SZ, _SB, _SC, _SDT, _SU, _GATE) = range(13)

ATTN_W = 1536
CONV_W = 1024
DT_W = 256
PROJ_W = ATTN_W + CONV_W + 512 + 512 + DT_W


def _params(sem=None):
    return pltpu.CompilerParams(dimension_semantics=sem, vmem_limit_bytes=VMEM_LIMIT)


def _ln(x):
    mu = jnp.mean(x, axis=-1, keepdims=True)
    xc = x - mu
    var = jnp.mean(xc * xc, axis=-1, keepdims=True)
    return xc * lax.rsqrt(var + NORM_EPS)


def _sigmoid(x):
    return 1.0 / (1.0 + jnp.exp(-x))


def _silu(x):
    return x * _sigmoid(x)


def _bdot(a, b):
    return jnp.dot(a.astype(BF16), b.astype(BF16), preferred_element_type=F32)


def _bdot_nt(a, b):
    return lax.dot_general(a.astype(BF16), b.astype(BF16), (((1,), (1,)), ((), ())),
                           preferred_element_type=F32)


def _split3(v):
    hi = v.astype(BF16)
    r1 = v - hi.astype(F32)
    mid = r1.astype(BF16)
    lo = (r1 - mid.astype(F32)).astype(BF16)
    return hi, mid, lo


def _mod_kernel(c_ref, w_ref, b_ref, o_ref):
    c = c_ref[...]
    o_ref[...] = _bdot(_silu(c), w_ref[...]) + b_ref[...]


def _modulation(cond, mod_w, mod_b):
    n_layer = mod_w.shape[0]
    rows = cond.shape[0]
    return pl.pallas_call(
        _mod_kernel,
        out_shape=jax.ShapeDtypeStruct((n_layer, rows, N_MOD * D_MODEL), F32),
        grid=(n_layer, N_MOD),
        in_specs=[
            pl.BlockSpec((rows, D_MODEL), lambda l, n: (0, 0)),
            pl.BlockSpec((None, D_MODEL, D_MODEL), lambda l, n: (l, 0, n)),
            pl.BlockSpec((None, 1, D_MODEL), lambda l, n: (l, 0, n)),
        ],
        out_specs=pl.BlockSpec((None, rows, D_MODEL), lambda l, n: (l, 0, n)),
        compiler_params=_params(("arbitrary", "arbitrary")),
        name="modulation",
    )(cond, mod_w, mod_b.reshape(n_layer, 1, N_MOD * D_MODEL))


def _mod_part(mod_ref, k):
    return mod_ref[:, k * D_MODEL:(k + 1) * D_MODEL]


def _mod_row(n_batch):
    return lambda b, j: (jnp.where(j == 0, n_batch, b), 0, 0)


def _inproj_kernel(x_ref, mod_ref, w_ref, attn_ref, conv_ref, sz_ref, su_ref, dt_ref):
    h = _ln(x_ref[...]) * (1.0 + _mod_part(mod_ref, 1)) + _mod_part(mod_ref, 0)
    hb = h.astype(BF16)
    off = 0
    for ref in (attn_ref, conv_ref, sz_ref, su_ref, dt_ref):
        width = ref.shape[-1]
        ref[...] = jnp.dot(hb, w_ref[:, off:off + width], preferred_element_type=F32)
        off += width


def _inproj(xcat, mods, w1):
    n_batch = xcat.shape[0]
    widths = (ATTN_W, CONV_W, 512, 512, DT_W)
    return pl.pallas_call(
        _inproj_kernel,
        out_shape=[jax.ShapeDtypeStruct((n_batch, TOK, w), F32) for w in widths],
        grid=(n_batch, TOK // TM),
        in_specs=[
            pl.BlockSpec((None, TM, D_MODEL), lambda b, j: (b, j, 0)),
            pl.BlockSpec((None, 1, N_MOD * D_MODEL), _mod_row(n_batch)),
            pl.BlockSpec((D_MODEL, PROJ_W), lambda b, j: (0, 0)),
        ],
        out_specs=[pl.BlockSpec((None, TM, w), lambda b, j: (b, j, 0)) for w in widths],
        compiler_params=_params(("arbitrary", "arbitrary")),
        name="inproj",
    )(xcat, mods, w1)


def _rope(t, cos, sin):
    width = t.shape[-1]
    lane = lax.broadcasted_iota(jnp.int32, t.shape, t.ndim - 1)
    rot = jnp.where(lane % 32 < 16, -pltpu.roll(t, width - 16, t.ndim - 1), pltpu.roll(t, 16, t.ndim - 1))
    return t * cos + rot * sin


def _tile4(v):
    return jnp.concatenate([v, v, v, v], axis=1)


def _head_inv_rms(t, n_heads):
    t2 = t * t
    lane = lax.broadcasted_iota(jnp.int32, t.shape, 1)
    inv = jnp.zeros_like(t)
    for h in range(n_heads):
        ms = jnp.sum(t2[:, h * HEAD_DIM:(h + 1) * HEAD_DIM], axis=1, keepdims=True) * (1.0 / HEAD_DIM)
        inv = jnp.where(lane // HEAD_DIM == h, lax.rsqrt(ms + NORM_EPS), inv)
    return inv


def _stack_heads(qr, kv):
    parts = [qr[:, (kv * GRP + g) * HEAD_DIM:(kv * GRP + g + 1) * HEAD_DIM] for g in range(GRP)]
    return jnp.concatenate(parts, axis=0).astype(BF16)


def _attend(qr, parts, sink_ref, out_ref, kb_ref, vb_ref, s_ref, p_ref):
    half_rows = GRP * QB // 2
    halves = [slice(0, half_rows), slice(half_rows, 2 * half_rows)]
    n_tiles = sum(width for _, width, _ in parts) // 128
    for kv in range(N_KV):
        qs = _stack_heads(qr, kv)
        col = 0
        for key_rows, width, bias in parts:
            for hr in halves:
                s = _bdot_nt(qs[hr], kb_ref[kv, key_rows, :])
                if bias is not None:
                    s = s + jnp.concatenate([bias] * (half_rows // QB), axis=0)
                s_ref[kv, hr, col:col + width] = s
            col += width
    dens = []
    for kv in range(N_KV):
        for r in range(GRP):
            rows = slice(r * QB, (r + 1) * QB)
            mx = s_ref[kv, rows, 0:128]
            for t in range(1, n_tiles):
                mx = jnp.maximum(mx, s_ref[kv, rows, t * 128:(t + 1) * 128])
            m = jnp.max(mx, axis=1, keepdims=True)
            if sink_ref is not None:
                m = jnp.maximum(m, sink_ref[kv, rows, :])
            m_b = jnp.broadcast_to(m, (QB, 128))
            acc = jnp.zeros((QB, 128), F32)
            for t in range(n_tiles):
                p = jnp.exp(s_ref[kv, rows, t * 128:(t + 1) * 128] - m_b)
                acc = acc + p
                p_ref[kv, rows, t * 128:(t + 1) * 128] = p.astype(BF16)
            den = jnp.sum(acc, axis=1, keepdims=True)
            if sink_ref is not None:
                den = den + jnp.exp(sink_ref[kv, rows, :] - m)
            dens.append(den)
    for kv in range(N_KV):
        for hi, hr in enumerate(halves):
            o = None
            col = 0
            for key_rows, width, _ in parts:
                part = jnp.dot(p_ref[kv, hr, col:col + width], vb_ref[kv, key_rows, :], preferred_element_type=F32)
                o = part if o is None else o + part
                col += width
            for gl in range(half_rows // QB):
                g = hi * (half_rows // QB) + gl
                h = kv * GRP + g
                out_ref[:, h * HEAD_DIM:(h + 1) * HEAD_DIM] = (
                    o[gl * QB:(gl + 1) * QB] / dens[kv * GRP + g]).astype(out_ref.dtype)


def _wattn_kernel(q_ref, k_ref, v_ref, cos_ref, sin_ref, sink_ref, bias_ref, out_ref, kb_ref, vb_ref, s_ref, p_ref):
    i = pl.program_id(1)

    @pl.when(i == 0)
    def _prep():
        kr = _rope(k_ref[...], cos_ref[...], sin_ref[...])
        v = v_ref[...]
        for kv in range(N_KV):
            kb_ref[kv] = kr[:, kv * HEAD_DIM:(kv + 1) * HEAD_DIM].astype(BF16)
            vb_ref[kv] = v[:, kv * HEAD_DIM:(kv + 1) * HEAD_DIM].astype(BF16)

    row0 = pl.multiple_of(i * QB, QB)
    cos = _tile4(cos_ref[pl.ds(row0, QB), :])
    sin = _tile4(sin_ref[pl.ds(row0, QB), :])
    qr = _rope(q_ref[...], cos, sin) * (HEAD_DIM ** -0.5)

    ctx_keys = (slice(0, CTX_LEN), CTX_LEN, None)

    @pl.when(i < CTX_LEN // QB)
    def _ctx_queries():
        _attend(qr, [ctx_keys], sink_ref, out_ref, kb_ref, vb_ref, s_ref, p_ref)

    @pl.when(i >= CTX_LEN // QB)
    def _latent_queries():
        j = i - CTX_LEN // QB
        band = jnp.clip(j - 1, 0, SEQ // QB - 3)
        start = pl.multiple_of(CTX_LEN + band * QB, QB)
        bias = bias_ref[j - band]
        _attend(qr, [ctx_keys, (pl.ds(start, 3 * QB), 3 * QB, bias)], sink_ref, out_ref, kb_ref, vb_ref,
                s_ref, p_ref)


def _gattn_kernel(q_ref, k_ref, v_ref, cos_ref, sin_ref, qw_ref, kw_ref, out_ref, kb_ref, vb_ref, s_ref, p_ref):
    i = pl.program_id(1)

    @pl.when(i == 0)
    def _prep():
        k = k_ref[...]
        kn = k * _head_inv_rms(k, N_KV) * kw_ref[...]
        kr = _rope(kn, cos_ref[...], sin_ref[...])
        v = v_ref[...]
        for kv in range(N_KV):
            kb_ref[kv] = kr[:, kv * HEAD_DIM:(kv + 1) * HEAD_DIM].astype(BF16)
            vb_ref[kv] = v[:, kv * HEAD_DIM:(kv + 1) * HEAD_DIM].astype(BF16)

    row0 = pl.multiple_of(i * QB, QB)
    cos = _tile4(cos_ref[pl.ds(row0, QB), :])
    sin = _tile4(sin_ref[pl.ds(row0, QB), :])
    q = q_ref[...]
    qn = q * _head_inv_rms(q, N_HEADS) * qw_ref[...]
    qr = _rope(qn, cos, sin) * (HEAD_DIM ** -0.5)

    @pl.when(i < CTX_LEN // QB)
    def _ctx_queries():
        _attend(qr, [(slice(0, CTX_LEN), CTX_LEN, None)], None, out_ref, kb_ref, vb_ref, s_ref, p_ref)

    @pl.when(i >= CTX_LEN // QB)
    def _latent_queries():
        _attend(qr, [(slice(0, TOK), TOK, None)], None, out_ref, kb_ref, vb_ref, s_ref, p_ref)


def _window_bias():
    q_pos = np.arange(QB)[None, :, None] + QB * np.arange(3)[:, None, None]
    k_pos = np.arange(3 * QB)[None, None, :]
    return jnp.asarray(np.where(np.abs(q_pos - k_pos) <= WINDOW, 0.0, -np.inf), F32)


def _attention(attn, cos, sin, sink_rows, qw, kw):
    n_batch = attn.shape[0]
    grid = (n_batch, TOK // QB)
    q_spec = lambda blk: pl.BlockSpec((None, QB, 512), lambda b, i: (b, i, blk))
    kv_spec = lambda blk: pl.BlockSpec((None, TOK, 128), lambda b, i: (b, 0, blk))
    tab_spec = pl.BlockSpec((TOK, 128), lambda b, i: (0, 0))
    out_spec = pl.BlockSpec((None, QB, 512), lambda b, i: (b, i, 0))
    out_shape = jax.ShapeDtypeStruct((n_batch, TOK, 512), BF16)
    kv_scratch = [pltpu.VMEM((N_KV, TOK, HEAD_DIM), BF16), pltpu.VMEM((N_KV, TOK, HEAD_DIM), BF16)]
    w_keys = CTX_LEN + 3 * QB
    ya = pl.pallas_call(
        _wattn_kernel, out_shape=out_shape, grid=grid,
        in_specs=[q_spec(0), kv_spec(8), kv_spec(9), tab_spec, tab_spec,
                  pl.BlockSpec((N_KV, GRP * QB, 1), lambda b, i: (0, 0, 0)),
                  pl.BlockSpec((3, QB, 3 * QB), lambda b, i: (0, 0, 0))],
        out_specs=out_spec,
        scratch_shapes=kv_scratch + [pltpu.VMEM((N_KV, GRP * QB, w_keys), F32), pltpu.VMEM((N_KV, GRP * QB, w_keys), BF16)],
        compiler_params=_params(("arbitrary", "arbitrary")), name="window_attention",
    )(attn, attn, attn, cos, sin, sink_rows, _window_bias())
    yg = pl.pallas_call(
        _gattn_kernel, out_shape=out_shape, grid=grid,
        in_specs=[q_spec(1), kv_spec(10), kv_spec(11), tab_spec, tab_spec,
                  pl.BlockSpec((1, 512), lambda b, i: (0, 0)),
                  pl.BlockSpec((1, 128), lambda b, i: (0, 0))],
        out_specs=out_spec,
        scratch_shapes=kv_scratch + [pltpu.VMEM((N_KV, GRP * QB, TOK), F32), pltpu.VMEM((N_KV, GRP * QB, TOK), BF16)],
        compiler_params=_params(("arbitrary", "arbitrary")), name="global_attention",
    )(attn, attn, attn, cos, sin, qw, kw)
    return ya, yg


def _ssd_chunk(s):
    r = s - N_CHUNK
    back = jnp.where(r == 0, 1, jnp.where(r == 1, 0, N_CHUNK + 1 - r))
    return jnp.where(s < N_CHUNK, s, back)


def _ssd_kernel(x_ref, prev_ref, next_ref, z_ref, dt_ref, cw_ref, cb_ref, dtb_ref, a_ref, d_ref, nw_ref, e_ref,
                out_ref, yf_ref, state_ref, y_ref):
    s = pl.program_id(1)
    chunk = _ssd_chunk(s)
    backward = s >= N_CHUNK

    @pl.when((s == 0) | (s == N_CHUNK))
    def _reset():
        state_ref[...] = jnp.zeros_like(state_ref)

    xin = x_ref[...]
    first = (chunk == 0) | (chunk == CTX_LEN // SSD_T)
    last = (chunk == CTX_LEN // SSD_T - 1) | (chunk == N_CHUNK - 1)
    prev = jnp.where(first, 0.0, prev_ref[7:8, :])
    nxt = jnp.where(last, 0.0, next_ref[0:1, :])
    row = lax.broadcasted_iota(jnp.int32, xin.shape, 0)
    xm1 = jnp.where(row == 0, prev, pltpu.roll(xin, 1, 0))
    xp1 = jnp.where(row == SSD_T - 1, nxt, pltpu.roll(xin, SSD_T - 1, 0))
    conv = cw_ref[0:1, :] * xm1 + cw_ref[1:2, :] * xin + cw_ref[2:3, :] * xp1 + cb_ref[...]
    xbc = _silu(conv)
    x = xbc[:, 0:SSD_WIDTH]
    bmat = [xbc[:, SSD_WIDTH + g * SSD_N:SSD_WIDTH + (g + 1) * SSD_N] for g in range(SSD_GROUPS)]
    cmat = [xbc[:, SSD_WIDTH + (SSD_GROUPS + g) * SSD_N:SSD_WIDTH + (SSD_GROUPS + g + 1) * SSD_N]
            for g in range(SSD_GROUPS)]

    dtv = dt_ref[...] + dtb_ref[...]
    dt = jnp.maximum(dtv, 0.0) + jnp.log(1.0 + jnp.exp(-jnp.abs(dtv)))
    da = dt * (-jnp.exp(a_ref[...]))
    tr = lax.broadcasted_iota(jnp.int32, (SSD_T, SSD_T), 0)
    tc = lax.broadcasted_iota(jnp.int32, (SSD_T, SSD_T), 1)
    causal = jnp.where(backward, tc - tr, tr - tc) >= 0
    tri = jnp.where(causal, 1.0, 0.0).astype(BF16)
    hi, mid, lo = _split3(da)
    cs = (jnp.dot(tri, hi, preferred_element_type=F32) + jnp.dot(tri, mid, preferred_element_type=F32)
          + jnp.dot(tri, lo, preferred_element_type=F32))
    cs_t = cs.T

    expand = e_ref[...]

    def widen(v):
        return sum(jnp.dot(p, expand, preferred_element_type=F32) for p in _split3(v))

    dt_x = widen(dt)
    cs_x = widen(cs)
    tot_x = jnp.where(backward, cs_x[0:1, :], cs_x[SSD_T - 1:SSD_T, :])
    ecs_x = jnp.exp(cs_x)
    etot_x = jnp.exp(tot_x)
    xd = x * dt_x
    xd_b = xd.astype(BF16)
    xd_end = (xd * jnp.exp(tot_x - cs_x)).astype(BF16)

    gw = SSD_WIDTH // SSD_GROUPS
    for g in range(SSD_GROUPS):
        glanes = slice(g * gw, (g + 1) * gw)
        cb = _bdot_nt(cmat[g], bmat[g])
        st = state_ref[g]
        y_off = ecs_x[:, glanes] * jnp.dot(cmat[g].astype(BF16), st.astype(BF16), preferred_element_type=F32)
        state_ref[g] = etot_x[:, glanes] * st + jnp.dot(bmat[g].T.astype(BF16), xd_end[:, glanes],
                                                        preferred_element_type=F32)
        for j in range(SSD_HEADS // SSD_GROUPS):
            h = g * (SSD_HEADS // SSD_GROUPS) + j
            lanes = slice(h * SSD_P, (h + 1) * SSD_P)
            seg = jnp.exp(jnp.where(causal, cs[:, h:h + 1] - cs_t[h:h + 1, :], -jnp.inf))
            y_ref[:, lanes] = (jnp.dot((cb * seg).astype(BF16), xd_b[:, lanes], preferred_element_type=F32)
                               + y_off[:, j * SSD_P:(j + 1) * SSD_P])

    row0 = pl.multiple_of(chunk * SSD_T, SSD_T)

    @pl.when(jnp.logical_not(backward))
    def _keep():
        yf_ref[pl.ds(row0, SSD_T), :] = y_ref[...]

    @pl.when(backward)
    def _finish():
        ytot = yf_ref[pl.ds(row0, SSD_T), :] + y_ref[...] + d_ref[...] * x
        gated = ytot * _silu(z_ref[...])
        ms = jnp.mean(gated * gated, axis=1, keepdims=True)
        out_ref[...] = (gated * lax.rsqrt(ms + NORM_EPS) * nw_ref[...]).astype(out_ref.dtype)


def _ssd(conv_in, sz, dt, conv_w, conv_b, dt_bias, a_log, d_exp, norm_w):
    n_batch = conv_in.shape[0]
    halo = SSD_T // 8
    n_halo = TOK // 8

    def chunk_map(b, s):
        return (b, _ssd_chunk(s), 0)

    def out_map(b, s):
        return (b, jnp.where(s < N_CHUNK, 1, _ssd_chunk(s)), 0)

    const = lambda shape: pl.BlockSpec(shape, lambda b, s: tuple(0 for _ in shape))
    dir_spec = pl.BlockSpec((None, 1, 128), lambda b, s: (s // N_CHUNK, 0, 0))
    expand = jnp.asarray(np.repeat(np.eye(128, SSD_HEADS), SSD_P, axis=1), BF16)
    return pl.pallas_call(
        _ssd_kernel,
        out_shape=jax.ShapeDtypeStruct((n_batch, TOK, SSD_WIDTH), BF16),
        grid=(n_batch, 2 * N_CHUNK),
        in_specs=[
            pl.BlockSpec((None, SSD_T, CONV_W), chunk_map),
            pl.BlockSpec((None, 8, CONV_W), lambda b, s: (b, jnp.maximum(_ssd_chunk(s) * halo - 1, 0), 0)),
            pl.BlockSpec((None, 8, CONV_W), lambda b, s: (b, jnp.minimum((_ssd_chunk(s) + 1) * halo, n_halo - 1), 0)),
            pl.BlockSpec((None, SSD_T, SSD_WIDTH), chunk_map),
            pl.BlockSpec((None, SSD_T, 128), lambda b, s: (b, _ssd_chunk(s), s // N_CHUNK)),
            const((3, CONV_W)), const((1, CONV_W)), dir_spec, dir_spec,
            const((1, SSD_WIDTH)), const((1, SSD_WIDTH)), const((128, SSD_WIDTH)),
        ],
        out_specs=pl.BlockSpec((None, SSD_T, SSD_WIDTH), out_map),
        scratch_shapes=[pltpu.VMEM((TOK, SSD_WIDTH), F32),
                        pltpu.VMEM((SSD_GROUPS, SSD_N, SSD_WIDTH // SSD_GROUPS), F32),
                        pltpu.VMEM((SSD_T, SSD_WIDTH), F32)],
        compiler_params=_params(("arbitrary", "arbitrary")),
        name="ssd",
    )(conv_in, conv_in, conv_in, sz, dt, conv_w, conv_b, dt_bias, a_log, d_exp, norm_w, expand)


S5_SW = 4 * 128


def _s5_kernel(u_ref, m_ref, h_ref, g_ref, a_ref, y_ref, loc_ref, prev_ref, *, n_batch):
    u = u_ref[...]
    loc_ref[...] = jnp.dot(u, h_ref[...], preferred_element_type=F32)
    a = [a_ref[k:k + 1, :] for k in range(4)]

    def step(k, carry):
        c_f = k
        c_b = jnp.where(k < S5_CTX_CH, S5_CTX_CH - 1 - k, S5_NCH + S5_CTX_CH - 1 - k)
        new = []
        for d, c in enumerate((c_f, c_b)):
            rows = pl.ds(pl.multiple_of(c * n_batch, n_batch), n_batch)
            re_l, im_l = slice(2 * d * 128, (2 * d + 1) * 128), slice((2 * d + 1) * 128, (2 * d + 2) * 128)
            s_re, s_im = carry[2 * d], carry[2 * d + 1]
            prev_ref[rows, re_l] = s_re
            prev_ref[rows, im_l] = s_im
            a_re, a_im = a[2 * d], a[2 * d + 1]
            new += [a_re * s_re - a_im * s_im + loc_ref[rows, re_l], a_re * s_im + a_im * s_re + loc_ref[rows, im_l]]
        return tuple(new)

    zero = jnp.zeros((n_batch, 128), F32)
    lax.fori_loop(0, S5_NCH, step, (zero,) * 4, unroll=2)
    y = jnp.dot(u, m_ref[...], preferred_element_type=F32)
    y = y + jnp.dot(prev_ref[...].astype(BF16), g_ref[...], preferred_element_type=F32)
    y_ref[...] = y.astype(y_ref.dtype)


def _s5(u_g, mats, n_batch):
    m_all, h_all, g_all, a16 = mats
    rows = u_g.shape[1]
    grp = lambda shape: pl.BlockSpec((None,) + shape, lambda g: (g,) + tuple(0 for _ in shape))
    return pl.pallas_call(
        functools.partial(_s5_kernel, n_batch=n_batch),
        out_shape=jax.ShapeDtypeStruct((S5_GROUPS, rows, S5_ROWW), BF16),
        grid=(S5_GROUPS,),
        in_specs=[grp((rows, S5_ROWW)), grp((S5_ROWW, S5_ROWW)), grp((S5_ROWW, S5_SW)), grp((S5_SW, S5_ROWW)),
                  grp((4, 128))],
        out_specs=grp((rows, S5_ROWW)),
        scratch_shapes=[pltpu.VMEM((rows, S5_SW), F32), pltpu.VMEM((rows, S5_SW), F32)],
        compiler_params=_params(("arbitrary",)),
        name="s5",
    )(u_g, m_all, h_all, g_all, a16)


def _cmul(a, b):
    return a[0] * b[0] - a[1] * b[1], a[0] * b[1] + a[1] * b[0]


def _s5_matrices(a_re, a_im, log_dt, b_re, b_im, c_re, c_im):
    hp = lax.Precision.HIGHEST
    t = jnp.arange(S5_T + 1, dtype=F32)
    m_sum = 0.0
    h_all, g_all, a16_all = [], [], []
    c = (c_re.astype(F32), c_im.astype(F32))
    for direction in range(2):
        are = jnp.minimum(a_re[direction].astype(F32), -1e-4)
        aim = a_im[direction].astype(F32)
        dt = jnp.exp(log_dt[direction].astype(F32))[:, None]
        mag = jnp.exp(t[:, None, None] * (are * dt)[None])
        ang = t[:, None, None] * (aim * dt)[None]
        pw = (mag * jnp.cos(ang), mag * jnp.sin(ang))
        num = (pw[0][1] - 1.0, pw[1][1])
        den = are * are + aim * aim
        coef = ((num[0] * are + num[1] * aim) / den, (num[1] * are - num[0] * aim) / den)
        bbar = _cmul((coef[0][..., None], coef[1][..., None]), (b_re.astype(F32), b_im.astype(F32)))
        pb = _cmul((pw[0][:S5_T, :, :, None], pw[1][:S5_T, :, :, None]), (bbar[0][None], bbar[1][None]))
        taps = (jnp.einsum('gon,tgni->tgoi', c[0], pb[0], precision=hp)
                - jnp.einsum('gon,tgni->tgoi', c[1], pb[1], precision=hp))
        ti = jnp.arange(S5_T)
        lag = (ti[None, :] - ti[:, None]) if direction == 0 else (ti[:, None] - ti[None, :])
        k_full = taps[jnp.clip(lag, 0, S5_T - 1)]
        k_full = jnp.where((lag >= 0)[:, :, None, None, None], k_full, 0.0)
        m_sum = m_sum + k_full.transpose(2, 0, 4, 1, 3).reshape(S5_GROUPS, S5_ROWW, S5_ROWW)
        e_in = (S5_T - 1 - ti) if direction == 0 else ti
        hb = _cmul((pw[0][e_in][..., None], pw[1][e_in][..., None]), (bbar[0][None], bbar[1][None]))
        zpad = jnp.zeros_like(hb[0])
        h_mat = jnp.concatenate([hb[0], zpad, hb[1], zpad], axis=2)
        h_all.append(h_mat.transpose(1, 0, 3, 2).reshape(S5_GROUPS, S5_ROWW, 4 * S5_STATE))
        e_out = (ti + 1) if direction == 0 else (S5_T - ti)
        cp = _cmul((c[0][None], c[1][None]),
                   (pw[0][e_out][:, :, None, :], pw[1][e_out][:, :, None, :]))
        zpad = jnp.zeros_like(cp[0])
        g_mat = jnp.concatenate([cp[0], zpad, -cp[1], zpad], axis=3)
        g_all.append(g_mat.transpose(1, 3, 0, 2).reshape(S5_GROUPS, 4 * S5_STATE, S5_ROWW))
        zrow = jnp.zeros_like(pw[0][S5_T])
        a16_all += [jnp.concatenate([pw[0][S5_T], zrow], axis=1), jnp.concatenate([pw[1][S5_T], zrow], axis=1)]
    return (m_sum.astype(BF16), jnp.concatenate(h_all, axis=2).astype(BF16),
            jnp.concatenate(g_all, axis=1).astype(BF16), jnp.stack(a16_all, axis=1))


def _route(logits_t, bias, base, before):
    scores = [_sigmoid(logits_t[j * N_EXPERT_GROUPS:(j + 1) * N_EXPERT_GROUPS]) for j in range(PER_GROUP)]
    sel = [scores[j] + bias[j * N_EXPERT_GROUPS:(j + 1) * N_EXPERT_GROUPS] for j in range(PER_GROUP)]
    hi1, lo1 = jnp.maximum(sel[0], sel[1]), jnp.minimum(sel[0], sel[1])
    hi2, lo2 = jnp.maximum(sel[2], sel[3]), jnp.minimum(sel[2], sel[3])
    group_score = jnp.maximum(hi1, hi2) + jnp.maximum(jnp.minimum(hi1, hi2), jnp.maximum(lo1, lo2))
    gid = lax.broadcasted_iota(jnp.int32, group_score.shape, 0)
    best = jnp.max(group_score, axis=0, keepdims=True)
    grp = jnp.min(jnp.where(group_score == best, gid, N_EXPERT_GROUPS), axis=0, keepdims=True)
    pick = gid == grp
    v = [jnp.sum(jnp.where(pick, sel[j], 0.0), axis=0, keepdims=True) for j in range(PER_GROUP)]
    sc = [jnp.sum(jnp.where(pick, scores[j], 0.0), axis=0, keepdims=True) for j in range(PER_GROUP)]
    rank = []
    for j in range(PER_GROUP):
        r = jnp.zeros_like(grp)
        for i in range(PER_GROUP):
            if i < j:
                r = r + jnp.where(v[i] >= v[j], 1, 0)
            elif i > j:
                r = r + jnp.where(v[i] > v[j], 1, 0)
        rank.append(r)
    e, w, locs = [], [], []
    for k in range(TOP_K):
        loc = sum(jnp.where(rank[j] == k, j, 0) for j in range(PER_GROUP))
        locs.append(loc)
        e.append((grp * PER_GROUP + loc).astype(F32))
        w.append(sum(jnp.where(rank[j] == k, sc[j], 0.0) for j in range(PER_GROUP)))
    wsum = w[0] + w[1]
    chosen = [jnp.where(pick & ((locs[0] == j) | (locs[1] == j)), 1.0, 0.0) for j in range(PER_GROUP)]
    onehot = jnp.concatenate(chosen, axis=0)
    seen = jnp.dot(onehot.astype(BF16), before, preferred_element_type=F32) + base
    pos = []
    for k in range(TOP_K):
        hit = sum(jnp.where(pick & (locs[k] == j), seen[j * N_EXPERT_GROUPS:(j + 1) * N_EXPERT_GROUPS], 0.0)
                  for j in range(PER_GROUP))
        pos.append(jnp.sum(hit, axis=0, keepdims=True))
    rows = [e[0], e[1], w[0] / wsum, w[1] / wsum, pos[0], pos[1]]
    route = jnp.concatenate(rows + [jnp.zeros_like(wsum)] * (8 - len(rows)), axis=0)
    return route, base + jnp.sum(onehot, axis=1, keepdims=True)


def _gelu_tanh(x):
    return 0.5 * x * (1.0 + jnp.tanh(math.sqrt(2.0 / math.pi) * (x + 0.044715 * (x * x * x))))


def _merge_kernel(x_ref, mod_ref, ya_ref, ys_ref, yg_ref, y5_ref, su_ref, s5d_ref, wg_ref, wb_ref, wglu_ref, wout_ref,
                  lng_ref, lnb_ref, rw_ref, rb_ref, before_ref, x1_ref, h2_ref, route_ref, count_ref, base_ref):
    @pl.when((pl.program_id(0) == 0) & (pl.program_id(1) == 0))
    def _reset():
        base_ref[...] = jnp.zeros_like(base_ref)

    x = x_ref[...]
    hb = (_ln(x) * (1.0 + _mod_part(mod_ref, 1)) + _mod_part(mod_ref, 0)).astype(BF16)
    y5_pre = y5_ref[...].astype(F32) + s5d_ref[...] * su_ref[...]
    glu = jnp.dot(_gelu_tanh(y5_pre).astype(BF16), wglu_ref[...], preferred_element_type=F32)
    y5 = (glu[:, 0:S5_WIDTH] * _sigmoid(glu[:, S5_WIDTH:2 * S5_WIDTH])).astype(BF16)
    acc = jnp.zeros(x.shape, F32)
    for n, y in enumerate((ya_ref[...], ys_ref[...], yg_ref[...], y5)):
        gate = _sigmoid(jnp.dot(hb, wg_ref[:, n * D_MODEL:(n + 1) * D_MODEL], preferred_element_type=F32))
        acc = acc + gate * jnp.dot(y, wb_ref[n], preferred_element_type=F32)
    mix = jnp.dot(acc.astype(BF16), wout_ref[...], preferred_element_type=F32)
    x1 = _ln(ALPHA * x + _mod_part(mod_ref, 2) * mix) * lng_ref[...] + lnb_ref[...]
    x1_ref[...] = x1
    h2 = (_ln(x1) * (1.0 + _mod_part(mod_ref, 4)) + _mod_part(mod_ref, 3)).astype(BF16)
    h2_ref[...] = h2
    route, base = _route(_bdot_nt(rw_ref[...], h2), rb_ref[...], base_ref[...], before_ref[...])
    route_ref[...] = route
    base_ref[...] = base
    count_ref[...] = jnp.broadcast_to(base, count_ref.shape)


def _merge(xcat, mods, ya, ys, yg, y5, su, s5_d, wg, wb, wglu, wout, ln_g, ln_b, rw_t, rb):
    n_batch = xcat.shape[0]
    n_tiles = TOK // TM
    tile = lambda w: pl.BlockSpec((None, TM, w), lambda b, j: (b, j, 0))
    const = lambda shape: pl.BlockSpec(shape, lambda b, j: tuple(0 for _ in shape))
    before = jnp.asarray(np.triu(np.ones((TM, TM)), 1), BF16)
    return pl.pallas_call(
        _merge_kernel,
        out_shape=[jax.ShapeDtypeStruct((n_batch, TOK, D_MODEL), F32),
                   jax.ShapeDtypeStruct((n_batch, TOK, D_MODEL), BF16),
                   jax.ShapeDtypeStruct((8, n_batch * TOK), F32),
                   jax.ShapeDtypeStruct((N_EXPERTS, 128), F32)],
        grid=(n_batch, n_tiles),
        in_specs=[tile(D_MODEL), pl.BlockSpec((None, 1, N_MOD * D_MODEL), _mod_row(n_batch)),
                  tile(512), tile(512), tile(512), tile(512), tile(512), const((1, S5_WIDTH)),
                  const((D_MODEL, N_BRANCH * D_MODEL)), const((N_BRANCH, 512, D_MODEL)),
                  const((S5_WIDTH, 2 * S5_WIDTH)), const((D_MODEL, D_MODEL)),
                  const((1, D_MODEL)), const((1, D_MODEL)),
                  const((N_EXPERTS, D_MODEL)), const((N_EXPERTS, 1)), const((TM, TM))],
        out_specs=[tile(D_MODEL), tile(D_MODEL),
                   pl.BlockSpec((8, TM), lambda b, j: (0, b * n_tiles + j)),
                   const((N_EXPERTS, 128))],
        scratch_shapes=[pltpu.VMEM((N_EXPERTS, 1), F32)],
        compiler_params=_params(("arbitrary", "arbitrary")),
        name="merge",
    )(xcat, mods, ya, ys, yg, y5, su, s5_d, wg, wb, wglu, wout, ln_g, ln_b, rw_t, rb, before)


def _moe_kernel(be_ref, x_ref, wgu_ref, wd_ref, y_ref):
    del be_ref
    gu = jnp.dot(x_ref[...], wgu_ref[...], preferred_element_type=F32)
    mid = _silu(gu[:, 0:D_EXPERT]) * gu[:, D_EXPERT:2 * D_EXPERT]
    y_ref[...] = jnp.dot(mid.astype(BF16), wd_ref[...], preferred_element_type=F32)


def _moe_experts(block_expert, xs, wgu, wd):
    n_blocks = block_expert.shape[0]
    return pl.pallas_call(
        _moe_kernel,
        out_shape=jax.ShapeDtypeStruct((n_blocks * MOE_ROWS, D_MODEL), F32),
        grid_spec=pltpu.PrefetchScalarGridSpec(
            num_scalar_prefetch=1, grid=(n_blocks,),
            in_specs=[pl.BlockSpec((MOE_ROWS, D_MODEL), lambda i, be: (i, 0)),
                      pl.BlockSpec((None, D_MODEL, 2 * D_EXPERT), lambda i, be: (be[i], 0, 0)),
                      pl.BlockSpec((None, D_EXPERT, D_MODEL), lambda i, be: (be[i], 0, 0))],
            out_specs=pl.BlockSpec((MOE_ROWS, D_MODEL), lambda i, be: (i, 0))),
        compiler_params=_params(("arbitrary",)),
        name="moe_experts",
    )(block_expert, xs, wgu, wd)


def _final_kernel(x_ref, mod_ref, f0_ref, f1_ref, w0_ref, w1_ref, lng_ref, lnb_ref, o_ref):
    ffn = w0_ref[...] * f0_ref[...] + w1_ref[...] * f1_ref[...]
    o_ref[...] = _ln(ALPHA * x_ref[...] + _mod_part(mod_ref, 5) * ffn) * lng_ref[...] + lnb_ref[...]


def _final(x1, mods, f0, f1, w0, w1, ln_g, ln_b):
    n_batch = x1.shape[0]
    tile = pl.BlockSpec((None, TM, D_MODEL), lambda b, j: (b, j, 0))
    col = pl.BlockSpec((None, TM, 1), lambda b, j: (b, j, 0))
    const = pl.BlockSpec((1, D_MODEL), lambda b, j: (0, 0))
    return pl.pallas_call(
        _final_kernel,
        out_shape=jax.ShapeDtypeStruct(x1.shape, F32),
        grid=(n_batch, TOK // TM),
        in_specs=[tile, pl.BlockSpec((None, 1, N_MOD * D_MODEL), _mod_row(n_batch)), tile, tile, col, col,
                  const, const],
        out_specs=tile,
        compiler_params=_params(("arbitrary", "arbitrary")),
        name="ffn_residual",
    )(x1, mods, f0, f1, w0, w1, ln_g, ln_b)


def _dispatch(route, counts_rows):
    n_tok = route.shape[1]
    n_blocks = -(-(n_tok * TOP_K + N_EXPERTS * (MOE_ROWS - 1)) // MOE_ROWS)
    cap = n_blocks * MOE_ROWS
    counts = counts_rows[:, 0].astype(jnp.int32).reshape(PER_GROUP, N_EXPERT_GROUPS).T.reshape(N_EXPERTS)
    padded = (counts + MOE_ROWS - 1) // MOE_ROWS * MOE_ROWS
    pad_end = jnp.cumsum(padded)
    pad_start = pad_end - padded
    first = jnp.cumsum(counts) - counts
    expert = route[0:TOP_K].astype(jnp.int32)
    pos = pad_start[expert] + route[2 * TOP_K:3 * TOP_K].astype(jnp.int32)
    tok = jnp.broadcast_to(jnp.arange(n_tok, dtype=jnp.int32), (TOP_K, n_tok))
    _, tok_sorted = lax.sort((pos.reshape(-1), tok.reshape(-1)), num_keys=1)
    block_expert = jnp.minimum(jnp.searchsorted(pad_end, jnp.arange(n_blocks) * MOE_ROWS, side='right'),
                               N_EXPERTS - 1).astype(jnp.int32)
    slot = jnp.arange(cap, dtype=jnp.int32)
    slot_e = jnp.repeat(block_expert, MOE_ROWS)
    local = slot - pad_start[slot_e]
    src = jnp.clip(first[slot_e] + local, 0, n_tok * TOP_K - 1)
    slot_tok = jnp.where(local < counts[slot_e], tok_sorted[src], 0)
    return slot_tok, pos, block_expert


def _col(w, part):
    return w[:, _IN_OFF[part]:_IN_OFF[part + 1]]


def _proj_weights(w_in):
    zeros = jnp.zeros((D_MODEL, 128 - SSD_HEADS), w_in.dtype)
    dt = _col(w_in, _SDT)
    parts = [_col(w_in, p) for p in (_AQ, _GQ, _AK, _AV, _GK, _GV, _SX, _SB, _SC, _SZ, _SU)]
    parts += [dt[:, 0:SSD_HEADS], zeros, dt[:, SSD_HEADS:2 * SSD_HEADS], zeros]
    return jnp.concatenate(parts, axis=1).astype(BF16), _col(w_in, _GATE).astype(BF16)


def _rope_tables():
    rows = SEQ // GRID_W
    row = jnp.repeat(jnp.arange(rows, dtype=F32), GRID_W)
    col = jnp.tile(jnp.arange(GRID_W, dtype=F32), rows)
    axis_dim = HEAD_DIM // 2
    inv_freq = ROPE_THETA ** (-jnp.arange(0, axis_dim, 2, dtype=F32) / axis_dim)
    ang_r = row[:, None] * inv_freq
    ang_c = col[:, None] * inv_freq
    ang = jnp.concatenate([ang_r, ang_r, ang_c, ang_c], axis=-1)
    cos = jnp.concatenate([jnp.ones((CTX_LEN, HEAD_DIM), F32), jnp.cos(ang)], axis=0)
    sin = jnp.concatenate([jnp.zeros((CTX_LEN, HEAD_DIM), F32), jnp.sin(ang)], axis=0)
    return jnp.tile(cos, (1, 2)), jnp.tile(sin, (1, 2))


def _pad_lanes(v, width=128):
    return jnp.pad(v, ((0, 0), (0, width - v.shape[-1])))


def kernel(x, c, ctx, c_ctx, mod_w, mod_b, w_in, wa_sink, ga_q_norm, ga_k_norm, ssd_conv_w, ssd_conv_b, ssd_dt_bias, ssd_a_log, ssd_d, ssd_norm_w, s5_a_re, s5_a_im, s5_log_dt, s5_b_re, s5_b_im, s5_c_re, s5_c_im, s5_d, s5_w_glu, w_branch, w_out, ln1_g, ln1_b, ln2_g, ln2_b, router_w, router_bias, moe_w_gate, moe_w_up, moe_w_down):
    n_batch = x.shape[0]
    n_tok = n_batch * TOK
    xcat = jnp.concatenate([ctx, x], axis=1)
    mod_rows = -(-(n_batch + 1) // 8) * 8
    cond = jnp.zeros((mod_rows, D_MODEL), F32).at[:n_batch].set(c).at[n_batch].set(c_ctx)
    mods_all = _modulation(cond, mod_w, mod_b)
    cos, sin = _rope_tables()

    perm = np.array([g * PER_GROUP + j for j in range(PER_GROUP) for g in range(N_EXPERT_GROUPS)])
    rw_t = router_w.T[perm].astype(BF16)
    rb = router_bias.astype(F32)[perm].reshape(N_EXPERTS, 1)

    for layer in range(DEPTH):
        mods = mods_all[layer].reshape(mod_rows, 1, N_MOD * D_MODEL)
        w1, wg = _proj_weights(w_in[layer])
        attn, conv_in, sz, su, dt = _inproj(xcat, mods, w1)

        sink_rows = jnp.repeat(wa_sink[layer].astype(F32).reshape(N_KV, GRP), QB, axis=1).reshape(N_KV, GRP * QB, 1)
        qw = jnp.tile(ga_q_norm[layer].astype(F32), N_HEADS).reshape(1, 512)
        kw = jnp.tile(ga_k_norm[layer].astype(F32), N_KV).reshape(1, 128)
        ya, yg = _attention(attn, cos, sin, sink_rows, qw, kw)

        dt_bias = _pad_lanes(ssd_dt_bias[layer].astype(F32).reshape(2, SSD_HEADS)).reshape(2, 1, 128)
        a_log = _pad_lanes(ssd_a_log[layer].astype(F32)).reshape(2, 1, 128)
        d_exp = jnp.repeat(ssd_d[layer].astype(F32), SSD_P).reshape(1, SSD_WIDTH)
        ys = _ssd(conv_in, sz, dt, ssd_conv_w[layer].astype(F32), ssd_conv_b[layer].astype(F32).reshape(1, CONV_W),
                  dt_bias, a_log, d_exp, ssd_norm_w[layer].astype(F32).reshape(1, SSD_WIDTH))

        mats = _s5_matrices(s5_a_re[layer], s5_a_im[layer], s5_log_dt[layer], s5_b_re[layer], s5_b_im[layer],
                            s5_c_re[layer], s5_c_im[layer])
        u_g = su.astype(BF16).reshape(n_batch, S5_NCH, S5_T, S5_GROUPS, S5_GROUP).transpose(3, 1, 0, 2, 4)
        y5_g = _s5(u_g.reshape(S5_GROUPS, S5_NCH * n_batch, S5_ROWW), mats, n_batch)
        y5 = y5_g.reshape(S5_GROUPS, S5_NCH, n_batch, S5_T, S5_GROUP).transpose(2, 1, 3, 0, 4)
        y5 = y5.reshape(n_batch, TOK, S5_WIDTH)

        x1, h2, route, counts = _merge(
            xcat, mods, ya, ys, yg, y5, su, s5_d[layer].astype(F32).reshape(1, S5_WIDTH), wg,
            w_branch[layer].astype(BF16), s5_w_glu[layer].astype(BF16), w_out[layer].astype(BF16),
            ln1_g[layer].reshape(1, D_MODEL), ln1_b[layer].reshape(1, D_MODEL), rw_t, rb)

        slot_tok, pos, block_expert = _dispatch(route, counts)
        xs = h2.reshape(n_tok, D_MODEL)[slot_tok]
        wgu = jnp.concatenate([moe_w_gate[layer], moe_w_up[layer]], axis=-1).astype(BF16)
        y_slots = _moe_experts(block_expert, xs, wgu, moe_w_down[layer].astype(BF16))
        f0 = y_slots[pos[0]].reshape(n_batch, TOK, D_MODEL)
        f1 = y_slots[pos[1]].reshape(n_batch, TOK, D_MODEL)
        w0 = route[TOP_K].reshape(n_batch, TOK, 1)
        w1 = route[TOP_K + 1].reshape(n_batch, TOK, 1)
        xcat = _final(x1, mods, f0, f1, w0, w1, ln2_g[layer].reshape(1, D_MODEL), ln2_b[layer].reshape(1, D_MODEL))
    return xcat[:, CTX_LEN:, :]
```

```python
import functools
import math

import jax
import jax.numpy as jnp
import numpy as np
from jax import lax
from jax.experimental import pallas as pl
from jax.experimental.pallas import tpu as pltpu

F32 = jnp.float32
BF16 = jnp.bfloat16

D_MODEL = 1024
DEPTH = 4
GRID_W = 64
CTX_LEN = 256
SEQ = 2048
TOK = CTX_LEN + SEQ
N_MOD = 6
HEAD_DIM = 64
N_BRANCH = 4
N_HEADS = 8
N_KV = 2
GRP = N_HEADS // N_KV
WINDOW = 128
QB = 128
ROPE_THETA = 10000.0

SSD_HEADS = 8
SSD_P = 64
SSD_GROUPS = 2
SSD_N = 128
SSD_T = 128
SSD_WIDTH = SSD_HEADS * SSD_P
SSD_CONV_CH = SSD_WIDTH + 2 * SSD_GROUPS * SSD_N
N_CHUNK = TOK // SSD_T

S5_GROUP = 16
S5_WIDTH = 512
S5_GROUPS = S5_WIDTH // S5_GROUP
S5_STATE = 64
S5_T = 16
S5_ROWW = S5_T * S5_GROUP
S5_NCH = TOK // S5_T
S5_CTX_CH = CTX_LEN // S5_T

N_EXPERTS = 32
N_EXPERT_GROUPS = 8
PER_GROUP = N_EXPERTS // N_EXPERT_GROUPS
TOP_K = 2
D_EXPERT = 512
MOE_ROWS = 256

ALPHA = (2 * DEPTH) ** 0.25
NORM_EPS = 1e-6

TM = 256
VMEM_LIMIT = 56 * 1024 * 1024

_IN_SIZES = (512, 128, 128, 512, 128, 128, 512, 512, 256, 256, 16, 512, 4096)
_IN_OFF = np.concatenate([[0], np.cumsum(_IN_SIZES)]).astype(int)
(_AQ, _AK, _AV, _GQ, _GK, _GV, _SX, _SZ, _SB, _SC, _SDT, _SU, _GATE) = range(13)

ATTN_W = 1536
CONV_W = 1024
DT_W = 256
PROJ_W = ATTN_W + CONV_W + 512 + 512 + DT_W


def _params(sem=None):
    return pltpu.CompilerParams(dimension_semantics=sem, vmem_limit_bytes=VMEM_LIMIT)


def _ln(x):
    mu = jnp.mean(x, axis=-1, keepdims=True)
    xc = x - mu
    var = jnp.mean(xc * xc, axis=-1, keepdims=True)
    return xc * lax.rsqrt(var + NORM_EPS)


def _sigmoid(x):
    return 1.0 / (1.0 + jnp.exp(-x))


def _silu(x):
    return x * _sigmoid(x)


def _bdot(a, b):
    return jnp.dot(a.astype(BF16), b.astype(BF16), preferred_element_type=F32)


def _bdot_nt(a, b):
    return lax.dot_general(a.astype(BF16), b.astype(BF16), (((1,), (1,)), ((), ())),
                           preferred_element_type=F32)


def _split3(v):
    hi = v.astype(BF16)
    r1 = v - hi.astype(F32)
    mid = r1.astype(BF16)
    lo = (r1 - mid.astype(F32)).astype(BF16)
    return hi, mid, lo


def _mod_kernel(c_ref, w_ref, b_ref, o_ref):
    c = c_ref[...]
    o_ref[...] = _bdot(_silu(c), w_ref[...]) + b_ref[...]


def _modulation(cond, mod_w, mod_b):
    n_layer = mod_w.shape[0]
    rows = cond.shape[0]
    return pl.pallas_call(
        _mod_kernel,
        out_shape=jax.ShapeDtypeStruct((n_layer, rows, N_MOD * D_MODEL), F32),
        grid=(n_layer, N_MOD),
        in_specs=[
            pl.BlockSpec((rows, D_MODEL), lambda l, n: (0, 0)),
            pl.BlockSpec((None, D_MODEL, D_MODEL), lambda l, n: (l, 0, n)),
            pl.BlockSpec((None, 1, D_MODEL), lambda l, n: (l, 0, n)),
        ],
        out_specs=pl.BlockSpec((None, rows, D_MODEL), lambda l, n: (l, 0, n)),
        compiler_params=_params(("arbitrary", "arbitrary")),
        name="modulation",
    )(cond, mod_w, mod_b.reshape(n_layer, 1, N_MOD * D_MODEL))


def _mod_part(mod_ref, k):
    return mod_ref[:, k * D_MODEL:(k + 1) * D_MODEL]


def _mod_row(n_batch):
    return lambda b, j: (jnp.where(j == 0, n_batch, b), 0, 0)


def _chunk_swap():
    r = np.arange(TM)
    p = np.zeros((TM, TM), np.float32)
    p[r, (r % S5_T) * (TM // S5_T) + r // S5_T] = 1.0
    return jnp.asarray(p, BF16)


def _inproj_kernel(x_ref, mod_ref, w_ref, swap_ref, attn_ref, conv_ref, sz_ref, su_ref, dt_ref, u_ref):
    h = _ln(x_ref[...]) * (1.0 + _mod_part(mod_ref, 1)) + _mod_part(mod_ref, 0)
    hb = h.astype(BF16)
    off = 0
    for ref in (attn_ref, conv_ref, sz_ref, su_ref, dt_ref):
        width = ref.shape[-1]
        ref[...] = jnp.dot(hb, w_ref[:, off:off + width], preferred_element_type=F32)
        off += width
    by_step = jnp.dot(swap_ref[...], su_ref[...].astype(BF16), preferred_element_type=F32)
    n_ch = TM // S5_T
    for t in range(S5_T):
        rows = by_step[t * n_ch:(t + 1) * n_ch]
        for g in range(S5_GROUPS):
            u_ref[g, :, t * S5_GROUP:(t + 1) * S5_GROUP] = rows[:, g * S5_GROUP:(g + 1) * S5_GROUP].astype(BF16)


def _inproj(xcat, mods, w1):
    n_batch = xcat.shape[0]
    widths = (ATTN_W, CONV_W, 512, 512, DT_W)
    tile = lambda w: pl.BlockSpec((None, TM, w), lambda b, j: (b, j, 0))
    return pl.pallas_call(
        _inproj_kernel,
        out_shape=([jax.ShapeDtypeStruct((n_batch, TOK, w), F32) for w in widths]
                   + [jax.ShapeDtypeStruct((S5_GROUPS, S5_NCH, n_batch * S5_ROWW), BF16)]),
        grid=(n_batch, TOK // TM),
        in_specs=[
            tile(D_MODEL),
            pl.BlockSpec((None, 1, N_MOD * D_MODEL), _mod_row(n_batch)),
            pl.BlockSpec((D_MODEL, PROJ_W), lambda b, j: (0, 0)),
            pl.BlockSpec((TM, TM), lambda b, j: (0, 0)),
        ],
        out_specs=([tile(w) for w in widths]
                   + [pl.BlockSpec((S5_GROUPS, TM // S5_T, S5_ROWW), lambda b, j: (0, j, b))]),
        compiler_params=_params(("arbitrary", "arbitrary")),
        name="inproj",
    )(xcat, mods, w1, _chunk_swap())


def _rope(t, cos, sin):
    width = t.shape[-1]
    lane = lax.broadcasted_iota(jnp.int32, t.shape, t.ndim - 1)
    rot = jnp.where(lane % 32 < 16, -pltpu.roll(t, width - 16, t.ndim - 1), pltpu.roll(t, 16, t.ndim - 1))
    return t * cos + rot * sin


def _tile4(v):
    return jnp.concatenate([v, v, v, v], axis=1)


def _head_inv_rms(t, n_heads):
    t2 = t * t
    lane = lax.broadcasted_iota(jnp.int32, t.shape, 1)
    inv = jnp.zeros_like(t)
    for h in range(n_heads):
        ms = jnp.sum(t2[:, h * HEAD_DIM:(h + 1) * HEAD_DIM], axis=1, keepdims=True) * (1.0 / HEAD_DIM)
        inv = jnp.where(lane // HEAD_DIM == h, lax.rsqrt(ms + NORM_EPS), inv)
    return inv


def _stack_heads(qr, kv):
    parts = [qr[:, (kv * GRP + g) * HEAD_DIM:(kv * GRP + g + 1) * HEAD_DIM] for g in range(GRP)]
    return jnp.concatenate(parts, axis=0).astype(BF16)


def _attend(qr, parts, sink_ref, out_ref, kb_ref, vb_ref, s_ref, p_ref):
    half_rows = GRP * QB // 2
    halves = [slice(0, half_rows), slice(half_rows, 2 * half_rows)]
    n_tiles = sum(width for _, width, _ in parts) // 128
    for kv in range(N_KV):
        qs = _stack_heads(qr, kv)
        col = 0
        for key_rows, width, bias in parts:
            for hr in halves:
                s = _bdot_nt(qs[hr], kb_ref[kv, key_rows, :])
                if bias is not None:
                    s = s + jnp.concatenate([bias] * (half_rows // QB), axis=0)
                s_ref[kv, hr, col:col + width] = s
            col += width
    dens = []
    for kv in range(N_KV):
        for r in range(GRP):
            rows = slice(r * QB, (r + 1) * QB)
            mx = s_ref[kv, rows, 0:128]
            for t in range(1, n_tiles):
                mx = jnp.maximum(mx, s_ref[kv, rows, t * 128:(t + 1) * 128])
            m = jnp.max(mx, axis=1, keepdims=True)
            if sink_ref is not None:
                m = jnp.maximum(m, sink_ref[kv, rows, :])
            m_b = jnp.broadcast_to(m, (QB, 128))
            acc = jnp.zeros((QB, 128), F32)
            for t in range(n_tiles):
                p = jnp.exp(s_ref[kv, rows, t * 128:(t + 1) * 128] - m_b)
                acc = acc + p
                p_ref[kv, rows, t * 128:(t + 1) * 128] = p.astype(BF16)
            den = jnp.sum(acc, axis=1, keepdims=True)
            if sink_ref is not None:
                den = den + jnp.exp(sink_ref[kv, rows, :] - m)
            dens.append(den)
    for kv in range(N_KV):
        for hi, hr in enumerate(halves):
            o = None
            col = 0
            for key_rows, width, _ in parts:
                part = jnp.dot(p_ref[kv, hr, col:col + width], vb_ref[kv, key_rows, :], preferred_element_type=F32)
                o = part if o is None else o + part
                col += width
            for gl in range(half_rows // QB):
                g = hi * (half_rows // QB) + gl
                h = kv * GRP + g
                out_ref[:, h * HEAD_DIM:(h + 1) * HEAD_DIM] = (
                    o[gl * QB:(gl + 1) * QB] / dens[kv * GRP + g]).astype(out_ref.dtype)


def _wattn_kernel(q_ref, k_ref, v_ref, cos_ref, sin_ref, sink_ref, bias_ref, out_ref, kb_ref, vb_ref, s_ref, p_ref):
    i = pl.program_id(1)

    @pl.when(i == 0)
    def _prep():
        kr = _rope(k_ref[...], cos_ref[...], sin_ref[...])
        v = v_ref[...]
        for kv in range(N_KV):
            kb_ref[kv] = kr[:, kv * HEAD_DIM:(kv + 1) * HEAD_DIM].astype(BF16)
            vb_ref[kv] = v[:, kv * HEAD_DIM:(kv + 1) * HEAD_DIM].astype(BF16)

    row0 = pl.multiple_of(i * QB, QB)
    cos = _tile4(cos_ref[pl.ds(row0, QB), :])
    sin = _tile4(sin_ref[pl.ds(row0, QB), :])
    qr = _rope(q_ref[...], cos, sin) * (HEAD_DIM ** -0.5)

    ctx_keys = (slice(0, CTX_LEN), CTX_LEN, None)

    @pl.when(i < CTX_LEN // QB)
    def _ctx_queries():
        _attend(qr, [ctx_keys], sink_ref, out_ref, kb_ref, vb_ref, s_ref, p_ref)

    @pl.when(i >= CTX_LEN // QB)
    def _latent_queries():
        j = i - CTX_LEN // QB
        band = jnp.clip(j - 1, 0, SEQ // QB - 3)
        start = pl.multiple_of(CTX_LEN + band * QB, QB)
        bias = bias_ref[j - band]
        _attend(qr, [ctx_keys, (pl.ds(start, 3 * QB), 3 * QB, bias)], sink_ref, out_ref, kb_ref, vb_ref,
                s_ref, p_ref)


def _gattn_kernel(q_ref, k_ref, v_ref, cos_ref, sin_ref, qw_ref, kw_ref, out_ref, kb_ref, vb_ref, s_ref, p_ref):
    i = pl.program_id(1)

    @pl.when(i == 0)
    def _prep():
        k = k_ref[...]
        kn = k * _head_inv_rms(k, N_KV) * kw_ref[...]
        kr = _rope(kn, cos_ref[...], sin_ref[...])
        v = v_ref[...]
        for kv in range(N_KV):
            kb_ref[kv] = kr[:, kv * HEAD_DIM:(kv + 1) * HEAD_DIM].astype(BF16)
            vb_ref[kv] = v[:, kv * HEAD_DIM:(kv + 1) * HEAD_DIM].astype(BF16)

    row0 = pl.multiple_of(i * QB, QB)
    cos = _tile4(cos_ref[pl.ds(row0, QB), :])
    sin = _tile4(sin_ref[pl.ds(row0, QB), :])
    q = q_ref[...]
    qn = q * _head_inv_rms(q, N_HEADS) * qw_ref[...]
    qr = _rope(qn, cos, sin) * (HEAD_DIM ** -0.5)

    @pl.when(i < CTX_LEN // QB)
    def _ctx_queries():
        _attend(qr, [(slice(0, CTX_LEN), CTX_LEN, None)], None, out_ref, kb_ref, vb_ref, s_ref, p_ref)

    @pl.when(i >= CTX_LEN // QB)
    def _latent_queries():
        _attend(qr, [(slice(0, TOK), TOK, None)], None, out_ref, kb_ref, vb_ref, s_ref, p_ref)


def _window_bias():
    q_pos = np.arange(QB)[None, :, None] + QB * np.arange(3)[:, None, None]
    k_pos = np.arange(3 * QB)[None, None, :]
    return jnp.asarray(np.where(np.abs(q_pos - k_pos) <= WINDOW, 0.0, -np.inf), F32)


def _attention(attn, cos, sin, sink_rows, qw, kw):
    n_batch = attn.shape[0]
    grid = (n_batch, TOK // QB)
    q_spec = lambda blk: pl.BlockSpec((None, QB, 512), lambda b, i: (b, i, blk))
    kv_spec = lambda blk: pl.BlockSpec((None, TOK, 128), lambda b, i: (b, 0, blk))
    tab_spec = pl.BlockSpec((TOK, 128), lambda b, i: (0, 0))
    out_spec = pl.BlockSpec((None, QB, 512), lambda b, i: (b, i, 0))
    out_shape = jax.ShapeDtypeStruct((n_batch, TOK, 512), BF16)
    kv_scratch = [pltpu.VMEM((N_KV, TOK, HEAD_DIM), BF16), pltpu.VMEM((N_KV, TOK, HEAD_DIM), BF16)]
    w_keys = CTX_LEN + 3 * QB
    ya = pl.pallas_call(
        _wattn_kernel, out_shape=out_shape, grid=grid,
        in_specs=[q_spec(0), kv_spec(8), kv_spec(9), tab_spec, tab_spec,
                  pl.BlockSpec((N_KV, GRP * QB, 1), lambda b, i: (0, 0, 0)),
                  pl.BlockSpec((3, QB, 3 * QB), lambda b, i: (0, 0, 0))],
        out_specs=out_spec,
        scratch_shapes=kv_scratch + [pltpu.VMEM((N_KV, GRP * QB, w_keys), F32), pltpu.VMEM((N_KV, GRP * QB, w_keys), BF16)],
        compiler_params=_params(("arbitrary", "arbitrary")), name="window_attention",
    )(attn, attn, attn, cos, sin, sink_rows, _window_bias())
    yg = pl.pallas_call(
        _gattn_kernel, out_shape=out_shape, grid=grid,
        in_specs=[q_spec(1), kv_spec(10), kv_spec(11), tab_spec, tab_spec,
                  pl.BlockSpec((1, 512), lambda b, i: (0, 0)),
                  pl.BlockSpec((1, 128), lambda b, i: (0, 0))],
        out_specs=out_spec,
        scratch_shapes=kv_scratch + [pltpu.VMEM((N_KV, GRP * QB, TOK), F32), pltpu.VMEM((N_KV, GRP * QB, TOK), BF16)],
        compiler_params=_params(("arbitrary", "arbitrary")), name="global_attention",
    )(attn, attn, attn, cos, sin, qw, kw)
    return ya, yg


def _ssd_chunk(s):
    r = s - N_CHUNK
    back = jnp.where(r == 0, 1, jnp.where(r == 1, 0, N_CHUNK + 1 - r))
    return jnp.where(s < N_CHUNK, s, back)


def _ssd_kernel(x_ref, prev_ref, next_ref, z_ref, dt_ref, cw_ref, cb_ref, dtb_ref, a_ref, d_ref, nw_ref, e_ref,
                out_ref, yf_ref, state_ref, y_ref):
    s = pl.program_id(1)
    chunk = _ssd_chunk(s)
    backward = s >= N_CHUNK

    @pl.when((s == 0) | (s == N_CHUNK))
    def _reset():
        state_ref[...] = jnp.zeros_like(state_ref)

    xin = x_ref[...]
    first = (chunk == 0) | (chunk == CTX_LEN // SSD_T)
    last = (chunk == CTX_LEN // SSD_T - 1) | (chunk == N_CHUNK - 1)
    prev = jnp.where(first, 0.0, prev_ref[7:8, :])
    nxt = jnp.where(last, 0.0, next_ref[0:1, :])
    row = lax.broadcasted_iota(jnp.int32, xin.shape, 0)
    xm1 = jnp.where(row == 0, prev, pltpu.roll(xin, 1, 0))
    xp1 = jnp.where(row == SSD_T - 1, nxt, pltpu.roll(xin, SSD_T - 1, 0))
    conv = cw_ref[0:1, :] * xm1 + cw_ref[1:2, :] * xin + cw_ref[2:3, :] * xp1 + cb_ref[...]
    xbc = _silu(conv)
    x = xbc[:, 0:SSD_WIDTH]
    bmat = [xbc[:, SSD_WIDTH + g * SSD_N:SSD_WIDTH + (g + 1) * SSD_N] for g in range(SSD_GROUPS)]
    cmat = [xbc[:, SSD_WIDTH + (SSD_GROUPS + g) * SSD_N:SSD_WIDTH + (SSD_GROUPS + g + 1) * SSD_N]
            for g in range(SSD_GROUPS)]

    dtv = dt_ref[...] + dtb_ref[...]
    dt = jnp.maximum(dtv, 0.0) + jnp.log(1.0 + jnp.exp(-jnp.abs(dtv)))
    da = dt * (-jnp.exp(a_ref[...]))
    tr = lax.broadcasted_iota(jnp.int32, (SSD_T, SSD_T), 0)
    tc = lax.broadcasted_iota(jnp.int32, (SSD_T, SSD_T), 1)
    causal = jnp.where(backward, tc - tr, tr - tc) >= 0
    tri = jnp.where(causal, 1.0, 0.0).astype(BF16)
    hi, mid, lo = _split3(da)
    cs = (jnp.dot(tri, hi, preferred_element_type=F32) + jnp.dot(tri, mid, preferred_element_type=F32)
          + jnp.dot(tri, lo, preferred_element_type=F32))
    cs_t = cs.T

    expand = e_ref[...]

    def widen(v):
        return sum(jnp.dot(p, expand, preferred_element_type=F32) for p in _split3(v))

    dt_x = widen(dt)
    cs_x = widen(cs)
    tot_x = jnp.where(backward, cs_x[0:1, :], cs_x[SSD_T - 1:SSD_T, :])
    ecs_x = jnp.exp(cs_x)
    etot_x = jnp.exp(tot_x)
    xd = x * dt_x
    xd_b = xd.astype(BF16)
    xd_end = (xd * jnp.exp(tot_x - cs_x)).astype(BF16)

    gw = SSD_WIDTH // SSD_GROUPS
    for g in range(SSD_GROUPS):
        glanes = slice(g * gw, (g + 1) * gw)
        cb = _bdot_nt(cmat[g], bmat[g])
        st = state_ref[g]
        y_off = ecs_x[:, glanes] * jnp.dot(cmat[g].astype(BF16), st.astype(BF16), preferred_element_type=F32)
        state_ref[g] = etot_x[:, glanes] * st + jnp.dot(bmat[g].T.astype(BF16), xd_end[:, glanes],
                                                        preferred_element_type=F32)
        for j in range(SSD_HEADS // SSD_GROUPS):
            h = g * (SSD_HEADS // SSD_GROUPS) + j
            lanes = slice(h * SSD_P, (h + 1) * SSD_P)
            seg = jnp.exp(jnp.where(causal, cs[:, h:h + 1] - cs_t[h:h + 1, :], -jnp.inf))
            y_ref[:, lanes] = (jnp.dot((cb * seg).astype(BF16), xd_b[:, lanes], preferred_element_type=F32)
                               + y_off[:, j * SSD_P:(j + 1) * SSD_P])

    row0 = pl.multiple_of(chunk * SSD_T, SSD_T)

    @pl.when(jnp.logical_not(backward))
    def _keep():
        yf_ref[pl.ds(row0, SSD_T), :] = y_ref[...]

    @pl.when(backward)
    def _finish():
        ytot = yf_ref[pl.ds(row0, SSD_T), :] + y_ref[...] + d_ref[...] * x
        gated = ytot * _silu(z_ref[...])
        ms = jnp.mean(gated * gated, axis=1, keepdims=True)
        out_ref[...] = (gated * lax.rsqrt(ms + NORM_EPS) * nw_ref[...]).astype(out_ref.dtype)


def _ssd(conv_in, sz, dt, conv_w, conv_b, dt_bias, a_log, d_exp, norm_w):
    n_batch = conv_in.shape[0]
    halo = SSD_T // 8
    n_halo = TOK // 8

    def chunk_map(b, s):
        return (b, _ssd_chunk(s), 0)

    def out_map(b, s):
        return (b, jnp.where(s < N_CHUNK, 1, _ssd_chunk(s)), 0)

    const = lambda shape: pl.BlockSpec(shape, lambda b, s: tuple(0 for _ in shape))
    dir_spec = pl.BlockSpec((None, 1, 128), lambda b, s: (s // N_CHUNK, 0, 0))
    expand = jnp.asarray(np.repeat(np.eye(128, SSD_HEADS), SSD_P, axis=1), BF16)
    return pl.pallas_call(
        _ssd_kernel,
        out_shape=jax.ShapeDtypeStruct((n_batch, TOK, SSD_WIDTH), BF16),
        grid=(n_batch, 2 * N_CHUNK),
        in_specs=[
            pl.BlockSpec((None, SSD_T, CONV_W), chunk_map),
            pl.BlockSpec((None, 8, CONV_W), lambda b, s: (b, jnp.maximum(_ssd_chunk(s) * halo - 1, 0), 0)),
            pl.BlockSpec((None, 8, CONV_W), lambda b, s: (b, jnp.minimum((_ssd_chunk(s) + 1) * halo, n_halo - 1), 0)),
            pl.BlockSpec((None, SSD_T, SSD_WIDTH), chunk_map),
            pl.BlockSpec((None, SSD_T, 128), lambda b, s: (b, _ssd_chunk(s), s // N_CHUNK)),
            const((3, CONV_W)), const((1, CONV_W)), dir_spec, dir_spec,
            const((1, SSD_WIDTH)), const((1, SSD_WIDTH)), const((128, SSD_WIDTH)),
        ],
        out_specs=pl.BlockSpec((None, SSD_T, SSD_WIDTH), out_map),
        scratch_shapes=[pltpu.VMEM((TOK, SSD_WIDTH), F32),
                        pltpu.VMEM((SSD_GROUPS, SSD_N, SSD_WIDTH // SSD_GROUPS), F32),
                        pltpu.VMEM((SSD_T, SSD_WIDTH), F32)],
        compiler_params=_params(("arbitrary", "arbitrary")),
        name="ssd",
    )(conv_in, conv_in, conv_in, sz, dt, conv_w, conv_b, dt_bias, a_log, d_exp, norm_w, expand)


S5_SW = 4 * 128


def _s5_kernel(u_ref, m_ref, h_ref, g_ref, a_ref, y_ref, loc_ref, prev_ref, *, n_batch):
    u = u_ref[...]
    loc_ref[...] = jnp.dot(u, h_ref[...], preferred_element_type=F32)
    a = [a_ref[k:k + 1, :] for k in range(4)]

    def step(k, carry):
        c_f = k
        c_b = jnp.where(k < S5_CTX_CH, S5_CTX_CH - 1 - k, S5_NCH + S5_CTX_CH - 1 - k)
        new = []
        for d, c in enumerate((c_f, c_b)):
            rows = pl.ds(pl.multiple_of(c * n_batch, n_batch), n_batch)
            re_l, im_l = slice(2 * d * 128, (2 * d + 1) * 128), slice((2 * d + 1) * 128, (2 * d + 2) * 128)
            s_re, s_im = carry[2 * d], carry[2 * d + 1]
            prev_ref[rows, re_l] = s_re
            prev_ref[rows, im_l] = s_im
            a_re, a_im = a[2 * d], a[2 * d + 1]
            new += [a_re * s_re - a_im * s_im + loc_ref[rows, re_l], a_re * s_im + a_im * s_re + loc_ref[rows, im_l]]
        return tuple(new)

    zero = jnp.zeros((n_batch, 128), F32)
    lax.fori_loop(0, S5_NCH, step, (zero,) * 4, unroll=2)
    y = jnp.dot(u, m_ref[...], preferred_element_type=F32)
    y = y + jnp.dot(prev_ref[...].astype(BF16), g_ref[...], preferred_element_type=F32)
    y_ref[...] = y.astype(y_ref.dtype)


def _s5(u_g, mats, n_batch):
    m_all, h_all, g_all, a16 = mats
    rows = u_g.shape[1]
    grp = lambda shape: pl.BlockSpec((None,) + shape, lambda g: (g,) + tuple(0 for _ in shape))
    return pl.pallas_call(
        functools.partial(_s5_kernel, n_batch=n_batch),
        out_shape=jax.ShapeDtypeStruct((S5_GROUPS, rows, S5_ROWW), BF16),
        grid=(S5_GROUPS,),
        in_specs=[grp((rows, S5_ROWW)), grp((S5_ROWW, S5_ROWW)), grp((S5_ROWW, S5_SW)), grp((S5_SW, S5_ROWW)),
                  grp((4, 128))],
        out_specs=grp((rows, S5_ROWW)),
        scratch_shapes=[pltpu.VMEM((rows, S5_SW), F32), pltpu.VMEM((rows, S5_SW), F32)],
        compiler_params=_params(("arbitrary",)),
        name="s5",
    )(u_g, m_all, h_all, g_all, a16)


def _cmul(a, b):
    return a[0] * b[0] - a[1] * b[1], a[0] * b[1] + a[1] * b[0]


def _s5_matrices(a_re, a_im, log_dt, b_re, b_im, c_re, c_im):
    hp = lax.Precision.HIGHEST
    t = jnp.arange(S5_T + 1, dtype=F32)
    m_sum = 0.0
    h_all, g_all, a16_all = [], [], []
    c = (c_re.astype(F32), c_im.astype(F32))
    for direction in range(2):
        are = jnp.minimum(a_re[direction].astype(F32), -1e-4)
        aim = a_im[direction].astype(F32)
        dt = jnp.exp(log_dt[direction].astype(F32))[:, None]
        mag = jnp.exp(t[:, None, None] * (are * dt)[None])
        ang = t[:, None, None] * (aim * dt)[None]
        pw = (mag * jnp.cos(ang), mag * jnp.sin(ang))
        num = (pw[0][1] - 1.0, pw[1][1])
        den = are * are + aim * aim
        coef = ((num[0] * are + num[1] * aim) / den, (num[1] * are - num[0] * aim) / den)
        bbar = _cmul((coef[0][..., None], coef[1][..., None]), (b_re.astype(F32), b_im.astype(F32)))
        pb = _cmul((pw[0][:S5_T, :, :, None], pw[1][:S5_T, :, :, None]), (bbar[0][None], bbar[1][None]))
        taps = (jnp.einsum('gon,tgni->tgoi', c[0], pb[0], precision=hp)
                - jnp.einsum('gon,tgni->tgoi', c[1], pb[1], precision=hp))
        ti = jnp.arange(S5_T)
        lag = (ti[None, :] - ti[:, None]) if direction == 0 else (ti[:, None] - ti[None, :])
        k_full = taps[jnp.clip(lag, 0, S5_T - 1)]
        k_full = jnp.where((lag >= 0)[:, :, None, None, None], k_full, 0.0)
        m_sum = m_sum + k_full.transpose(2, 0, 4, 1, 3).reshape(S5_GROUPS, S5_ROWW, S5_ROWW)
        e_in = (S5_T - 1 - ti) if direction == 0 else ti
        hb = _cmul((pw[0][e_in][..., None], pw[1][e_in][..., None]), (bbar[0][None], bbar[1][None]))
        zpad = jnp.zeros_like(hb[0])
        h_mat = jnp.concatenate([hb[0], zpad, hb[1], zpad], axis=2)
        h_all.append(h_mat.transpose(1, 0, 3, 2).reshape(S5_GROUPS, S5_ROWW, 4 * S5_STATE))
        e_out = (ti + 1) if direction == 0 else (S5_T - ti)
        cp = _cmul((c[0][None], c[1][None]),
                   (pw[0][e_out][:, :, None, :], pw[1][e_out][:, :, None, :]))
        zpad = jnp.zeros_like(cp[0])
        g_mat = jnp.concatenate([cp[0], zpad, -cp[1], zpad], axis=3)
        g_all.append(g_mat.transpose(1, 3, 0, 2).reshape(S5_GROUPS, 4 * S5_STATE, S5_ROWW))
        zrow = jnp.zeros_like(pw[0][S5_T])
        a16_all += [jnp.concatenate([pw[0][S5_T], zrow], axis=1), jnp.concatenate([pw[1][S5_T], zrow], axis=1)]
    return (m_sum.astype(BF16), jnp.concatenate(h_all, axis=2).astype(BF16),
            jnp.concatenate(g_all, axis=1).astype(BF16), jnp.stack(a16_all, axis=1))


def _route(logits_t, bias, base, before):
    scores = [_sigmoid(logits_t[j * N_EXPERT_GROUPS:(j + 1) * N_EXPERT_GROUPS]) for j in range(PER_GROUP)]
    sel = [scores[j] + bias[j * N_EXPERT_GROUPS:(j + 1) * N_EXPERT_GROUPS] for j in range(PER_GROUP)]
    hi1, lo1 = jnp.maximum(sel[0], sel[1]), jnp.minimum(sel[0], sel[1])
    hi2, lo2 = jnp.maximum(sel[2], sel[3]), jnp.minimum(sel[2], sel[3])
    group_score = jnp.maximum(hi1, hi2) + jnp.maximum(jnp.minimum(hi1, hi2), jnp.maximum(lo1, lo2))
    gid = lax.broadcasted_iota(jnp.int32, group_score.shape, 0)
    best = jnp.max(group_score, axis=0, keepdims=True)
    grp = jnp.min(jnp.where(group_score == best, gid, N_EXPERT_GROUPS), axis=0, keepdims=True)
    pick = gid == grp
    v = [jnp.sum(jnp.where(pick, sel[j], 0.0), axis=0, keepdims=True) for j in range(PER_GROUP)]
    sc = [jnp.sum(jnp.where(pick, scores[j], 0.0), axis=0, keepdims=True) for j in range(PER_GROUP)]
    rank = []
    for j in range(PER_GROUP):
        r = jnp.zeros_like(grp)
        for i in range(PER_GROUP):
            if i < j:
                r = r + jnp.where(v[i] >= v[j], 1, 0)
            elif i > j:
                r = r + jnp.where(v[i] > v[j], 1, 0)
        rank.append(r)
    e, w, locs = [], [], []
    for k in range(TOP_K):
        loc = sum(jnp.where(rank[j] == k, j, 0) for j in range(PER_GROUP))
        locs.append(loc)
        e.append((grp * PER_GROUP + loc).astype(F32))
        w.append(sum(jnp.where(rank[j] == k, sc[j], 0.0) for j in range(PER_GROUP)))
    wsum = w[0] + w[1]
    chosen = [jnp.where(pick & ((locs[0] == j) | (locs[1] == j)), 1.0, 0.0) for j in range(PER_GROUP)]
    onehot = jnp.concatenate(chosen, axis=0)
    seen = jnp.dot(onehot.astype(BF16), before, preferred_element_type=F32) + base
    pos = []
    for k in range(TOP_K):
        hit = sum(jnp.where(pick & (locs[k] == j), seen[j * N_EXPERT_GROUPS:(j + 1) * N_EXPERT_GROUPS], 0.0)
                  for j in range(PER_GROUP))
        pos.append(jnp.sum(hit, axis=0, keepdims=True))
    rows = [e[0], e[1], w[0] / wsum, w[1] / wsum, pos[0], pos[1]]
    route = jnp.concatenate(rows + [jnp.zeros_like(wsum)] * (8 - len(rows)), axis=0)
    return route, base + jnp.sum(onehot, axis=1, keepdims=True)


def _gelu_tanh(x):
    return 0.5 * x * (1.0 + jnp.tanh(math.sqrt(2.0 / math.pi) * (x + 0.044715 * (x * x * x))))


def _merge_kernel(x_ref, mod_ref, ya_ref, ys_ref, yg_ref, y5_ref, su_ref, s5d_ref, wg_ref, wb_ref, wglu_ref, wout_ref,
                  lng_ref, lnb_ref, rw_ref, rb_ref, before_ref, swap_ref, x1_ref, h2_ref, route_ref, count_ref,
                  base_ref, y5s_ref):
    @pl.when((pl.program_id(0) == 0) & (pl.program_id(1) == 0))
    def _reset():
        base_ref[...] = jnp.zeros_like(base_ref)

    x = x_ref[...]
    hb = (_ln(x) * (1.0 + _mod_part(mod_ref, 1)) + _mod_part(mod_ref, 0)).astype(BF16)
    n_ch = TM // S5_T
    for t in range(S5_T):
        for g in range(S5_GROUPS):
            y5s_ref[t * n_ch:(t + 1) * n_ch, g * S5_GROUP:(g + 1) * S5_GROUP] = (
                y5_ref[g, :, t * S5_GROUP:(t + 1) * S5_GROUP])
    y5_scan = jnp.dot(swap_ref[...], y5s_ref[...], preferred_element_type=F32)
    y5_pre = y5_scan + s5d_ref[...] * su_ref[...]
    glu = jnp.dot(_gelu_tanh(y5_pre).astype(BF16), wglu_ref[...], preferred_element_type=F32)
    y5 = (glu[:, 0:S5_WIDTH] * _sigmoid(glu[:, S5_WIDTH:2 * S5_WIDTH])).astype(BF16)
    acc = jnp.zeros(x.shape, F32)
    for n, y in enumerate((ya_ref[...], ys_ref[...], yg_ref[...], y5)):
        gate = _sigmoid(jnp.dot(hb, wg_ref[:, n * D_MODEL:(n + 1) * D_MODEL], preferred_element_type=F32))
        acc = acc + gate * jnp.dot(y, wb_ref[n], preferred_element_type=F32)
    mix = jnp.dot(acc.astype(BF16), wout_ref[...], preferred_element_type=F32)
    x1 = _ln(ALPHA * x + _mod_part(mod_ref, 2) * mix) * lng_ref[...] + lnb_ref[...]
    x1_ref[...] = x1
    h2 = (_ln(x1) * (1.0 + _mod_part(mod_ref, 4)) + _mod_part(mod_ref, 3)).astype(BF16)
    h2_ref[...] = h2
    route, base = _route(_bdot_nt(rw_ref[...], h2), rb_ref[...], base_ref[...], before_ref[...])
    route_ref[...] = route
    base_ref[...] = base
    count_ref[...] = jnp.broadcast_to(base, count_ref.shape)


def _merge(xcat, mods, ya, ys, yg, y5, su, s5_d, wg, wb, wglu, wout, ln_g, ln_b, rw_t, rb):
    n_batch = xcat.shape[0]
    n_tiles = TOK // TM
    tile = lambda w: pl.BlockSpec((None, TM, w), lambda b, j: (b, j, 0))
    const = lambda shape: pl.BlockSpec(shape, lambda b, j: tuple(0 for _ in shape))
    before = jnp.asarray(np.triu(np.ones((TM, TM)), 1), BF16)
    return pl.pallas_call(
        _merge_kernel,
        out_shape=[jax.ShapeDtypeStruct((n_batch, TOK, D_MODEL), F32),
                   jax.ShapeDtypeStruct((n_batch, TOK, D_MODEL), BF16),
                   jax.ShapeDtypeStruct((8, n_batch * TOK), F32),
                   jax.ShapeDtypeStruct((N_EXPERTS, 128), F32)],
        grid=(n_batch, n_tiles),
        in_specs=[tile(D_MODEL), pl.BlockSpec((None, 1, N_MOD * D_MODEL), _mod_row(n_batch)),
                  tile(512), tile(512), tile(512),
                  pl.BlockSpec((S5_GROUPS, TM // S5_T, S5_ROWW), lambda b, j: (0, j, b)),
                  tile(512), const((1, S5_WIDTH)),
                  const((D_MODEL, N_BRANCH * D_MODEL)), const((N_BRANCH, 512, D_MODEL)),
                  const((S5_WIDTH, 2 * S5_WIDTH)), const((D_MODEL, D_MODEL)),
                  const((1, D_MODEL)), const((1, D_MODEL)),
                  const((N_EXPERTS, D_MODEL)), const((N_EXPERTS, 1)), const((TM, TM)), const((TM, TM))],
        out_specs=[tile(D_MODEL), tile(D_MODEL),
                   pl.BlockSpec((8, TM), lambda b, j: (0, b * n_tiles + j)),
                   const((N_EXPERTS, 128))],
        scratch_shapes=[pltpu.VMEM((N_EXPERTS, 1), F32), pltpu.VMEM((TM, S5_WIDTH), BF16)],
        compiler_params=_params(("arbitrary", "arbitrary")),
        name="merge",
    )(xcat, mods, ya, ys, yg, y5, su, s5_d, wg, wb, wglu, wout, ln_g, ln_b, rw_t, rb, before, _chunk_swap())


def _moe_kernel(be_ref, nu_ref, x_ref, wg_ref, wu_ref, wd_ref, y_ref, wgu_s, wd_s):
    i = pl.program_id(0)
    used = i < nu_ref[0]

    @pl.when(used & ((i == 0) | (be_ref[i] != be_ref[jnp.maximum(i - 1, 0)])))
    def _cast():
        wgu_s[:, 0:D_EXPERT] = wg_ref[...].astype(BF16)
        wgu_s[:, D_EXPERT:2 * D_EXPERT] = wu_ref[...].astype(BF16)
        wd_s[...] = wd_ref[...].astype(BF16)

    @pl.when(used)
    def _block():
        gu = jnp.dot(x_ref[...], wgu_s[...], preferred_element_type=F32)
        mid = _silu(gu[:, 0:D_EXPERT]) * gu[:, D_EXPERT:2 * D_EXPERT]
        y_ref[...] = jnp.dot(mid.astype(BF16), wd_s[...], preferred_element_type=F32).astype(y_ref.dtype)


def _moe_experts(block_expert, n_used, xs, w_gate, w_up, w_down):
    n_blocks = block_expert.shape[0]
    return pl.pallas_call(
        _moe_kernel,
        out_shape=jax.ShapeDtypeStruct((n_blocks * MOE_ROWS, D_MODEL), BF16),
        grid_spec=pltpu.PrefetchScalarGridSpec(
            num_scalar_prefetch=2, grid=(n_blocks,),
            in_specs=[pl.BlockSpec((MOE_ROWS, D_MODEL), lambda i, be, nu: (i, 0)),
                      pl.BlockSpec((None, D_MODEL, D_EXPERT), lambda i, be, nu: (be[i], 0, 0)),
                      pl.BlockSpec((None, D_MODEL, D_EXPERT), lambda i, be, nu: (be[i], 0, 0)),
                      pl.BlockSpec((None, D_EXPERT, D_MODEL), lambda i, be, nu: (be[i], 0, 0))],
            out_specs=pl.BlockSpec((MOE_ROWS, D_MODEL), lambda i, be, nu: (i, 0)),
            scratch_shapes=[pltpu.VMEM((D_MODEL, 2 * D_EXPERT), BF16), pltpu.VMEM((D_EXPERT, D_MODEL), BF16)]),
        compiler_params=_params(("arbitrary",)),
        name="moe_experts",
    )(block_expert, n_used, xs, w_gate, w_up, w_down)


def _final_kernel(x_ref, mod_ref, f0_ref, f1_ref, route_ref, lng_ref, lnb_ref, o_ref):
    r = lax.broadcasted_iota(jnp.int32, (TM, TM), 0)
    c = lax.broadcasted_iota(jnp.int32, (TM, TM), 1)
    w = [jnp.sum(jnp.where(r == c, route_ref[TOP_K + k:TOP_K + k + 1, :], 0.0), axis=1, keepdims=True)
         for k in range(TOP_K)]
    ffn = w[0] * f0_ref[...].astype(F32) + w[1] * f1_ref[...].astype(F32)
    o_ref[...] = _ln(ALPHA * x_ref[...] + _mod_part(mod_ref, 5) * ffn) * lng_ref[...] + lnb_ref[...]


def _final(x1, mods, f, route, ln_g, ln_b):
    n_batch = x1.shape[0]
    n_tiles = TOK // TM
    tile = pl.BlockSpec((None, TM, D_MODEL), lambda b, j: (b, j, 0))
    const = pl.BlockSpec((1, D_MODEL), lambda b, j: (0, 0))
    return pl.pallas_call(
        _final_kernel,
        out_shape=jax.ShapeDtypeStruct(x1.shape, F32),
        grid=(n_batch, n_tiles),
        in_specs=[tile, pl.BlockSpec((None, 1, N_MOD * D_MODEL), _mod_row(n_batch)),
                  pl.BlockSpec((None, None, TM, D_MODEL), lambda b, j: (0, b, j, 0)),
                  pl.BlockSpec((None, None, TM, D_MODEL), lambda b, j: (1, b, j, 0)),
                  pl.BlockSpec((8, TM), lambda b, j: (0, b * n_tiles + j)),
                  const, const],
        out_specs=tile,
        compiler_params=_params(("arbitrary", "arbitrary")),
        name="ffn_residual",
    )(x1, mods, f, f, route, ln_g, ln_b)


def _lookup(table, idx):
    ids = jnp.arange(table.shape[0], dtype=jnp.int32).reshape((-1,) + (1,) * idx.ndim)
    return jnp.sum(jnp.where(idx[None] == ids, table.reshape(ids.shape), 0), axis=0)


def _dispatch(route, counts_rows):
    n_tok = route.shape[1]
    n_assign = n_tok * TOP_K
    n_blocks = -(-(n_assign + N_EXPERTS * (MOE_ROWS - 1)) // MOE_ROWS)
    cap = n_blocks * MOE_ROWS
    counts = counts_rows[:, 0].astype(jnp.int32).reshape(PER_GROUP, N_EXPERT_GROUPS).T.reshape(N_EXPERTS)
    padded = (counts + MOE_ROWS - 1) // MOE_ROWS * MOE_ROWS
    pad_end = jnp.cumsum(padded)
    pad_start = pad_end - padded
    expert = route[0:TOP_K].astype(jnp.int32)
    pos = _lookup(pad_start, expert) + route[2 * TOP_K:3 * TOP_K].astype(jnp.int32)
    block_start = jnp.arange(n_blocks, dtype=jnp.int32) * MOE_ROWS
    block_expert = jnp.minimum(jnp.sum((pad_end[None, :] <= block_start[:, None]).astype(jnp.int32), axis=1),
                               N_EXPERTS - 1)
    n_used = (pad_end[-1:] // MOE_ROWS).astype(jnp.int32)
    gap = padded - counts
    gap_end = jnp.cumsum(gap)
    gap_first = jnp.concatenate([pad_start + counts - (gap_end - gap), pad_end[-1:] - gap_end[-1:]])
    k = jnp.arange(cap - n_assign, dtype=jnp.int32)
    owner = jnp.sum((gap_end[None, :] <= k[:, None]).astype(jnp.int32), axis=1)
    free_slot = _lookup(gap_first, owner) + k
    tok = jnp.broadcast_to(jnp.arange(n_tok, dtype=jnp.int32), (TOP_K, n_tok)).reshape(-1)
    keys = jnp.concatenate([pos.reshape(-1), free_slot])
    vals = jnp.concatenate([tok, jnp.zeros((cap - n_assign,), jnp.int32)])
    _, slot_tok = lax.sort((keys, vals), num_keys=1)
    return slot_tok, pos, block_expert, n_used


def _col(w, part):
    return w[:, _IN_OFF[part]:_IN_OFF[part + 1]]


def _proj_weights(w_in):
    zeros = jnp.zeros((D_MODEL, 128 - SSD_HEADS), w_in.dtype)
    dt = _col(w_in, _SDT)
    parts = [_col(w_in, p) for p in (_AQ, _GQ, _AK, _AV, _GK, _GV, _SX, _SB, _SC, _SZ, _SU)]
    parts += [dt[:, 0:SSD_HEADS], zeros, dt[:, SSD_HEADS:2 * SSD_HEADS], zeros]
    return jnp.concatenate(parts, axis=1).astype(BF16), _col(w_in, _GATE).astype(BF16)


def _rope_tables():
    rows = SEQ // GRID_W
    row = jnp.repeat(jnp.arange(rows, dtype=F32), GRID_W)
    col = jnp.tile(jnp.arange(GRID_W, dtype=F32), rows)
    axis_dim = HEAD_DIM // 2
    inv_freq = ROPE_THETA ** (-jnp.arange(0, axis_dim, 2, dtype=F32) / axis_dim)
    ang_r = row[:, None] * inv_freq
    ang_c = col[:, None] * inv_freq
    ang = jnp.concatenate([ang_r, ang_r, ang_c, ang_c], axis=-1)
    cos = jnp.concatenate([jnp.ones((CTX_LEN, HEAD_DIM), F32), jnp.cos(ang)], axis=0)
    sin = jnp.concatenate([jnp.zeros((CTX_LEN, HEAD_DIM), F32), jnp.sin(ang)], axis=0)
    return jnp.tile(cos, (1, 2)), jnp.tile(sin, (1, 2))


def _pad_lanes(v, width=128):
    return jnp.pad(v, ((0, 0), (0, width - v.shape[-1])))


def kernel(x, c, ctx, c_ctx, mod_w, mod_b, w_in, wa_sink, ga_q_norm, ga_k_norm, ssd_conv_w, ssd_conv_b, ssd_dt_bias, ssd_a_log, ssd_d, ssd_norm_w, s5_a_re, s5_a_im, s5_log_dt, s5_b_re, s5_b_im, s5_c_re, s5_c_im, s5_d, s5_w_glu, w_branch, w_out, ln1_g, ln1_b, ln2_g, ln2_b, router_w, router_bias, moe_w_gate, moe_w_up, moe_w_down):
    n_batch = x.shape[0]
    n_tok = n_batch * TOK
    xcat = jnp.concatenate([ctx, x], axis=1)
    mod_rows = -(-(n_batch + 1) // 8) * 8
    cond = jnp.zeros((mod_rows, D_MODEL), F32).at[:n_batch].set(c).at[n_batch].set(c_ctx)
    mods_all = _modulation(cond, mod_w, mod_b)
    cos, sin = _rope_tables()

    perm = np.array([g * PER_GROUP + j for j in range(PER_GROUP) for g in range(N_EXPERT_GROUPS)])
    rw_t = router_w.T[perm].astype(BF16)
    rb = router_bias.astype(F32)[perm].reshape(N_EXPERTS, 1)

    for layer in range(DEPTH):
        mods = mods_all[layer].reshape(mod_rows, 1, N_MOD * D_MODEL)
        w1, wg = _proj_weights(w_in[layer])
        attn, conv_in, sz, su, dt, u_g = _inproj(xcat, mods, w1)

        sink_rows = jnp.repeat(wa_sink[layer].astype(F32).reshape(N_KV, GRP), QB, axis=1).reshape(N_KV, GRP * QB, 1)
        qw = jnp.tile(ga_q_norm[layer].astype(F32), N_HEADS).reshape(1, 512)
        kw = jnp.tile(ga_k_norm[layer].astype(F32), N_KV).reshape(1, 128)
        ya, yg = _attention(attn, cos, sin, sink_rows, qw, kw)

        dt_bias = _pad_lanes(ssd_dt_bias[layer].astype(F32).reshape(2, SSD_HEADS)).reshape(2, 1, 128)
        a_log = _pad_lanes(ssd_a_log[layer].astype(F32)).reshape(2, 1, 128)
        d_exp = jnp.repeat(ssd_d[layer].astype(F32), SSD_P).reshape(1, SSD_WIDTH)
        ys = _ssd(conv_in, sz, dt, ssd_conv_w[layer].astype(F32), ssd_conv_b[layer].astype(F32).reshape(1, CONV_W),
                  dt_bias, a_log, d_exp, ssd_norm_w[layer].astype(F32).reshape(1, SSD_WIDTH))

        mats = _s5_matrices(s5_a_re[layer], s5_a_im[layer], s5_log_dt[layer], s5_b_re[layer], s5_b_im[layer],
                            s5_c_re[layer], s5_c_im[layer])
        y5 = _s5(u_g.reshape(S5_GROUPS, S5_NCH * n_batch, S5_ROWW), mats, n_batch)
        y5 = y5.reshape(S5_GROUPS, S5_NCH, n_batch * S5_ROWW)

        x1, h2, route, counts = _merge(
            xcat, mods, ya, ys, yg, y5, su, s5_d[layer].astype(F32).reshape(1, S5_WIDTH), wg,
            w_branch[layer].astype(BF16), s5_w_glu[layer].astype(BF16), w_out[layer].astype(BF16),
            ln1_g[layer].reshape(1, D_MODEL), ln1_b[layer].reshape(1, D_MODEL), rw_t, rb)

        slot_tok, pos, block_expert, n_used = _dispatch(route, counts)
        xs = h2.reshape(n_tok, D_MODEL)[slot_tok]
        y_slots = _moe_experts(block_expert, n_used, xs, moe_w_gate[layer], moe_w_up[layer], moe_w_down[layer])
        f = y_slots[pos.reshape(-1)].reshape(TOP_K, n_batch, TOK, D_MODEL)
        xcat = _final(x1, mods, f, route, ln2_g[layer].reshape(1, D_MODEL), ln2_b[layer].reshape(1, D_MODEL))
    return xcat[:, CTX_LEN:, :]
```

```python
import functools
import math

import jax
import jax.numpy as jnp
import numpy as np
from jax import lax
from jax.experimental import pallas as pl
from jax.experimental.pallas import tpu as pltpu

F32 = jnp.float32
BF16 = jnp.bfloat16

D_MODEL = 1024
DEPTH = 4
GRID_W = 64
CTX_LEN = 256
SEQ = 2048
TOK = CTX_LEN + SEQ
N_MOD = 6
HEAD_DIM = 64
N_BRANCH = 4
N_HEADS = 8
N_KV = 2
GRP = N_HEADS // N_KV
WINDOW = 128
QB = 128
ROPE_THETA = 10000.0

SSD_HEADS = 8
SSD_P = 64
SSD_GROUPS = 2
SSD_N = 128
SSD_T = 128
SSD_WIDTH = SSD_HEADS * SSD_P
SSD_CONV_CH = SSD_WIDTH + 2 * SSD_GROUPS * SSD_N
N_CHUNK = TOK // SSD_T

S5_GROUP = 16
S5_WIDTH = 512
S5_GROUPS = S5_WIDTH // S5_GROUP
S5_STATE = 64
S5_T = 16
S5_ROWW = S5_T * S5_GROUP
S5_NCH = TOK // S5_T
S5_CTX_CH = CTX_LEN // S5_T

N_EXPERTS = 32
N_EXPERT_GROUPS = 8
PER_GROUP = N_EXPERTS // N_EXPERT_GROUPS
TOP_K = 2
D_EXPERT = 512
MOE_ROWS = 256

ALPHA = (2 * DEPTH) ** 0.25
NORM_EPS = 1e-6

TM = 256
VMEM_LIMIT = 56 * 1024 * 1024

_IN_SIZES = (512, 128, 128, 512, 128, 128, 512, 512, 256, 256, 16, 512, 4096)
_IN_OFF = np.concatenate([[0], np.cumsum(_IN_SIZES)]).astype(int)
(_AQ, _AK, _AV, _GQ, _GK, _GV, _SX, _SZ, _SB, _SC, _SDT, _SU, _GATE) = range(13)

ATTN_W = 1536
CONV_W = 1024
DT_W = 256
PROJ_W = ATTN_W + CONV_W + 512 + 512 + DT_W


def _params(sem=None):
    return pltpu.CompilerParams(dimension_semantics=sem, vmem_limit_bytes=VMEM_LIMIT)


def _ln(x):
    mu = jnp.mean(x, axis=-1, keepdims=True)
    xc = x - mu
    var = jnp.mean(xc * xc, axis=-1, keepdims=True)
    return xc * lax.rsqrt(var + NORM_EPS)


def _sigmoid(x):
    return 1.0 / (1.0 + jnp.exp(-x))


def _silu(x):
    return x * _sigmoid(x)


def _bdot(a, b):
    return jnp.dot(a.astype(BF16), b.astype(BF16), preferred_element_type=F32)


def _bdot_nt(a, b):
    return lax.dot_general(a.astype(BF16), b.astype(BF16), (((1,), (1,)), ((), ())),
                           preferred_element_type=F32)


def _split3(v):
    hi = v.astype(BF16)
    r1 = v - hi.astype(F32)
    mid = r1.astype(BF16)
    lo = (r1 - mid.astype(F32)).astype(BF16)
    return hi, mid, lo


def _mod_kernel(c_ref, w_ref, b_ref, o_ref):
    c = c_ref[...]
    o_ref[...] = _bdot(_silu(c), w_ref[...]) + b_ref[...]


def _modulation(cond, mod_w, mod_b):
    n_layer = mod_w.shape[0]
    rows = cond.shape[0]
    return pl.pallas_call(
        _mod_kernel,
        out_shape=jax.ShapeDtypeStruct((n_layer, rows, N_MOD * D_MODEL), F32),
        grid=(n_layer, N_MOD),
        in_specs=[
            pl.BlockSpec((rows, D_MODEL), lambda l, n: (0, 0)),
            pl.BlockSpec((None, D_MODEL, D_MODEL), lambda l, n: (l, 0, n)),
            pl.BlockSpec((None, 1, D_MODEL), lambda l, n: (l, 0, n)),
        ],
        out_specs=pl.BlockSpec((None, rows, D_MODEL), lambda l, n: (l, 0, n)),
        compiler_params=_params(("arbitrary", "arbitrary")),
        name="modulation",
    )(cond, mod_w, mod_b.reshape(n_layer, 1, N_MOD * D_MODEL))


def _mod_part(mod_ref, k):
    return mod_ref[:, k * D_MODEL:(k + 1) * D_MODEL]


def _mod_row(n_batch):
    return lambda b, j: (jnp.where(j == 0, n_batch, b), 0, 0)


def _chunk_swap():
    r = np.arange(TM)
    p = np.zeros((TM, TM), np.float32)
    p[r, (r % S5_T) * (TM // S5_T) + r // S5_T] = 1.0
    return jnp.asarray(p, BF16)


def _inproj_kernel(x_ref, mod_ref, w_ref, swap_ref, attn_ref, conv_ref, sz_ref, su_ref, dt_ref, u_ref):
    h = _ln(x_ref[...]) * (1.0 + _mod_part(mod_ref, 1)) + _mod_part(mod_ref, 0)
    hb = h.astype(BF16)
    off = 0
    for ref in (attn_ref, conv_ref, sz_ref, su_ref, dt_ref):
        width = ref.shape[-1]
        ref[...] = jnp.dot(hb, w_ref[:, off:off + width], preferred_element_type=F32)
        off += width
    by_step = jnp.dot(swap_ref[...], su_ref[...].astype(BF16), preferred_element_type=F32)
    n_ch = TM // S5_T
    for t in range(S5_T):
        rows = by_step[t * n_ch:(t + 1) * n_ch]
        for g in range(S5_GROUPS):
            u_ref[g, :, t * S5_GROUP:(t + 1) * S5_GROUP] = rows[:, g * S5_GROUP:(g + 1) * S5_GROUP].astype(BF16)


def _inproj(xcat, mods, w1):
    n_batch = xcat.shape[0]
    widths = (ATTN_W, CONV_W, 512, 512, DT_W)
    tile = lambda w: pl.BlockSpec((None, TM, w), lambda b, j: (b, j, 0))
    return pl.pallas_call(
        _inproj_kernel,
        out_shape=([jax.ShapeDtypeStruct((n_batch, TOK, w), F32) for w in widths]
                   + [jax.ShapeDtypeStruct((S5_GROUPS, S5_NCH, n_batch * S5_ROWW), BF16)]),
        grid=(n_batch, TOK // TM),
        in_specs=[
            tile(D_MODEL),
            pl.BlockSpec((None, 1, N_MOD * D_MODEL), _mod_row(n_batch)),
            pl.BlockSpec((D_MODEL, PROJ_W), lambda b, j: (0, 0)),
            pl.BlockSpec((TM, TM), lambda b, j: (0, 0)),
        ],
        out_specs=([tile(w) for w in widths]
                   + [pl.BlockSpec((S5_GROUPS, TM // S5_T, S5_ROWW), lambda b, j: (0, j, b))]),
        compiler_params=_params(("arbitrary", "arbitrary")),
        name="inproj",
    )(xcat, mods, w1, _chunk_swap())


def _rope(t, cos, sin):
    width = t.shape[-1]
    lane = lax.broadcasted_iota(jnp.int32, t.shape, t.ndim - 1)
    rot = jnp.where(lane % 32 < 16, -pltpu.roll(t, width - 16, t.ndim - 1), pltpu.roll(t, 16, t.ndim - 1))
    return t * cos + rot * sin


def _tile4(v):
    return jnp.concatenate([v, v, v, v], axis=1)


def _head_inv_rms(t, n_heads):
    t2 = t * t
    lane = lax.broadcasted_iota(jnp.int32, t.shape, 1)
    inv = jnp.zeros_like(t)
    for h in range(n_heads):
        ms = jnp.sum(t2[:, h * HEAD_DIM:(h + 1) * HEAD_DIM], axis=1, keepdims=True) * (1.0 / HEAD_DIM)
        inv = jnp.where(lane // HEAD_DIM == h, lax.rsqrt(ms + NORM_EPS), inv)
    return inv


def _stack_heads(qr, kv):
    parts = [qr[:, (kv * GRP + g) * HEAD_DIM:(kv * GRP + g + 1) * HEAD_DIM] for g in range(GRP)]
    return jnp.concatenate(parts, axis=0).astype(BF16)


QPS = 2
LOG2E = math.log2(math.e)


def _attend(jobs, sink_ref, out_ref, kb_ref, vb_ref, s_ref, p_ref):
    half_rows = GRP * QB // 2
    halves = [slice(0, half_rows), slice(half_rows, 2 * half_rows)]
    for u, (qr, parts) in enumerate(jobs):
        for kv in range(N_KV):
            qs = _stack_heads(qr, kv)
            col = 0
            for key_rows, width, bias in parts:
                for hr in halves:
                    s = _bdot_nt(qs[hr], kb_ref[kv, key_rows, :])
                    if bias is not None:
                        s = s + jnp.concatenate([bias] * (half_rows // QB), axis=0)
                    s_ref[u, kv, hr, col:col + width] = s
                col += width
    dens = {}
    for u, (_, parts) in enumerate(jobs):
        n_tiles = sum(width for _, width, _ in parts) // 128
        for kv in range(N_KV):
            for r in range(GRP):
                rows = slice(r * QB, (r + 1) * QB)
                mx = s_ref[u, kv, rows, 0:128]
                for t in range(1, n_tiles):
                    mx = jnp.maximum(mx, s_ref[u, kv, rows, t * 128:(t + 1) * 128])
                m = jnp.max(mx, axis=1, keepdims=True)
                if sink_ref is not None:
                    m = jnp.maximum(m, sink_ref[kv, rows, :])
                m_b = jnp.broadcast_to(m, (QB, 128))
                acc = jnp.zeros((QB, 128), F32)
                for t in range(n_tiles):
                    p = jnp.exp2(s_ref[u, kv, rows, t * 128:(t + 1) * 128] - m_b)
                    acc = acc + p
                    p_ref[u, kv, rows, t * 128:(t + 1) * 128] = p.astype(BF16)
                den = jnp.sum(acc, axis=1, keepdims=True)
                if sink_ref is not None:
                    den = den + jnp.exp2(sink_ref[kv, rows, :] - m)
                dens[u, kv, r] = den
    for u, (_, parts) in enumerate(jobs):
        for kv in range(N_KV):
            for hi, hr in enumerate(halves):
                o = None
                col = 0
                for key_rows, width, _ in parts:
                    part = jnp.dot(p_ref[u, kv, hr, col:col + width], vb_ref[kv, key_rows, :],
                                   preferred_element_type=F32)
                    o = part if o is None else o + part
                    col += width
                for gl in range(half_rows // QB):
                    g = hi * (half_rows // QB) + gl
                    h = kv * GRP + g
                    out_ref[u * QB:(u + 1) * QB, h * HEAD_DIM:(h + 1) * HEAD_DIM] = (
                        o[gl * QB:(gl + 1) * QB] / dens[u, kv, g]).astype(out_ref.dtype)


def _wattn_kernel(q_ref, k_ref, v_ref, cos_ref, sin_ref, sink_ref, bias_ref, out_ref, kb_ref, vb_ref, s_ref, p_ref):
    i = pl.program_id(1)

    @pl.when(i == 0)
    def _prep():
        kr = _rope(k_ref[...], cos_ref[...], sin_ref[...])
        v = v_ref[...]
        for kv in range(N_KV):
            kb_ref[kv] = kr[:, kv * HEAD_DIM:(kv + 1) * HEAD_DIM].astype(BF16)
            vb_ref[kv] = v[:, kv * HEAD_DIM:(kv + 1) * HEAD_DIM].astype(BF16)

    row0 = pl.multiple_of(i * (QPS * QB), QPS * QB)
    cos = _tile4(cos_ref[pl.ds(row0, QPS * QB), :])
    sin = _tile4(sin_ref[pl.ds(row0, QPS * QB), :])
    qr = _rope(q_ref[...], cos, sin) * (HEAD_DIM ** -0.5 * LOG2E)
    qrs = [qr[u * QB:(u + 1) * QB] for u in range(QPS)]

    ctx_keys = (slice(0, CTX_LEN), CTX_LEN, None)

    @pl.when(i < CTX_LEN // (QPS * QB))
    def _ctx_queries():
        _attend([(q, [ctx_keys]) for q in qrs], sink_ref, out_ref, kb_ref, vb_ref, s_ref, p_ref)

    @pl.when(i >= CTX_LEN // (QPS * QB))
    def _latent_queries():
        jobs = []
        for u in range(QPS):
            j = i * QPS + u - CTX_LEN // QB
            band = jnp.clip(j - 1, 0, SEQ // QB - 3)
            start = pl.multiple_of(CTX_LEN + band * QB, QB)
            bias = bias_ref[j - band]
            jobs.append((qrs[u], [ctx_keys, (pl.ds(start, 3 * QB), 3 * QB, bias)]))
        _attend(jobs, sink_ref, out_ref, kb_ref, vb_ref, s_ref, p_ref)


def _gattn_kernel(q_ref, k_ref, v_ref, cos_ref, sin_ref, qw_ref, kw_ref, out_ref, kb_ref, vb_ref, s_ref, p_ref):
    i = pl.program_id(1)

    @pl.when(i == 0)
    def _prep():
        k = k_ref[...]
        kn = k * _head_inv_rms(k, N_KV) * kw_ref[...]
        kr = _rope(kn, cos_ref[...], sin_ref[...])
        v = v_ref[...]
        for kv in range(N_KV):
            kb_ref[kv] = kr[:, kv * HEAD_DIM:(kv + 1) * HEAD_DIM].astype(BF16)
            vb_ref[kv] = v[:, kv * HEAD_DIM:(kv + 1) * HEAD_DIM].astype(BF16)

    row0 = pl.multiple_of(i * (QPS * QB), QPS * QB)
    cos = _tile4(cos_ref[pl.ds(row0, QPS * QB), :])
    sin = _tile4(sin_ref[pl.ds(row0, QPS * QB), :])
    q = q_ref[...]
    qn = q * _head_inv_rms(q, N_HEADS) * qw_ref[...]
    qr = _rope(qn, cos, sin) * (HEAD_DIM ** -0.5 * LOG2E)
    qrs = [qr[u * QB:(u + 1) * QB] for u in range(QPS)]

    @pl.when(i < CTX_LEN // (QPS * QB))
    def _ctx_queries():
        _attend([(q_u, [(slice(0, CTX_LEN), CTX_LEN, None)]) for q_u in qrs], None, out_ref, kb_ref, vb_ref,
                s_ref, p_ref)

    @pl.when(i >= CTX_LEN // (QPS * QB))
    def _latent_queries():
        _attend([(q_u, [(slice(0, TOK), TOK, None)]) for q_u in qrs], None, out_ref, kb_ref, vb_ref, s_ref, p_ref)


def _window_bias():
    q_pos = np.arange(QB)[None, :, None] + QB * np.arange(3)[:, None, None]
    k_pos = np.arange(3 * QB)[None, None, :]
    return jnp.asarray(np.where(np.abs(q_pos - k_pos) <= WINDOW, 0.0, -np.inf), F32)


def _attention(attn, cos, sin, sink_rows, qw, kw):
    n_batch = attn.shape[0]
    grid = (n_batch, TOK // (QPS * QB))
    q_spec = lambda blk: pl.BlockSpec((None, QPS * QB, 512), lambda b, i: (b, i, blk))
    kv_spec = lambda blk: pl.BlockSpec((None, TOK, 128), lambda b, i: (b, 0, blk))
    tab_spec = pl.BlockSpec((TOK, 128), lambda b, i: (0, 0))
    out_spec = pl.BlockSpec((None, QPS * QB, 512), lambda b, i: (b, i, 0))
    out_shape = jax.ShapeDtypeStruct((n_batch, TOK, 512), BF16)
    kv_scratch = [pltpu.VMEM((N_KV, TOK, HEAD_DIM), BF16), pltpu.VMEM((N_KV, TOK, HEAD_DIM), BF16)]
    w_keys = CTX_LEN + 3 * QB
    ya = pl.pallas_call(
        _wattn_kernel, out_shape=out_shape, grid=grid,
        in_specs=[q_spec(0), kv_spec(8), kv_spec(9), tab_spec, tab_spec,
                  pl.BlockSpec((N_KV, GRP * QB, 1), lambda b, i: (0, 0, 0)),
                  pl.BlockSpec((3, QB, 3 * QB), lambda b, i: (0, 0, 0))],
        out_specs=out_spec,
        scratch_shapes=kv_scratch + [pltpu.VMEM((QPS, N_KV, GRP * QB, w_keys), F32), pltpu.VMEM((QPS, N_KV, GRP * QB, w_keys), BF16)],
        compiler_params=_params(("arbitrary", "arbitrary")), name="window_attention",
    )(attn, attn, attn, cos, sin, sink_rows, _window_bias())
    yg = pl.pallas_call(
        _gattn_kernel, out_shape=out_shape, grid=grid,
        in_specs=[q_spec(1), kv_spec(10), kv_spec(11), tab_spec, tab_spec,
                  pl.BlockSpec((1, 512), lambda b, i: (0, 0)),
                  pl.BlockSpec((1, 128), lambda b, i: (0, 0))],
        out_specs=out_spec,
        scratch_shapes=kv_scratch + [pltpu.VMEM((QPS, N_KV, GRP * QB, TOK), F32), pltpu.VMEM((QPS, N_KV, GRP * QB, TOK), BF16)],
        compiler_params=_params(("arbitrary", "arbitrary")), name="global_attention",
    )(attn, attn, attn, cos, sin, qw, kw)
    return ya, yg


def _ssd_chunk(s):
    r = s - N_CHUNK
    back = jnp.where(r == 0, 1, jnp.where(r == 1, 0, N_CHUNK + 1 - r))
    return jnp.where(s < N_CHUNK, s, back)


SSD_NB = 2


def _ssd_kernel(x_ref, prev_ref, next_ref, z_ref, dt_ref, cw_ref, cb_ref, dtb_ref, a_ref, d_ref, nw_ref, e_ref,
                out_ref, yf_ref, state_ref, y_ref):
    s = pl.program_id(1)
    chunk = _ssd_chunk(s)
    backward = s >= N_CHUNK

    @pl.when((s == 0) | (s == N_CHUNK))
    def _reset():
        state_ref[...] = jnp.zeros_like(state_ref)

    first = (chunk == 0) | (chunk == CTX_LEN // SSD_T)
    last = (chunk == CTX_LEN // SSD_T - 1) | (chunk == N_CHUNK - 1)
    row = lax.broadcasted_iota(jnp.int32, (SSD_T, CONV_W), 0)
    tr = lax.broadcasted_iota(jnp.int32, (SSD_T, SSD_T), 0)
    tc = lax.broadcasted_iota(jnp.int32, (SSD_T, SSD_T), 1)
    causal = jnp.where(backward, tc - tr, tr - tc) >= 0
    tri = jnp.where(causal, 1.0, 0.0).astype(BF16)
    neg_a = -jnp.exp(a_ref[...])
    expand = e_ref[...]

    def widen(v):
        return sum(jnp.dot(p, expand, preferred_element_type=F32) for p in _split3(v))

    xs = []
    for u in range(SSD_NB):
        xin = x_ref[u]
        prev = jnp.where(first, 0.0, prev_ref[u, 7:8, :])
        nxt = jnp.where(last, 0.0, next_ref[u, 0:1, :])
        xm1 = jnp.where(row == 0, prev, pltpu.roll(xin, 1, 0))
        xp1 = jnp.where(row == SSD_T - 1, nxt, pltpu.roll(xin, SSD_T - 1, 0))
        conv = cw_ref[0:1, :] * xm1 + cw_ref[1:2, :] * xin + cw_ref[2:3, :] * xp1 + cb_ref[...]
        xbc = _silu(conv)
        x = xbc[:, 0:SSD_WIDTH]
        xs.append(x)
        bmat = [xbc[:, SSD_WIDTH + g * SSD_N:SSD_WIDTH + (g + 1) * SSD_N] for g in range(SSD_GROUPS)]
        cmat = [xbc[:, SSD_WIDTH + (SSD_GROUPS + g) * SSD_N:SSD_WIDTH + (SSD_GROUPS + g + 1) * SSD_N]
                for g in range(SSD_GROUPS)]

        dtv = dt_ref[u] + dtb_ref[...]
        dt = jnp.maximum(dtv, 0.0) + jnp.log(1.0 + jnp.exp(-jnp.abs(dtv)))
        hi, mid, lo = _split3(dt * neg_a)
        cs = (jnp.dot(tri, hi, preferred_element_type=F32) + jnp.dot(tri, mid, preferred_element_type=F32)
              + jnp.dot(tri, lo, preferred_element_type=F32))
        cs_t = cs.T

        dt_x = widen(dt)
        cs_x = widen(cs)
        tot_x = jnp.where(backward, cs_x[0:1, :], cs_x[SSD_T - 1:SSD_T, :])
        ecs_x = jnp.exp(cs_x)
        etot_x = jnp.exp(tot_x)
        xd = x * dt_x
        xd_b = xd.astype(BF16)
        xd_end = (xd * jnp.exp(tot_x - cs_x)).astype(BF16)

        gw = SSD_WIDTH // SSD_GROUPS
        for g in range(SSD_GROUPS):
            glanes = slice(g * gw, (g + 1) * gw)
            cb = _bdot_nt(cmat[g], bmat[g])
            st = state_ref[u, g]
            y_off = ecs_x[:, glanes] * jnp.dot(cmat[g].astype(BF16), st.astype(BF16), preferred_element_type=F32)
            state_ref[u, g] = etot_x[:, glanes] * st + jnp.dot(bmat[g].T.astype(BF16), xd_end[:, glanes],
                                                               preferred_element_type=F32)
            for j in range(SSD_HEADS // SSD_GROUPS):
                h = g * (SSD_HEADS // SSD_GROUPS) + j
                lanes = slice(h * SSD_P, (h + 1) * SSD_P)
                seg = jnp.exp(jnp.where(causal, cs[:, h:h + 1] - cs_t[h:h + 1, :], -jnp.inf))
                y_ref[u, :, lanes] = (jnp.dot((cb * seg).astype(BF16), xd_b[:, lanes], preferred_element_type=F32)
                                      + y_off[:, j * SSD_P:(j + 1) * SSD_P])

    row0 = pl.multiple_of(chunk * SSD_T, SSD_T)

    @pl.when(jnp.logical_not(backward))
    def _keep():
        for u in range(SSD_NB):
            yf_ref[u, pl.ds(row0, SSD_T), :] = y_ref[u]

    @pl.when(backward)
    def _finish():
        for u in range(SSD_NB):
            ytot = yf_ref[u, pl.ds(row0, SSD_T), :] + y_ref[u] + d_ref[...] * xs[u]
            gated = ytot * _silu(z_ref[u])
            ms = jnp.mean(gated * gated, axis=1, keepdims=True)
            out_ref[u] = (gated * lax.rsqrt(ms + NORM_EPS) * nw_ref[...]).astype(out_ref.dtype)


def _ssd(conv_in, sz, dt, conv_w, conv_b, dt_bias, a_log, d_exp, norm_w):
    n_batch = conv_in.shape[0]
    assert n_batch % SSD_NB == 0
    halo = SSD_T // 8
    n_halo = TOK // 8

    def chunk_map(b, s):
        return (b, _ssd_chunk(s), 0)

    def out_map(b, s):
        return (b, jnp.where(s < N_CHUNK, 1, _ssd_chunk(s)), 0)

    const = lambda shape: pl.BlockSpec(shape, lambda b, s: tuple(0 for _ in shape))
    dir_spec = pl.BlockSpec((None, 1, 128), lambda b, s: (s // N_CHUNK, 0, 0))
    expand = jnp.asarray(np.repeat(np.eye(128, SSD_HEADS), SSD_P, axis=1), BF16)
    return pl.pallas_call(
        _ssd_kernel,
        out_shape=jax.ShapeDtypeStruct((n_batch, TOK, SSD_WIDTH), BF16),
        grid=(n_batch // SSD_NB, 2 * N_CHUNK),
        in_specs=[
            pl.BlockSpec((SSD_NB, SSD_T, CONV_W), chunk_map),
            pl.BlockSpec((SSD_NB, 8, CONV_W), lambda b, s: (b, jnp.maximum(_ssd_chunk(s) * halo - 1, 0), 0)),
            pl.BlockSpec((SSD_NB, 8, CONV_W), lambda b, s: (b, jnp.minimum((_ssd_chunk(s) + 1) * halo, n_halo - 1), 0)),
            pl.BlockSpec((SSD_NB, SSD_T, SSD_WIDTH), chunk_map),
            pl.BlockSpec((SSD_NB, SSD_T, 128), lambda b, s: (b, _ssd_chunk(s), s // N_CHUNK)),
            const((3, CONV_W)), const((1, CONV_W)), dir_spec, dir_spec,
            const((1, SSD_WIDTH)), const((1, SSD_WIDTH)), const((128, SSD_WIDTH)),
        ],
        out_specs=pl.BlockSpec((SSD_NB, SSD_T, SSD_WIDTH), out_map),
        scratch_shapes=[pltpu.VMEM((SSD_NB, TOK, SSD_WIDTH), F32),
                        pltpu.VMEM((SSD_NB, SSD_GROUPS, SSD_N, SSD_WIDTH // SSD_GROUPS), F32),
                        pltpu.VMEM((SSD_NB, SSD_T, SSD_WIDTH), F32)],
        compiler_params=_params(("arbitrary", "arbitrary")),
        name="ssd",
    )(conv_in, conv_in, conv_in, sz, dt, conv_w, conv_b, dt_bias, a_log, d_exp, norm_w, expand)


S5_SW = 4 * 128


def _s5_kernel(u_ref, m_ref, h_ref, g_ref, a_ref, y_ref, loc_ref, prev_ref, *, n_batch):
    u = u_ref[...]
    loc_ref[...] = jnp.dot(u, h_ref[...], preferred_element_type=F32)
    a = [a_ref[k:k + 1, :] for k in range(4)]

    def step(k, carry):
        c_f = k
        c_b = jnp.where(k < S5_CTX_CH, S5_CTX_CH - 1 - k, S5_NCH + S5_CTX_CH - 1 - k)
        new = []
        for d, c in enumerate((c_f, c_b)):
            rows = pl.ds(pl.multiple_of(c * n_batch, n_batch), n_batch)
            re_l, im_l = slice(2 * d * 128, (2 * d + 1) * 128), slice((2 * d + 1) * 128, (2 * d + 2) * 128)
            s_re, s_im = carry[2 * d], carry[2 * d + 1]
            prev_ref[rows, re_l] = s_re
            prev_ref[rows, im_l] = s_im
            a_re, a_im = a[2 * d], a[2 * d + 1]
            new += [a_re * s_re - a_im * s_im + loc_ref[rows, re_l], a_re * s_im + a_im * s_re + loc_ref[rows, im_l]]
        return tuple(new)

    zero = jnp.zeros((n_batch, 128), F32)
    lax.fori_loop(0, S5_NCH, step, (zero,) * 4, unroll=2)
    y = jnp.dot(u, m_ref[...], preferred_element_type=F32)
    y = y + jnp.dot(prev_ref[...].astype(BF16), g_ref[...], preferred_element_type=F32)
    y_ref[...] = y.astype(y_ref.dtype)


def _s5(u_g, mats, n_batch):
    m_all, h_all, g_all, a16 = mats
    rows = u_g.shape[1]
    grp = lambda shape: pl.BlockSpec((None,) + shape, lambda g: (g,) + tuple(0 for _ in shape))
    return pl.pallas_call(
        functools.partial(_s5_kernel, n_batch=n_batch),
        out_shape=jax.ShapeDtypeStruct((S5_GROUPS, rows, S5_ROWW), BF16),
        grid=(S5_GROUPS,),
        in_specs=[grp((rows, S5_ROWW)), grp((S5_ROWW, S5_ROWW)), grp((S5_ROWW, S5_SW)), grp((S5_SW, S5_ROWW)),
                  grp((4, 128))],
        out_specs=grp((rows, S5_ROWW)),
        scratch_shapes=[pltpu.VMEM((rows, S5_SW), F32), pltpu.VMEM((rows, S5_SW), F32)],
        compiler_params=_params(("arbitrary",)),
        name="s5",
    )(u_g, m_all, h_all, g_all, a16)


def _cmul(a, b):
    return a[0] * b[0] - a[1] * b[1], a[0] * b[1] + a[1] * b[0]


def _s5_matrices(a_re, a_im, log_dt, b_re, b_im, c_re, c_im):
    hp = lax.Precision.HIGHEST
    t = jnp.arange(S5_T + 1, dtype=F32)
    m_sum = 0.0
    h_all, g_all, a16_all = [], [], []
    c = (c_re.astype(F32), c_im.astype(F32))
    for direction in range(2):
        are = jnp.minimum(a_re[direction].astype(F32), -1e-4)
        aim = a_im[direction].astype(F32)
        dt = jnp.exp(log_dt[direction].astype(F32))[:, None]
        mag = jnp.exp(t[:, None, None] * (are * dt)[None])
        ang = t[:, None, None] * (aim * dt)[None]
        pw = (mag * jnp.cos(ang), mag * jnp.sin(ang))
        num = (pw[0][1] - 1.0, pw[1][1])
        den = are * are + aim * aim
        coef = ((num[0] * are + num[1] * aim) / den, (num[1] * are - num[0] * aim) / den)
        bbar = _cmul((coef[0][..., None], coef[1][..., None]), (b_re.astype(F32), b_im.astype(F32)))
        pb = _cmul((pw[0][:S5_T, :, :, None], pw[1][:S5_T, :, :, None]), (bbar[0][None], bbar[1][None]))
        taps = (jnp.einsum('gon,tgni->tgoi', c[0], pb[0], precision=hp)
                - jnp.einsum('gon,tgni->tgoi', c[1], pb[1], precision=hp))
        ti = jnp.arange(S5_T)
        lag = (ti[None, :] - ti[:, None]) if direction == 0 else (ti[:, None] - ti[None, :])
        k_full = taps[jnp.clip(lag, 0, S5_T - 1)]
        k_full = jnp.where((lag >= 0)[:, :, None, None, None], k_full, 0.0)
        m_sum = m_sum + k_full.transpose(2, 0, 4, 1, 3).reshape(S5_GROUPS, S5_ROWW, S5_ROWW)
        e_in = (S5_T - 1 - ti) if direction == 0 else ti
        hb = _cmul((pw[0][e_in][..., None], pw[1][e_in][..., None]), (bbar[0][None], bbar[1][None]))
        zpad = jnp.zeros_like(hb[0])
        h_mat = jnp.concatenate([hb[0], zpad, hb[1], zpad], axis=2)
        h_all.append(h_mat.transpose(1, 0, 3, 2).reshape(S5_GROUPS, S5_ROWW, 4 * S5_STATE))
        e_out = (ti + 1) if direction == 0 else (S5_T - ti)
        cp = _cmul((c[0][None], c[1][None]),
                   (pw[0][e_out][:, :, None, :], pw[1][e_out][:, :, None, :]))
        zpad = jnp.zeros_like(cp[0])
        g_mat = jnp.concatenate([cp[0], zpad, -cp[1], zpad], axis=3)
        g_all.append(g_mat.transpose(1, 3, 0, 2).reshape(S5_GROUPS, 4 * S5_STATE, S5_ROWW))
        zrow = jnp.zeros_like(pw[0][S5_T])
        a16_all += [jnp.concatenate([pw[0][S5_T], zrow], axis=1), jnp.concatenate([pw[1][S5_T], zrow], axis=1)]
    return (m_sum.astype(BF16), jnp.concatenate(h_all, axis=2).astype(BF16),
            jnp.concatenate(g_all, axis=1).astype(BF16), jnp.stack(a16_all, axis=1))


def _route(logits_t, bias, base, before):
    scores = [_sigmoid(logits_t[j * N_EXPERT_GROUPS:(j + 1) * N_EXPERT_GROUPS]) for j in range(PER_GROUP)]
    sel = [scores[j] + bias[j * N_EXPERT_GROUPS:(j + 1) * N_EXPERT_GROUPS] for j in range(PER_GROUP)]
    hi1, lo1 = jnp.maximum(sel[0], sel[1]), jnp.minimum(sel[0], sel[1])
    hi2, lo2 = jnp.maximum(sel[2], sel[3]), jnp.minimum(sel[2], sel[3])
    group_score = jnp.maximum(hi1, hi2) + jnp.maximum(jnp.minimum(hi1, hi2), jnp.maximum(lo1, lo2))
    gid = lax.broadcasted_iota(jnp.int32, group_score.shape, 0)
    best = jnp.max(group_score, axis=0, keepdims=True)
    grp = jnp.min(jnp.where(group_score == best, gid, N_EXPERT_GROUPS), axis=0, keepdims=True)
    pick = gid == grp
    v = [jnp.sum(jnp.where(pick, sel[j], 0.0), axis=0, keepdims=True) for j in range(PER_GROUP)]
    sc = [jnp.sum(jnp.where(pick, scores[j], 0.0), axis=0, keepdims=True) for j in range(PER_GROUP)]
    rank = []
    for j in range(PER_GROUP):
        r = jnp.zeros_like(grp)
        for i in range(PER_GROUP):
            if i < j:
                r = r + jnp.where(v[i] >= v[j], 1, 0)
            elif i > j:
                r = r + jnp.where(v[i] > v[j], 1, 0)
        rank.append(r)
    e, w, locs = [], [], []
    for k in range(TOP_K):
        loc = sum(jnp.where(rank[j] == k, j, 0) for j in range(PER_GROUP))
        locs.append(loc)
        e.append((grp * PER_GROUP + loc).astype(F32))
        w.append(sum(jnp.where(rank[j] == k, sc[j], 0.0) for j in range(PER_GROUP)))
    wsum = w[0] + w[1]
    chosen = [jnp.where(pick & ((locs[0] == j) | (locs[1] == j)), 1.0, 0.0) for j in range(PER_GROUP)]
    onehot = jnp.concatenate(chosen, axis=0)
    seen = jnp.dot(onehot.astype(BF16), before, preferred_element_type=F32) + base
    pos = []
    for k in range(TOP_K):
        hit = sum(jnp.where(pick & (locs[k] == j), seen[j * N_EXPERT_GROUPS:(j + 1) * N_EXPERT_GROUPS], 0.0)
                  for j in range(PER_GROUP))
        pos.append(jnp.sum(hit, axis=0, keepdims=True))
    rows = [e[0], e[1], w[0] / wsum, w[1] / wsum, pos[0], pos[1]]
    route = jnp.concatenate(rows + [jnp.zeros_like(wsum)] * (8 - len(rows)), axis=0)
    return route, base + jnp.sum(onehot, axis=1, keepdims=True)


def _gelu_tanh(x):
    return 0.5 * x * (1.0 + jnp.tanh(math.sqrt(2.0 / math.pi) * (x + 0.044715 * (x * x * x))))


def _merge_kernel(x_ref, mod_ref, ya_ref, ys_ref, yg_ref, y5_ref, su_ref, s5d_ref, wg_ref, wb_ref, wglu_ref, wout_ref,
                  lng_ref, lnb_ref, rw_ref, rb_ref, before_ref, swap_ref, x1_ref, h2_ref, route_ref, count_ref,
                  base_ref, y5s_ref):
    @pl.when((pl.program_id(0) == 0) & (pl.program_id(1) == 0))
    def _reset():
        base_ref[...] = jnp.zeros_like(base_ref)

    x = x_ref[...]
    hb = (_ln(x) * (1.0 + _mod_part(mod_ref, 1)) + _mod_part(mod_ref, 0)).astype(BF16)
    n_ch = TM // S5_T
    for t in range(S5_T):
        for g in range(S5_GROUPS):
            y5s_ref[t * n_ch:(t + 1) * n_ch, g * S5_GROUP:(g + 1) * S5_GROUP] = (
                y5_ref[g, :, t * S5_GROUP:(t + 1) * S5_GROUP])
    y5_scan = jnp.dot(swap_ref[...], y5s_ref[...], preferred_element_type=F32)
    y5_pre = y5_scan + s5d_ref[...] * su_ref[...]
    glu = jnp.dot(_gelu_tanh(y5_pre).astype(BF16), wglu_ref[...], preferred_element_type=F32)
    y5 = (glu[:, 0:S5_WIDTH] * _sigmoid(glu[:, S5_WIDTH:2 * S5_WIDTH])).astype(BF16)
    acc = jnp.zeros(x.shape, F32)
    for n, y in enumerate((ya_ref[...], ys_ref[...], yg_ref[...], y5)):
        gate = _sigmoid(jnp.dot(hb, wg_ref[:, n * D_MODEL:(n + 1) * D_MODEL], preferred_element_type=F32))
        acc = acc + gate * jnp.dot(y, wb_ref[n], preferred_element_type=F32)
    mix = jnp.dot(acc.astype(BF16), wout_ref[...], preferred_element_type=F32)
    x1 = _ln(ALPHA * x + _mod_part(mod_ref, 2) * mix) * lng_ref[...] + lnb_ref[...]
    x1_ref[...] = x1
    h2 = (_ln(x1) * (1.0 + _mod_part(mod_ref, 4)) + _mod_part(mod_ref, 3)).astype(BF16)
    h2_ref[...] = h2
    route, base = _route(_bdot_nt(rw_ref[...], h2), rb_ref[...], base_ref[...], before_ref[...])
    route_ref[...] = route
    base_ref[...] = base
    count_ref[...] = jnp.broadcast_to(base, count_ref.shape)


def _merge(xcat, mods, ya, ys, yg, y5, su, s5_d, wg, wb, wglu, wout, ln_g, ln_b, rw_t, rb):
    n_batch = xcat.shape[0]
    n_tiles = TOK // TM
    tile = lambda w: pl.BlockSpec((None, TM, w), lambda b, j: (b, j, 0))
    const = lambda shape: pl.BlockSpec(shape, lambda b, j: tuple(0 for _ in shape))
    before = jnp.asarray(np.triu(np.ones((TM, TM)), 1), BF16)
    return pl.pallas_call(
        _merge_kernel,
        out_shape=[jax.ShapeDtypeStruct((n_batch, TOK, D_MODEL), F32),
                   jax.ShapeDtypeStruct((n_batch, TOK, D_MODEL), BF16),
                   jax.ShapeDtypeStruct((8, n_batch * TOK), F32),
                   jax.ShapeDtypeStruct((N_EXPERTS, 128), F32)],
        grid=(n_batch, n_tiles),
        in_specs=[tile(D_MODEL), pl.BlockSpec((None, 1, N_MOD * D_MODEL), _mod_row(n_batch)),
                  tile(512), tile(512), tile(512),
                  pl.BlockSpec((S5_GROUPS, TM // S5_T, S5_ROWW), lambda b, j: (0, j, b)),
                  tile(512), const((1, S5_WIDTH)),
                  const((D_MODEL, N_BRANCH * D_MODEL)), const((N_BRANCH, 512, D_MODEL)),
                  const((S5_WIDTH, 2 * S5_WIDTH)), const((D_MODEL, D_MODEL)),
                  const((1, D_MODEL)), const((1, D_MODEL)),
                  const((N_EXPERTS, D_MODEL)), const((N_EXPERTS, 1)), const((TM, TM)), const((TM, TM))],
        out_specs=[tile(D_MODEL), tile(D_MODEL),
                   pl.BlockSpec((8, TM), lambda b, j: (0, b * n_tiles + j)),
                   const((N_EXPERTS, 128))],
        scratch_shapes=[pltpu.VMEM((N_EXPERTS, 1), F32), pltpu.VMEM((TM, S5_WIDTH), BF16)],
        compiler_params=_params(("arbitrary", "arbitrary")),
        name="merge",
    )(xcat, mods, ya, ys, yg, y5, su, s5_d, wg, wb, wglu, wout, ln_g, ln_b, rw_t, rb, before, _chunk_swap())


def _moe_kernel(be_ref, nu_ref, x_ref, wg_ref, wu_ref, wd_ref, y_ref, wgu_s, wd_s):
    i = pl.program_id(0)
    used = i < nu_ref[0]

    @pl.when(used & ((i == 0) | (be_ref[i] != be_ref[jnp.maximum(i - 1, 0)])))
    def _cast():
        wgu_s[:, 0:D_EXPERT] = wg_ref[...].astype(BF16)
        wgu_s[:, D_EXPERT:2 * D_EXPERT] = wu_ref[...].astype(BF16)
        wd_s[...] = wd_ref[...].astype(BF16)

    @pl.when(used)
    def _block():
        gu = jnp.dot(x_ref[...], wgu_s[...], preferred_element_type=F32)
        mid = _silu(gu[:, 0:D_EXPERT]) * gu[:, D_EXPERT:2 * D_EXPERT]
        y_ref[...] = jnp.dot(mid.astype(BF16), wd_s[...], preferred_element_type=F32).astype(y_ref.dtype)


def _moe_experts(block_expert, n_used, xs, w_gate, w_up, w_down, layer):
    n_blocks = block_expert.shape[0]
    return pl.pallas_call(
        _moe_kernel,
        out_shape=jax.ShapeDtypeStruct((n_blocks * MOE_ROWS, D_MODEL), BF16),
        grid_spec=pltpu.PrefetchScalarGridSpec(
            num_scalar_prefetch=2, grid=(n_blocks,),
            in_specs=[pl.BlockSpec((MOE_ROWS, D_MODEL), lambda i, be, nu: (i, 0)),
                      pl.BlockSpec((None, None, D_MODEL, D_EXPERT), lambda i, be, nu: (layer, be[i], 0, 0)),
                      pl.BlockSpec((None, None, D_MODEL, D_EXPERT), lambda i, be, nu: (layer, be[i], 0, 0)),
                      pl.BlockSpec((None, None, D_EXPERT, D_MODEL), lambda i, be, nu: (layer, be[i], 0, 0))],
            out_specs=pl.BlockSpec((MOE_ROWS, D_MODEL), lambda i, be, nu: (i, 0)),
            scratch_shapes=[pltpu.VMEM((D_MODEL, 2 * D_EXPERT), BF16), pltpu.VMEM((D_EXPERT, D_MODEL), BF16)]),
        compiler_params=_params(("arbitrary",)),
        name="moe_experts",
    )(block_expert, n_used, xs, w_gate, w_up, w_down)


def _final_kernel(x_ref, mod_ref, f0_ref, f1_ref, route_ref, lng_ref, lnb_ref, o_ref):
    r = lax.broadcasted_iota(jnp.int32, (TM, TM), 0)
    c = lax.broadcasted_iota(jnp.int32, (TM, TM), 1)
    w = [jnp.sum(jnp.where(r == c, route_ref[TOP_K + k:TOP_K + k + 1, :], 0.0), axis=1, keepdims=True)
         for k in range(TOP_K)]
    ffn = w[0] * f0_ref[...].astype(F32) + w[1] * f1_ref[...].astype(F32)
    o_ref[...] = _ln(ALPHA * x_ref[...] + _mod_part(mod_ref, 5) * ffn) * lng_ref[...] + lnb_ref[...]


def _final(x1, mods, f, route, ln_g, ln_b):
    n_batch = x1.shape[0]
    n_tiles = TOK // TM
    tile = pl.BlockSpec((None, TM, D_MODEL), lambda b, j: (b, j, 0))
    const = pl.BlockSpec((1, D_MODEL), lambda b, j: (0, 0))
    return pl.pallas_call(
        _final_kernel,
        out_shape=jax.ShapeDtypeStruct(x1.shape, F32),
        grid=(n_batch, n_tiles),
        in_specs=[tile, pl.BlockSpec((None, 1, N_MOD * D_MODEL), _mod_row(n_batch)),
                  pl.BlockSpec((None, None, TM, D_MODEL), lambda b, j: (0, b, j, 0)),
                  pl.BlockSpec((None, None, TM, D_MODEL), lambda b, j: (1, b, j, 0)),
                  pl.BlockSpec((8, TM), lambda b, j: (0, b * n_tiles + j)),
                  const, const],
        out_specs=tile,
        compiler_params=_params(("arbitrary", "arbitrary")),
        name="ffn_residual",
    )(x1, mods, f, f, route, ln_g, ln_b)


def _lookup(table, idx):
    ids = jnp.arange(table.shape[0], dtype=jnp.int32).reshape((-1,) + (1,) * idx.ndim)
    return jnp.sum(jnp.where(idx[None] == ids, table.reshape(ids.shape), 0), axis=0)


def _dispatch(route, counts_rows):
    n_tok = route.shape[1]
    n_assign = n_tok * TOP_K
    n_blocks = -(-(n_assign + N_EXPERTS * (MOE_ROWS - 1)) // MOE_ROWS)
    cap = n_blocks * MOE_ROWS
    counts = counts_rows[:, 0].astype(jnp.int32).reshape(PER_GROUP, N_EXPERT_GROUPS).T.reshape(N_EXPERTS)
    padded = (counts + MOE_ROWS - 1) // MOE_ROWS * MOE_ROWS
    pad_end = jnp.cumsum(padded)
    pad_start = pad_end - padded
    expert = route[0:TOP_K].astype(jnp.int32)
    pos = _lookup(pad_start, expert) + route[2 * TOP_K:3 * TOP_K].astype(jnp.int32)
    block_start = jnp.arange(n_blocks, dtype=jnp.int32) * MOE_ROWS
    block_expert = jnp.minimum(jnp.sum((pad_end[None, :] <= block_start[:, None]).astype(jnp.int32), axis=1),
                               N_EXPERTS - 1)
    n_used = (pad_end[-1:] // MOE_ROWS).astype(jnp.int32)
    gap = padded - counts
    gap_end = jnp.cumsum(gap)
    gap_first = jnp.concatenate([pad_start + counts - (gap_end - gap), pad_end[-1:] - gap_end[-1:]])
    k = jnp.arange(cap - n_assign, dtype=jnp.int32)
    owner = jnp.sum((gap_end[None, :] <= k[:, None]).astype(jnp.int32), axis=1)
    free_slot = _lookup(gap_first, owner) + k
    tok = jnp.broadcast_to(jnp.arange(n_tok, dtype=jnp.int32), (TOP_K, n_tok)).reshape(-1)
    keys = jnp.concatenate([pos.reshape(-1), free_slot])
    vals = jnp.concatenate([tok, jnp.zeros((cap - n_assign,), jnp.int32)])
    _, slot_tok = lax.sort((keys, vals), num_keys=1)
    return slot_tok, pos, block_expert, n_used


def _col(w, part):
    return w[:, _IN_OFF[part]:_IN_OFF[part + 1]]


def _proj_weights(w_in):
    zeros = jnp.zeros((D_MODEL, 128 - SSD_HEADS), w_in.dtype)
    dt = _col(w_in, _SDT)
    parts = [_col(w_in, p) for p in (_AQ, _GQ, _AK, _AV, _GK, _GV, _SX, _SB, _SC, _SZ, _SU)]
    parts += [dt[:, 0:SSD_HEADS], zeros, dt[:, SSD_HEADS:2 * SSD_HEADS], zeros]
    return jnp.concatenate(parts, axis=1).astype(BF16), _col(w_in, _GATE).astype(BF16)


def _rope_tables():
    rows = SEQ // GRID_W
    row = jnp.repeat(jnp.arange(rows, dtype=F32), GRID_W)
    col = jnp.tile(jnp.arange(GRID_W, dtype=F32), rows)
    axis_dim = HEAD_DIM // 2
    inv_freq = ROPE_THETA ** (-jnp.arange(0, axis_dim, 2, dtype=F32) / axis_dim)
    ang_r = row[:, None] * inv_freq
    ang_c = col[:, None] * inv_freq
    ang = jnp.concatenate([ang_r, ang_r, ang_c, ang_c], axis=-1)
    cos = jnp.concatenate([jnp.ones((CTX_LEN, HEAD_DIM), F32), jnp.cos(ang)], axis=0)
    sin = jnp.concatenate([jnp.zeros((CTX_LEN, HEAD_DIM), F32), jnp.sin(ang)], axis=0)
    return jnp.tile(cos, (1, 2)), jnp.tile(sin, (1, 2))


def _pad_lanes(v, width=128):
    return jnp.pad(v, ((0, 0), (0, width - v.shape[-1])))


def kernel(x, c, ctx, c_ctx, mod_w, mod_b, w_in, wa_sink, ga_q_norm, ga_k_norm, ssd_conv_w, ssd_conv_b, ssd_dt_bias, ssd_a_log, ssd_d, ssd_norm_w, s5_a_re, s5_a_im, s5_log_dt, s5_b_re, s5_b_im, s5_c_re, s5_c_im, s5_d, s5_w_glu, w_branch, w_out, ln1_g, ln1_b, ln2_g, ln2_b, router_w, router_bias, moe_w_gate, moe_w_up, moe_w_down):
    n_batch = x.shape[0]
    n_tok = n_batch * TOK
    xcat = jnp.concatenate([ctx, x], axis=1)
    mod_rows = -(-(n_batch + 1) // 8) * 8
    cond = jnp.zeros((mod_rows, D_MODEL), F32).at[:n_batch].set(c).at[n_batch].set(c_ctx)
    mods_all = _modulation(cond, mod_w, mod_b)
    cos, sin = _rope_tables()

    perm = np.array([g * PER_GROUP + j for j in range(PER_GROUP) for g in range(N_EXPERT_GROUPS)])
    rw_t = router_w.T[perm].astype(BF16)
    rb = router_bias.astype(F32)[perm].reshape(N_EXPERTS, 1)

    for layer in range(DEPTH):
        mods = mods_all[layer].reshape(mod_rows, 1, N_MOD * D_MODEL)
        w1, wg = _proj_weights(w_in[layer])
        attn, conv_in, sz, su, dt, u_g = _inproj(xcat, mods, w1)

        sink_rows = jnp.repeat(wa_sink[layer].astype(F32).reshape(N_KV, GRP) * LOG2E, QB,
                               axis=1).reshape(N_KV, GRP * QB, 1)
        qw = jnp.tile(ga_q_norm[layer].astype(F32), N_HEADS).reshape(1, 512)
        kw = jnp.tile(ga_k_norm[layer].astype(F32), N_KV).reshape(1, 128)
        ya, yg = _attention(attn, cos, sin, sink_rows, qw, kw)

        dt_bias = _pad_lanes(ssd_dt_bias[layer].astype(F32).reshape(2, SSD_HEADS)).reshape(2, 1, 128)
        a_log = _pad_lanes(ssd_a_log[layer].astype(F32)).reshape(2, 1, 128)
        d_exp = jnp.repeat(ssd_d[layer].astype(F32), SSD_P).reshape(1, SSD_WIDTH)
        ys = _ssd(conv_in, sz, dt, ssd_conv_w[layer].astype(F32), ssd_conv_b[layer].astype(F32).reshape(1, CONV_W),
                  dt_bias, a_log, d_exp, ssd_norm_w[layer].astype(F32).reshape(1, SSD_WIDTH))

        mats = _s5_matrices(s5_a_re[layer], s5_a_im[layer], s5_log_dt[layer], s5_b_re[layer], s5_b_im[layer],
                            s5_c_re[layer], s5_c_im[layer])
        y5 = _s5(u_g.reshape(S5_GROUPS, S5_NCH * n_batch, S5_ROWW), mats, n_batch)
        y5 = y5.reshape(S5_GROUPS, S5_NCH, n_batch * S5_ROWW)

        x1, h2, route, counts = _merge(
            xcat, mods, ya, ys, yg, y5, su, s5_d[layer].astype(F32).reshape(1, S5_WIDTH), wg,
            w_branch[layer].astype(BF16), s5_w_glu[layer].astype(BF16), w_out[layer].astype(BF16),
            ln1_g[layer].reshape(1, D_MODEL), ln1_b[layer].reshape(1, D_MODEL), rw_t, rb)

        slot_tok, pos, block_expert, n_used = _dispatch(route, counts)
        xs = h2.reshape(n_tok, D_MODEL)[slot_tok]
        y_slots = _moe_experts(block_expert, n_used, xs, moe_w_gate, moe_w_up, moe_w_down, layer)
        f = y_slots[pos.reshape(-1)].reshape(TOP_K, n_batch, TOK, D_MODEL)
        xcat = _final(x1, mods, f, route, ln2_g[layer].reshape(1, D_MODEL), ln2_b[layer].reshape(1, D_MODEL))
    return xcat[:, CTX_LEN:, :]
```

```python
import functools
import math

import jax
import jax.numpy as jnp
import numpy as np
from jax import lax
from jax.experimental import pallas as pl
from jax.experimental.pallas import tpu as pltpu

F32 = jnp.float32
BF16 = jnp.bfloat16

D_MODEL = 1024
DEPTH = 4
GRID_W = 64
CTX_LEN = 256
SEQ = 2048
TOK = CTX_LEN + SEQ
N_MOD = 6
HEAD_DIM = 64
N_BRANCH = 4
N_HEADS = 8
N_KV = 2
GRP = N_HEADS // N_KV
WINDOW = 128
QB = 128
ROPE_THETA = 10000.0

SSD_HEADS = 8
SSD_P = 64
SSD_GROUPS = 2
SSD_N = 128
SSD_T = 128
SSD_WIDTH = SSD_HEADS * SSD_P
SSD_CONV_CH = SSD_WIDTH + 2 * SSD_GROUPS * SSD_N
N_CHUNK = TOK // SSD_T

S5_GROUP = 16
S5_WIDTH = 512
S5_GROUPS = S5_WIDTH // S5_GROUP
S5_STATE = 64
S5_T = 16
S5_ROWW = S5_T * S5_GROUP
S5_NCH = TOK // S5_T
S5_CTX_CH = CTX_LEN // S5_T

N_EXPERTS = 32
N_EXPERT_GROUPS = 8
PER_GROUP = N_EXPERTS // N_EXPERT_GROUPS
TOP_K = 2
D_EXPERT = 512
MOE_ROWS = 256

ALPHA = (2 * DEPTH) ** 0.25
NORM_EPS = 1e-6

TM = 256
VMEM_LIMIT = 56 * 1024 * 1024

_IN_SIZES = (512, 128, 128, 512, 128, 128, 512, 512, 256, 256, 16, 512, 4096)
_IN_OFF = np.concatenate([[0], np.cumsum(_IN_SIZES)]).astype(int)
(_AQ, _AK, _AV, _GQ, _GK, _GV, _SX, _SZ, _SB, _SC, _SDT, _SU, _GATE) = range(13)

ATTN_W = 1536
CONV_W = 1024
DT_W = 256
PROJ_W = ATTN_W + CONV_W + 512 + 512 + DT_W


def _params(sem=None):
    return pltpu.CompilerParams(dimension_semantics=sem, vmem_limit_bytes=VMEM_LIMIT)


def _ln(x):
    mu = jnp.mean(x, axis=-1, keepdims=True)
    xc = x - mu
    var = jnp.mean(xc * xc, axis=-1, keepdims=True)
    return xc * lax.rsqrt(var + NORM_EPS)


def _sigmoid(x):
    return 1.0 / (1.0 + jnp.exp(-x))


def _silu(x):
    return x * _sigmoid(x)


def _bdot(a, b):
    return jnp.dot(a.astype(BF16), b.astype(BF16), preferred_element_type=F32)


def _bdot_nt(a, b):
    return lax.dot_general(a.astype(BF16), b.astype(BF16), (((1,), (1,)), ((), ())),
                           preferred_element_type=F32)


def _split3(v):
    hi = v.astype(BF16)
    r1 = v - hi.astype(F32)
    mid = r1.astype(BF16)
    lo = (r1 - mid.astype(F32)).astype(BF16)
    return hi, mid, lo


def _mod_kernel(c_ref, w_ref, b_ref, o_ref):
    c = c_ref[...]
    o_ref[...] = _bdot(_silu(c), w_ref[...]) + b_ref[...]


def _modulation(cond, mod_w, mod_b):
    n_layer = mod_w.shape[0]
    rows = cond.shape[0]
    return pl.pallas_call(
        _mod_kernel,
        out_shape=jax.ShapeDtypeStruct((n_layer, rows, N_MOD * D_MODEL), F32),
        grid=(n_layer, N_MOD),
        in_specs=[
            pl.BlockSpec((rows, D_MODEL), lambda l, n: (0, 0)),
            pl.BlockSpec((None, D_MODEL, D_MODEL), lambda l, n: (l, 0, n)),
            pl.BlockSpec((None, 1, D_MODEL), lambda l, n: (l, 0, n)),
        ],
        out_specs=pl.BlockSpec((None, rows, D_MODEL), lambda l, n: (l, 0, n)),
        compiler_params=_params(("arbitrary", "arbitrary")),
        name="modulation",
    )(cond, mod_w, mod_b.reshape(n_layer, 1, N_MOD * D_MODEL))


def _mod_part(mod_ref, k):
    return mod_ref[:, k * D_MODEL:(k + 1) * D_MODEL]


def _mod_row(n_batch):
    return lambda b, j: (jnp.where(j == 0, n_batch, b), 0, 0)


def _chunk_swap():
    r = np.arange(TM)
    p = np.zeros((TM, TM), np.float32)
    p[r, (r % S5_T) * (TM // S5_T) + r // S5_T] = 1.0
    return jnp.asarray(p, BF16)


def _inproj_kernel(x_ref, mod_ref, w_ref, swap_ref, attn_ref, conv_ref, sz_ref, su_ref, dt_ref, u_ref):
    h = _ln(x_ref[...]) * (1.0 + _mod_part(mod_ref, 1)) + _mod_part(mod_ref, 0)
    hb = h.astype(BF16)
    off = 0
    for ref in (attn_ref, conv_ref, sz_ref, su_ref, dt_ref):
        width = ref.shape[-1]
        ref[...] = jnp.dot(hb, w_ref[:, off:off + width], preferred_element_type=F32)
        off += width
    by_step = jnp.dot(swap_ref[...], su_ref[...].astype(BF16), preferred_element_type=F32)
    n_ch = TM // S5_T
    for t in range(S5_T):
        rows = by_step[t * n_ch:(t + 1) * n_ch]
        for g in range(S5_GROUPS):
            u_ref[g, :, t * S5_GROUP:(t + 1) * S5_GROUP] = rows[:, g * S5_GROUP:(g + 1) * S5_GROUP].astype(BF16)


def _inproj(xcat, mods, w1):
    n_batch = xcat.shape[0]
    widths = (ATTN_W, CONV_W, 512, 512, DT_W)
    tile = lambda w: pl.BlockSpec((None, TM, w), lambda b, j: (b, j, 0))
    return pl.pallas_call(
        _inproj_kernel,
        out_shape=([jax.ShapeDtypeStruct((n_batch, TOK, w), F32) for w in widths]
                   + [jax.ShapeDtypeStruct((S5_GROUPS, S5_NCH, n_batch * S5_ROWW), BF16)]),
        grid=(n_batch, TOK // TM),
        in_specs=[
            tile(D_MODEL),
            pl.BlockSpec((None, 1, N_MOD * D_MODEL), _mod_row(n_batch)),
            pl.BlockSpec((D_MODEL, PROJ_W), lambda b, j: (0, 0)),
            pl.BlockSpec((TM, TM), lambda b, j: (0, 0)),
        ],
        out_specs=([tile(w) for w in widths]
                   + [pl.BlockSpec((S5_GROUPS, TM // S5_T, S5_ROWW), lambda b, j: (0, j, b))]),
        compiler_params=_params(("arbitrary", "arbitrary")),
        name="inproj",
    )(xcat, mods, w1, _chunk_swap())


def _rope(t, cos, sin):
    width = t.shape[-1]
    lane = lax.broadcasted_iota(jnp.int32, t.shape, t.ndim - 1)
    rot = jnp.where(lane % 32 < 16, -pltpu.roll(t, width - 16, t.ndim - 1), pltpu.roll(t, 16, t.ndim - 1))
    return t * cos + rot * sin


def _tile4(v):
    return jnp.concatenate([v, v, v, v], axis=1)


def _head_inv_rms(t, n_heads):
    t2 = t * t
    lane = lax.broadcasted_iota(jnp.int32, t.shape, 1)
    inv = jnp.zeros_like(t)
    for h in range(n_heads):
        ms = jnp.sum(t2[:, h * HEAD_DIM:(h + 1) * HEAD_DIM], axis=1, keepdims=True) * (1.0 / HEAD_DIM)
        inv = jnp.where(lane // HEAD_DIM == h, lax.rsqrt(ms + NORM_EPS), inv)
    return inv


def _stack_heads(qr, kv):
    parts = [qr[:, (kv * GRP + g) * HEAD_DIM:(kv * GRP + g + 1) * HEAD_DIM] for g in range(GRP)]
    return jnp.concatenate(parts, axis=0).astype(BF16)


QPS = 2
LOG2E = math.log2(math.e)


KEY_BLOCK = 128


def _attend(jobs, sink_ref, out_ref, kb_ref, vt_ref, s_ref, p_ref):
    width_q = GRP * QB
    halves = [slice(0, width_q // 2), slice(width_q // 2, width_q)]
    sub = KEY_BLOCK // 8
    for u, (qr, parts) in enumerate(jobs):
        for kv in range(N_KV):
            qs = _stack_heads(qr, kv)
            row = 0
            for key_idx, width, bias in parts:
                for hc in halves:
                    s = _bdot_nt(kb_ref[kv, key_idx, :], qs[hc])
                    if bias is not None:
                        s = s + jnp.concatenate([bias] * (width_q // 2 // QB), axis=1)
                    s_ref[u, kv, row:row + width, hc] = s
                row += width
    for u, (_, parts) in enumerate(jobs):
        n_keys = sum(width for _, width, _ in parts)
        sink_term = {}
        for kv in range(N_KV):
            for g in range(GRP):
                cols = slice(g * QB, (g + 1) * QB)
                mx = None
                for r0 in range(0, n_keys, KEY_BLOCK):
                    blk = jnp.max(s_ref[u, kv, r0:r0 + KEY_BLOCK, cols].reshape(sub, 8, QB), axis=0)
                    mx = blk if mx is None else jnp.maximum(mx, blk)
                m = jnp.max(mx, axis=0, keepdims=True)
                if sink_ref is not None:
                    m = jnp.maximum(m, sink_ref[kv, :, cols])
                    sink_term[kv, g] = jnp.exp2(sink_ref[kv, :, cols] - m)
                m_b = jnp.broadcast_to(m, (KEY_BLOCK, QB))
                for r0 in range(0, n_keys, KEY_BLOCK):
                    p_ref[u, kv, r0:r0 + KEY_BLOCK, cols] = jnp.exp2(
                        s_ref[u, kv, r0:r0 + KEY_BLOCK, cols] - m_b).astype(BF16)
        o_t = {}
        for kv in range(N_KV):
            for hi, hc in enumerate(halves):
                o = None
                row = 0
                for key_idx, width, _ in parts:
                    part = jnp.dot(vt_ref[kv, :, key_idx], p_ref[u, kv, row:row + width, hc],
                                   preferred_element_type=F32)
                    o = part if o is None else o + part
                    row += width
                o_t[kv, hi] = o
        for g in range(GRP):
            hi, gl = divmod(g, GRP // 2)
            cols = slice(gl * QB, (gl + 1) * QB)
            scaled = []
            for kv in range(N_KV):
                den = o_t[kv, hi][HEAD_DIM:HEAD_DIM + 1, cols]
                if sink_ref is not None:
                    den = den + sink_term[kv, g]
                scaled.append(o_t[kv, hi][0:HEAD_DIM, cols] / den)
            pair = jnp.concatenate(scaled, axis=0).T
            for kv in range(N_KV):
                h = kv * GRP + g
                out_ref[u * QB:(u + 1) * QB, h * HEAD_DIM:(h + 1) * HEAD_DIM] = (
                    pair[:, kv * HEAD_DIM:(kv + 1) * HEAD_DIM].astype(out_ref.dtype))


def _wattn_kernel(q_ref, k_ref, v_ref, cos_ref, sin_ref, sink_ref, bias_ref, out_ref, kb_ref, vt_ref, s_ref, p_ref):
    i = pl.program_id(1)

    @pl.when(i == 0)
    def _prep():
        kr = _rope(k_ref[...], cos_ref[...], sin_ref[...])
        v_t = v_ref[...].T
        for kv in range(N_KV):
            kb_ref[kv] = kr[:, kv * HEAD_DIM:(kv + 1) * HEAD_DIM].astype(BF16)
            vt_ref[kv, 0:HEAD_DIM, :] = v_t[kv * HEAD_DIM:(kv + 1) * HEAD_DIM, :].astype(BF16)
            vt_ref[kv, HEAD_DIM:HEAD_DIM + 16, :] = jnp.ones((16, TOK), BF16)

    row0 = pl.multiple_of(i * (QPS * QB), QPS * QB)
    cos = _tile4(cos_ref[pl.ds(row0, QPS * QB), :])
    sin = _tile4(sin_ref[pl.ds(row0, QPS * QB), :])
    qr = _rope(q_ref[...], cos, sin) * (HEAD_DIM ** -0.5 * LOG2E)
    qrs = [qr[u * QB:(u + 1) * QB] for u in range(QPS)]

    ctx_keys = (slice(0, CTX_LEN), CTX_LEN, None)

    @pl.when(i < CTX_LEN // (QPS * QB))
    def _ctx_queries():
        _attend([(q, [ctx_keys]) for q in qrs], sink_ref, out_ref, kb_ref, vt_ref, s_ref, p_ref)

    @pl.when(i >= CTX_LEN // (QPS * QB))
    def _latent_queries():
        jobs = []
        for u in range(QPS):
            j = i * QPS + u - CTX_LEN // QB
            band = jnp.clip(j - 1, 0, SEQ // QB - 3)
            start = pl.multiple_of(CTX_LEN + band * QB, QB)
            bias = bias_ref[j - band]
            jobs.append((qrs[u], [ctx_keys, (pl.ds(start, 3 * QB), 3 * QB, bias)]))
        _attend(jobs, sink_ref, out_ref, kb_ref, vt_ref, s_ref, p_ref)


def _gattn_kernel(q_ref, k_ref, v_ref, cos_ref, sin_ref, qw_ref, kw_ref, out_ref, kb_ref, vt_ref, s_ref, p_ref):
    i = pl.program_id(1)

    @pl.when(i == 0)
    def _prep():
        k = k_ref[...]
        kn = k * _head_inv_rms(k, N_KV) * kw_ref[...]
        kr = _rope(kn, cos_ref[...], sin_ref[...])
        v_t = v_ref[...].T
        for kv in range(N_KV):
            kb_ref[kv] = kr[:, kv * HEAD_DIM:(kv + 1) * HEAD_DIM].astype(BF16)
            vt_ref[kv, 0:HEAD_DIM, :] = v_t[kv * HEAD_DIM:(kv + 1) * HEAD_DIM, :].astype(BF16)
            vt_ref[kv, HEAD_DIM:HEAD_DIM + 16, :] = jnp.ones((16, TOK), BF16)

    row0 = pl.multiple_of(i * (QPS * QB), QPS * QB)
    cos = _tile4(cos_ref[pl.ds(row0, QPS * QB), :])
    sin = _tile4(sin_ref[pl.ds(row0, QPS * QB), :])
    q = q_ref[...]
    qn = q * _head_inv_rms(q, N_HEADS) * qw_ref[...]
    qr = _rope(qn, cos, sin) * (HEAD_DIM ** -0.5 * LOG2E)
    qrs = [qr[u * QB:(u + 1) * QB] for u in range(QPS)]

    @pl.when(i < CTX_LEN // (QPS * QB))
    def _ctx_queries():
        _attend([(q_u, [(slice(0, CTX_LEN), CTX_LEN, None)]) for q_u in qrs], None, out_ref, kb_ref, vt_ref,
                s_ref, p_ref)

    @pl.when(i >= CTX_LEN // (QPS * QB))
    def _latent_queries():
        _attend([(q_u, [(slice(0, TOK), TOK, None)]) for q_u in qrs], None, out_ref, kb_ref, vt_ref, s_ref, p_ref)


def _window_bias():
    q_pos = np.arange(QB)[None, None, :] + QB * np.arange(3)[:, None, None]
    k_pos = np.arange(3 * QB)[None, :, None]
    return jnp.asarray(np.where(np.abs(q_pos - k_pos) <= WINDOW, 0.0, -np.inf), F32)


def _attention(attn, cos, sin, sink_rows, qw, kw):
    n_batch = attn.shape[0]
    grid = (n_batch, TOK // (QPS * QB))
    q_spec = lambda blk: pl.BlockSpec((None, QPS * QB, 512), lambda b, i: (b, i, blk))
    kv_spec = lambda blk: pl.BlockSpec((None, TOK, 128), lambda b, i: (b, 0, blk))
    tab_spec = pl.BlockSpec((TOK, 128), lambda b, i: (0, 0))
    out_spec = pl.BlockSpec((None, QPS * QB, 512), lambda b, i: (b, i, 0))
    out_shape = jax.ShapeDtypeStruct((n_batch, TOK, 512), BF16)
    kv_scratch = [pltpu.VMEM((N_KV, TOK, HEAD_DIM), BF16), pltpu.VMEM((N_KV, HEAD_DIM + 16, TOK), BF16)]
    w_keys = CTX_LEN + 3 * QB
    ya = pl.pallas_call(
        _wattn_kernel, out_shape=out_shape, grid=grid,
        in_specs=[q_spec(0), kv_spec(8), kv_spec(9), tab_spec, tab_spec,
                  pl.BlockSpec((N_KV, 1, GRP * QB), lambda b, i: (0, 0, 0)),
                  pl.BlockSpec((3, 3 * QB, QB), lambda b, i: (0, 0, 0))],
        out_specs=out_spec,
        scratch_shapes=kv_scratch + [pltpu.VMEM((QPS, N_KV, w_keys, GRP * QB), F32),
                                     pltpu.VMEM((QPS, N_KV, w_keys, GRP * QB), BF16)],
        compiler_params=_params(("arbitrary", "arbitrary")), name="window_attention",
    )(attn, attn, attn, cos, sin, sink_rows, _window_bias())
    yg = pl.pallas_call(
        _gattn_kernel, out_shape=out_shape, grid=grid,
        in_specs=[q_spec(1), kv_spec(10), kv_spec(11), tab_spec, tab_spec,
                  pl.BlockSpec((1, 512), lambda b, i: (0, 0)),
                  pl.BlockSpec((1, 128), lambda b, i: (0, 0))],
        out_specs=out_spec,
        scratch_shapes=kv_scratch + [pltpu.VMEM((QPS, N_KV, TOK, GRP * QB), F32),
                                     pltpu.VMEM((QPS, N_KV, TOK, GRP * QB), BF16)],
        compiler_params=_params(("arbitrary", "arbitrary")), name="global_attention",
    )(attn, attn, attn, cos, sin, qw, kw)
    return ya, yg


def _ssd_chunk(s):
    r = s - N_CHUNK
    back = jnp.where(r == 0, 1, jnp.where(r == 1, 0, N_CHUNK + 1 - r))
    return jnp.where(s < N_CHUNK, s, back)


SSD_NB = 2


def _ssd_kernel(x_ref, prev_ref, next_ref, z_ref, dt_ref, cw_ref, cb_ref, dtb_ref, a_ref, d_ref, nw_ref, e_ref,
                out_ref, yf_ref, state_ref, y_ref):
    s = pl.program_id(1)
    chunk = _ssd_chunk(s)
    backward = s >= N_CHUNK

    @pl.when((s == 0) | (s == N_CHUNK))
    def _reset():
        state_ref[...] = jnp.zeros_like(state_ref)

    first = (chunk == 0) | (chunk == CTX_LEN // SSD_T)
    last = (chunk == CTX_LEN // SSD_T - 1) | (chunk == N_CHUNK - 1)
    row = lax.broadcasted_iota(jnp.int32, (SSD_T, CONV_W), 0)
    tr = lax.broadcasted_iota(jnp.int32, (SSD_T, SSD_T), 0)
    tc = lax.broadcasted_iota(jnp.int32, (SSD_T, SSD_T), 1)
    causal = jnp.where(backward, tc - tr, tr - tc) >= 0
    tri = jnp.where(causal, 1.0, 0.0).astype(BF16)
    neg_a = -jnp.exp(a_ref[...])
    expand = e_ref[...]

    def widen(v):
        return sum(jnp.dot(p, expand, preferred_element_type=F32) for p in _split3(v))

    xs = []
    for u in range(SSD_NB):
        xin = x_ref[u]
        prev = jnp.where(first, 0.0, prev_ref[u, 7:8, :])
        nxt = jnp.where(last, 0.0, next_ref[u, 0:1, :])
        xm1 = jnp.where(row == 0, prev, pltpu.roll(xin, 1, 0))
        xp1 = jnp.where(row == SSD_T - 1, nxt, pltpu.roll(xin, SSD_T - 1, 0))
        conv = cw_ref[0:1, :] * xm1 + cw_ref[1:2, :] * xin + cw_ref[2:3, :] * xp1 + cb_ref[...]
        xbc = _silu(conv)
        x = xbc[:, 0:SSD_WIDTH]
        xs.append(x)
        bmat = [xbc[:, SSD_WIDTH + g * SSD_N:SSD_WIDTH + (g + 1) * SSD_N] for g in range(SSD_GROUPS)]
        cmat = [xbc[:, SSD_WIDTH + (SSD_GROUPS + g) * SSD_N:SSD_WIDTH + (SSD_GROUPS + g + 1) * SSD_N]
                for g in range(SSD_GROUPS)]

        dtv = dt_ref[u] + dtb_ref[...]
        dt = jnp.maximum(dtv, 0.0) + jnp.log(1.0 + jnp.exp(-jnp.abs(dtv)))
        hi, mid, lo = _split3(dt * neg_a)
        cs = (jnp.dot(tri, hi, preferred_element_type=F32) + jnp.dot(tri, mid, preferred_element_type=F32)
              + jnp.dot(tri, lo, preferred_element_type=F32))
        cs_t = cs.T

        dt_x = widen(dt)
        cs_x = widen(cs)
        tot_x = jnp.where(backward, cs_x[0:1, :], cs_x[SSD_T - 1:SSD_T, :])
        ecs_x = jnp.exp(cs_x)
        etot_x = jnp.exp(tot_x)
        xd = x * dt_x
        xd_b = xd.astype(BF16)
        xd_end = (xd * jnp.exp(tot_x - cs_x)).astype(BF16)

        gw = SSD_WIDTH // SSD_GROUPS
        for g in range(SSD_GROUPS):
            glanes = slice(g * gw, (g + 1) * gw)
            cb = _bdot_nt(cmat[g], bmat[g])
            st = state_ref[u, g]
            y_off = ecs_x[:, glanes] * jnp.dot(cmat[g].astype(BF16), st.astype(BF16), preferred_element_type=F32)
            state_ref[u, g] = etot_x[:, glanes] * st + jnp.dot(bmat[g].T.astype(BF16), xd_end[:, glanes],
                                                               preferred_element_type=F32)
            for j in range(SSD_HEADS // SSD_GROUPS):
                h = g * (SSD_HEADS // SSD_GROUPS) + j
                lanes = slice(h * SSD_P, (h + 1) * SSD_P)
                seg = jnp.exp(jnp.where(causal, cs[:, h:h + 1] - cs_t[h:h + 1, :], -jnp.inf))
                y_ref[u, :, lanes] = (jnp.dot((cb * seg).astype(BF16), xd_b[:, lanes], preferred_element_type=F32)
                                      + y_off[:, j * SSD_P:(j + 1) * SSD_P])

    row0 = pl.multiple_of(chunk * SSD_T, SSD_T)

    @pl.when(jnp.logical_not(backward))
    def _keep():
        for u in range(SSD_NB):
            yf_ref[u, pl.ds(row0, SSD_T), :] = y_ref[u]

    @pl.when(backward)
    def _finish():
        for u in range(SSD_NB):
            ytot = yf_ref[u, pl.ds(row0, SSD_T), :] + y_ref[u] + d_ref[...] * xs[u]
            gated = ytot * _silu(z_ref[u])
            ms = jnp.mean(gated * gated, axis=1, keepdims=True)
            out_ref[u] = (gated * lax.rsqrt(ms + NORM_EPS) * nw_ref[...]).astype(out_ref.dtype)


def _ssd(conv_in, sz, dt, conv_w, conv_b, dt_bias, a_log, d_exp, norm_w):
    n_batch = conv_in.shape[0]
    assert n_batch % SSD_NB == 0
    halo = SSD_T // 8
    n_halo = TOK // 8

    def chunk_map(b, s):
        return (b, _ssd_chunk(s), 0)

    def out_map(b, s):
        return (b, jnp.where(s < N_CHUNK, 1, _ssd_chunk(s)), 0)

    const = lambda shape: pl.BlockSpec(shape, lambda b, s: tuple(0 for _ in shape))
    dir_spec = pl.BlockSpec((None, 1, 128), lambda b, s: (s // N_CHUNK, 0, 0))
    expand = jnp.asarray(np.repeat(np.eye(128, SSD_HEADS), SSD_P, axis=1), BF16)
    return pl.pallas_call(
        _ssd_kernel,
        out_shape=jax.ShapeDtypeStruct((n_batch, TOK, SSD_WIDTH), BF16),
        grid=(n_batch // SSD_NB, 2 * N_CHUNK),
        in_specs=[
            pl.BlockSpec((SSD_NB, SSD_T, CONV_W), chunk_map),
            pl.BlockSpec((SSD_NB, 8, CONV_W), lambda b, s: (b, jnp.maximum(_ssd_chunk(s) * halo - 1, 0), 0)),
            pl.BlockSpec((SSD_NB, 8, CONV_W), lambda b, s: (b, jnp.minimum((_ssd_chunk(s) + 1) * halo, n_halo - 1), 0)),
            pl.BlockSpec((SSD_NB, SSD_T, SSD_WIDTH), chunk_map),
            pl.BlockSpec((SSD_NB, SSD_T, 128), lambda b, s: (b, _ssd_chunk(s), s // N_CHUNK)),
            const((3, CONV_W)), const((1, CONV_W)), dir_spec, dir_spec,
            const((1, SSD_WIDTH)), const((1, SSD_WIDTH)), const((128, SSD_WIDTH)),
        ],
        out_specs=pl.BlockSpec((SSD_NB, SSD_T, SSD_WIDTH), out_map),
        scratch_shapes=[pltpu.VMEM((SSD_NB, TOK, SSD_WIDTH), F32),
                        pltpu.VMEM((SSD_NB, SSD_GROUPS, SSD_N, SSD_WIDTH // SSD_GROUPS), F32),
                        pltpu.VMEM((SSD_NB, SSD_T, SSD_WIDTH), F32)],
        compiler_params=_params(("arbitrary", "arbitrary")),
        name="ssd",
    )(conv_in, conv_in, conv_in, sz, dt, conv_w, conv_b, dt_bias, a_log, d_exp, norm_w, expand)


S5_SW = 4 * 128


def _s5_kernel(u_ref, m_ref, h_ref, g_ref, a_ref, y_ref, loc_ref, prev_ref, *, n_batch):
    u = u_ref[...]
    loc_ref[...] = jnp.dot(u, h_ref[...], preferred_element_type=F32)
    a = [a_ref[k:k + 1, :] for k in range(4)]

    def step(k, carry):
        c_f = k
        c_b = jnp.where(k < S5_CTX_CH, S5_CTX_CH - 1 - k, S5_NCH + S5_CTX_CH - 1 - k)
        new = []
        for d, c in enumerate((c_f, c_b)):
            rows = pl.ds(pl.multiple_of(c * n_batch, n_batch), n_batch)
            re_l, im_l = slice(2 * d * 128, (2 * d + 1) * 128), slice((2 * d + 1) * 128, (2 * d + 2) * 128)
            s_re, s_im = carry[2 * d], carry[2 * d + 1]
            prev_ref[rows, re_l] = s_re
            prev_ref[rows, im_l] = s_im
            a_re, a_im = a[2 * d], a[2 * d + 1]
            new += [a_re * s_re - a_im * s_im + loc_ref[rows, re_l], a_re * s_im + a_im * s_re + loc_ref[rows, im_l]]
        return tuple(new)

    zero = jnp.zeros((n_batch, 128), F32)
    lax.fori_loop(0, S5_NCH, step, (zero,) * 4, unroll=2)
    y = jnp.dot(u, m_ref[...], preferred_element_type=F32)
    y = y + jnp.dot(prev_ref[...].astype(BF16), g_ref[...], preferred_element_type=F32)
    y_ref[...] = y.astype(y_ref.dtype)


def _s5(u_g, mats, n_batch):
    m_all, h_all, g_all, a16 = mats
    rows = u_g.shape[1]
    grp = lambda shape: pl.BlockSpec((None,) + shape, lambda g: (g,) + tuple(0 for _ in shape))
    return pl.pallas_call(
        functools.partial(_s5_kernel, n_batch=n_batch),
        out_shape=jax.ShapeDtypeStruct((S5_GROUPS, rows, S5_ROWW), BF16),
        grid=(S5_GROUPS,),
        in_specs=[grp((rows, S5_ROWW)), grp((S5_ROWW, S5_ROWW)), grp((S5_ROWW, S5_SW)), grp((S5_SW, S5_ROWW)),
                  grp((4, 128))],
        out_specs=grp((rows, S5_ROWW)),
        scratch_shapes=[pltpu.VMEM((rows, S5_SW), F32), pltpu.VMEM((rows, S5_SW), F32)],
        compiler_params=_params(("arbitrary",)),
        name="s5",
    )(u_g, m_all, h_all, g_all, a16)


def _cmul(a, b):
    return a[0] * b[0] - a[1] * b[1], a[0] * b[1] + a[1] * b[0]


def _s5_matrices(a_re, a_im, log_dt, b_re, b_im, c_re, c_im):
    hp = lax.Precision.HIGHEST
    t = jnp.arange(S5_T + 1, dtype=F32)
    m_sum = 0.0
    h_all, g_all, a16_all = [], [], []
    c = (c_re.astype(F32), c_im.astype(F32))
    for direction in range(2):
        are = jnp.minimum(a_re[direction].astype(F32), -1e-4)
        aim = a_im[direction].astype(F32)
        dt = jnp.exp(log_dt[direction].astype(F32))[:, None]
        mag = jnp.exp(t[:, None, None] * (are * dt)[None])
        ang = t[:, None, None] * (aim * dt)[None]
        pw = (mag * jnp.cos(ang), mag * jnp.sin(ang))
        num = (pw[0][1] - 1.0, pw[1][1])
        den = are * are + aim * aim
        coef = ((num[0] * are + num[1] * aim) / den, (num[1] * are - num[0] * aim) / den)
        bbar = _cmul((coef[0][..., None], coef[1][..., None]), (b_re.astype(F32), b_im.astype(F32)))
        pb = _cmul((pw[0][:S5_T, :, :, None], pw[1][:S5_T, :, :, None]), (bbar[0][None], bbar[1][None]))
        taps = (jnp.einsum('gon,tgni->tgoi', c[0], pb[0], precision=hp)
                - jnp.einsum('gon,tgni->tgoi', c[1], pb[1], precision=hp))
        ti = jnp.arange(S5_T)
        lag = (ti[None, :] - ti[:, None]) if direction == 0 else (ti[:, None] - ti[None, :])
        k_full = taps[jnp.clip(lag, 0, S5_T - 1)]
        k_full = jnp.where((lag >= 0)[:, :, None, None, None], k_full, 0.0)
        m_sum = m_sum + k_full.transpose(2, 0, 4, 1, 3).reshape(S5_GROUPS, S5_ROWW, S5_ROWW)
        e_in = (S5_T - 1 - ti) if direction == 0 else ti
        hb = _cmul((pw[0][e_in][..., None], pw[1][e_in][..., None]), (bbar[0][None], bbar[1][None]))
        zpad = jnp.zeros_like(hb[0])
        h_mat = jnp.concatenate([hb[0], zpad, hb[1], zpad], axis=2)
        h_all.append(h_mat.transpose(1, 0, 3, 2).reshape(S5_GROUPS, S5_ROWW, 4 * S5_STATE))
        e_out = (ti + 1) if direction == 0 else (S5_T - ti)
        cp = _cmul((c[0][None], c[1][None]),
                   (pw[0][e_out][:, :, None, :], pw[1][e_out][:, :, None, :]))
        zpad = jnp.zeros_like(cp[0])
        g_mat = jnp.concatenate([cp[0], zpad, -cp[1], zpad], axis=3)
        g_all.append(g_mat.transpose(1, 3, 0, 2).reshape(S5_GROUPS, 4 * S5_STATE, S5_ROWW))
        zrow = jnp.zeros_like(pw[0][S5_T])
        a16_all += [jnp.concatenate([pw[0][S5_T], zrow], axis=1), jnp.concatenate([pw[1][S5_T], zrow], axis=1)]
    return (m_sum.astype(BF16), jnp.concatenate(h_all, axis=2).astype(BF16),
            jnp.concatenate(g_all, axis=1).astype(BF16), jnp.stack(a16_all, axis=1))


def _route(logits_t, bias, base, before):
    scores = [_sigmoid(logits_t[j * N_EXPERT_GROUPS:(j + 1) * N_EXPERT_GROUPS]) for j in range(PER_GROUP)]
    sel = [scores[j] + bias[j * N_EXPERT_GROUPS:(j + 1) * N_EXPERT_GROUPS] for j in range(PER_GROUP)]
    hi1, lo1 = jnp.maximum(sel[0], sel[1]), jnp.minimum(sel[0], sel[1])
    hi2, lo2 = jnp.maximum(sel[2], sel[3]), jnp.minimum(sel[2], sel[3])
    group_score = jnp.maximum(hi1, hi2) + jnp.maximum(jnp.minimum(hi1, hi2), jnp.maximum(lo1, lo2))
    gid = lax.broadcasted_iota(jnp.int32, group_score.shape, 0)
    best = jnp.max(group_score, axis=0, keepdims=True)
    grp = jnp.min(jnp.where(group_score == best, gid, N_EXPERT_GROUPS), axis=0, keepdims=True)
    pick = gid == grp
    v = [jnp.sum(jnp.where(pick, sel[j], 0.0), axis=0, keepdims=True) for j in range(PER_GROUP)]
    sc = [jnp.sum(jnp.where(pick, scores[j], 0.0), axis=0, keepdims=True) for j in range(PER_GROUP)]
    rank = []
    for j in range(PER_GROUP):
        r = jnp.zeros_like(grp)
        for i in range(PER_GROUP):
            if i < j:
                r = r + jnp.where(v[i] >= v[j], 1, 0)
            elif i > j:
                r = r + jnp.where(v[i] > v[j], 1, 0)
        rank.append(r)
    e, w, locs = [], [], []
    for k in range(TOP_K):
        loc = sum(jnp.where(rank[j] == k, j, 0) for j in range(PER_GROUP))
        locs.append(loc)
        e.append((grp * PER_GROUP + loc).astype(F32))
        w.append(sum(jnp.where(rank[j] == k, sc[j], 0.0) for j in range(PER_GROUP)))
    wsum = w[0] + w[1]
    chosen = [jnp.where(pick & ((locs[0] == j) | (locs[1] == j)), 1.0, 0.0) for j in range(PER_GROUP)]
    onehot = jnp.concatenate(chosen, axis=0)
    seen = jnp.dot(onehot.astype(BF16), before, preferred_element_type=F32) + base
    pos = []
    for k in range(TOP_K):
        hit = sum(jnp.where(pick & (locs[k] == j), seen[j * N_EXPERT_GROUPS:(j + 1) * N_EXPERT_GROUPS], 0.0)
                  for j in range(PER_GROUP))
        pos.append(jnp.sum(hit, axis=0, keepdims=True))
    rows = [e[0], e[1], w[0] / wsum, w[1] / wsum, pos[0], pos[1]]
    route = jnp.concatenate(rows + [jnp.zeros_like(wsum)] * (8 - len(rows)), axis=0)
    return route, base + jnp.sum(onehot, axis=1, keepdims=True)


def _gelu_tanh(x):
    return 0.5 * x * (1.0 + jnp.tanh(math.sqrt(2.0 / math.pi) * (x + 0.044715 * (x * x * x))))


def _merge_kernel(x_ref, mod_ref, ya_ref, ys_ref, yg_ref, y5_ref, su_ref, s5d_ref, wg_ref, wb_ref, wglu_ref, wout_ref,
                  lng_ref, lnb_ref, rw_ref, rb_ref, before_ref, swap_ref, x1_ref, h2_ref, route_ref, count_ref,
                  base_ref, y5s_ref):
    @pl.when((pl.program_id(0) == 0) & (pl.program_id(1) == 0))
    def _reset():
        base_ref[...] = jnp.zeros_like(base_ref)

    x = x_ref[...]
    hb = (_ln(x) * (1.0 + _mod_part(mod_ref, 1)) + _mod_part(mod_ref, 0)).astype(BF16)
    n_ch = TM // S5_T
    for t in range(S5_T):
        for g in range(S5_GROUPS):
            y5s_ref[t * n_ch:(t + 1) * n_ch, g * S5_GROUP:(g + 1) * S5_GROUP] = (
                y5_ref[g, :, t * S5_GROUP:(t + 1) * S5_GROUP])
    y5_scan = jnp.dot(swap_ref[...], y5s_ref[...], preferred_element_type=F32)
    y5_pre = y5_scan + s5d_ref[...] * su_ref[...]
    glu = jnp.dot(_gelu_tanh(y5_pre).astype(BF16), wglu_ref[...], preferred_element_type=F32)
    y5 = (glu[:, 0:S5_WIDTH] * _sigmoid(glu[:, S5_WIDTH:2 * S5_WIDTH])).astype(BF16)
    acc = jnp.zeros(x.shape, F32)
    for n, y in enumerate((ya_ref[...], ys_ref[...], yg_ref[...], y5)):
        gate = _sigmoid(jnp.dot(hb, wg_ref[:, n * D_MODEL:(n + 1) * D_MODEL], preferred_element_type=F32))
        acc = acc + gate * jnp.dot(y, wb_ref[n], preferred_element_type=F32)
    mix = jnp.dot(acc.astype(BF16), wout_ref[...], preferred_element_type=F32)
    x1 = _ln(ALPHA * x + _mod_part(mod_ref, 2) * mix) * lng_ref[...] + lnb_ref[...]
    x1_ref[...] = x1
    h2 = (_ln(x1) * (1.0 + _mod_part(mod_ref, 4)) + _mod_part(mod_ref, 3)).astype(BF16)
    h2_ref[...] = h2
    route, base = _route(_bdot_nt(rw_ref[...], h2), rb_ref[...], base_ref[...], before_ref[...])
    route_ref[...] = route
    base_ref[...] = base
    count_ref[...] = jnp.broadcast_to(base, count_ref.shape)


def _merge(xcat, mods, ya, ys, yg, y5, su, s5_d, wg, wb, wglu, wout, ln_g, ln_b, rw_t, rb):
    n_batch = xcat.shape[0]
    n_tiles = TOK // TM
    tile = lambda w: pl.BlockSpec((None, TM, w), lambda b, j: (b, j, 0))
    const = lambda shape: pl.BlockSpec(shape, lambda b, j: tuple(0 for _ in shape))
    before = jnp.asarray(np.triu(np.ones((TM, TM)), 1), BF16)
    return pl.pallas_call(
        _merge_kernel,
        out_shape=[jax.ShapeDtypeStruct((n_batch, TOK, D_MODEL), F32),
                   jax.ShapeDtypeStruct((n_batch, TOK, D_MODEL), BF16),
                   jax.ShapeDtypeStruct((8, n_batch * TOK), F32),
                   jax.ShapeDtypeStruct((N_EXPERTS, 128), F32)],
        grid=(n_batch, n_tiles),
        in_specs=[tile(D_MODEL), pl.BlockSpec((None, 1, N_MOD * D_MODEL), _mod_row(n_batch)),
                  tile(512), tile(512), tile(512),
                  pl.BlockSpec((S5_GROUPS, TM // S5_T, S5_ROWW), lambda b, j: (0, j, b)),
                  tile(512), const((1, S5_WIDTH)),
                  const((D_MODEL, N_BRANCH * D_MODEL)), const((N_BRANCH, 512, D_MODEL)),
                  const((S5_WIDTH, 2 * S5_WIDTH)), const((D_MODEL, D_MODEL)),
                  const((1, D_MODEL)), const((1, D_MODEL)),
                  const((N_EXPERTS, D_MODEL)), const((N_EXPERTS, 1)), const((TM, TM)), const((TM, TM))],
        out_specs=[tile(D_MODEL), tile(D_MODEL),
                   pl.BlockSpec((8, TM), lambda b, j: (0, b * n_tiles + j)),
                   const((N_EXPERTS, 128))],
        scratch_shapes=[pltpu.VMEM((N_EXPERTS, 1), F32), pltpu.VMEM((TM, S5_WIDTH), BF16)],
        compiler_params=_params(("arbitrary", "arbitrary")),
        name="merge",
    )(xcat, mods, ya, ys, yg, y5, su, s5_d, wg, wb, wglu, wout, ln_g, ln_b, rw_t, rb, before, _chunk_swap())


def _moe_kernel(be_ref, nu_ref, x_ref, wg_ref, wu_ref, wd_ref, y_ref, wgu_s, wd_s):
    i = pl.program_id(0)
    used = i < nu_ref[0]

    @pl.when(used & ((i == 0) | (be_ref[i] != be_ref[jnp.maximum(i - 1, 0)])))
    def _cast():
        wgu_s[:, 0:D_EXPERT] = wg_ref[...].astype(BF16)
        wgu_s[:, D_EXPERT:2 * D_EXPERT] = wu_ref[...].astype(BF16)
        wd_s[...] = wd_ref[...].astype(BF16)

    @pl.when(used)
    def _block():
        gu = jnp.dot(x_ref[...], wgu_s[...], preferred_element_type=F32)
        mid = _silu(gu[:, 0:D_EXPERT]) * gu[:, D_EXPERT:2 * D_EXPERT]
        y_ref[...] = jnp.dot(mid.astype(BF16), wd_s[...], preferred_element_type=F32).astype(y_ref.dtype)


def _moe_experts(block_expert, n_used, xs, w_gate, w_up, w_down, layer):
    n_blocks = block_expert.shape[0]
    return pl.pallas_call(
        _moe_kernel,
        out_shape=jax.ShapeDtypeStruct((n_blocks * MOE_ROWS, D_MODEL), BF16),
        grid_spec=pltpu.PrefetchScalarGridSpec(
            num_scalar_prefetch=2, grid=(n_blocks,),
            in_specs=[pl.BlockSpec((MOE_ROWS, D_MODEL), lambda i, be, nu: (i, 0)),
                      pl.BlockSpec((None, None, D_MODEL, D_EXPERT), lambda i, be, nu: (layer, be[i], 0, 0)),
                      pl.BlockSpec((None, None, D_MODEL, D_EXPERT), lambda i, be, nu: (layer, be[i], 0, 0)),
                      pl.BlockSpec((None, None, D_EXPERT, D_MODEL), lambda i, be, nu: (layer, be[i], 0, 0))],
            out_specs=pl.BlockSpec((MOE_ROWS, D_MODEL), lambda i, be, nu: (i, 0)),
            scratch_shapes=[pltpu.VMEM((D_MODEL, 2 * D_EXPERT), BF16), pltpu.VMEM((D_EXPERT, D_MODEL), BF16)]),
        compiler_params=_params(("arbitrary",)),
        name="moe_experts",
    )(block_expert, n_used, xs, w_gate, w_up, w_down)


def _final_kernel(x_ref, mod_ref, f0_ref, f1_ref, route_ref, lng_ref, lnb_ref, o_ref):
    r = lax.broadcasted_iota(jnp.int32, (TM, TM), 0)
    c = lax.broadcasted_iota(jnp.int32, (TM, TM), 1)
    w = [jnp.sum(jnp.where(r == c, route_ref[TOP_K + k:TOP_K + k + 1, :], 0.0), axis=1, keepdims=True)
         for k in range(TOP_K)]
    ffn = w[0] * f0_ref[...].astype(F32) + w[1] * f1_ref[...].astype(F32)
    o_ref[...] = _ln(ALPHA * x_ref[...] + _mod_part(mod_ref, 5) * ffn) * lng_ref[...] + lnb_ref[...]


def _final(x1, mods, f, route, ln_g, ln_b):
    n_batch = x1.shape[0]
    n_tiles = TOK // TM
    tile = pl.BlockSpec((None, TM, D_MODEL), lambda b, j: (b, j, 0))
    const = pl.BlockSpec((1, D_MODEL), lambda b, j: (0, 0))
    return pl.pallas_call(
        _final_kernel,
        out_shape=jax.ShapeDtypeStruct(x1.shape, F32),
        grid=(n_batch, n_tiles),
        in_specs=[tile, pl.BlockSpec((None, 1, N_MOD * D_MODEL), _mod_row(n_batch)),
                  pl.BlockSpec((None, None, TM, D_MODEL), lambda b, j: (0, b, j, 0)),
                  pl.BlockSpec((None, None, TM, D_MODEL), lambda b, j: (1, b, j, 0)),
                  pl.BlockSpec((8, TM), lambda b, j: (0, b * n_tiles + j)),
                  const, const],
        out_specs=tile,
        compiler_params=_params(("arbitrary", "arbitrary")),
        name="ffn_residual",
    )(x1, mods, f, f, route, ln_g, ln_b)


def _lookup(table, idx):
    ids = jnp.arange(table.shape[0], dtype=jnp.int32).reshape((-1,) + (1,) * idx.ndim)
    return jnp.sum(jnp.where(idx[None] == ids, table.reshape(ids.shape), 0), axis=0)


def _dispatch(route, counts_rows):
    n_tok = route.shape[1]
    n_assign = n_tok * TOP_K
    n_blocks = -(-(n_assign + N_EXPERTS * (MOE_ROWS - 1)) // MOE_ROWS)
    cap = n_blocks * MOE_ROWS
    counts = counts_rows[:, 0].astype(jnp.int32).reshape(PER_GROUP, N_EXPERT_GROUPS).T.reshape(N_EXPERTS)
    padded = (counts + MOE_ROWS - 1) // MOE_ROWS * MOE_ROWS
    pad_end = jnp.cumsum(padded)
    pad_start = pad_end - padded
    expert = route[0:TOP_K].astype(jnp.int32)
    pos = _lookup(pad_start, expert) + route[2 * TOP_K:3 * TOP_K].astype(jnp.int32)
    block_start = jnp.arange(n_blocks, dtype=jnp.int32) * MOE_ROWS
    block_expert = jnp.minimum(jnp.sum((pad_end[None, :] <= block_start[:, None]).astype(jnp.int32), axis=1),
                               N_EXPERTS - 1)
    n_used = (pad_end[-1:] // MOE_ROWS).astype(jnp.int32)
    gap = padded - counts
    gap_end = jnp.cumsum(gap)
    gap_first = jnp.concatenate([pad_start + counts - (gap_end - gap), pad_end[-1:] - gap_end[-1:]])
    k = jnp.arange(cap - n_assign, dtype=jnp.int32)
    owner = jnp.sum((gap_end[None, :] <= k[:, None]).astype(jnp.int32), axis=1)
    free_slot = _lookup(gap_first, owner) + k
    tok = jnp.broadcast_to(jnp.arange(n_tok, dtype=jnp.int32), (TOP_K, n_tok)).reshape(-1)
    keys = jnp.concatenate([pos.reshape(-1), free_slot])
    vals = jnp.concatenate([tok, k % n_tok])
    _, slot_tok = lax.sort((keys, vals), num_keys=1)
    return slot_tok, pos, block_expert, n_used


def _col(w, part):
    return w[:, _IN_OFF[part]:_IN_OFF[part + 1]]


def _proj_weights(w_in):
    zeros = jnp.zeros((D_MODEL, 128 - SSD_HEADS), w_in.dtype)
    dt = _col(w_in, _SDT)
    parts = [_col(w_in, p) for p in (_AQ, _GQ, _AK, _AV, _GK, _GV, _SX, _SB, _SC, _SZ, _SU)]
    parts += [dt[:, 0:SSD_HEADS], zeros, dt[:, SSD_HEADS:2 * SSD_HEADS], zeros]
    return jnp.concatenate(parts, axis=1).astype(BF16), _col(w_in, _GATE).astype(BF16)


def _rope_tables():
    rows = SEQ // GRID_W
    row = jnp.repeat(jnp.arange(rows, dtype=F32), GRID_W)
    col = jnp.tile(jnp.arange(GRID_W, dtype=F32), rows)
    axis_dim = HEAD_DIM // 2
    inv_freq = ROPE_THETA ** (-jnp.arange(0, axis_dim, 2, dtype=F32) / axis_dim)
    ang_r = row[:, None] * inv_freq
    ang_c = col[:, None] * inv_freq
    ang = jnp.concatenate([ang_r, ang_r, ang_c, ang_c], axis=-1)
    cos = jnp.concatenate([jnp.ones((CTX_LEN, HEAD_DIM), F32), jnp.cos(ang)], axis=0)
    sin = jnp.concatenate([jnp.zeros((CTX_LEN, HEAD_DIM), F32), jnp.sin(ang)], axis=0)
    return jnp.tile(cos, (1, 2)), jnp.tile(sin, (1, 2))


def _pad_lanes(v, width=128):
    return jnp.pad(v, ((0, 0), (0, width - v.shape[-1])))


def kernel(x, c, ctx, c_ctx, mod_w, mod_b, w_in, wa_sink, ga_q_norm, ga_k_norm, ssd_conv_w, ssd_conv_b, ssd_dt_bias, ssd_a_log, ssd_d, ssd_norm_w, s5_a_re, s5_a_im, s5_log_dt, s5_b_re, s5_b_im, s5_c_re, s5_c_im, s5_d, s5_w_glu, w_branch, w_out, ln1_g, ln1_b, ln2_g, ln2_b, router_w, router_bias, moe_w_gate, moe_w_up, moe_w_down):
    n_batch = x.shape[0]
    n_tok = n_batch * TOK
    xcat = jnp.concatenate([ctx, x], axis=1)
    mod_rows = -(-(n_batch + 1) // 8) * 8
    cond = jnp.zeros((mod_rows, D_MODEL), F32).at[:n_batch].set(c).at[n_batch].set(c_ctx)
    mods_all = _modulation(cond, mod_w, mod_b)
    cos, sin = _rope_tables()

    perm = np.array([g * PER_GROUP + j for j in range(PER_GROUP) for g in range(N_EXPERT_GROUPS)])
    rw_t = router_w.T[perm].astype(BF16)
    rb = router_bias.astype(F32)[perm].reshape(N_EXPERTS, 1)

    for layer in range(DEPTH):
        mods = mods_all[layer].reshape(mod_rows, 1, N_MOD * D_MODEL)
        w1, wg = _proj_weights(w_in[layer])
        attn, conv_in, sz, su, dt, u_g = _inproj(xcat, mods, w1)

        sink_rows = jnp.repeat(wa_sink[layer].astype(F32).reshape(N_KV, GRP) * LOG2E, QB,
                               axis=1).reshape(N_KV, 1, GRP * QB)
        qw = jnp.tile(ga_q_norm[layer].astype(F32), N_HEADS).reshape(1, 512)
        kw = jnp.tile(ga_k_norm[layer].astype(F32), N_KV).reshape(1, 128)
        ya, yg = _attention(attn, cos, sin, sink_rows, qw, kw)

        dt_bias = _pad_lanes(ssd_dt_bias[layer].astype(F32).reshape(2, SSD_HEADS)).reshape(2, 1, 128)
        a_log = _pad_lanes(ssd_a_log[layer].astype(F32)).reshape(2, 1, 128)
        d_exp = jnp.repeat(ssd_d[layer].astype(F32), SSD_P).reshape(1, SSD_WIDTH)
        ys = _ssd(conv_in, sz, dt, ssd_conv_w[layer].astype(F32), ssd_conv_b[layer].astype(F32).reshape(1, CONV_W),
                  dt_bias, a_log, d_exp, ssd_norm_w[layer].astype(F32).reshape(1, SSD_WIDTH))

        mats = _s5_matrices(s5_a_re[layer], s5_a_im[layer], s5_log_dt[layer], s5_b_re[layer], s5_b_im[layer],
                            s5_c_re[layer], s5_c_im[layer])
        y5 = _s5(u_g.reshape(S5_GROUPS, S5_NCH * n_batch, S5_ROWW), mats, n_batch)
        y5 = y5.reshape(S5_GROUPS, S5_NCH, n_batch * S5_ROWW)

        x1, h2, route, counts = _merge(
            xcat, mods, ya, ys, yg, y5, su, s5_d[layer].astype(F32).reshape(1, S5_WIDTH), wg,
            w_branch[layer].astype(BF16), s5_w_glu[layer].astype(BF16), w_out[layer].astype(BF16),
            ln1_g[layer].reshape(1, D_MODEL), ln1_b[layer].reshape(1, D_MODEL), rw_t, rb)

        slot_tok, pos, block_expert, n_used = _dispatch(route, counts)
        xs = h2.reshape(n_tok, D_MODEL)[slot_tok]
        y_slots = _moe_experts(block_expert, n_used, xs, moe_w_gate, moe_w_up, moe_w_down, layer)
        f = y_slots[pos.reshape(-1)].reshape(TOP_K, n_batch, TOK, D_MODEL)
        xcat = _final(x1, mods, f, route, ln2_g[layer].reshape(1, D_MODEL), ln2_b[layer].reshape(1, D_MODEL))
    return xcat[:, CTX_LEN:, :]
```

```python
import functools
import math

import jax
import jax.numpy as jnp
import numpy as np
from jax import lax
from jax.experimental import pallas as pl
from jax.experimental.pallas import tpu as pltpu

F32 = jnp.float32
BF16 = jnp.bfloat16

D_MODEL = 1024
DEPTH = 4
GRID_W = 64
CTX_LEN = 256
SEQ = 2048
TOK = CTX_LEN + SEQ
N_MOD = 6
HEAD_DIM = 64
N_BRANCH = 4
N_HEADS = 8
N_KV = 2
GRP = N_HEADS // N_KV
WINDOW = 128
QB = 128
ROPE_THETA = 10000.0

SSD_HEADS = 8
SSD_P = 64
SSD_GROUPS = 2
SSD_N = 128
SSD_T = 128
SSD_WIDTH = SSD_HEADS * SSD_P
SSD_CONV_CH = SSD_WIDTH + 2 * SSD_GROUPS * SSD_N
N_CHUNK = TOK // SSD_T

S5_GROUP = 16
S5_WIDTH = 512
S5_GROUPS = S5_WIDTH // S5_GROUP
S5_STATE = 64
S5_T = 16
S5_ROWW = S5_T * S5_GROUP
S5_NCH = TOK // S5_T
S5_CTX_CH = CTX_LEN // S5_T

N_EXPERTS = 32
N_EXPERT_GROUPS = 8
PER_GROUP = N_EXPERTS // N_EXPERT_GROUPS
TOP_K = 2
D_EXPERT = 512
MOE_ROWS = 256

ALPHA = (2 * DEPTH) ** 0.25
NORM_EPS = 1e-6

TM = 256
VMEM_LIMIT = 56 * 1024 * 1024

_IN_SIZES = (512, 128, 128, 512, 128, 128, 512, 512, 256, 256, 16, 512, 4096)
_IN_OFF = np.concatenate([[0], np.cumsum(_IN_SIZES)]).astype(int)
(_AQ, _AK, _AV, _GQ, _GK, _GV, _SX, _SZ, _SB, _SC, _SDT, _SU, _GATE) = range(13)

ATTN_W = 1536
CONV_W = 1024
DT_W = 256
PROJ_W = ATTN_W + CONV_W + 512 + 512 + DT_W


def _params(sem=None):
    return pltpu.CompilerParams(dimension_semantics=sem, vmem_limit_bytes=VMEM_LIMIT)


def _ln(x):
    mu = jnp.mean(x, axis=-1, keepdims=True)
    xc = x - mu
    var = jnp.mean(xc * xc, axis=-1, keepdims=True)
    return xc * lax.rsqrt(var + NORM_EPS)


def _sigmoid(x):
    return 1.0 / (1.0 + jnp.exp(-x))


def _silu(x):
    return x * _sigmoid(x)


def _bdot(a, b):
    return jnp.dot(a.astype(BF16), b.astype(BF16), preferred_element_type=F32)


def _bdot_nt(a, b):
    return lax.dot_general(a.astype(BF16), b.astype(BF16), (((1,), (1,)), ((), ())),
                           preferred_element_type=F32)


def _split3(v):
    hi = v.astype(BF16)
    r1 = v - hi.astype(F32)
    mid = r1.astype(BF16)
    lo = (r1 - mid.astype(F32)).astype(BF16)
    return hi, mid, lo


def _mod_kernel(c_ref, w_ref, b_ref, o_ref):
    c = c_ref[...]
    o_ref[...] = _bdot(_silu(c), w_ref[...]) + b_ref[...]


def _modulation(cond, mod_w, mod_b):
    n_layer = mod_w.shape[0]
    rows = cond.shape[0]
    return pl.pallas_call(
        _mod_kernel,
        out_shape=jax.ShapeDtypeStruct((n_layer, rows, N_MOD * D_MODEL), F32),
        grid=(n_layer, N_MOD),
        in_specs=[
            pl.BlockSpec((rows, D_MODEL), lambda l, n: (0, 0)),
            pl.BlockSpec((None, D_MODEL, D_MODEL), lambda l, n: (l, 0, n)),
            pl.BlockSpec((None, 1, D_MODEL), lambda l, n: (l, 0, n)),
        ],
        out_specs=pl.BlockSpec((None, rows, D_MODEL), lambda l, n: (l, 0, n)),
        compiler_params=_params(("arbitrary", "arbitrary")),
        name="modulation",
    )(cond, mod_w, mod_b.reshape(n_layer, 1, N_MOD * D_MODEL))


def _mod_part(mod_ref, k):
    return mod_ref[:, k * D_MODEL:(k + 1) * D_MODEL]


def _mod_row(n_batch):
    return lambda b, j: (jnp.where(j == 0, n_batch, b), 0, 0)


def _chunk_swap():
    r = np.arange(TM)
    p = np.zeros((TM, TM), np.float32)
    p[r, (r % S5_T) * (TM // S5_T) + r // S5_T] = 1.0
    return jnp.asarray(p, BF16)


def _ffn_residual(x1, mod_ref, f0_ref, f1_ref, route_ref, lng_ref, lnb_ref):
    r = lax.broadcasted_iota(jnp.int32, (TM, TM), 0)
    c = lax.broadcasted_iota(jnp.int32, (TM, TM), 1)
    w = [jnp.sum(jnp.where(r == c, route_ref[TOP_K + k:TOP_K + k + 1, :], 0.0), axis=1, keepdims=True)
         for k in range(TOP_K)]
    ffn = w[0] * f0_ref[...].astype(F32) + w[1] * f1_ref[...].astype(F32)
    return _ln(ALPHA * x1 + _mod_part(mod_ref, 5) * ffn) * lng_ref[...] + lnb_ref[...]


def _inproj_kernel(*refs, fused):
    if fused:
        (x1_ref, modp_ref, f0_ref, f1_ref, route_ref, lng_ref, lnb_ref, mod_ref, w_ref, swap_ref,
         x_out_ref, attn_ref, conv_ref, sz_ref, su_ref, dt_ref, u_ref) = refs
        x = _ffn_residual(x1_ref[...], modp_ref, f0_ref, f1_ref, route_ref, lng_ref, lnb_ref)
        x_out_ref[...] = x
    else:
        x_ref, mod_ref, w_ref, swap_ref, attn_ref, conv_ref, sz_ref, su_ref, dt_ref, u_ref = refs
        x = x_ref[...]
    h = _ln(x) * (1.0 + _mod_part(mod_ref, 1)) + _mod_part(mod_ref, 0)
    hb = h.astype(BF16)
    off = 0
    for ref in (attn_ref, conv_ref, sz_ref, su_ref, dt_ref):
        width = ref.shape[-1]
        ref[...] = jnp.dot(hb, w_ref[:, off:off + width], preferred_element_type=F32)
        off += width
    by_step = jnp.dot(swap_ref[...], su_ref[...].astype(BF16), preferred_element_type=F32)
    n_ch = TM // S5_T
    for t in range(S5_T):
        rows = by_step[t * n_ch:(t + 1) * n_ch]
        for g in range(S5_GROUPS):
            u_ref[g, :, t * S5_GROUP:(t + 1) * S5_GROUP] = rows[:, g * S5_GROUP:(g + 1) * S5_GROUP].astype(BF16)


def _inproj(xcat, mods, w1, prev=None):
    n_batch = xcat.shape[0] if prev is None else prev[0].shape[0]
    n_tiles = TOK // TM
    widths = (ATTN_W, CONV_W, 512, 512, DT_W)
    tile = lambda w: pl.BlockSpec((None, TM, w), lambda b, j: (b, j, 0))
    mod_spec = pl.BlockSpec((None, 1, N_MOD * D_MODEL), _mod_row(n_batch))
    const = lambda shape: pl.BlockSpec(shape, lambda b, j: tuple(0 for _ in shape))
    out_shape = ([jax.ShapeDtypeStruct((n_batch, TOK, w), F32) for w in widths]
                 + [jax.ShapeDtypeStruct((S5_GROUPS, S5_NCH, n_batch * S5_ROWW), BF16)])
    out_specs = ([tile(w) for w in widths]
                 + [pl.BlockSpec((S5_GROUPS, TM // S5_T, S5_ROWW), lambda b, j: (0, j, b))])
    own = [mod_spec, const((D_MODEL, PROJ_W)), const((TM, TM))]
    if prev is None:
        in_specs, args = [tile(D_MODEL)] + own, (xcat, mods, w1, _chunk_swap())
    else:
        x1, mods_prev, f, route, ln_g, ln_b = prev
        in_specs = [tile(D_MODEL), mod_spec,
                    pl.BlockSpec((None, None, TM, D_MODEL), lambda b, j: (0, b, j, 0)),
                    pl.BlockSpec((None, None, TM, D_MODEL), lambda b, j: (1, b, j, 0)),
                    pl.BlockSpec((8, TM), lambda b, j: (0, b * n_tiles + j)),
                    const((1, D_MODEL)), const((1, D_MODEL))] + own
        args = (x1, mods_prev, f, f, route, ln_g, ln_b, mods, w1, _chunk_swap())
        out_shape = [jax.ShapeDtypeStruct((n_batch, TOK, D_MODEL), F32)] + out_shape
        out_specs = [tile(D_MODEL)] + out_specs
    return pl.pallas_call(
        functools.partial(_inproj_kernel, fused=prev is not None),
        out_shape=out_shape,
        grid=(n_batch, n_tiles),
        in_specs=in_specs,
        out_specs=out_specs,
        compiler_params=_params(("arbitrary", "arbitrary")),
        name="inproj",
    )(*args)


def _rope(t, cos, sin):
    width = t.shape[-1]
    lane = lax.broadcasted_iota(jnp.int32, t.shape, t.ndim - 1)
    rot = jnp.where(lane % 32 < 16, -pltpu.roll(t, width - 16, t.ndim - 1), pltpu.roll(t, 16, t.ndim - 1))
    return t * cos + rot * sin


def _tile4(v):
    return jnp.concatenate([v, v, v, v], axis=1)


def _head_inv_rms(t, n_heads):
    t2 = t * t
    lane = lax.broadcasted_iota(jnp.int32, t.shape, 1)
    inv = jnp.zeros_like(t)
    for h in range(n_heads):
        ms = jnp.sum(t2[:, h * HEAD_DIM:(h + 1) * HEAD_DIM], axis=1, keepdims=True) * (1.0 / HEAD_DIM)
        inv = jnp.where(lane // HEAD_DIM == h, lax.rsqrt(ms + NORM_EPS), inv)
    return inv


def _stack_heads(qr, kv):
    parts = [qr[:, (kv * GRP + g) * HEAD_DIM:(kv * GRP + g + 1) * HEAD_DIM] for g in range(GRP)]
    return jnp.concatenate(parts, axis=0).astype(BF16)


QPS = 2
LOG2E = math.log2(math.e)


KEY_BLOCK = 128


def _attend(jobs, sink_ref, out_ref, kb_ref, vt_ref, s_ref, p_ref):
    width_q = GRP * QB
    halves = [slice(0, width_q // 2), slice(width_q // 2, width_q)]
    sub = KEY_BLOCK // 8
    for u, (qr, parts) in enumerate(jobs):
        for kv in range(N_KV):
            qs = _stack_heads(qr, kv)
            row = 0
            for key_idx, width, bias in parts:
                for hc in halves:
                    s = _bdot_nt(kb_ref[kv, key_idx, :], qs[hc])
                    if bias is not None:
                        s = s + jnp.concatenate([bias] * (width_q // 2 // QB), axis=1)
                    s_ref[u, kv, row:row + width, hc] = s
                row += width
    for u, (_, parts) in enumerate(jobs):
        n_keys = sum(width for _, width, _ in parts)
        sink_term = {}
        for kv in range(N_KV):
            for g in range(GRP):
                cols = slice(g * QB, (g + 1) * QB)
                mx = None
                for r0 in range(0, n_keys, KEY_BLOCK):
                    blk = jnp.max(s_ref[u, kv, r0:r0 + KEY_BLOCK, cols].reshape(sub, 8, QB), axis=0)
                    mx = blk if mx is None else jnp.maximum(mx, blk)
                m = jnp.max(mx, axis=0, keepdims=True)
                if sink_ref is not None:
                    m = jnp.maximum(m, sink_ref[kv, :, cols])
                    sink_term[kv, g] = jnp.exp2(sink_ref[kv, :, cols] - m)
                m_b = jnp.broadcast_to(m, (KEY_BLOCK, QB))
                for r0 in range(0, n_keys, KEY_BLOCK):
                    p_ref[u, kv, r0:r0 + KEY_BLOCK, cols] = jnp.exp2(
                        s_ref[u, kv, r0:r0 + KEY_BLOCK, cols] - m_b).astype(BF16)
        o_t = {}
        for kv in range(N_KV):
            for hi, hc in enumerate(halves):
                o = None
                row = 0
                for key_idx, width, _ in parts:
                    part = jnp.dot(vt_ref[kv, :, key_idx], p_ref[u, kv, row:row + width, hc],
                                   preferred_element_type=F32)
                    o = part if o is None else o + part
                    row += width
                o_t[kv, hi] = o
        for g in range(GRP):
            hi, gl = divmod(g, GRP // 2)
            cols = slice(gl * QB, (gl + 1) * QB)
            scaled = []
            for kv in range(N_KV):
                den = o_t[kv, hi][HEAD_DIM:HEAD_DIM + 1, cols]
                if sink_ref is not None:
                    den = den + sink_term[kv, g]
                scaled.append(o_t[kv, hi][0:HEAD_DIM, cols] / den)
            pair = jnp.concatenate(scaled, axis=0).T
            for kv in range(N_KV):
                h = kv * GRP + g
                out_ref[u * QB:(u + 1) * QB, h * HEAD_DIM:(h + 1) * HEAD_DIM] = (
                    pair[:, kv * HEAD_DIM:(kv + 1) * HEAD_DIM].astype(out_ref.dtype))


def _wattn_kernel(q_ref, k_ref, v_ref, cos_ref, sin_ref, sink_ref, bias_ref, out_ref, kb_ref, vt_ref, s_ref, p_ref):
    i = pl.program_id(1)

    @pl.when(i == 0)
    def _prep():
        kr = _rope(k_ref[...], cos_ref[...], sin_ref[...])
        v_t = v_ref[...].T
        for kv in range(N_KV):
            kb_ref[kv] = kr[:, kv * HEAD_DIM:(kv + 1) * HEAD_DIM].astype(BF16)
            vt_ref[kv, 0:HEAD_DIM, :] = v_t[kv * HEAD_DIM:(kv + 1) * HEAD_DIM, :].astype(BF16)
            vt_ref[kv, HEAD_DIM:HEAD_DIM + 16, :] = jnp.ones((16, TOK), BF16)

    row0 = pl.multiple_of(i * (QPS * QB), QPS * QB)
    cos = _tile4(cos_ref[pl.ds(row0, QPS * QB), :])
    sin = _tile4(sin_ref[pl.ds(row0, QPS * QB), :])
    qr = _rope(q_ref[...], cos, sin) * (HEAD_DIM ** -0.5 * LOG2E)
    qrs = [qr[u * QB:(u + 1) * QB] for u in range(QPS)]

    ctx_keys = (slice(0, CTX_LEN), CTX_LEN, None)

    @pl.when(i < CTX_LEN // (QPS * QB))
    def _ctx_queries():
        _attend([(q, [ctx_keys]) for q in qrs], sink_ref, out_ref, kb_ref, vt_ref, s_ref, p_ref)

    @pl.when(i >= CTX_LEN // (QPS * QB))
    def _latent_queries():
        jobs = []
        for u in range(QPS):
            j = i * QPS + u - CTX_LEN // QB
            band = jnp.clip(j - 1, 0, SEQ // QB - 3)
            start = pl.multiple_of(CTX_LEN + band * QB, QB)
            bias = bias_ref[j - band]
            jobs.append((qrs[u], [ctx_keys, (pl.ds(start, 3 * QB), 3 * QB, bias)]))
        _attend(jobs, sink_ref, out_ref, kb_ref, vt_ref, s_ref, p_ref)


def _gattn_kernel(q_ref, k_ref, v_ref, cos_ref, sin_ref, qw_ref, kw_ref, out_ref, kb_ref, vt_ref, s_ref, p_ref):
    i = pl.program_id(1)

    @pl.when(i == 0)
    def _prep():
        k = k_ref[...]
        kn = k * _head_inv_rms(k, N_KV) * kw_ref[...]
        kr = _rope(kn, cos_ref[...], sin_ref[...])
        v_t = v_ref[...].T
        for kv in range(N_KV):
            kb_ref[kv] = kr[:, kv * HEAD_DIM:(kv + 1) * HEAD_DIM].astype(BF16)
            vt_ref[kv, 0:HEAD_DIM, :] = v_t[kv * HEAD_DIM:(kv + 1) * HEAD_DIM, :].astype(BF16)
            vt_ref[kv, HEAD_DIM:HEAD_DIM + 16, :] = jnp.ones((16, TOK), BF16)

    row0 = pl.multiple_of(i * (QPS * QB), QPS * QB)
    cos = _tile4(cos_ref[pl.ds(row0, QPS * QB), :])
    sin = _tile4(sin_ref[pl.ds(row0, QPS * QB), :])
    q = q_ref[...]
    qn = q * _head_inv_rms(q, N_HEADS) * qw_ref[...]
    qr = _rope(qn, cos, sin) * (HEAD_DIM ** -0.5 * LOG2E)
    qrs = [qr[u * QB:(u + 1) * QB] for u in range(QPS)]

    @pl.when(i < CTX_LEN // (QPS * QB))
    def _ctx_queries():
        _attend([(q_u, [(slice(0, CTX_LEN), CTX_LEN, None)]) for q_u in qrs], None, out_ref, kb_ref, vt_ref,
                s_ref, p_ref)

    @pl.when(i >= CTX_LEN // (QPS * QB))
    def _latent_queries():
        _attend([(q_u, [(slice(0, TOK), TOK, None)]) for q_u in qrs], None, out_ref, kb_ref, vt_ref, s_ref, p_ref)


def _window_bias():
    q_pos = np.arange(QB)[None, None, :] + QB * np.arange(3)[:, None, None]
    k_pos = np.arange(3 * QB)[None, :, None]
    return jnp.asarray(np.where(np.abs(q_pos - k_pos) <= WINDOW, 0.0, -np.inf), F32)


def _attention(attn, cos, sin, sink_rows, qw, kw):
    n_batch = attn.shape[0]
    grid = (n_batch, TOK // (QPS * QB))
    q_spec = lambda blk: pl.BlockSpec((None, QPS * QB, 512), lambda b, i: (b, i, blk))
    kv_spec = lambda blk: pl.BlockSpec((None, TOK, 128), lambda b, i: (b, 0, blk))
    tab_spec = pl.BlockSpec((TOK, 128), lambda b, i: (0, 0))
    out_spec = pl.BlockSpec((None, QPS * QB, 512), lambda b, i: (b, i, 0))
    out_shape = jax.ShapeDtypeStruct((n_batch, TOK, 512), BF16)
    kv_scratch = [pltpu.VMEM((N_KV, TOK, HEAD_DIM), BF16), pltpu.VMEM((N_KV, HEAD_DIM + 16, TOK), BF16)]
    w_keys = CTX_LEN + 3 * QB
    ya = pl.pallas_call(
        _wattn_kernel, out_shape=out_shape, grid=grid,
        in_specs=[q_spec(0), kv_spec(8), kv_spec(9), tab_spec, tab_spec,
                  pl.BlockSpec((N_KV, 1, GRP * QB), lambda b, i: (0, 0, 0)),
                  pl.BlockSpec((3, 3 * QB, QB), lambda b, i: (0, 0, 0))],
        out_specs=out_spec,
        scratch_shapes=kv_scratch + [pltpu.VMEM((QPS, N_KV, w_keys, GRP * QB), F32),
                                     pltpu.VMEM((QPS, N_KV, w_keys, GRP * QB), BF16)],
        compiler_params=_params(("arbitrary", "arbitrary")), name="window_attention",
    )(attn, attn, attn, cos, sin, sink_rows, _window_bias())
    yg = pl.pallas_call(
        _gattn_kernel, out_shape=out_shape, grid=grid,
        in_specs=[q_spec(1), kv_spec(10), kv_spec(11), tab_spec, tab_spec,
                  pl.BlockSpec((1, 512), lambda b, i: (0, 0)),
                  pl.BlockSpec((1, 128), lambda b, i: (0, 0))],
        out_specs=out_spec,
        scratch_shapes=kv_scratch + [pltpu.VMEM((QPS, N_KV, TOK, GRP * QB), F32),
                                     pltpu.VMEM((QPS, N_KV, TOK, GRP * QB), BF16)],
        compiler_params=_params(("arbitrary", "arbitrary")), name="global_attention",
    )(attn, attn, attn, cos, sin, qw, kw)
    return ya, yg


def _ssd_chunk(s):
    r = s - N_CHUNK
    back = jnp.where(r == 0, 1, jnp.where(r == 1, 0, N_CHUNK + 1 - r))
    return jnp.where(s < N_CHUNK, s, back)


SSD_NB = 2


def _ssd_kernel(x_ref, prev_ref, next_ref, z_ref, dt_ref, cw_ref, cb_ref, dtb_ref, a_ref, d_ref, nw_ref, e_ref,
                out_ref, yf_ref, state_ref, y_ref):
    s = pl.program_id(1)
    chunk = _ssd_chunk(s)
    backward = s >= N_CHUNK

    @pl.when((s == 0) | (s == N_CHUNK))
    def _reset():
        state_ref[...] = jnp.zeros_like(state_ref)

    first = (chunk == 0) | (chunk == CTX_LEN // SSD_T)
    last = (chunk == CTX_LEN // SSD_T - 1) | (chunk == N_CHUNK - 1)
    row = lax.broadcasted_iota(jnp.int32, (SSD_T, CONV_W), 0)
    tr = lax.broadcasted_iota(jnp.int32, (SSD_T, SSD_T), 0)
    tc = lax.broadcasted_iota(jnp.int32, (SSD_T, SSD_T), 1)
    causal = jnp.where(backward, tc - tr, tr - tc) >= 0
    tri = jnp.where(causal, 1.0, 0.0).astype(BF16)
    neg_a = -jnp.exp(a_ref[...])
    expand = e_ref[...]

    def widen(v):
        return sum(jnp.dot(p, expand, preferred_element_type=F32) for p in _split3(v))

    xs = []
    for u in range(SSD_NB):
        xin = x_ref[u]
        prev = jnp.where(first, 0.0, prev_ref[u, 7:8, :])
        nxt = jnp.where(last, 0.0, next_ref[u, 0:1, :])
        xm1 = jnp.where(row == 0, prev, pltpu.roll(xin, 1, 0))
        xp1 = jnp.where(row == SSD_T - 1, nxt, pltpu.roll(xin, SSD_T - 1, 0))
        conv = cw_ref[0:1, :] * xm1 + cw_ref[1:2, :] * xin + cw_ref[2:3, :] * xp1 + cb_ref[...]
        xbc = _silu(conv)
        x = xbc[:, 0:SSD_WIDTH]
        xs.append(x)
        bmat = [xbc[:, SSD_WIDTH + g * SSD_N:SSD_WIDTH + (g + 1) * SSD_N] for g in range(SSD_GROUPS)]
        cmat = [xbc[:, SSD_WIDTH + (SSD_GROUPS + g) * SSD_N:SSD_WIDTH + (SSD_GROUPS + g + 1) * SSD_N]
                for g in range(SSD_GROUPS)]

        dtv = dt_ref[u] + dtb_ref[...]
        dt = jnp.maximum(dtv, 0.0) + jnp.log(1.0 + jnp.exp(-jnp.abs(dtv)))
        hi, mid, lo = _split3(dt * neg_a)
        cs = (jnp.dot(tri, hi, preferred_element_type=F32) + jnp.dot(tri, mid, preferred_element_type=F32)
              + jnp.dot(tri, lo, preferred_element_type=F32))
        cs_t = cs.T

        dt_x = widen(dt)
        cs_x = widen(cs)
        tot_x = jnp.where(backward, cs_x[0:1, :], cs_x[SSD_T - 1:SSD_T, :])
        ecs_x = jnp.exp(cs_x)
        etot_x = jnp.exp(tot_x)
        xd = x * dt_x
        xd_b = xd.astype(BF16)
        xd_end = (xd * jnp.exp(tot_x - cs_x)).astype(BF16)

        gw = SSD_WIDTH // SSD_GROUPS
        for g in range(SSD_GROUPS):
            glanes = slice(g * gw, (g + 1) * gw)
            cb = _bdot_nt(cmat[g], bmat[g])
            st = state_ref[u, g]
            y_off = ecs_x[:, glanes] * jnp.dot(cmat[g].astype(BF16), st.astype(BF16), preferred_element_type=F32)
            state_ref[u, g] = etot_x[:, glanes] * st + jnp.dot(bmat[g].T.astype(BF16), xd_end[:, glanes],
                                                               preferred_element_type=F32)
            for j in range(SSD_HEADS // SSD_GROUPS):
                h = g * (SSD_HEADS // SSD_GROUPS) + j
                lanes = slice(h * SSD_P, (h + 1) * SSD_P)
                seg = jnp.exp(jnp.where(causal, cs[:, h:h + 1] - cs_t[h:h + 1, :], -jnp.inf))
                y_ref[u, :, lanes] = (jnp.dot((cb * seg).astype(BF16), xd_b[:, lanes], preferred_element_type=F32)
                                      + y_off[:, j * SSD_P:(j + 1) * SSD_P])

    row0 = pl.multiple_of(chunk * SSD_T, SSD_T)

    @pl.when(jnp.logical_not(backward))
    def _keep():
        for u in range(SSD_NB):
            yf_ref[u, pl.ds(row0, SSD_T), :] = y_ref[u]

    @pl.when(backward)
    def _finish():
        for u in range(SSD_NB):
            ytot = yf_ref[u, pl.ds(row0, SSD_T), :] + y_ref[u] + d_ref[...] * xs[u]
            gated = ytot * _silu(z_ref[u])
            ms = jnp.mean(gated * gated, axis=1, keepdims=True)
            out_ref[u] = (gated * lax.rsqrt(ms + NORM_EPS) * nw_ref[...]).astype(out_ref.dtype)


def _ssd(conv_in, sz, dt, conv_w, conv_b, dt_bias, a_log, d_exp, norm_w):
    n_batch = conv_in.shape[0]
    assert n_batch % SSD_NB == 0
    halo = SSD_T // 8
    n_halo = TOK // 8

    def chunk_map(b, s):
        return (b, _ssd_chunk(s), 0)

    def out_map(b, s):
        return (b, jnp.where(s < N_CHUNK, 1, _ssd_chunk(s)), 0)

    const = lambda shape: pl.BlockSpec(shape, lambda b, s: tuple(0 for _ in shape))
    dir_spec = pl.BlockSpec((None, 1, 128), lambda b, s: (s // N_CHUNK, 0, 0))
    expand = jnp.asarray(np.repeat(np.eye(128, SSD_HEADS), SSD_P, axis=1), BF16)
    return pl.pallas_call(
        _ssd_kernel,
        out_shape=jax.ShapeDtypeStruct((n_batch, TOK, SSD_WIDTH), BF16),
        grid=(n_batch // SSD_NB, 2 * N_CHUNK),
        in_specs=[
            pl.BlockSpec((SSD_NB, SSD_T, CONV_W), chunk_map),
            pl.BlockSpec((SSD_NB, 8, CONV_W), lambda b, s: (b, jnp.maximum(_ssd_chunk(s) * halo - 1, 0), 0)),
            pl.BlockSpec((SSD_NB, 8, CONV_W), lambda b, s: (b, jnp.minimum((_ssd_chunk(s) + 1) * halo, n_halo - 1), 0)),
            pl.BlockSpec((SSD_NB, SSD_T, SSD_WIDTH), chunk_map),
            pl.BlockSpec((SSD_NB, SSD_T, 128), lambda b, s: (b, _ssd_chunk(s), s // N_CHUNK)),
            const((3, CONV_W)), const((1, CONV_W)), dir_spec, dir_spec,
            const((1, SSD_WIDTH)), const((1, SSD_WIDTH)), const((128, SSD_WIDTH)),
        ],
        out_specs=pl.BlockSpec((SSD_NB, SSD_T, SSD_WIDTH), out_map),
        scratch_shapes=[pltpu.VMEM((SSD_NB, TOK, SSD_WIDTH), F32),
                        pltpu.VMEM((SSD_NB, SSD_GROUPS, SSD_N, SSD_WIDTH // SSD_GROUPS), F32),
                        pltpu.VMEM((SSD_NB, SSD_T, SSD_WIDTH), F32)],
        compiler_params=_params(("arbitrary", "arbitrary")),
        name="ssd",
    )(conv_in, conv_in, conv_in, sz, dt, conv_w, conv_b, dt_bias, a_log, d_exp, norm_w, expand)


S5_SW = 4 * 128


def _s5_kernel(u_ref, m_ref, h_ref, g_ref, a_ref, y_ref, loc_ref, prev_ref, *, n_batch):
    u = u_ref[...]
    loc_ref[...] = jnp.dot(u, h_ref[...], preferred_element_type=F32)
    a = [a_ref[k:k + 1, :] for k in range(4)]

    def step(k, carry):
        c_f = k
        c_b = jnp.where(k < S5_CTX_CH, S5_CTX_CH - 1 - k, S5_NCH + S5_CTX_CH - 1 - k)
        new = []
        for d, c in enumerate((c_f, c_b)):
            rows = pl.ds(pl.multiple_of(c * n_batch, n_batch), n_batch)
            re_l, im_l = slice(2 * d * 128, (2 * d + 1) * 128), slice((2 * d + 1) * 128, (2 * d + 2) * 128)
            s_re, s_im = carry[2 * d], carry[2 * d + 1]
            prev_ref[rows, re_l] = s_re
            prev_ref[rows, im_l] = s_im
            a_re, a_im = a[2 * d], a[2 * d + 1]
            new += [a_re * s_re - a_im * s_im + loc_ref[rows, re_l], a_re * s_im + a_im * s_re + loc_ref[rows, im_l]]
        return tuple(new)

    zero = jnp.zeros((n_batch, 128), F32)
    lax.fori_loop(0, S5_NCH, step, (zero,) * 4, unroll=2)
    y = jnp.dot(u, m_ref[...], preferred_element_type=F32)
    y = y + jnp.dot(prev_ref[...].astype(BF16), g_ref[...], preferred_element_type=F32)
    y_ref[...] = y.astype(y_ref.dtype)


def _s5(u_g, mats, n_batch):
    m_all, h_all, g_all, a16 = mats
    rows = u_g.shape[1]
    grp = lambda shape: pl.BlockSpec((None,) + shape, lambda g: (g,) + tuple(0 for _ in shape))
    return pl.pallas_call(
        functools.partial(_s5_kernel, n_batch=n_batch),
        out_shape=jax.ShapeDtypeStruct((S5_GROUPS, rows, S5_ROWW), BF16),
        grid=(S5_GROUPS,),
        in_specs=[grp((rows, S5_ROWW)), grp((S5_ROWW, S5_ROWW)), grp((S5_ROWW, S5_SW)), grp((S5_SW, S5_ROWW)),
                  grp((4, 128))],
        out_specs=grp((rows, S5_ROWW)),
        scratch_shapes=[pltpu.VMEM((rows, S5_SW), F32), pltpu.VMEM((rows, S5_SW), F32)],
        compiler_params=_params(("arbitrary",)),
        name="s5",
    )(u_g, m_all, h_all, g_all, a16)


def _cmul(a, b):
    return a[0] * b[0] - a[1] * b[1], a[0] * b[1] + a[1] * b[0]


def _s5_matrices(a_re, a_im, log_dt, b_re, b_im, c_re, c_im):
    hp = lax.Precision.HIGHEST
    t = jnp.arange(S5_T + 1, dtype=F32)
    m_sum = 0.0
    h_all, g_all, a16_all = [], [], []
    c = (c_re.astype(F32), c_im.astype(F32))
    for direction in range(2):
        are = jnp.minimum(a_re[direction].astype(F32), -1e-4)
        aim = a_im[direction].astype(F32)
        dt = jnp.exp(log_dt[direction].astype(F32))[:, None]
        mag = jnp.exp(t[:, None, None] * (are * dt)[None])
        ang = t[:, None, None] * (aim * dt)[None]
        pw = (mag * jnp.cos(ang), mag * jnp.sin(ang))
        num = (pw[0][1] - 1.0, pw[1][1])
        den = are * are + aim * aim
        coef = ((num[0] * are + num[1] * aim) / den, (num[1] * are - num[0] * aim) / den)
        bbar = _cmul((coef[0][..., None], coef[1][..., None]), (b_re.astype(F32), b_im.astype(F32)))
        pb = _cmul((pw[0][:S5_T, :, :, None], pw[1][:S5_T, :, :, None]), (bbar[0][None], bbar[1][None]))
        taps = (jnp.einsum('gon,tgni->tgoi', c[0], pb[0], precision=hp)
                - jnp.einsum('gon,tgni->tgoi', c[1], pb[1], precision=hp))
        ti = jnp.arange(S5_T)
        lag = (ti[None, :] - ti[:, None]) if direction == 0 else (ti[:, None] - ti[None, :])
        k_full = taps[jnp.clip(lag, 0, S5_T - 1)]
        k_full = jnp.where((lag >= 0)[:, :, None, None, None], k_full, 0.0)
        m_sum = m_sum + k_full.transpose(2, 0, 4, 1, 3).reshape(S5_GROUPS, S5_ROWW, S5_ROWW)
        e_in = (S5_T - 1 - ti) if direction == 0 else ti
        hb = _cmul((pw[0][e_in][..., None], pw[1][e_in][..., None]), (bbar[0][None], bbar[1][None]))
        zpad = jnp.zeros_like(hb[0])
        h_mat = jnp.concatenate([hb[0], zpad, hb[1], zpad], axis=2)
        h_all.append(h_mat.transpose(1, 0, 3, 2).reshape(S5_GROUPS, S5_ROWW, 4 * S5_STATE))
        e_out = (ti + 1) if direction == 0 else (S5_T - ti)
        cp = _cmul((c[0][None], c[1][None]),
                   (pw[0][e_out][:, :, None, :], pw[1][e_out][:, :, None, :]))
        zpad = jnp.zeros_like(cp[0])
        g_mat = jnp.concatenate([cp[0], zpad, -cp[1], zpad], axis=3)
        g_all.append(g_mat.transpose(1, 3, 0, 2).reshape(S5_GROUPS, 4 * S5_STATE, S5_ROWW))
        zrow = jnp.zeros_like(pw[0][S5_T])
        a16_all += [jnp.concatenate([pw[0][S5_T], zrow], axis=1), jnp.concatenate([pw[1][S5_T], zrow], axis=1)]
    return (m_sum.astype(BF16), jnp.concatenate(h_all, axis=2).astype(BF16),
            jnp.concatenate(g_all, axis=1).astype(BF16), jnp.stack(a16_all, axis=1))


def _route(logits_t, bias, base, before):
    scores = [_sigmoid(logits_t[j * N_EXPERT_GROUPS:(j + 1) * N_EXPERT_GROUPS]) for j in range(PER_GROUP)]
    sel = [scores[j] + bias[j * N_EXPERT_GROUPS:(j + 1) * N_EXPERT_GROUPS] for j in range(PER_GROUP)]
    hi1, lo1 = jnp.maximum(sel[0], sel[1]), jnp.minimum(sel[0], sel[1])
    hi2, lo2 = jnp.maximum(sel[2], sel[3]), jnp.minimum(sel[2], sel[3])
    group_score = jnp.maximum(hi1, hi2) + jnp.maximum(jnp.minimum(hi1, hi2), jnp.maximum(lo1, lo2))
    gid = lax.broadcasted_iota(jnp.int32, group_score.shape, 0)
    best = jnp.max(group_score, axis=0, keepdims=True)
    grp = jnp.min(jnp.where(group_score == best, gid, N_EXPERT_GROUPS), axis=0, keepdims=True)
    pick = gid == grp
    v = [jnp.sum(jnp.where(pick, sel[j], 0.0), axis=0, keepdims=True) for j in range(PER_GROUP)]
    sc = [jnp.sum(jnp.where(pick, scores[j], 0.0), axis=0, keepdims=True) for j in range(PER_GROUP)]
    rank = []
    for j in range(PER_GROUP):
        r = jnp.zeros_like(grp)
        for i in range(PER_GROUP):
            if i < j:
                r = r + jnp.where(v[i] >= v[j], 1, 0)
            elif i > j:
                r = r + jnp.where(v[i] > v[j], 1, 0)
        rank.append(r)
    e, w, locs = [], [], []
    for k in range(TOP_K):
        loc = sum(jnp.where(rank[j] == k, j, 0) for j in range(PER_GROUP))
        locs.append(loc)
        e.append((grp * PER_GROUP + loc).astype(F32))
        w.append(sum(jnp.where(rank[j] == k, sc[j], 0.0) for j in range(PER_GROUP)))
    wsum = w[0] + w[1]
    chosen = [jnp.where(pick & ((locs[0] == j) | (locs[1] == j)), 1.0, 0.0) for j in range(PER_GROUP)]
    onehot = jnp.concatenate(chosen, axis=0)
    seen = jnp.dot(onehot.astype(BF16), before, preferred_element_type=F32) + base
    pos = []
    for k in range(TOP_K):
        hit = sum(jnp.where(pick & (locs[k] == j), seen[j * N_EXPERT_GROUPS:(j + 1) * N_EXPERT_GROUPS], 0.0)
                  for j in range(PER_GROUP))
        pos.append(jnp.sum(hit, axis=0, keepdims=True))
    rows = [e[0], e[1], w[0] / wsum, w[1] / wsum, pos[0], pos[1]]
    route = jnp.concatenate(rows + [jnp.zeros_like(wsum)] * (8 - len(rows)), axis=0)
    return route, base + jnp.sum(onehot, axis=1, keepdims=True)


def _gelu_tanh(x):
    return 0.5 * x * (1.0 + jnp.tanh(math.sqrt(2.0 / math.pi) * (x + 0.044715 * (x * x * x))))


def _merge_kernel(x_ref, mod_ref, ya_ref, ys_ref, yg_ref, y5_ref, su_ref, s5d_ref, wg_ref, wb_ref, wglu_ref, wout_ref,
                  lng_ref, lnb_ref, rw_ref, rb_ref, before_ref, swap_ref, x1_ref, h2_ref, route_ref, count_ref,
                  base_ref, y5s_ref):
    @pl.when((pl.program_id(0) == 0) & (pl.program_id(1) == 0))
    def _reset():
        base_ref[...] = jnp.zeros_like(base_ref)

    x = x_ref[...]
    hb = (_ln(x) * (1.0 + _mod_part(mod_ref, 1)) + _mod_part(mod_ref, 0)).astype(BF16)
    n_ch = TM // S5_T
    for t in range(S5_T):
        for g in range(S5_GROUPS):
            y5s_ref[t * n_ch:(t + 1) * n_ch, g * S5_GROUP:(g + 1) * S5_GROUP] = (
                y5_ref[g, :, t * S5_GROUP:(t + 1) * S5_GROUP])
    y5_scan = jnp.dot(swap_ref[...], y5s_ref[...], preferred_element_type=F32)
    y5_pre = y5_scan + s5d_ref[...] * su_ref[...]
    glu = jnp.dot(_gelu_tanh(y5_pre).astype(BF16), wglu_ref[...], preferred_element_type=F32)
    y5 = (glu[:, 0:S5_WIDTH] * _sigmoid(glu[:, S5_WIDTH:2 * S5_WIDTH])).astype(BF16)
    acc = jnp.zeros(x.shape, F32)
    for n, y in enumerate((ya_ref[...], ys_ref[...], yg_ref[...], y5)):
        gate = _sigmoid(jnp.dot(hb, wg_ref[:, n * D_MODEL:(n + 1) * D_MODEL], preferred_element_type=F32))
        acc = acc + gate * jnp.dot(y, wb_ref[n], preferred_element_type=F32)
    mix = jnp.dot(acc.astype(BF16), wout_ref[...], preferred_element_type=F32)
    x1 = _ln(ALPHA * x + _mod_part(mod_ref, 2) * mix) * lng_ref[...] + lnb_ref[...]
    x1_ref[...] = x1
    h2 = (_ln(x1) * (1.0 + _mod_part(mod_ref, 4)) + _mod_part(mod_ref, 3)).astype(BF16)
    h2_ref[...] = h2
    route, base = _route(_bdot_nt(rw_ref[...], h2), rb_ref[...], base_ref[...], before_ref[...])
    route_ref[...] = route
    base_ref[...] = base
    count_ref[...] = jnp.broadcast_to(base, count_ref.shape)


def _merge(xcat, mods, ya, ys, yg, y5, su, s5_d, wg, wb, wglu, wout, ln_g, ln_b, rw_t, rb):
    n_batch = xcat.shape[0]
    n_tiles = TOK // TM
    tile = lambda w: pl.BlockSpec((None, TM, w), lambda b, j: (b, j, 0))
    const = lambda shape: pl.BlockSpec(shape, lambda b, j: tuple(0 for _ in shape))
    before = jnp.asarray(np.triu(np.ones((TM, TM)), 1), BF16)
    return pl.pallas_call(
        _merge_kernel,
        out_shape=[jax.ShapeDtypeStruct((n_batch, TOK, D_MODEL), F32),
                   jax.ShapeDtypeStruct((n_batch, TOK, D_MODEL), BF16),
                   jax.ShapeDtypeStruct((8, n_batch * TOK), F32),
                   jax.ShapeDtypeStruct((N_EXPERTS, 128), F32)],
        grid=(n_batch, n_tiles),
        in_specs=[tile(D_MODEL), pl.BlockSpec((None, 1, N_MOD * D_MODEL), _mod_row(n_batch)),
                  tile(512), tile(512), tile(512),
                  pl.BlockSpec((S5_GROUPS, TM // S5_T, S5_ROWW), lambda b, j: (0, j, b)),
                  tile(512), const((1, S5_WIDTH)),
                  const((D_MODEL, N_BRANCH * D_MODEL)), const((N_BRANCH, 512, D_MODEL)),
                  const((S5_WIDTH, 2 * S5_WIDTH)), const((D_MODEL, D_MODEL)),
                  const((1, D_MODEL)), const((1, D_MODEL)),
                  const((N_EXPERTS, D_MODEL)), const((N_EXPERTS, 1)), const((TM, TM)), const((TM, TM))],
        out_specs=[tile(D_MODEL), tile(D_MODEL),
                   pl.BlockSpec((8, TM), lambda b, j: (0, b * n_tiles + j)),
                   const((N_EXPERTS, 128))],
        scratch_shapes=[pltpu.VMEM((N_EXPERTS, 1), F32), pltpu.VMEM((TM, S5_WIDTH), BF16)],
        compiler_params=_params(("arbitrary", "arbitrary")),
        name="merge",
    )(xcat, mods, ya, ys, yg, y5, su, s5_d, wg, wb, wglu, wout, ln_g, ln_b, rw_t, rb, before, _chunk_swap())


def _moe_kernel(be_ref, nu_ref, x_ref, wg_ref, wu_ref, wd_ref, y_ref, wgu_s, wd_s):
    i = pl.program_id(0)
    used = i < nu_ref[0]

    @pl.when(used & ((i == 0) | (be_ref[i] != be_ref[jnp.maximum(i - 1, 0)])))
    def _cast():
        wgu_s[:, 0:D_EXPERT] = wg_ref[...].astype(BF16)
        wgu_s[:, D_EXPERT:2 * D_EXPERT] = wu_ref[...].astype(BF16)
        wd_s[...] = wd_ref[...].astype(BF16)

    @pl.when(used)
    def _block():
        gu = jnp.dot(x_ref[...], wgu_s[...], preferred_element_type=F32)
        mid = _silu(gu[:, 0:D_EXPERT]) * gu[:, D_EXPERT:2 * D_EXPERT]
        y_ref[...] = jnp.dot(mid.astype(BF16), wd_s[...], preferred_element_type=F32).astype(y_ref.dtype)


def _moe_experts(block_expert, n_used, xs, w_gate, w_up, w_down, layer):
    n_blocks = block_expert.shape[0]
    return pl.pallas_call(
        _moe_kernel,
        out_shape=jax.ShapeDtypeStruct((n_blocks * MOE_ROWS, D_MODEL), BF16),
        grid_spec=pltpu.PrefetchScalarGridSpec(
            num_scalar_prefetch=2, grid=(n_blocks,),
            in_specs=[pl.BlockSpec((MOE_ROWS, D_MODEL), lambda i, be, nu: (i, 0)),
                      pl.BlockSpec((None, None, D_MODEL, D_EXPERT), lambda i, be, nu: (layer, be[i], 0, 0)),
                      pl.BlockSpec((None, None, D_MODEL, D_EXPERT), lambda i, be, nu: (layer, be[i], 0, 0)),
                      pl.BlockSpec((None, None, D_EXPERT, D_MODEL), lambda i, be, nu: (layer, be[i], 0, 0))],
            out_specs=pl.BlockSpec((MOE_ROWS, D_MODEL), lambda i, be, nu: (i, 0)),
            scratch_shapes=[pltpu.VMEM((D_MODEL, 2 * D_EXPERT), BF16), pltpu.VMEM((D_EXPERT, D_MODEL), BF16)]),
        compiler_params=_params(("arbitrary",)),
        name="moe_experts",
    )(block_expert, n_used, xs, w_gate, w_up, w_down)


def _final_kernel(x_ref, mod_ref, f0_ref, f1_ref, route_ref, lng_ref, lnb_ref, o_ref):
    o_ref[...] = _ffn_residual(x_ref[...], mod_ref, f0_ref, f1_ref, route_ref, lng_ref, lnb_ref)


def _final(x1, mods, f, route, ln_g, ln_b):
    n_batch = x1.shape[0]
    n_tiles = TOK // TM
    skip = CTX_LEN // TM
    const = pl.BlockSpec((1, D_MODEL), lambda b, j: (0, 0))
    return pl.pallas_call(
        _final_kernel,
        out_shape=jax.ShapeDtypeStruct((n_batch, SEQ, D_MODEL), F32),
        grid=(n_batch, n_tiles - skip),
        in_specs=[pl.BlockSpec((None, TM, D_MODEL), lambda b, j: (b, j + skip, 0)),
                  pl.BlockSpec((None, 1, N_MOD * D_MODEL), lambda b, j: (b, 0, 0)),
                  pl.BlockSpec((None, None, TM, D_MODEL), lambda b, j: (0, b, j + skip, 0)),
                  pl.BlockSpec((None, None, TM, D_MODEL), lambda b, j: (1, b, j + skip, 0)),
                  pl.BlockSpec((8, TM), lambda b, j: (0, b * n_tiles + j + skip)),
                  const, const],
        out_specs=pl.BlockSpec((None, TM, D_MODEL), lambda b, j: (b, j, 0)),
        compiler_params=_params(("arbitrary", "arbitrary")),
        name="ffn_residual",
    )(x1, mods, f, f, route, ln_g, ln_b)


def _lookup(table, idx):
    ids = jnp.arange(table.shape[0], dtype=jnp.int32).reshape((-1,) + (1,) * idx.ndim)
    return jnp.sum(jnp.where(idx[None] == ids, table.reshape(ids.shape), 0), axis=0)


def _dispatch(route, counts_rows):
    n_tok = route.shape[1]
    n_assign = n_tok * TOP_K
    n_blocks = -(-(n_assign + N_EXPERTS * (MOE_ROWS - 1)) // MOE_ROWS)
    cap = n_blocks * MOE_ROWS
    counts = counts_rows[:, 0].astype(jnp.int32).reshape(PER_GROUP, N_EXPERT_GROUPS).T.reshape(N_EXPERTS)
    padded = (counts + MOE_ROWS - 1) // MOE_ROWS * MOE_ROWS
    pad_end = jnp.cumsum(padded)
    pad_start = pad_end - padded
    expert = route[0:TOP_K].astype(jnp.int32)
    pos = _lookup(pad_start, expert) + route[2 * TOP_K:3 * TOP_K].astype(jnp.int32)
    block_start = jnp.arange(n_blocks, dtype=jnp.int32) * MOE_ROWS
    block_expert = jnp.minimum(jnp.sum((pad_end[None, :] <= block_start[:, None]).astype(jnp.int32), axis=1),
                               N_EXPERTS - 1)
    n_used = (pad_end[-1:] // MOE_ROWS).astype(jnp.int32)
    gap = padded - counts
    gap_end = jnp.cumsum(gap)
    gap_first = jnp.concatenate([pad_start + counts - (gap_end - gap), pad_end[-1:] - gap_end[-1:]])
    k = jnp.arange(cap - n_assign, dtype=jnp.int32)
    owner = jnp.sum((gap_end[None, :] <= k[:, None]).astype(jnp.int32), axis=1)
    free_slot = _lookup(gap_first, owner) + k
    tok = jnp.broadcast_to(jnp.arange(n_tok, dtype=jnp.int32), (TOP_K, n_tok)).reshape(-1)
    keys = jnp.concatenate([pos.reshape(-1), free_slot])
    vals = jnp.concatenate([tok, k % n_tok])
    _, slot_tok = lax.sort((keys, vals), num_keys=1)
    return slot_tok, pos, block_expert, n_used


def _col(w, part):
    return w[:, _IN_OFF[part]:_IN_OFF[part + 1]]


def _proj_weights(w_in):
    zeros = jnp.zeros((D_MODEL, 128 - SSD_HEADS), w_in.dtype)
    dt = _col(w_in, _SDT)
    parts = [_col(w_in, p) for p in (_AQ, _GQ, _AK, _AV, _GK, _GV, _SX, _SB, _SC, _SZ, _SU)]
    parts += [dt[:, 0:SSD_HEADS], zeros, dt[:, SSD_HEADS:2 * SSD_HEADS], zeros]
    return jnp.concatenate(parts, axis=1).astype(BF16), _col(w_in, _GATE).astype(BF16)


def _rope_tables():
    rows = SEQ // GRID_W
    row = jnp.repeat(jnp.arange(rows, dtype=F32), GRID_W)
    col = jnp.tile(jnp.arange(GRID_W, dtype=F32), rows)
    axis_dim = HEAD_DIM // 2
    inv_freq = ROPE_THETA ** (-jnp.arange(0, axis_dim, 2, dtype=F32) / axis_dim)
    ang_r = row[:, None] * inv_freq
    ang_c = col[:, None] * inv_freq
    ang = jnp.concatenate([ang_r, ang_r, ang_c, ang_c], axis=-1)
    cos = jnp.concatenate([jnp.ones((CTX_LEN, HEAD_DIM), F32), jnp.cos(ang)], axis=0)
    sin = jnp.concatenate([jnp.zeros((CTX_LEN, HEAD_DIM), F32), jnp.sin(ang)], axis=0)
    return jnp.tile(cos, (1, 2)), jnp.tile(sin, (1, 2))


def _pad_lanes(v, width=128):
    return jnp.pad(v, ((0, 0), (0, width - v.shape[-1])))


def kernel(x, c, ctx, c_ctx, mod_w, mod_b, w_in, wa_sink, ga_q_norm, ga_k_norm, ssd_conv_w, ssd_conv_b, ssd_dt_bias, ssd_a_log, ssd_d, ssd_norm_w, s5_a_re, s5_a_im, s5_log_dt, s5_b_re, s5_b_im, s5_c_re, s5_c_im, s5_d, s5_w_glu, w_branch, w_out, ln1_g, ln1_b, ln2_g, ln2_b, router_w, router_bias, moe_w_gate, moe_w_up, moe_w_down):
    n_batch = x.shape[0]
    n_tok = n_batch * TOK
    xcat = jnp.concatenate([ctx, x], axis=1)
    mod_rows = -(-(n_batch + 1) // 8) * 8
    cond = jnp.zeros((mod_rows, D_MODEL), F32).at[:n_batch].set(c).at[n_batch].set(c_ctx)
    mods_all = _modulation(cond, mod_w, mod_b)
    cos, sin = _rope_tables()

    perm = np.array([g * PER_GROUP + j for j in range(PER_GROUP) for g in range(N_EXPERT_GROUPS)])
    rw_t = router_w.T[perm].astype(BF16)
    rb = router_bias.astype(F32)[perm].reshape(N_EXPERTS, 1)

    prev = None
    for layer in range(DEPTH):
        mods = mods_all[layer].reshape(mod_rows, 1, N_MOD * D_MODEL)
        w1, wg = _proj_weights(w_in[layer])
        if prev is None:
            attn, conv_in, sz, su, dt, u_g = _inproj(xcat, mods, w1)
        else:
            xcat, attn, conv_in, sz, su, dt, u_g = _inproj(None, mods, w1, prev)

        sink_rows = jnp.repeat(wa_sink[layer].astype(F32).reshape(N_KV, GRP) * LOG2E, QB,
                               axis=1).reshape(N_KV, 1, GRP * QB)
        qw = jnp.tile(ga_q_norm[layer].astype(F32), N_HEADS).reshape(1, 512)
        kw = jnp.tile(ga_k_norm[layer].astype(F32), N_KV).reshape(1, 128)
        ya, yg = _attention(attn, cos, sin, sink_rows, qw, kw)

        dt_bias = _pad_lanes(ssd_dt_bias[layer].astype(F32).reshape(2, SSD_HEADS)).reshape(2, 1, 128)
        a_log = _pad_lanes(ssd_a_log[layer].astype(F32)).reshape(2, 1, 128)
        d_exp = jnp.repeat(ssd_d[layer].astype(F32), SSD_P).reshape(1, SSD_WIDTH)
        ys = _ssd(conv_in, sz, dt, ssd_conv_w[layer].astype(F32), ssd_conv_b[layer].astype(F32).reshape(1, CONV_W),
                  dt_bias, a_log, d_exp, ssd_norm_w[layer].astype(F32).reshape(1, SSD_WIDTH))

        mats = _s5_matrices(s5_a_re[layer], s5_a_im[layer], s5_log_dt[layer], s5_b_re[layer], s5_b_im[layer],
                            s5_c_re[layer], s5_c_im[layer])
        y5 = _s5(u_g.reshape(S5_GROUPS, S5_NCH * n_batch, S5_ROWW), mats, n_batch)
        y5 = y5.reshape(S5_GROUPS, S5_NCH, n_batch * S5_ROWW)

        x1, h2, route, counts = _merge(
            xcat, mods, ya, ys, yg, y5, su, s5_d[layer].astype(F32).reshape(1, S5_WIDTH), wg,
            w_branch[layer].astype(BF16), s5_w_glu[layer].astype(BF16), w_out[layer].astype(BF16),
            ln1_g[layer].reshape(1, D_MODEL), ln1_b[layer].reshape(1, D_MODEL), rw_t, rb)

        slot_tok, pos, block_expert, n_used = _dispatch(route, counts)
        xs = h2.reshape(n_tok, D_MODEL)[slot_tok]
        y_slots = _moe_experts(block_expert, n_used, xs, moe_w_gate, moe_w_up, moe_w_down, layer)
        f = y_slots[pos.reshape(-1)].reshape(TOP_K, n_batch, TOK, D_MODEL)
        prev = (x1, mods, f, route, ln2_g[layer].reshape(1, D_MODEL), ln2_b[layer].reshape(1, D_MODEL))
    return _final(*prev)
```

```python
import functools
import math

import jax
import jax.numpy as jnp
import numpy as np
from jax import lax
from jax.experimental import pallas as pl
from jax.experimental.pallas import tpu as pltpu

F32 = jnp.float32
BF16 = jnp.bfloat16

D_MODEL = 1024
DEPTH = 4
GRID_W = 64
CTX_LEN = 256
SEQ = 2048
TOK = CTX_LEN + SEQ
N_MOD = 6
HEAD_DIM = 64
N_BRANCH = 4
N_HEADS = 8
N_KV = 2
GRP = N_HEADS // N_KV
WINDOW = 128
QB = 128
ROPE_THETA = 10000.0

SSD_HEADS = 8
SSD_P = 64
SSD_GROUPS = 2
SSD_N = 128
SSD_T = 128
SSD_WIDTH = SSD_HEADS * SSD_P
SSD_CONV_CH = SSD_WIDTH + 2 * SSD_GROUPS * SSD_N
N_CHUNK = TOK // SSD_T

S5_GROUP = 16
S5_WIDTH = 512
S5_GROUPS = S5_WIDTH // S5_GROUP
S5_STATE = 64
S5_T = 16
S5_ROWW = S5_T * S5_GROUP
S5_NCH = TOK // S5_T
S5_CTX_CH = CTX_LEN // S5_T

N_EXPERTS = 32
N_EXPERT_GROUPS = 8
PER_GROUP = N_EXPERTS // N_EXPERT_GROUPS
TOP_K = 2
D_EXPERT = 512
MOE_ROWS = 256

ALPHA = (2 * DEPTH) ** 0.25
NORM_EPS = 1e-6

TM = 256
VMEM_LIMIT = 56 * 1024 * 1024

_IN_SIZES = (512, 128, 128, 512, 128, 128, 512, 512, 256, 256, 16, 512, 4096)
_IN_OFF = np.concatenate([[0], np.cumsum(_IN_SIZES)]).astype(int)
(_AQ, _AK, _AV, _GQ, _GK, _GV, _SX, _SZ, _SB, _SC, _SDT, _SU, _GATE) = range(13)

ATTN_W = 1536
CONV_W = 1024
DT_W = 256
PROJ_W = ATTN_W + CONV_W + 512 + 512 + DT_W


def _params(sem=None):
    return pltpu.CompilerParams(dimension_semantics=sem, vmem_limit_bytes=VMEM_LIMIT)


def _ln(x):
    mu = jnp.mean(x, axis=-1, keepdims=True)
    xc = x - mu
    var = jnp.mean(xc * xc, axis=-1, keepdims=True)
    return xc * lax.rsqrt(var + NORM_EPS)


def _sigmoid(x):
    return 1.0 / (1.0 + jnp.exp(-x))


def _silu(x):
    return x * _sigmoid(x)


def _bdot(a, b):
    return jnp.dot(a.astype(BF16), b.astype(BF16), preferred_element_type=F32)


def _bdot_nt(a, b):
    return lax.dot_general(a.astype(BF16), b.astype(BF16), (((1,), (1,)), ((), ())),
                           preferred_element_type=F32)


def _split3(v):
    hi = v.astype(BF16)
    r1 = v - hi.astype(F32)
    mid = r1.astype(BF16)
    lo = (r1 - mid.astype(F32)).astype(BF16)
    return hi, mid, lo


def _mod_kernel(c_ref, w_ref, b_ref, o_ref):
    c = c_ref[...]
    o_ref[...] = _bdot(_silu(c), w_ref[...]) + b_ref[...]


def _modulation(cond, mod_w, mod_b):
    n_layer = mod_w.shape[0]
    rows = cond.shape[0]
    return pl.pallas_call(
        _mod_kernel,
        out_shape=jax.ShapeDtypeStruct((n_layer, rows, N_MOD * D_MODEL), F32),
        grid=(n_layer, N_MOD),
        in_specs=[
            pl.BlockSpec((rows, D_MODEL), lambda l, n: (0, 0)),
            pl.BlockSpec((None, D_MODEL, D_MODEL), lambda l, n: (l, 0, n)),
            pl.BlockSpec((None, 1, D_MODEL), lambda l, n: (l, 0, n)),
        ],
        out_specs=pl.BlockSpec((None, rows, D_MODEL), lambda l, n: (l, 0, n)),
        compiler_params=_params(("arbitrary", "arbitrary")),
        name="modulation",
    )(cond, mod_w, mod_b.reshape(n_layer, 1, N_MOD * D_MODEL))


def _mod_part(mod_ref, k):
    return mod_ref[:, k * D_MODEL:(k + 1) * D_MODEL]


def _mod_row(n_batch):
    return lambda b, j: (jnp.where(j == 0, n_batch, b), 0, 0)


def _chunk_swap():
    r = np.arange(TM)
    p = np.zeros((TM, TM), np.float32)
    p[r, (r % S5_T) * (TM // S5_T) + r // S5_T] = 1.0
    return jnp.asarray(p, BF16)


def _ffn_residual(x1, mod_ref, f0_ref, f1_ref, route_ref, lng_ref, lnb_ref):
    r = lax.broadcasted_iota(jnp.int32, (TM, TM), 0)
    c = lax.broadcasted_iota(jnp.int32, (TM, TM), 1)
    w = [jnp.sum(jnp.where(r == c, route_ref[TOP_K + k:TOP_K + k + 1, :], 0.0), axis=1, keepdims=True)
         for k in range(TOP_K)]
    ffn = w[0] * f0_ref[...].astype(F32) + w[1] * f1_ref[...].astype(F32)
    return _ln(ALPHA * x1 + _mod_part(mod_ref, 5) * ffn) * lng_ref[...] + lnb_ref[...]


def _inproj_kernel(*refs, fused):
    if fused:
        (x1_ref, modp_ref, f0_ref, f1_ref, route_ref, lng_ref, lnb_ref, mod_ref, w_ref, swap_ref,
         x_out_ref, attn_ref, conv_ref, sz_ref, su_ref, dt_ref, u_ref) = refs
        x = _ffn_residual(x1_ref[...], modp_ref, f0_ref, f1_ref, route_ref, lng_ref, lnb_ref)
        x_out_ref[...] = x
    else:
        x_ref, mod_ref, w_ref, swap_ref, attn_ref, conv_ref, sz_ref, su_ref, dt_ref, u_ref = refs
        x = x_ref[...]
    h = _ln(x) * (1.0 + _mod_part(mod_ref, 1)) + _mod_part(mod_ref, 0)
    hb = h.astype(BF16)
    off = 0
    for ref in (attn_ref, conv_ref, sz_ref, su_ref, dt_ref):
        width = ref.shape[-1]
        ref[...] = jnp.dot(hb, w_ref[:, off:off + width], preferred_element_type=F32)
        off += width
    by_step = jnp.dot(swap_ref[...], su_ref[...].astype(BF16), preferred_element_type=F32)
    n_ch = TM // S5_T
    for t in range(S5_T):
        rows = by_step[t * n_ch:(t + 1) * n_ch]
        for g in range(S5_GROUPS):
            u_ref[g, :, t * S5_GROUP:(t + 1) * S5_GROUP] = rows[:, g * S5_GROUP:(g + 1) * S5_GROUP].astype(BF16)


def _inproj(xcat, mods, w1, prev=None):
    n_batch = xcat.shape[0] if prev is None else prev[0].shape[0]
    n_tiles = TOK // TM
    widths = (ATTN_W, CONV_W, 512, 512, DT_W)
    tile = lambda w: pl.BlockSpec((None, TM, w), lambda b, j: (b, j, 0))
    mod_spec = pl.BlockSpec((None, 1, N_MOD * D_MODEL), _mod_row(n_batch))
    const = lambda shape: pl.BlockSpec(shape, lambda b, j: tuple(0 for _ in shape))
    out_shape = ([jax.ShapeDtypeStruct((n_batch, TOK, w), F32) for w in widths]
                 + [jax.ShapeDtypeStruct((S5_GROUPS, S5_NCH, n_batch * S5_ROWW), BF16)])
    out_specs = ([tile(w) for w in widths]
                 + [pl.BlockSpec((S5_GROUPS, TM // S5_T, S5_ROWW), lambda b, j: (0, j, b))])
    own = [mod_spec, const((D_MODEL, PROJ_W)), const((TM, TM))]
    if prev is None:
        in_specs, args = [tile(D_MODEL)] + own, (xcat, mods, w1, _chunk_swap())
    else:
        x1, mods_prev, f, route, ln_g, ln_b = prev
        in_specs = [tile(D_MODEL), mod_spec,
                    pl.BlockSpec((None, None, TM, D_MODEL), lambda b, j: (0, b, j, 0)),
                    pl.BlockSpec((None, None, TM, D_MODEL), lambda b, j: (1, b, j, 0)),
                    pl.BlockSpec((8, TM), lambda b, j: (0, b * n_tiles + j)),
                    const((1, D_MODEL)), const((1, D_MODEL))] + own
        args = (x1, mods_prev, f, f, route, ln_g, ln_b, mods, w1, _chunk_swap())
        out_shape = [jax.ShapeDtypeStruct((n_batch, TOK, D_MODEL), F32)] + out_shape
        out_specs = [tile(D_MODEL)] + out_specs
    return pl.pallas_call(
        functools.partial(_inproj_kernel, fused=prev is not None),
        out_shape=out_shape,
        grid=(n_batch, n_tiles),
        in_specs=in_specs,
        out_specs=out_specs,
        compiler_params=_params(("arbitrary", "arbitrary")),
        name="inproj",
    )(*args)


def _rope(t, cos, sin):
    width = t.shape[-1]
    lane = lax.broadcasted_iota(jnp.int32, t.shape, t.ndim - 1)
    rot = jnp.where(lane % 32 < 16, -pltpu.roll(t, width - 16, t.ndim - 1), pltpu.roll(t, 16, t.ndim - 1))
    return t * cos + rot * sin


def _tile4(v):
    return jnp.concatenate([v, v, v, v], axis=1)


def _head_inv_rms(t, n_heads):
    t2 = t * t
    lane = lax.broadcasted_iota(jnp.int32, t.shape, 1)
    inv = jnp.zeros_like(t)
    for h in range(n_heads):
        ms = jnp.sum(t2[:, h * HEAD_DIM:(h + 1) * HEAD_DIM], axis=1, keepdims=True) * (1.0 / HEAD_DIM)
        inv = jnp.where(lane // HEAD_DIM == h, lax.rsqrt(ms + NORM_EPS), inv)
    return inv


def _stack_heads(qr, kv):
    parts = [qr[:, (kv * GRP + g) * HEAD_DIM:(kv * GRP + g + 1) * HEAD_DIM] for g in range(GRP)]
    return jnp.concatenate(parts, axis=0).astype(BF16)


QPS = 2
LOG2E = math.log2(math.e)


KEY_BLOCK = 128


def _attend(jobs, sink_ref, out_ref, kb_ref, vt_ref, s_ref, p_ref):
    width_q = GRP * QB
    halves = [slice(0, width_q // 2), slice(width_q // 2, width_q)]
    sub = KEY_BLOCK // 8
    for u, (qr, parts) in enumerate(jobs):
        for kv in range(N_KV):
            qs = _stack_heads(qr, kv)
            row = 0
            for key_idx, width, bias in parts:
                for hc in halves:
                    s = _bdot_nt(kb_ref[kv, key_idx, :], qs[hc])
                    if bias is not None:
                        s = s + jnp.concatenate([bias] * (width_q // 2 // QB), axis=1)
                    s_ref[u, kv, row:row + width, hc] = s
                row += width
    for u, (_, parts) in enumerate(jobs):
        n_keys = sum(width for _, width, _ in parts)
        sink_term = {}
        for kv in range(N_KV):
            for g in range(GRP):
                cols = slice(g * QB, (g + 1) * QB)
                mx = None
                for r0 in range(0, n_keys, KEY_BLOCK):
                    blk = jnp.max(s_ref[u, kv, r0:r0 + KEY_BLOCK, cols].reshape(sub, 8, QB), axis=0)
                    mx = blk if mx is None else jnp.maximum(mx, blk)
                m = jnp.max(mx, axis=0, keepdims=True)
                if sink_ref is not None:
                    m = jnp.maximum(m, sink_ref[kv, :, cols])
                    sink_term[kv, g] = jnp.exp2(sink_ref[kv, :, cols] - m)
                m_b = jnp.broadcast_to(m, (KEY_BLOCK, QB))
                for r0 in range(0, n_keys, KEY_BLOCK):
                    p_ref[u, kv, r0:r0 + KEY_BLOCK, cols] = jnp.exp2(
                        s_ref[u, kv, r0:r0 + KEY_BLOCK, cols] - m_b).astype(BF16)
        o_t = {}
        for kv in range(N_KV):
            for hi, hc in enumerate(halves):
                o = None
                row = 0
                for key_idx, width, _ in parts:
                    part = jnp.dot(vt_ref[kv, :, key_idx], p_ref[u, kv, row:row + width, hc],
                                   preferred_element_type=F32)
                    o = part if o is None else o + part
                    row += width
                o_t[kv, hi] = o
        for g in range(GRP):
            hi, gl = divmod(g, GRP // 2)
            cols = slice(gl * QB, (gl + 1) * QB)
            scaled = []
            for kv in range(N_KV):
                den = o_t[kv, hi][HEAD_DIM:HEAD_DIM + 1, cols]
                if sink_ref is not None:
                    den = den + sink_term[kv, g]
                scaled.append(o_t[kv, hi][0:HEAD_DIM, cols] / den)
            pair = jnp.concatenate(scaled, axis=0).T
            for kv in range(N_KV):
                h = kv * GRP + g
                out_ref[u * QB:(u + 1) * QB, h * HEAD_DIM:(h + 1) * HEAD_DIM] = (
                    pair[:, kv * HEAD_DIM:(kv + 1) * HEAD_DIM].astype(out_ref.dtype))


def _wattn_kernel(q_ref, k_ref, v_ref, cos_ref, sin_ref, sink_ref, bias_ref, out_ref, kb_ref, vt_ref, s_ref, p_ref):
    i = pl.program_id(1)

    @pl.when(i == 0)
    def _prep():
        kr = _rope(k_ref[...], cos_ref[...], sin_ref[...])
        v_t = v_ref[...].T
        for kv in range(N_KV):
            kb_ref[kv] = kr[:, kv * HEAD_DIM:(kv + 1) * HEAD_DIM].astype(BF16)
            vt_ref[kv, 0:HEAD_DIM, :] = v_t[kv * HEAD_DIM:(kv + 1) * HEAD_DIM, :].astype(BF16)
            vt_ref[kv, HEAD_DIM:HEAD_DIM + 16, :] = jnp.ones((16, TOK), BF16)

    row0 = pl.multiple_of(i * (QPS * QB), QPS * QB)
    cos = _tile4(cos_ref[pl.ds(row0, QPS * QB), :])
    sin = _tile4(sin_ref[pl.ds(row0, QPS * QB), :])
    qr = _rope(q_ref[...], cos, sin) * (HEAD_DIM ** -0.5 * LOG2E)
    qrs = [qr[u * QB:(u + 1) * QB] for u in range(QPS)]

    ctx_keys = (slice(0, CTX_LEN), CTX_LEN, None)

    @pl.when(i < CTX_LEN // (QPS * QB))
    def _ctx_queries():
        _attend([(q, [ctx_keys]) for q in qrs], sink_ref, out_ref, kb_ref, vt_ref, s_ref, p_ref)

    @pl.when(i >= CTX_LEN // (QPS * QB))
    def _latent_queries():
        jobs = []
        for u in range(QPS):
            j = i * QPS + u - CTX_LEN // QB
            band = jnp.clip(j - 1, 0, SEQ // QB - 3)
            start = pl.multiple_of(CTX_LEN + band * QB, QB)
            bias = bias_ref[j - band]
            jobs.append((qrs[u], [ctx_keys, (pl.ds(start, 3 * QB), 3 * QB, bias)]))
        _attend(jobs, sink_ref, out_ref, kb_ref, vt_ref, s_ref, p_ref)


def _gattn_kernel(q_ref, k_ref, v_ref, cos_ref, sin_ref, qw_ref, kw_ref, out_ref, kb_ref, vt_ref, s_ref, p_ref):
    i = pl.program_id(1)

    @pl.when(i == 0)
    def _prep():
        k = k_ref[...]
        kn = k * _head_inv_rms(k, N_KV) * kw_ref[...]
        kr = _rope(kn, cos_ref[...], sin_ref[...])
        v_t = v_ref[...].T
        for kv in range(N_KV):
            kb_ref[kv] = kr[:, kv * HEAD_DIM:(kv + 1) * HEAD_DIM].astype(BF16)
            vt_ref[kv, 0:HEAD_DIM, :] = v_t[kv * HEAD_DIM:(kv + 1) * HEAD_DIM, :].astype(BF16)
            vt_ref[kv, HEAD_DIM:HEAD_DIM + 16, :] = jnp.ones((16, TOK), BF16)

    row0 = pl.multiple_of(i * (QPS * QB), QPS * QB)
    cos = _tile4(cos_ref[pl.ds(row0, QPS * QB), :])
    sin = _tile4(sin_ref[pl.ds(row0, QPS * QB), :])
    q = q_ref[...]
    qn = q * _head_inv_rms(q, N_HEADS) * qw_ref[...]
    qr = _rope(qn, cos, sin) * (HEAD_DIM ** -0.5 * LOG2E)
    qrs = [qr[u * QB:(u + 1) * QB] for u in range(QPS)]

    @pl.when(i < CTX_LEN // (QPS * QB))
    def _ctx_queries():
        _attend([(q_u, [(slice(0, CTX_LEN), CTX_LEN, None)]) for q_u in qrs], None, out_ref, kb_ref, vt_ref,
                s_ref, p_ref)

    @pl.when(i >= CTX_LEN // (QPS * QB))
    def _latent_queries():
        _attend([(q_u, [(slice(0, TOK), TOK, None)]) for q_u in qrs], None, out_ref, kb_ref, vt_ref, s_ref, p_ref)


def _window_bias():
    q_pos = np.arange(QB)[None, None, :] + QB * np.arange(3)[:, None, None]
    k_pos = np.arange(3 * QB)[None, :, None]
    return jnp.asarray(np.where(np.abs(q_pos - k_pos) <= WINDOW, 0.0, -np.inf), F32)


def _attention(attn, cos, sin, sink_rows, qw, kw):
    n_batch = attn.shape[0]
    grid = (n_batch, TOK // (QPS * QB))
    q_spec = lambda blk: pl.BlockSpec((None, QPS * QB, 512), lambda b, i: (b, i, blk))
    kv_spec = lambda blk: pl.BlockSpec((None, TOK, 128), lambda b, i: (b, 0, blk))
    tab_spec = pl.BlockSpec((TOK, 128), lambda b, i: (0, 0))
    out_spec = pl.BlockSpec((None, QPS * QB, 512), lambda b, i: (b, i, 0))
    out_shape = jax.ShapeDtypeStruct((n_batch, TOK, 512), BF16)
    kv_scratch = [pltpu.VMEM((N_KV, TOK, HEAD_DIM), BF16), pltpu.VMEM((N_KV, HEAD_DIM + 16, TOK), BF16)]
    w_keys = CTX_LEN + 3 * QB
    ya = pl.pallas_call(
        _wattn_kernel, out_shape=out_shape, grid=grid,
        in_specs=[q_spec(0), kv_spec(8), kv_spec(9), tab_spec, tab_spec,
                  pl.BlockSpec((N_KV, 1, GRP * QB), lambda b, i: (0, 0, 0)),
                  pl.BlockSpec((3, 3 * QB, QB), lambda b, i: (0, 0, 0))],
        out_specs=out_spec,
        scratch_shapes=kv_scratch + [pltpu.VMEM((QPS, N_KV, w_keys, GRP * QB), F32),
                                     pltpu.VMEM((QPS, N_KV, w_keys, GRP * QB), BF16)],
        compiler_params=_params(("arbitrary", "arbitrary")), name="window_attention",
    )(attn, attn, attn, cos, sin, sink_rows, _window_bias())
    yg = pl.pallas_call(
        _gattn_kernel, out_shape=out_shape, grid=grid,
        in_specs=[q_spec(1), kv_spec(10), kv_spec(11), tab_spec, tab_spec,
                  pl.BlockSpec((1, 512), lambda b, i: (0, 0)),
                  pl.BlockSpec((1, 128), lambda b, i: (0, 0))],
        out_specs=out_spec,
        scratch_shapes=kv_scratch + [pltpu.VMEM((QPS, N_KV, TOK, GRP * QB), F32),
                                     pltpu.VMEM((QPS, N_KV, TOK, GRP * QB), BF16)],
        compiler_params=_params(("arbitrary", "arbitrary")), name="global_attention",
    )(attn, attn, attn, cos, sin, qw, kw)
    return ya, yg


def _ssd_chunk(s):
    r = s - N_CHUNK
    back = jnp.where(r == 0, 1, jnp.where(r == 1, 0, N_CHUNK + 1 - r))
    return jnp.where(s < N_CHUNK, s, back)


SSD_NB = 2


def _ssd_kernel(x_ref, prev_ref, next_ref, z_ref, dt_ref, cw_ref, cb_ref, dtb_ref, a_ref, d_ref, nw_ref, e_ref,
                out_ref, yf_ref, state_ref, y_ref):
    s = pl.program_id(1)
    chunk = _ssd_chunk(s)
    backward = s >= N_CHUNK

    @pl.when((s == 0) | (s == N_CHUNK))
    def _reset():
        state_ref[...] = jnp.zeros_like(state_ref)

    first = (chunk == 0) | (chunk == CTX_LEN // SSD_T)
    last = (chunk == CTX_LEN // SSD_T - 1) | (chunk == N_CHUNK - 1)
    row = lax.broadcasted_iota(jnp.int32, (SSD_T, CONV_W), 0)
    tr = lax.broadcasted_iota(jnp.int32, (SSD_T, SSD_T), 0)
    tc = lax.broadcasted_iota(jnp.int32, (SSD_T, SSD_T), 1)
    causal = jnp.where(backward, tc - tr, tr - tc) >= 0
    tri = jnp.where(causal, 1.0, 0.0).astype(BF16)
    neg_a = -jnp.exp(a_ref[...])
    expand = e_ref[...]

    def widen(v):
        return sum(jnp.dot(p, expand, preferred_element_type=F32) for p in _split3(v))

    xs = []
    for u in range(SSD_NB):
        xin = x_ref[u]
        prev = jnp.where(first, 0.0, prev_ref[u, 7:8, :])
        nxt = jnp.where(last, 0.0, next_ref[u, 0:1, :])
        xm1 = jnp.where(row == 0, prev, pltpu.roll(xin, 1, 0))
        xp1 = jnp.where(row == SSD_T - 1, nxt, pltpu.roll(xin, SSD_T - 1, 0))
        conv = cw_ref[0:1, :] * xm1 + cw_ref[1:2, :] * xin + cw_ref[2:3, :] * xp1 + cb_ref[...]
        xbc = _silu(conv)
        x = xbc[:, 0:SSD_WIDTH]
        xs.append(x)
        bmat = [xbc[:, SSD_WIDTH + g * SSD_N:SSD_WIDTH + (g + 1) * SSD_N] for g in range(SSD_GROUPS)]
        cmat = [xbc[:, SSD_WIDTH + (SSD_GROUPS + g) * SSD_N:SSD_WIDTH + (SSD_GROUPS + g + 1) * SSD_N]
                for g in range(SSD_GROUPS)]

        dtv = dt_ref[u] + dtb_ref[...]
        dt = jnp.maximum(dtv, 0.0) + jnp.log(1.0 + jnp.exp(-jnp.abs(dtv)))
        hi, mid, lo = _split3(dt * neg_a)
        cs = (jnp.dot(tri, hi, preferred_element_type=F32) + jnp.dot(tri, mid, preferred_element_type=F32)
              + jnp.dot(tri, lo, preferred_element_type=F32))
        cs_t = cs.T

        dt_x = widen(dt)
        cs_x = widen(cs)
        tot_x = jnp.where(backward, cs_x[0:1, :], cs_x[SSD_T - 1:SSD_T, :])
        ecs_x = jnp.exp(cs_x)
        etot_x = jnp.exp(tot_x)
        xd = x * dt_x
        xd_b = xd.astype(BF16)
        xd_end = (xd * jnp.exp(tot_x - cs_x)).astype(BF16)

        gw = SSD_WIDTH // SSD_GROUPS
        for g in range(SSD_GROUPS):
            glanes = slice(g * gw, (g + 1) * gw)
            cb = _bdot_nt(cmat[g], bmat[g])
            st = state_ref[u, g]
            y_off = ecs_x[:, glanes] * jnp.dot(cmat[g].astype(BF16), st.astype(BF16), preferred_element_type=F32)
            state_ref[u, g] = etot_x[:, glanes] * st + jnp.dot(bmat[g].T.astype(BF16), xd_end[:, glanes],
                                                               preferred_element_type=F32)
            for j in range(SSD_HEADS // SSD_GROUPS):
                h = g * (SSD_HEADS // SSD_GROUPS) + j
                lanes = slice(h * SSD_P, (h + 1) * SSD_P)
                seg = jnp.exp(jnp.where(causal, cs[:, h:h + 1] - cs_t[h:h + 1, :], -jnp.inf))
                y_ref[u, :, lanes] = (jnp.dot((cb * seg).astype(BF16), xd_b[:, lanes], preferred_element_type=F32)
                                      + y_off[:, j * SSD_P:(j + 1) * SSD_P])

    row0 = pl.multiple_of(chunk * SSD_T, SSD_T)

    @pl.when(jnp.logical_not(backward))
    def _keep():
        for u in range(SSD_NB):
            yf_ref[u, pl.ds(row0, SSD_T), :] = y_ref[u]

    @pl.when(backward)
    def _finish():
        for u in range(SSD_NB):
            ytot = yf_ref[u, pl.ds(row0, SSD_T), :] + y_ref[u] + d_ref[...] * xs[u]
            gated = ytot * _silu(z_ref[u])
            ms = jnp.mean(gated * gated, axis=1, keepdims=True)
            out_ref[u] = (gated * lax.rsqrt(ms + NORM_EPS) * nw_ref[...]).astype(out_ref.dtype)


def _ssd(conv_in, sz, dt, conv_w, conv_b, dt_bias, a_log, d_exp, norm_w):
    n_batch = conv_in.shape[0]
    assert n_batch % SSD_NB == 0
    halo = SSD_T // 8
    n_halo = TOK // 8

    def chunk_map(b, s):
        return (b, _ssd_chunk(s), 0)

    def out_map(b, s):
        return (b, jnp.where(s < N_CHUNK, 1, _ssd_chunk(s)), 0)

    const = lambda shape: pl.BlockSpec(shape, lambda b, s: tuple(0 for _ in shape))
    dir_spec = pl.BlockSpec((None, 1, 128), lambda b, s: (s // N_CHUNK, 0, 0))
    expand = jnp.asarray(np.repeat(np.eye(128, SSD_HEADS), SSD_P, axis=1), BF16)
    return pl.pallas_call(
        _ssd_kernel,
        out_shape=jax.ShapeDtypeStruct((n_batch, TOK, SSD_WIDTH), BF16),
        grid=(n_batch // SSD_NB, 2 * N_CHUNK),
        in_specs=[
            pl.BlockSpec((SSD_NB, SSD_T, CONV_W), chunk_map),
            pl.BlockSpec((SSD_NB, 8, CONV_W), lambda b, s: (b, jnp.maximum(_ssd_chunk(s) * halo - 1, 0), 0)),
            pl.BlockSpec((SSD_NB, 8, CONV_W), lambda b, s: (b, jnp.minimum((_ssd_chunk(s) + 1) * halo, n_halo - 1), 0)),
            pl.BlockSpec((SSD_NB, SSD_T, SSD_WIDTH), chunk_map),
            pl.BlockSpec((SSD_NB, SSD_T, 128), lambda b, s: (b, _ssd_chunk(s), s // N_CHUNK)),
            const((3, CONV_W)), const((1, CONV_W)), dir_spec, dir_spec,
            const((1, SSD_WIDTH)), const((1, SSD_WIDTH)), const((128, SSD_WIDTH)),
        ],
        out_specs=pl.BlockSpec((SSD_NB, SSD_T, SSD_WIDTH), out_map),
        scratch_shapes=[pltpu.VMEM((SSD_NB, TOK, SSD_WIDTH), F32),
                        pltpu.VMEM((SSD_NB, SSD_GROUPS, SSD_N, SSD_WIDTH // SSD_GROUPS), F32),
                        pltpu.VMEM((SSD_NB, SSD_T, SSD_WIDTH), F32)],
        compiler_params=_params(("arbitrary", "arbitrary")),
        name="ssd",
    )(conv_in, conv_in, conv_in, sz, dt, conv_w, conv_b, dt_bias, a_log, d_exp, norm_w, expand)


S5_SW = 4 * 128


def _s5_kernel(u_ref, m_ref, h_ref, g_ref, a_ref, y_ref, loc_ref, prev_ref, *, n_batch):
    u = u_ref[...]
    loc_ref[...] = jnp.dot(u, h_ref[...], preferred_element_type=F32)
    a = [a_ref[k:k + 1, :] for k in range(4)]

    def step(k, carry):
        c_f = k
        c_b = jnp.where(k < S5_CTX_CH, S5_CTX_CH - 1 - k, S5_NCH + S5_CTX_CH - 1 - k)
        new = []
        for d, c in enumerate((c_f, c_b)):
            rows = pl.ds(pl.multiple_of(c * n_batch, n_batch), n_batch)
            re_l, im_l = slice(2 * d * 128, (2 * d + 1) * 128), slice((2 * d + 1) * 128, (2 * d + 2) * 128)
            s_re, s_im = carry[2 * d], carry[2 * d + 1]
            prev_ref[rows, re_l] = s_re
            prev_ref[rows, im_l] = s_im
            a_re, a_im = a[2 * d], a[2 * d + 1]
            new += [a_re * s_re - a_im * s_im + loc_ref[rows, re_l], a_re * s_im + a_im * s_re + loc_ref[rows, im_l]]
        return tuple(new)

    zero = jnp.zeros((n_batch, 128), F32)
    lax.fori_loop(0, S5_NCH, step, (zero,) * 4, unroll=2)
    y = jnp.dot(u, m_ref[...], preferred_element_type=F32)
    y = y + jnp.dot(prev_ref[...].astype(BF16), g_ref[...], preferred_element_type=F32)
    y_ref[...] = y.astype(y_ref.dtype)


def _s5(u_g, mats, n_batch):
    m_all, h_all, g_all, a16 = mats
    rows = u_g.shape[1]
    grp = lambda shape: pl.BlockSpec((None,) + shape, lambda g: (g,) + tuple(0 for _ in shape))
    return pl.pallas_call(
        functools.partial(_s5_kernel, n_batch=n_batch),
        out_shape=jax.ShapeDtypeStruct((S5_GROUPS, rows, S5_ROWW), BF16),
        grid=(S5_GROUPS,),
        in_specs=[grp((rows, S5_ROWW)), grp((S5_ROWW, S5_ROWW)), grp((S5_ROWW, S5_SW)), grp((S5_SW, S5_ROWW)),
                  grp((4, 128))],
        out_specs=grp((rows, S5_ROWW)),
        scratch_shapes=[pltpu.VMEM((rows, S5_SW), F32), pltpu.VMEM((rows, S5_SW), F32)],
        compiler_params=_params(("arbitrary",)),
        name="s5",
    )(u_g, m_all, h_all, g_all, a16)


def _cmul(a, b):
    return a[0] * b[0] - a[1] * b[1], a[0] * b[1] + a[1] * b[0]


def _s5_matrices(a_re, a_im, log_dt, b_re, b_im, c_re, c_im):
    hp = lax.Precision.HIGHEST
    t = jnp.arange(S5_T + 1, dtype=F32)
    m_sum = 0.0
    h_all, g_all, a16_all = [], [], []
    c = (c_re.astype(F32), c_im.astype(F32))
    for direction in range(2):
        are = jnp.minimum(a_re[direction].astype(F32), -1e-4)
        aim = a_im[direction].astype(F32)
        dt = jnp.exp(log_dt[direction].astype(F32))[:, None]
        mag = jnp.exp(t[:, None, None] * (are * dt)[None])
        ang = t[:, None, None] * (aim * dt)[None]
        pw = (mag * jnp.cos(ang), mag * jnp.sin(ang))
        num = (pw[0][1] - 1.0, pw[1][1])
        den = are * are + aim * aim
        coef = ((num[0] * are + num[1] * aim) / den, (num[1] * are - num[0] * aim) / den)
        bbar = _cmul((coef[0][..., None], coef[1][..., None]), (b_re.astype(F32), b_im.astype(F32)))
        pb = _cmul((pw[0][:S5_T, :, :, None], pw[1][:S5_T, :, :, None]), (bbar[0][None], bbar[1][None]))
        taps = (jnp.einsum('gon,tgni->tgoi', c[0], pb[0], precision=hp)
                - jnp.einsum('gon,tgni->tgoi', c[1], pb[1], precision=hp))
        ti = jnp.arange(S5_T)
        lag = (ti[None, :] - ti[:, None]) if direction == 0 else (ti[:, None] - ti[None, :])
        k_full = taps[jnp.clip(lag, 0, S5_T - 1)]
        k_full = jnp.where((lag >= 0)[:, :, None, None, None], k_full, 0.0)
        m_sum = m_sum + k_full.transpose(2, 0, 4, 1, 3).reshape(S5_GROUPS, S5_ROWW, S5_ROWW)
        e_in = (S5_T - 1 - ti) if direction == 0 else ti
        hb = _cmul((pw[0][e_in][..., None], pw[1][e_in][..., None]), (bbar[0][None], bbar[1][None]))
        zpad = jnp.zeros_like(hb[0])
        h_mat = jnp.concatenate([hb[0], zpad, hb[1], zpad], axis=2)
        h_all.append(h_mat.transpose(1, 0, 3, 2).reshape(S5_GROUPS, S5_ROWW, 4 * S5_STATE))
        e_out = (ti + 1) if direction == 0 else (S5_T - ti)
        cp = _cmul((c[0][None], c[1][None]),
                   (pw[0][e_out][:, :, None, :], pw[1][e_out][:, :, None, :]))
        zpad = jnp.zeros_like(cp[0])
        g_mat = jnp.concatenate([cp[0], zpad, -cp[1], zpad], axis=3)
        g_all.append(g_mat.transpose(1, 3, 0, 2).reshape(S5_GROUPS, 4 * S5_STATE, S5_ROWW))
        zrow = jnp.zeros_like(pw[0][S5_T])
        a16_all += [jnp.concatenate([pw[0][S5_T], zrow], axis=1), jnp.concatenate([pw[1][S5_T], zrow], axis=1)]
    return (m_sum.astype(BF16), jnp.concatenate(h_all, axis=2).astype(BF16),
            jnp.concatenate(g_all, axis=1).astype(BF16), jnp.stack(a16_all, axis=1))


def _route(logits_t, bias, base, before):
    scores = [_sigmoid(logits_t[j * N_EXPERT_GROUPS:(j + 1) * N_EXPERT_GROUPS]) for j in range(PER_GROUP)]
    sel = [scores[j] + bias[j * N_EXPERT_GROUPS:(j + 1) * N_EXPERT_GROUPS] for j in range(PER_GROUP)]
    hi1, lo1 = jnp.maximum(sel[0], sel[1]), jnp.minimum(sel[0], sel[1])
    hi2, lo2 = jnp.maximum(sel[2], sel[3]), jnp.minimum(sel[2], sel[3])
    group_score = jnp.maximum(hi1, hi2) + jnp.maximum(jnp.minimum(hi1, hi2), jnp.maximum(lo1, lo2))
    gid = lax.broadcasted_iota(jnp.int32, group_score.shape, 0)
    best = jnp.max(group_score, axis=0, keepdims=True)
    grp = jnp.min(jnp.where(group_score == best, gid, N_EXPERT_GROUPS), axis=0, keepdims=True)
    pick = gid == grp
    v = [jnp.sum(jnp.where(pick, sel[j], 0.0), axis=0, keepdims=True) for j in range(PER_GROUP)]
    sc = [jnp.sum(jnp.where(pick, scores[j], 0.0), axis=0, keepdims=True) for j in range(PER_GROUP)]
    rank = []
    for j in range(PER_GROUP):
        r = jnp.zeros_like(grp)
        for i in range(PER_GROUP):
            if i < j:
                r = r + jnp.where(v[i] >= v[j], 1, 0)
            elif i > j:
                r = r + jnp.where(v[i] > v[j], 1, 0)
        rank.append(r)
    e, w, locs = [], [], []
    for k in range(TOP_K):
        loc = sum(jnp.where(rank[j] == k, j, 0) for j in range(PER_GROUP))
        locs.append(loc)
        e.append((grp * PER_GROUP + loc).astype(F32))
        w.append(sum(jnp.where(rank[j] == k, sc[j], 0.0) for j in range(PER_GROUP)))
    wsum = w[0] + w[1]
    chosen = [jnp.where(pick & ((locs[0] == j) | (locs[1] == j)), 1.0, 0.0) for j in range(PER_GROUP)]
    onehot = jnp.concatenate(chosen, axis=0)
    seen = jnp.dot(onehot.astype(BF16), before, preferred_element_type=F32) + base
    pos = []
    for k in range(TOP_K):
        hit = sum(jnp.where(pick & (locs[k] == j), seen[j * N_EXPERT_GROUPS:(j + 1) * N_EXPERT_GROUPS], 0.0)
                  for j in range(PER_GROUP))
        pos.append(jnp.sum(hit, axis=0, keepdims=True))
    rows = [e[0], e[1], w[0] / wsum, w[1] / wsum, pos[0], pos[1]]
    route = jnp.concatenate(rows + [jnp.zeros_like(wsum)] * (8 - len(rows)), axis=0)
    return route, base + jnp.sum(onehot, axis=1, keepdims=True)


def _gelu_tanh(x):
    return 0.5 * x * (1.0 + jnp.tanh(math.sqrt(2.0 / math.pi) * (x + 0.044715 * (x * x * x))))


MERGE_LAG = 3


def _merge_kernel(xn_ref, modn_ref, x_ref, mod_ref, modt_ref, ya_ref, ys_ref, yg_ref, y5_ref, su_ref, s5d_ref,
                  wg_ref, wb_ref, wglu_ref, wout_ref, lng_ref, lnb_ref, rw_ref, rb_ref, before_ref, swap_ref,
                  x1_ref, h2_ref, route_ref, count_ref, base_ref, y5s_ref, y5p_ref, hbs_ref, v_ref, h2s_ref):
    s = pl.program_id(0)

    @pl.when(s == 0)
    def _reset():
        base_ref[...] = jnp.zeros_like(base_ref)
        y5p_ref[...] = jnp.zeros_like(y5p_ref)
        hbs_ref[...] = jnp.zeros_like(hbs_ref)
        v_ref[...] = jnp.zeros_like(v_ref)
        h2s_ref[...] = jnp.zeros_like(h2s_ref)

    logits_t = _bdot_nt(rw_ref[...], h2s_ref[...])
    v_prev = v_ref[...]

    hb = hbs_ref[...]
    glu = jnp.dot(y5p_ref[...], wglu_ref[...], preferred_element_type=F32)
    y5 = (glu[:, 0:S5_WIDTH] * _sigmoid(glu[:, S5_WIDTH:2 * S5_WIDTH])).astype(BF16)
    def select_experts():
        base_old = base_ref[...]
        route, base_new = _route(logits_t, rb_ref[...], base_old, before_ref[...])
        route_ref[...] = route
        base = jnp.where(s >= MERGE_LAG, base_new, base_old)
        base_ref[...] = base
        count_ref[...] = jnp.broadcast_to(base, count_ref.shape)

    def post_norm():
        x1 = _ln(v_prev) * lng_ref[...] + lnb_ref[...]
        x1_ref[...] = x1
        h2 = (_ln(x1) * (1.0 + _mod_part(modt_ref, 4)) + _mod_part(modt_ref, 3)).astype(BF16)
        h2_ref[...] = h2
        h2s_ref[...] = h2

    def relayout():
        n_ch = TM // S5_T
        for t in range(S5_T):
            for g in range(S5_GROUPS):
                y5s_ref[t * n_ch:(t + 1) * n_ch, g * S5_GROUP:(g + 1) * S5_GROUP] = (
                    y5_ref[g, :, t * S5_GROUP:(t + 1) * S5_GROUP])

    side_work = [post_norm, select_experts, relayout, lambda: None]
    acc = jnp.zeros((TM, D_MODEL), F32)
    for n, y in enumerate((ya_ref[...], ys_ref[...], yg_ref[...], y5)):
        gate = _sigmoid(jnp.dot(hb, wg_ref[:, n * D_MODEL:(n + 1) * D_MODEL], preferred_element_type=F32))
        acc = acc + gate * jnp.dot(y, wb_ref[n], preferred_element_type=F32)
        side_work[n]()
    mix = jnp.dot(acc.astype(BF16), wout_ref[...], preferred_element_type=F32)
    v_ref[...] = ALPHA * x_ref[...] + _mod_part(mod_ref, 2) * mix

    y5_pre = jnp.dot(swap_ref[...], y5s_ref[...], preferred_element_type=F32) + s5d_ref[...] * su_ref[...]
    y5p_ref[...] = _gelu_tanh(y5_pre).astype(BF16)
    hbs_ref[...] = (_ln(xn_ref[...]) * (1.0 + _mod_part(modn_ref, 1)) + _mod_part(modn_ref, 0)).astype(BF16)


def _merge(xcat, mods, ya, ys, yg, y5, su, s5_d, wg, wb, wglu, wout, ln_g, ln_b, rw_t, rb):
    n_batch = xcat.shape[0]
    n_tiles = TOK // TM
    n_all = n_batch * n_tiles

    def at(lag):
        def split(s):
            t = jnp.clip(s - lag, 0, n_all - 1)
            return t // n_tiles, t % n_tiles
        return split

    def tile(w, lag):
        return pl.BlockSpec((None, TM, w), lambda s: at(lag)(s) + (0,))

    def mod_spec(lag):
        def index(s):
            b, j = at(lag)(s)
            return jnp.where(j == 0, n_batch, b), 0, 0
        return pl.BlockSpec((None, 1, N_MOD * D_MODEL), index)

    def flat(lag):
        return lambda s: (0, jnp.clip(s - lag, 0, n_all - 1))

    const = lambda shape: pl.BlockSpec(shape, lambda s: tuple(0 for _ in shape))
    before = jnp.asarray(np.triu(np.ones((TM, TM)), 1), BF16)
    return pl.pallas_call(
        _merge_kernel,
        out_shape=[jax.ShapeDtypeStruct((n_batch, TOK, D_MODEL), F32),
                   jax.ShapeDtypeStruct((n_batch, TOK, D_MODEL), BF16),
                   jax.ShapeDtypeStruct((8, n_batch * TOK), F32),
                   jax.ShapeDtypeStruct((N_EXPERTS, 128), F32)],
        grid=(n_all + MERGE_LAG,),
        in_specs=[tile(D_MODEL, 0), mod_spec(0), tile(D_MODEL, 1), mod_spec(1), mod_spec(2),
                  tile(512, 1), tile(512, 1), tile(512, 1),
                  pl.BlockSpec((S5_GROUPS, TM // S5_T, S5_ROWW), lambda s: (0,) + at(0)(s)[::-1]),
                  tile(512, 0), const((1, S5_WIDTH)),
                  const((D_MODEL, N_BRANCH * D_MODEL)), const((N_BRANCH, 512, D_MODEL)),
                  const((S5_WIDTH, 2 * S5_WIDTH)), const((D_MODEL, D_MODEL)),
                  const((1, D_MODEL)), const((1, D_MODEL)),
                  const((N_EXPERTS, D_MODEL)), const((N_EXPERTS, 1)), const((TM, TM)), const((TM, TM))],
        out_specs=[tile(D_MODEL, 2), tile(D_MODEL, 2),
                   pl.BlockSpec((8, TM), flat(MERGE_LAG)),
                   const((N_EXPERTS, 128))],
        scratch_shapes=[pltpu.VMEM((N_EXPERTS, 1), F32), pltpu.VMEM((TM, S5_WIDTH), BF16),
                        pltpu.VMEM((TM, S5_WIDTH), BF16), pltpu.VMEM((TM, D_MODEL), BF16),
                        pltpu.VMEM((TM, D_MODEL), F32), pltpu.VMEM((TM, D_MODEL), BF16)],
        compiler_params=_params(("arbitrary",)),
        name="merge",
    )(xcat, mods, xcat, mods, mods, ya, ys, yg, y5, su, s5_d, wg, wb, wglu, wout, ln_g, ln_b, rw_t, rb, before,
      _chunk_swap())


def _moe_kernel(be_ref, nu_ref, x_ref, wg_ref, wu_ref, wd_ref, y_ref, wgu_s, wd_s):
    i = pl.program_id(0)
    used = i < nu_ref[0]

    @pl.when(used & ((i == 0) | (be_ref[i] != be_ref[jnp.maximum(i - 1, 0)])))
    def _cast():
        wgu_s[:, 0:D_EXPERT] = wg_ref[...].astype(BF16)
        wgu_s[:, D_EXPERT:2 * D_EXPERT] = wu_ref[...].astype(BF16)
        wd_s[...] = wd_ref[...].astype(BF16)

    @pl.when(used)
    def _block():
        gu = jnp.dot(x_ref[...], wgu_s[...], preferred_element_type=F32)
        mid = _silu(gu[:, 0:D_EXPERT]) * gu[:, D_EXPERT:2 * D_EXPERT]
        y_ref[...] = jnp.dot(mid.astype(BF16), wd_s[...], preferred_element_type=F32).astype(y_ref.dtype)


def _moe_experts(block_expert, n_used, xs, w_gate, w_up, w_down, layer):
    n_blocks = block_expert.shape[0]
    return pl.pallas_call(
        _moe_kernel,
        out_shape=jax.ShapeDtypeStruct((n_blocks * MOE_ROWS, D_MODEL), BF16),
        grid_spec=pltpu.PrefetchScalarGridSpec(
            num_scalar_prefetch=2, grid=(n_blocks,),
            in_specs=[pl.BlockSpec((MOE_ROWS, D_MODEL), lambda i, be, nu: (i, 0)),
                      pl.BlockSpec((None, None, D_MODEL, D_EXPERT), lambda i, be, nu: (layer, be[i], 0, 0)),
                      pl.BlockSpec((None, None, D_MODEL, D_EXPERT), lambda i, be, nu: (layer, be[i], 0, 0)),
                      pl.BlockSpec((None, None, D_EXPERT, D_MODEL), lambda i, be, nu: (layer, be[i], 0, 0))],
            out_specs=pl.BlockSpec((MOE_ROWS, D_MODEL), lambda i, be, nu: (i, 0)),
            scratch_shapes=[pltpu.VMEM((D_MODEL, 2 * D_EXPERT), BF16), pltpu.VMEM((D_EXPERT, D_MODEL), BF16)]),
        compiler_params=_params(("arbitrary",)),
        name="moe_experts",
    )(block_expert, n_used, xs, w_gate, w_up, w_down)


def _final_kernel(x_ref, mod_ref, f0_ref, f1_ref, route_ref, lng_ref, lnb_ref, o_ref):
    o_ref[...] = _ffn_residual(x_ref[...], mod_ref, f0_ref, f1_ref, route_ref, lng_ref, lnb_ref)


def _final(x1, mods, f, route, ln_g, ln_b):
    n_batch = x1.shape[0]
    n_tiles = TOK // TM
    skip = CTX_LEN // TM
    const = pl.BlockSpec((1, D_MODEL), lambda b, j: (0, 0))
    return pl.pallas_call(
        _final_kernel,
        out_shape=jax.ShapeDtypeStruct((n_batch, SEQ, D_MODEL), F32),
        grid=(n_batch, n_tiles - skip),
        in_specs=[pl.BlockSpec((None, TM, D_MODEL), lambda b, j: (b, j + skip, 0)),
                  pl.BlockSpec((None, 1, N_MOD * D_MODEL), lambda b, j: (b, 0, 0)),
                  pl.BlockSpec((None, None, TM, D_MODEL), lambda b, j: (0, b, j + skip, 0)),
                  pl.BlockSpec((None, None, TM, D_MODEL), lambda b, j: (1, b, j + skip, 0)),
                  pl.BlockSpec((8, TM), lambda b, j: (0, b * n_tiles + j + skip)),
                  const, const],
        out_specs=pl.BlockSpec((None, TM, D_MODEL), lambda b, j: (b, j, 0)),
        compiler_params=_params(("arbitrary", "arbitrary")),
        name="ffn_residual",
    )(x1, mods, f, f, route, ln_g, ln_b)


def _lookup(table, idx):
    ids = jnp.arange(table.shape[0], dtype=jnp.int32).reshape((-1,) + (1,) * idx.ndim)
    return jnp.sum(jnp.where(idx[None] == ids, table.reshape(ids.shape), 0), axis=0)


def _dispatch(route, counts_rows):
    n_tok = route.shape[1]
    n_assign = n_tok * TOP_K
    n_blocks = -(-(n_assign + N_EXPERTS * (MOE_ROWS - 1)) // MOE_ROWS)
    cap = n_blocks * MOE_ROWS
    counts = counts_rows[:, 0].astype(jnp.int32).reshape(PER_GROUP, N_EXPERT_GROUPS).T.reshape(N_EXPERTS)
    padded = (counts + MOE_ROWS - 1) // MOE_ROWS * MOE_ROWS
    pad_end = jnp.cumsum(padded)
    pad_start = pad_end - padded
    expert = route[0:TOP_K].astype(jnp.int32)
    pos = _lookup(pad_start, expert) + route[2 * TOP_K:3 * TOP_K].astype(jnp.int32)
    block_start = jnp.arange(n_blocks, dtype=jnp.int32) * MOE_ROWS
    block_expert = jnp.minimum(jnp.sum((pad_end[None, :] <= block_start[:, None]).astype(jnp.int32), axis=1),
                               N_EXPERTS - 1)
    n_used = (pad_end[-1:] // MOE_ROWS).astype(jnp.int32)
    gap = padded - counts
    gap_end = jnp.cumsum(gap)
    gap_first = jnp.concatenate([pad_start + counts - (gap_end - gap), pad_end[-1:] - gap_end[-1:]])
    k = jnp.arange(cap - n_assign, dtype=jnp.int32)
    owner = jnp.sum((gap_end[None, :] <= k[:, None]).astype(jnp.int32), axis=1)
    free_slot = _lookup(gap_first, owner) + k
    tok = jnp.broadcast_to(jnp.arange(n_tok, dtype=jnp.int32), (TOP_K, n_tok)).reshape(-1)
    keys = jnp.concatenate([pos.reshape(-1), free_slot])
    vals = jnp.concatenate([tok, k % n_tok])
    _, slot_tok = lax.sort((keys, vals), num_keys=1)
    return slot_tok, pos, block_expert, n_used


def _col(w, part):
    return w[:, _IN_OFF[part]:_IN_OFF[part + 1]]


def _proj_weights(w_in):
    zeros = jnp.zeros((D_MODEL, 128 - SSD_HEADS), w_in.dtype)
    dt = _col(w_in, _SDT)
    parts = [_col(w_in, p) for p in (_AQ, _GQ, _AK, _AV, _GK, _GV, _SX, _SB, _SC, _SZ, _SU)]
    parts += [dt[:, 0:SSD_HEADS], zeros, dt[:, SSD_HEADS:2 * SSD_HEADS], zeros]
    return jnp.concatenate(parts, axis=1).astype(BF16), _col(w_in, _GATE).astype(BF16)


def _rope_tables():
    rows = SEQ // GRID_W
    row = jnp.repeat(jnp.arange(rows, dtype=F32), GRID_W)
    col = jnp.tile(jnp.arange(GRID_W, dtype=F32), rows)
    axis_dim = HEAD_DIM // 2
    inv_freq = ROPE_THETA ** (-jnp.arange(0, axis_dim, 2, dtype=F32) / axis_dim)
    ang_r = row[:, None] * inv_freq
    ang_c = col[:, None] * inv_freq
    ang = jnp.concatenate([ang_r, ang_r, ang_c, ang_c], axis=-1)
    cos = jnp.concatenate([jnp.ones((CTX_LEN, HEAD_DIM), F32), jnp.cos(ang)], axis=0)
    sin = jnp.concatenate([jnp.zeros((CTX_LEN, HEAD_DIM), F32), jnp.sin(ang)], axis=0)
    return jnp.tile(cos, (1, 2)), jnp.tile(sin, (1, 2))


def _pad_lanes(v, width=128):
    return jnp.pad(v, ((0, 0), (0, width - v.shape[-1])))


def kernel(x, c, ctx, c_ctx, mod_w, mod_b, w_in, wa_sink, ga_q_norm, ga_k_norm, ssd_conv_w, ssd_conv_b, ssd_dt_bias, ssd_a_log, ssd_d, ssd_norm_w, s5_a_re, s5_a_im, s5_log_dt, s5_b_re, s5_b_im, s5_c_re, s5_c_im, s5_d, s5_w_glu, w_branch, w_out, ln1_g, ln1_b, ln2_g, ln2_b, router_w, router_bias, moe_w_gate, moe_w_up, moe_w_down):
    n_batch = x.shape[0]
    n_tok = n_batch * TOK
    xcat = jnp.concatenate([ctx, x], axis=1)
    mod_rows = -(-(n_batch + 1) // 8) * 8
    cond = jnp.zeros((mod_rows, D_MODEL), F32).at[:n_batch].set(c).at[n_batch].set(c_ctx)
    mods_all = _modulation(cond, mod_w, mod_b)
    cos, sin = _rope_tables()

    perm = np.array([g * PER_GROUP + j for j in range(PER_GROUP) for g in range(N_EXPERT_GROUPS)])
    rw_t = router_w.T[perm].astype(BF16)
    rb = router_bias.astype(F32)[perm].reshape(N_EXPERTS, 1)

    prev = None
    for layer in range(DEPTH):
        mods = mods_all[layer].reshape(mod_rows, 1, N_MOD * D_MODEL)
        w1, wg = _proj_weights(w_in[layer])
        if prev is None:
            attn, conv_in, sz, su, dt, u_g = _inproj(xcat, mods, w1)
        else:
            xcat, attn, conv_in, sz, su, dt, u_g = _inproj(None, mods, w1, prev)

        sink_rows = jnp.repeat(wa_sink[layer].astype(F32).reshape(N_KV, GRP) * LOG2E, QB,
                               axis=1).reshape(N_KV, 1, GRP * QB)
        qw = jnp.tile(ga_q_norm[layer].astype(F32), N_HEADS).reshape(1, 512)
        kw = jnp.tile(ga_k_norm[layer].astype(F32), N_KV).reshape(1, 128)
        ya, yg = _attention(attn, cos, sin, sink_rows, qw, kw)

        dt_bias = _pad_lanes(ssd_dt_bias[layer].astype(F32).reshape(2, SSD_HEADS)).reshape(2, 1, 128)
        a_log = _pad_lanes(ssd_a_log[layer].astype(F32)).reshape(2, 1, 128)
        d_exp = jnp.repeat(ssd_d[layer].astype(F32), SSD_P).reshape(1, SSD_WIDTH)
        ys = _ssd(conv_in, sz, dt, ssd_conv_w[layer].astype(F32), ssd_conv_b[layer].astype(F32).reshape(1, CONV_W),
                  dt_bias, a_log, d_exp, ssd_norm_w[layer].astype(F32).reshape(1, SSD_WIDTH))

        mats = _s5_matrices(s5_a_re[layer], s5_a_im[layer], s5_log_dt[layer], s5_b_re[layer], s5_b_im[layer],
                            s5_c_re[layer], s5_c_im[layer])
        y5 = _s5(u_g.reshape(S5_GROUPS, S5_NCH * n_batch, S5_ROWW), mats, n_batch)
        y5 = y5.reshape(S5_GROUPS, S5_NCH, n_batch * S5_ROWW)

        x1, h2, route, counts = _merge(
            xcat, mods, ya, ys, yg, y5, su, s5_d[layer].astype(F32).reshape(1, S5_WIDTH), wg,
            w_branch[layer].astype(BF16), s5_w_glu[layer].astype(BF16), w_out[layer].astype(BF16),
            ln1_g[layer].reshape(1, D_MODEL), ln1_b[layer].reshape(1, D_MODEL), rw_t, rb)

        slot_tok, pos, block_expert, n_used = _dispatch(route, counts)
        xs = h2.reshape(n_tok, D_MODEL)[slot_tok]
        y_slots = _moe_experts(block_expert, n_used, xs, moe_w_gate, moe_w_up, moe_w_down, layer)
        f = y_slots[pos.reshape(-1)].reshape(TOP_K, n_batch, TOK, D_MODEL)
        prev = (x1, mods, f, route, ln2_g[layer].reshape(1, D_MODEL), ln2_b[layer].reshape(1, D_MODEL))
    return _final(*prev)
```

```python
import functools
import math

import jax
import jax.numpy as jnp
import numpy as np
from jax import lax
from jax.experimental import pallas as pl
from jax.experimental.pallas import tpu as pltpu

F32 = jnp.float32
BF16 = jnp.bfloat16

D_MODEL = 1024
DEPTH = 4
GRID_W = 64
CTX_LEN = 256
SEQ = 2048
TOK = CTX_LEN + SEQ
N_MOD = 6
HEAD_DIM = 64
N_BRANCH = 4
N_HEADS = 8
N_KV = 2
GRP = N_HEADS // N_KV
WINDOW = 128
QB = 128
ROPE_THETA = 10000.0

SSD_HEADS = 8
SSD_P = 64
SSD_GROUPS = 2
SSD_N = 128
SSD_T = 128
SSD_WIDTH = SSD_HEADS * SSD_P
SSD_CONV_CH = SSD_WIDTH + 2 * SSD_GROUPS * SSD_N
N_CHUNK = TOK // SSD_T

S5_GROUP = 16
S5_WIDTH = 512
S5_GROUPS = S5_WIDTH // S5_GROUP
S5_STATE = 64
S5_T = 16
S5_ROWW = S5_T * S5_GROUP
S5_NCH = TOK // S5_T
S5_CTX_CH = CTX_LEN // S5_T

N_EXPERTS = 32
N_EXPERT_GROUPS = 8
PER_GROUP = N_EXPERTS // N_EXPERT_GROUPS
TOP_K = 2
D_EXPERT = 512
MOE_ROWS = 256

ALPHA = (2 * DEPTH) ** 0.25
NORM_EPS = 1e-6

TM = 256
VMEM_LIMIT = 56 * 1024 * 1024

_IN_SIZES = (512, 128, 128, 512, 128, 128, 512, 512, 256, 256, 16, 512, 4096)
_IN_OFF = np.concatenate([[0], np.cumsum(_IN_SIZES)]).astype(int)
(_AQ, _AK, _AV, _GQ, _GK, _GV, _SX, _SZ, _SB, _SC, _SDT, _SU, _GATE) = range(13)

ATTN_W = 1536
CONV_W = 1024
DT_W = 256
PROJ_W = ATTN_W + CONV_W + 512 + 512 + DT_W


def _params(sem=None):
    return pltpu.CompilerParams(dimension_semantics=sem, vmem_limit_bytes=VMEM_LIMIT)


def _ln(x):
    mu = jnp.mean(x, axis=-1, keepdims=True)
    xc = x - mu
    var = jnp.mean(xc * xc, axis=-1, keepdims=True)
    return xc * lax.rsqrt(var + NORM_EPS)


def _sigmoid(x):
    return 0.5 * jnp.tanh(0.5 * x) + 0.5


def _silu(x):
    return x * _sigmoid(x)


def _bdot(a, b):
    return jnp.dot(a.astype(BF16), b.astype(BF16), preferred_element_type=F32)


def _bdot_nt(a, b):
    return lax.dot_general(a.astype(BF16), b.astype(BF16), (((1,), (1,)), ((), ())),
                           preferred_element_type=F32)


def _split3(v):
    hi = v.astype(BF16)
    r1 = v - hi.astype(F32)
    mid = r1.astype(BF16)
    lo = (r1 - mid.astype(F32)).astype(BF16)
    return hi, mid, lo


def _mod_kernel(c_ref, w_ref, b_ref, o_ref):
    c = c_ref[...]
    o_ref[...] = _bdot(_silu(c), w_ref[...]) + b_ref[...]


def _modulation(cond, mod_w, mod_b):
    n_layer = mod_w.shape[0]
    rows = cond.shape[0]
    return pl.pallas_call(
        _mod_kernel,
        out_shape=jax.ShapeDtypeStruct((n_layer, rows, N_MOD * D_MODEL), F32),
        grid=(n_layer, N_MOD),
        in_specs=[
            pl.BlockSpec((rows, D_MODEL), lambda l, n: (0, 0)),
            pl.BlockSpec((None, D_MODEL, D_MODEL), lambda l, n: (l, 0, n)),
            pl.BlockSpec((None, 1, D_MODEL), lambda l, n: (l, 0, n)),
        ],
        out_specs=pl.BlockSpec((None, rows, D_MODEL), lambda l, n: (l, 0, n)),
        compiler_params=_params(("arbitrary", "arbitrary")),
        name="modulation",
    )(cond, mod_w, mod_b.reshape(n_layer, 1, N_MOD * D_MODEL))


def _mod_part(mod_ref, k):
    return mod_ref[:, k * D_MODEL:(k + 1) * D_MODEL]


def _mod_row(n_batch):
    return lambda b, j: (jnp.where(j == 0, n_batch, b), 0, 0)


def _chunk_swap():
    r = np.arange(TM)
    p = np.zeros((TM, TM), np.float32)
    p[r, (r % S5_T) * (TM // S5_T) + r // S5_T] = 1.0
    return jnp.asarray(p, BF16)


def _ffn_residual(x1, mod_ref, f0_ref, f1_ref, route_ref, lng_ref, lnb_ref):
    r = lax.broadcasted_iota(jnp.int32, (TM, TM), 0)
    c = lax.broadcasted_iota(jnp.int32, (TM, TM), 1)
    w = [jnp.sum(jnp.where(r == c, route_ref[TOP_K + k:TOP_K + k + 1, :], 0.0), axis=1, keepdims=True)
         for k in range(TOP_K)]
    ffn = w[0] * f0_ref[...].astype(F32) + w[1] * f1_ref[...].astype(F32)
    return _ln(ALPHA * x1 + _mod_part(mod_ref, 5) * ffn) * lng_ref[...] + lnb_ref[...]


def _inproj_kernel(*refs, fused):
    if fused:
        (x1_ref, modp_ref, f0_ref, f1_ref, route_ref, lng_ref, lnb_ref, mod_ref, w_ref, swap_ref,
         x_out_ref, attn_ref, conv_ref, sz_ref, su_ref, dt_ref, u_ref) = refs
        x = _ffn_residual(x1_ref[...], modp_ref, f0_ref, f1_ref, route_ref, lng_ref, lnb_ref)
        x_out_ref[...] = x
    else:
        x_ref, mod_ref, w_ref, swap_ref, attn_ref, conv_ref, sz_ref, su_ref, dt_ref, u_ref = refs
        x = x_ref[...]
    h = _ln(x) * (1.0 + _mod_part(mod_ref, 1)) + _mod_part(mod_ref, 0)
    hb = h.astype(BF16)
    off = 0
    for ref in (attn_ref, conv_ref, sz_ref, su_ref, dt_ref):
        width = ref.shape[-1]
        ref[...] = jnp.dot(hb, w_ref[:, off:off + width], preferred_element_type=F32)
        off += width
    by_step = jnp.dot(swap_ref[...], su_ref[...].astype(BF16), preferred_element_type=F32)
    n_ch = TM // S5_T
    for t in range(S5_T):
        rows = by_step[t * n_ch:(t + 1) * n_ch]
        for g in range(S5_GROUPS):
            u_ref[g, :, t * S5_GROUP:(t + 1) * S5_GROUP] = rows[:, g * S5_GROUP:(g + 1) * S5_GROUP].astype(BF16)


def _inproj(xcat, mods, w1, prev=None):
    n_batch = xcat.shape[0] if prev is None else prev[0].shape[0]
    n_tiles = TOK // TM
    widths = (ATTN_W, CONV_W, 512, 512, DT_W)
    tile = lambda w: pl.BlockSpec((None, TM, w), lambda b, j: (b, j, 0))
    mod_spec = pl.BlockSpec((None, 1, N_MOD * D_MODEL), _mod_row(n_batch))
    const = lambda shape: pl.BlockSpec(shape, lambda b, j: tuple(0 for _ in shape))
    out_shape = ([jax.ShapeDtypeStruct((n_batch, TOK, w), F32) for w in widths]
                 + [jax.ShapeDtypeStruct((S5_GROUPS, S5_NCH, n_batch * S5_ROWW), BF16)])
    out_specs = ([tile(w) for w in widths]
                 + [pl.BlockSpec((S5_GROUPS, TM // S5_T, S5_ROWW), lambda b, j: (0, j, b))])
    own = [mod_spec, const((D_MODEL, PROJ_W)), const((TM, TM))]
    if prev is None:
        in_specs, args = [tile(D_MODEL)] + own, (xcat, mods, w1, _chunk_swap())
    else:
        x1, mods_prev, f, route, ln_g, ln_b = prev
        in_specs = [tile(D_MODEL), mod_spec,
                    pl.BlockSpec((None, None, TM, D_MODEL), lambda b, j: (0, b, j, 0)),
                    pl.BlockSpec((None, None, TM, D_MODEL), lambda b, j: (1, b, j, 0)),
                    pl.BlockSpec((8, TM), lambda b, j: (0, b * n_tiles + j)),
                    const((1, D_MODEL)), const((1, D_MODEL))] + own
        args = (x1, mods_prev, f, f, route, ln_g, ln_b, mods, w1, _chunk_swap())
        out_shape = [jax.ShapeDtypeStruct((n_batch, TOK, D_MODEL), F32)] + out_shape
        out_specs = [tile(D_MODEL)] + out_specs
    return pl.pallas_call(
        functools.partial(_inproj_kernel, fused=prev is not None),
        out_shape=out_shape,
        grid=(n_batch, n_tiles),
        in_specs=in_specs,
        out_specs=out_specs,
        compiler_params=_params(("arbitrary", "arbitrary")),
        name="inproj",
    )(*args)


def _rope(t, cos, sin):
    width = t.shape[-1]
    lane = lax.broadcasted_iota(jnp.int32, t.shape, t.ndim - 1)
    rot = jnp.where(lane % 32 < 16, -pltpu.roll(t, width - 16, t.ndim - 1), pltpu.roll(t, 16, t.ndim - 1))
    return t * cos + rot * sin


def _tile4(v):
    return jnp.concatenate([v, v, v, v], axis=1)


def _head_inv_rms(t, n_heads):
    t2 = t * t
    lane = lax.broadcasted_iota(jnp.int32, t.shape, 1)
    inv = jnp.zeros_like(t)
    for h in range(n_heads):
        ms = jnp.sum(t2[:, h * HEAD_DIM:(h + 1) * HEAD_DIM], axis=1, keepdims=True) * (1.0 / HEAD_DIM)
        inv = jnp.where(lane // HEAD_DIM == h, lax.rsqrt(ms + NORM_EPS), inv)
    return inv


def _stack_heads(qr, kv):
    parts = [qr[:, (kv * GRP + g) * HEAD_DIM:(kv * GRP + g + 1) * HEAD_DIM] for g in range(GRP)]
    return jnp.concatenate(parts, axis=0).astype(BF16)


QPS = 2
LOG2E = math.log2(math.e)


KEY_BLOCK = 128


def _attend(jobs, sink_ref, out_ref, kb_ref, vt_ref, s_ref, p_ref):
    width_q = GRP * QB
    halves = [slice(0, width_q // 2), slice(width_q // 2, width_q)]
    sub = KEY_BLOCK // 8
    for u, (qr, parts) in enumerate(jobs):
        for kv in range(N_KV):
            qs = _stack_heads(qr, kv)
            row = 0
            for key_idx, width, bias in parts:
                for hc in halves:
                    s = _bdot_nt(kb_ref[kv, key_idx, :], qs[hc])
                    if bias is not None:
                        s = s + jnp.concatenate([bias] * (width_q // 2 // QB), axis=1)
                    s_ref[u, kv, row:row + width, hc] = s
                row += width
    for u, (_, parts) in enumerate(jobs):
        n_keys = sum(width for _, width, _ in parts)
        sink_term = {}
        for kv in range(N_KV):
            for g in range(GRP):
                cols = slice(g * QB, (g + 1) * QB)
                mx = None
                for r0 in range(0, n_keys, KEY_BLOCK):
                    blk = jnp.max(s_ref[u, kv, r0:r0 + KEY_BLOCK, cols].reshape(sub, 8, QB), axis=0)
                    mx = blk if mx is None else jnp.maximum(mx, blk)
                m = jnp.max(mx, axis=0, keepdims=True)
                if sink_ref is not None:
                    m = jnp.maximum(m, sink_ref[kv, :, cols])
                    sink_term[kv, g] = jnp.exp2(sink_ref[kv, :, cols] - m)
                m_b = jnp.broadcast_to(m, (KEY_BLOCK, QB))
                for r0 in range(0, n_keys, KEY_BLOCK):
                    p_ref[u, kv, r0:r0 + KEY_BLOCK, cols] = jnp.exp2(
                        s_ref[u, kv, r0:r0 + KEY_BLOCK, cols] - m_b).astype(BF16)
        o_t = {}
        for kv in range(N_KV):
            for hi, hc in enumerate(halves):
                o = None
                row = 0
                for key_idx, width, _ in parts:
                    part = jnp.dot(vt_ref[kv, :, key_idx], p_ref[u, kv, row:row + width, hc],
                                   preferred_element_type=F32)
                    o = part if o is None else o + part
                    row += width
                o_t[kv, hi] = o
        for g in range(GRP):
            hi, gl = divmod(g, GRP // 2)
            cols = slice(gl * QB, (gl + 1) * QB)
            scaled = []
            for kv in range(N_KV):
                den = o_t[kv, hi][HEAD_DIM:HEAD_DIM + 1, cols]
                if sink_ref is not None:
                    den = den + sink_term[kv, g]
                scaled.append(o_t[kv, hi][0:HEAD_DIM, cols] / den)
            pair = jnp.concatenate(scaled, axis=0).T
            for kv in range(N_KV):
                h = kv * GRP + g
                out_ref[u * QB:(u + 1) * QB, h * HEAD_DIM:(h + 1) * HEAD_DIM] = (
                    pair[:, kv * HEAD_DIM:(kv + 1) * HEAD_DIM].astype(out_ref.dtype))


def _wattn_kernel(q_ref, k_ref, v_ref, cos_ref, sin_ref, sink_ref, bias_ref, out_ref, kb_ref, vt_ref, s_ref, p_ref):
    i = pl.program_id(1)

    @pl.when(i == 0)
    def _prep():
        kr = _rope(k_ref[...], cos_ref[...], sin_ref[...])
        v_t = v_ref[...].T
        for kv in range(N_KV):
            kb_ref[kv] = kr[:, kv * HEAD_DIM:(kv + 1) * HEAD_DIM].astype(BF16)
            vt_ref[kv, 0:HEAD_DIM, :] = v_t[kv * HEAD_DIM:(kv + 1) * HEAD_DIM, :].astype(BF16)
            vt_ref[kv, HEAD_DIM:HEAD_DIM + 16, :] = jnp.ones((16, TOK), BF16)

    row0 = pl.multiple_of(i * (QPS * QB), QPS * QB)
    cos = _tile4(cos_ref[pl.ds(row0, QPS * QB), :])
    sin = _tile4(sin_ref[pl.ds(row0, QPS * QB), :])
    qr = _rope(q_ref[...], cos, sin) * (HEAD_DIM ** -0.5 * LOG2E)
    qrs = [qr[u * QB:(u + 1) * QB] for u in range(QPS)]

    ctx_keys = (slice(0, CTX_LEN), CTX_LEN, None)

    @pl.when(i < CTX_LEN // (QPS * QB))
    def _ctx_queries():
        _attend([(q, [ctx_keys]) for q in qrs], sink_ref, out_ref, kb_ref, vt_ref, s_ref, p_ref)

    @pl.when(i >= CTX_LEN // (QPS * QB))
    def _latent_queries():
        jobs = []
        for u in range(QPS):
            j = i * QPS + u - CTX_LEN // QB
            band = jnp.clip(j - 1, 0, SEQ // QB - 3)
            start = pl.multiple_of(CTX_LEN + band * QB, QB)
            bias = bias_ref[j - band]
            jobs.append((qrs[u], [ctx_keys, (pl.ds(start, 3 * QB), 3 * QB, bias)]))
        _attend(jobs, sink_ref, out_ref, kb_ref, vt_ref, s_ref, p_ref)


def _gattn_kernel(q_ref, k_ref, v_ref, cos_ref, sin_ref, qw_ref, kw_ref, out_ref, kb_ref, vt_ref, s_ref, p_ref):
    i = pl.program_id(1)

    @pl.when(i == 0)
    def _prep():
        k = k_ref[...]
        kn = k * _head_inv_rms(k, N_KV) * kw_ref[...]
        kr = _rope(kn, cos_ref[...], sin_ref[...])
        v_t = v_ref[...].T
        for kv in range(N_KV):
            kb_ref[kv] = kr[:, kv * HEAD_DIM:(kv + 1) * HEAD_DIM].astype(BF16)
            vt_ref[kv, 0:HEAD_DIM, :] = v_t[kv * HEAD_DIM:(kv + 1) * HEAD_DIM, :].astype(BF16)
            vt_ref[kv, HEAD_DIM:HEAD_DIM + 16, :] = jnp.ones((16, TOK), BF16)

    row0 = pl.multiple_of(i * (QPS * QB), QPS * QB)
    cos = _tile4(cos_ref[pl.ds(row0, QPS * QB), :])
    sin = _tile4(sin_ref[pl.ds(row0, QPS * QB), :])
    q = q_ref[...]
    qn = q * _head_inv_rms(q, N_HEADS) * qw_ref[...]
    qr = _rope(qn, cos, sin) * (HEAD_DIM ** -0.5 * LOG2E)
    qrs = [qr[u * QB:(u + 1) * QB] for u in range(QPS)]

    @pl.when(i < CTX_LEN // (QPS * QB))
    def _ctx_queries():
        _attend([(q_u, [(slice(0, CTX_LEN), CTX_LEN, None)]) for q_u in qrs], None, out_ref, kb_ref, vt_ref,
                s_ref, p_ref)

    @pl.when(i >= CTX_LEN // (QPS * QB))
    def _latent_queries():
        _attend([(q_u, [(slice(0, TOK), TOK, None)]) for q_u in qrs], None, out_ref, kb_ref, vt_ref, s_ref, p_ref)


def _window_bias():
    q_pos = np.arange(QB)[None, None, :] + QB * np.arange(3)[:, None, None]
    k_pos = np.arange(3 * QB)[None, :, None]
    return jnp.asarray(np.where(np.abs(q_pos - k_pos) <= WINDOW, 0.0, -np.inf), F32)


def _attention(attn, cos, sin, sink_rows, qw, kw):
    n_batch = attn.shape[0]
    grid = (n_batch, TOK // (QPS * QB))
    q_spec = lambda blk: pl.BlockSpec((None, QPS * QB, 512), lambda b, i: (b, i, blk))
    kv_spec = lambda blk: pl.BlockSpec((None, TOK, 128), lambda b, i: (b, 0, blk))
    tab_spec = pl.BlockSpec((TOK, 128), lambda b, i: (0, 0))
    out_spec = pl.BlockSpec((None, QPS * QB, 512), lambda b, i: (b, i, 0))
    out_shape = jax.ShapeDtypeStruct((n_batch, TOK, 512), BF16)
    kv_scratch = [pltpu.VMEM((N_KV, TOK, HEAD_DIM), BF16), pltpu.VMEM((N_KV, HEAD_DIM + 16, TOK), BF16)]
    w_keys = CTX_LEN + 3 * QB
    ya = pl.pallas_call(
        _wattn_kernel, out_shape=out_shape, grid=grid,
        in_specs=[q_spec(0), kv_spec(8), kv_spec(9), tab_spec, tab_spec,
                  pl.BlockSpec((N_KV, 1, GRP * QB), lambda b, i: (0, 0, 0)),
                  pl.BlockSpec((3, 3 * QB, QB), lambda b, i: (0, 0, 0))],
        out_specs=out_spec,
        scratch_shapes=kv_scratch + [pltpu.VMEM((QPS, N_KV, w_keys, GRP * QB), F32),
                                     pltpu.VMEM((QPS, N_KV, w_keys, GRP * QB), BF16)],
        compiler_params=_params(("arbitrary", "arbitrary")), name="window_attention",
    )(attn, attn, attn, cos, sin, sink_rows, _window_bias())
    yg = pl.pallas_call(
        _gattn_kernel, out_shape=out_shape, grid=grid,
        in_specs=[q_spec(1), kv_spec(10), kv_spec(11), tab_spec, tab_spec,
                  pl.BlockSpec((1, 512), lambda b, i: (0, 0)),
                  pl.BlockSpec((1, 128), lambda b, i: (0, 0))],
        out_specs=out_spec,
        scratch_shapes=kv_scratch + [pltpu.VMEM((QPS, N_KV, TOK, GRP * QB), F32),
                                     pltpu.VMEM((QPS, N_KV, TOK, GRP * QB), BF16)],
        compiler_params=_params(("arbitrary", "arbitrary")), name="global_attention",
    )(attn, attn, attn, cos, sin, qw, kw)
    return ya, yg


def _ssd_chunk(s):
    r = s - N_CHUNK
    back = jnp.where(r == 0, 1, jnp.where(r == 1, 0, N_CHUNK + 1 - r))
    return jnp.where(s < N_CHUNK, s, back)


SSD_NB = 2


def _ssd_kernel(x_ref, prev_ref, next_ref, z_ref, dt_ref, cw_ref, cb_ref, dtb_ref, a_ref, d_ref, nw_ref, e_ref,
                out_ref, yf_ref, state_ref, y_ref, xc_ref, bc_ref):
    s = pl.program_id(1)
    chunk = _ssd_chunk(s)
    backward = s >= N_CHUNK

    @pl.when((s == 0) | (s == N_CHUNK))
    def _reset():
        state_ref[...] = jnp.zeros_like(state_ref)

    first = (chunk == 0) | (chunk == CTX_LEN // SSD_T)
    last = (chunk == CTX_LEN // SSD_T - 1) | (chunk == N_CHUNK - 1)
    row = lax.broadcasted_iota(jnp.int32, (SSD_T, CONV_W), 0)
    tr = lax.broadcasted_iota(jnp.int32, (SSD_T, SSD_T), 0)
    tc = lax.broadcasted_iota(jnp.int32, (SSD_T, SSD_T), 1)
    causal = jnp.where(backward, tc - tr, tr - tc) >= 0
    tri = jnp.where(causal, 1.0, 0.0).astype(BF16)
    neg_a = -jnp.exp(a_ref[...])
    expand = e_ref[...]

    def widen(v):
        return jnp.dot(jnp.concatenate(_split3(v), axis=1), expand, preferred_element_type=F32)

    row0 = pl.multiple_of(chunk * SSD_T, SSD_T)
    rows = pl.ds(row0, SSD_T)

    @pl.when(jnp.logical_not(backward))
    def _conv():
        for u in range(SSD_NB):
            xin = x_ref[u]
            prev = jnp.where(first, 0.0, prev_ref[u, 7:8, :])
            nxt = jnp.where(last, 0.0, next_ref[u, 0:1, :])
            xm1 = jnp.where(row == 0, prev, pltpu.roll(xin, 1, 0))
            xp1 = jnp.where(row == SSD_T - 1, nxt, pltpu.roll(xin, SSD_T - 1, 0))
            conv = cw_ref[0:1, :] * xm1 + cw_ref[1:2, :] * xin + cw_ref[2:3, :] * xp1 + cb_ref[...]
            xbc = _silu(conv)
            xc_ref[u, rows, :] = xbc[:, 0:SSD_WIDTH]
            bc_ref[u, rows, :] = xbc[:, SSD_WIDTH:CONV_W].astype(BF16)

    xs = []
    for u in range(SSD_NB):
        x = xc_ref[u, rows, :]
        xs.append(x)
        b_c = bc_ref[u, rows, :]
        bmat = [b_c[:, g * SSD_N:(g + 1) * SSD_N] for g in range(SSD_GROUPS)]
        cmat = [b_c[:, (SSD_GROUPS + g) * SSD_N:(SSD_GROUPS + g + 1) * SSD_N] for g in range(SSD_GROUPS)]

        dtv = dt_ref[u] + dtb_ref[...]
        dt = jnp.maximum(dtv, 0.0) + jnp.log(1.0 + jnp.exp(-jnp.abs(dtv)))
        hi, mid, lo = _split3(dt * neg_a)
        cs = (jnp.dot(tri, hi, preferred_element_type=F32) + jnp.dot(tri, mid, preferred_element_type=F32)
              + jnp.dot(tri, lo, preferred_element_type=F32))
        cs_t = cs.T

        dt_x = widen(dt)
        cs_x = widen(cs)
        tot_x = jnp.where(backward, cs_x[0:1, :], cs_x[SSD_T - 1:SSD_T, :])
        ecs_x = jnp.exp(cs_x)
        etot_x = jnp.exp(tot_x)
        xd = x * dt_x
        xd_b = xd.astype(BF16)
        xd_end = (xd * jnp.exp(tot_x - cs_x)).astype(BF16)

        gw = SSD_WIDTH // SSD_GROUPS
        for g in range(SSD_GROUPS):
            glanes = slice(g * gw, (g + 1) * gw)
            cb = _bdot_nt(cmat[g], bmat[g])
            st = state_ref[u, g]
            y_off = ecs_x[:, glanes] * jnp.dot(cmat[g].astype(BF16), st.astype(BF16), preferred_element_type=F32)
            state_ref[u, g] = etot_x[:, glanes] * st + jnp.dot(bmat[g].astype(F32).T.astype(BF16),
                                                               xd_end[:, glanes], preferred_element_type=F32)
            for j in range(SSD_HEADS // SSD_GROUPS):
                h = g * (SSD_HEADS // SSD_GROUPS) + j
                lanes = slice(h * SSD_P, (h + 1) * SSD_P)
                seg = jnp.exp(jnp.where(causal, cs[:, h:h + 1] - cs_t[h:h + 1, :], -jnp.inf))
                y_ref[u, :, lanes] = (jnp.dot((cb * seg).astype(BF16), xd_b[:, lanes], preferred_element_type=F32)
                                      + y_off[:, j * SSD_P:(j + 1) * SSD_P])

    @pl.when(jnp.logical_not(backward))
    def _keep():
        for u in range(SSD_NB):
            yf_ref[u, rows, :] = y_ref[u]

    @pl.when(backward)
    def _finish():
        for u in range(SSD_NB):
            ytot = yf_ref[u, rows, :] + y_ref[u] + d_ref[...] * xs[u]
            gated = ytot * _silu(z_ref[u])
            ms = jnp.mean(gated * gated, axis=1, keepdims=True)
            out_ref[u] = (gated * lax.rsqrt(ms + NORM_EPS) * nw_ref[...]).astype(out_ref.dtype)


def _ssd(conv_in, sz, dt, conv_w, conv_b, dt_bias, a_log, d_exp, norm_w):
    n_batch = conv_in.shape[0]
    assert n_batch % SSD_NB == 0
    halo = SSD_T // 8
    n_halo = TOK // 8

    def chunk_map(b, s):
        return (b, _ssd_chunk(s), 0)

    def out_map(b, s):
        return (b, jnp.where(s < N_CHUNK, 1, _ssd_chunk(s)), 0)

    const = lambda shape: pl.BlockSpec(shape, lambda b, s: tuple(0 for _ in shape))
    dir_spec = pl.BlockSpec((None, 1, 128), lambda b, s: (s // N_CHUNK, 0, 0))
    expand = jnp.asarray(np.tile(np.repeat(np.eye(128, SSD_HEADS), SSD_P, axis=1), (3, 1)), BF16)
    return pl.pallas_call(
        _ssd_kernel,
        out_shape=jax.ShapeDtypeStruct((n_batch, TOK, SSD_WIDTH), BF16),
        grid=(n_batch // SSD_NB, 2 * N_CHUNK),
        in_specs=[
            pl.BlockSpec((SSD_NB, SSD_T, CONV_W), chunk_map),
            pl.BlockSpec((SSD_NB, 8, CONV_W), lambda b, s: (b, jnp.maximum(_ssd_chunk(s) * halo - 1, 0), 0)),
            pl.BlockSpec((SSD_NB, 8, CONV_W), lambda b, s: (b, jnp.minimum((_ssd_chunk(s) + 1) * halo, n_halo - 1), 0)),
            pl.BlockSpec((SSD_NB, SSD_T, SSD_WIDTH), chunk_map),
            pl.BlockSpec((SSD_NB, SSD_T, 128), lambda b, s: (b, _ssd_chunk(s), s // N_CHUNK)),
            const((3, CONV_W)), const((1, CONV_W)), dir_spec, dir_spec,
            const((1, SSD_WIDTH)), const((1, SSD_WIDTH)), const((3 * 128, SSD_WIDTH)),
        ],
        out_specs=pl.BlockSpec((SSD_NB, SSD_T, SSD_WIDTH), out_map),
        scratch_shapes=[pltpu.VMEM((SSD_NB, TOK, SSD_WIDTH), F32),
                        pltpu.VMEM((SSD_NB, SSD_GROUPS, SSD_N, SSD_WIDTH // SSD_GROUPS), F32),
                        pltpu.VMEM((SSD_NB, SSD_T, SSD_WIDTH), F32),
                        pltpu.VMEM((SSD_NB, TOK, SSD_WIDTH), F32),
                        pltpu.VMEM((SSD_NB, TOK, CONV_W - SSD_WIDTH), BF16)],
        compiler_params=_params(("arbitrary", "arbitrary")),
        name="ssd",
    )(conv_in, conv_in, conv_in, sz, dt, conv_w, conv_b, dt_bias, a_log, d_exp, norm_w, expand)


S5_SW = 4 * 128


def _s5_kernel(u_ref, m_ref, h_ref, g_ref, a_ref, y_ref, loc_ref, prev_ref, *, n_batch):
    u = u_ref[...]
    loc_ref[...] = jnp.dot(u, h_ref[...], preferred_element_type=F32)
    a = [a_ref[k:k + 1, :] for k in range(4)]

    def step(k, carry):
        c_f = k
        c_b = jnp.where(k < S5_CTX_CH, S5_CTX_CH - 1 - k, S5_NCH + S5_CTX_CH - 1 - k)
        new = []
        for d, c in enumerate((c_f, c_b)):
            rows = pl.ds(pl.multiple_of(c * n_batch, n_batch), n_batch)
            re_l, im_l = slice(2 * d * 128, (2 * d + 1) * 128), slice((2 * d + 1) * 128, (2 * d + 2) * 128)
            s_re, s_im = carry[2 * d], carry[2 * d + 1]
            prev_ref[rows, re_l] = s_re
            prev_ref[rows, im_l] = s_im
            a_re, a_im = a[2 * d], a[2 * d + 1]
            new += [a_re * s_re - a_im * s_im + loc_ref[rows, re_l], a_re * s_im + a_im * s_re + loc_ref[rows, im_l]]
        return tuple(new)

    zero = jnp.zeros((n_batch, 128), F32)
    lax.fori_loop(0, S5_NCH, step, (zero,) * 4, unroll=2)
    y = jnp.dot(u, m_ref[...], preferred_element_type=F32)
    y = y + jnp.dot(prev_ref[...].astype(BF16), g_ref[...], preferred_element_type=F32)
    y_ref[...] = y.astype(y_ref.dtype)


def _s5(u_g, mats, n_batch):
    m_all, h_all, g_all, a16 = mats
    rows = u_g.shape[1]
    grp = lambda shape: pl.BlockSpec((None,) + shape, lambda g: (g,) + tuple(0 for _ in shape))
    return pl.pallas_call(
        functools.partial(_s5_kernel, n_batch=n_batch),
        out_shape=jax.ShapeDtypeStruct((S5_GROUPS, rows, S5_ROWW), BF16),
        grid=(S5_GROUPS,),
        in_specs=[grp((rows, S5_ROWW)), grp((S5_ROWW, S5_ROWW)), grp((S5_ROWW, S5_SW)), grp((S5_SW, S5_ROWW)),
                  grp((4, 128))],
        out_specs=grp((rows, S5_ROWW)),
        scratch_shapes=[pltpu.VMEM((rows, S5_SW), F32), pltpu.VMEM((rows, S5_SW), F32)],
        compiler_params=_params(("arbitrary",)),
        name="s5",
    )(u_g, m_all, h_all, g_all, a16)


def _cmul(a, b):
    return a[0] * b[0] - a[1] * b[1], a[0] * b[1] + a[1] * b[0]


def _s5_matrices(a_re, a_im, log_dt, b_re, b_im, c_re, c_im):
    hp = lax.Precision.HIGHEST
    t = jnp.arange(S5_T + 1, dtype=F32)
    m_sum = 0.0
    h_all, g_all, a16_all = [], [], []
    c = (c_re.astype(F32), c_im.astype(F32))
    for direction in range(2):
        are = jnp.minimum(a_re[direction].astype(F32), -1e-4)
        aim = a_im[direction].astype(F32)
        dt = jnp.exp(log_dt[direction].astype(F32))[:, None]
        mag = jnp.exp(t[:, None, None] * (are * dt)[None])
        ang = t[:, None, None] * (aim * dt)[None]
        pw = (mag * jnp.cos(ang), mag * jnp.sin(ang))
        num = (pw[0][1] - 1.0, pw[1][1])
        den = are * are + aim * aim
        coef = ((num[0] * are + num[1] * aim) / den, (num[1] * are - num[0] * aim) / den)
        bbar = _cmul((coef[0][..., None], coef[1][..., None]), (b_re.astype(F32), b_im.astype(F32)))
        pb = _cmul((pw[0][:S5_T, :, :, None], pw[1][:S5_T, :, :, None]), (bbar[0][None], bbar[1][None]))
        taps = (jnp.einsum('gon,tgni->tgoi', c[0], pb[0], precision=hp)
                - jnp.einsum('gon,tgni->tgoi', c[1], pb[1], precision=hp))
        ti = jnp.arange(S5_T)
        lag = (ti[None, :] - ti[:, None]) if direction == 0 else (ti[:, None] - ti[None, :])
        k_full = taps[jnp.clip(lag, 0, S5_T - 1)]
        k_full = jnp.where((lag >= 0)[:, :, None, None, None], k_full, 0.0)
        m_sum = m_sum + k_full.transpose(2, 0, 4, 1, 3).reshape(S5_GROUPS, S5_ROWW, S5_ROWW)
        e_in = (S5_T - 1 - ti) if direction == 0 else ti
        hb = _cmul((pw[0][e_in][..., None], pw[1][e_in][..., None]), (bbar[0][None], bbar[1][None]))
        zpad = jnp.zeros_like(hb[0])
        h_mat = jnp.concatenate([hb[0], zpad, hb[1], zpad], axis=2)
        h_all.append(h_mat.transpose(1, 0, 3, 2).reshape(S5_GROUPS, S5_ROWW, 4 * S5_STATE))
        e_out = (ti + 1) if direction == 0 else (S5_T - ti)
        cp = _cmul((c[0][None], c[1][None]),
                   (pw[0][e_out][:, :, None, :], pw[1][e_out][:, :, None, :]))
        zpad = jnp.zeros_like(cp[0])
        g_mat = jnp.concatenate([cp[0], zpad, -cp[1], zpad], axis=3)
        g_all.append(g_mat.transpose(1, 3, 0, 2).reshape(S5_GROUPS, 4 * S5_STATE, S5_ROWW))
        zrow = jnp.zeros_like(pw[0][S5_T])
        a16_all += [jnp.concatenate([pw[0][S5_T], zrow], axis=1), jnp.concatenate([pw[1][S5_T], zrow], axis=1)]
    return (m_sum.astype(BF16), jnp.concatenate(h_all, axis=2).astype(BF16),
            jnp.concatenate(g_all, axis=1).astype(BF16), jnp.stack(a16_all, axis=1))


def _route(logits_t, bias, base, before):
    scores = [_sigmoid(logits_t[j * N_EXPERT_GROUPS:(j + 1) * N_EXPERT_GROUPS]) for j in range(PER_GROUP)]
    sel = [scores[j] + bias[j * N_EXPERT_GROUPS:(j + 1) * N_EXPERT_GROUPS] for j in range(PER_GROUP)]
    hi1, lo1 = jnp.maximum(sel[0], sel[1]), jnp.minimum(sel[0], sel[1])
    hi2, lo2 = jnp.maximum(sel[2], sel[3]), jnp.minimum(sel[2], sel[3])
    group_score = jnp.maximum(hi1, hi2) + jnp.maximum(jnp.minimum(hi1, hi2), jnp.maximum(lo1, lo2))
    gid = lax.broadcasted_iota(jnp.int32, group_score.shape, 0)
    best = jnp.max(group_score, axis=0, keepdims=True)
    grp = jnp.min(jnp.where(group_score == best, gid, N_EXPERT_GROUPS), axis=0, keepdims=True)
    pick = gid == grp
    v = [jnp.sum(jnp.where(pick, sel[j], 0.0), axis=0, keepdims=True) for j in range(PER_GROUP)]
    sc = [jnp.sum(jnp.where(pick, scores[j], 0.0), axis=0, keepdims=True) for j in range(PER_GROUP)]
    rank = []
    for j in range(PER_GROUP):
        r = jnp.zeros_like(grp)
        for i in range(PER_GROUP):
            if i < j:
                r = r + jnp.where(v[i] >= v[j], 1, 0)
            elif i > j:
                r = r + jnp.where(v[i] > v[j], 1, 0)
        rank.append(r)
    e, w, locs = [], [], []
    for k in range(TOP_K):
        loc = sum(jnp.where(rank[j] == k, j, 0) for j in range(PER_GROUP))
        locs.append(loc)
        e.append((grp * PER_GROUP + loc).astype(F32))
        w.append(sum(jnp.where(rank[j] == k, sc[j], 0.0) for j in range(PER_GROUP)))
    wsum = w[0] + w[1]
    chosen = [jnp.where(pick & ((locs[0] == j) | (locs[1] == j)), 1.0, 0.0) for j in range(PER_GROUP)]
    onehot = jnp.concatenate(chosen, axis=0)
    seen = jnp.dot(onehot.astype(BF16), before, preferred_element_type=F32) + base
    pos = []
    for k in range(TOP_K):
        hit = sum(jnp.where(pick & (locs[k] == j), seen[j * N_EXPERT_GROUPS:(j + 1) * N_EXPERT_GROUPS], 0.0)
                  for j in range(PER_GROUP))
        pos.append(jnp.sum(hit, axis=0, keepdims=True))
    rows = [e[0], e[1], w[0] / wsum, w[1] / wsum, pos[0], pos[1]]
    route = jnp.concatenate(rows + [jnp.zeros_like(wsum)] * (8 - len(rows)), axis=0)
    return route, base + jnp.sum(onehot, axis=1, keepdims=True)


def _gelu_tanh(x):
    return 0.5 * x * (1.0 + jnp.tanh(math.sqrt(2.0 / math.pi) * (x + 0.044715 * (x * x * x))))


MERGE_LAG = 3


def _merge_kernel(xn_ref, modn_ref, x_ref, mod_ref, modt_ref, ya_ref, ys_ref, yg_ref, y5_ref, su_ref, s5d_ref,
                  wg_ref, wb_ref, wglu_ref, wout_ref, lng_ref, lnb_ref, rw_ref, rb_ref, before_ref, swap_ref,
                  x1_ref, h2_ref, route_ref, count_ref, base_ref, y5s_ref, y5p_ref, hbs_ref, v_ref, h2s_ref):
    s = pl.program_id(0)

    @pl.when(s == 0)
    def _reset():
        base_ref[...] = jnp.zeros_like(base_ref)
        y5p_ref[...] = jnp.zeros_like(y5p_ref)
        hbs_ref[...] = jnp.zeros_like(hbs_ref)
        v_ref[...] = jnp.zeros_like(v_ref)
        h2s_ref[...] = jnp.zeros_like(h2s_ref)

    logits_t = _bdot_nt(rw_ref[...], h2s_ref[...])
    v_prev = v_ref[...]

    hb = hbs_ref[...]
    glu = jnp.dot(y5p_ref[...], wglu_ref[...], preferred_element_type=F32)
    y5 = (glu[:, 0:S5_WIDTH] * _sigmoid(glu[:, S5_WIDTH:2 * S5_WIDTH])).astype(BF16)
    def select_experts():
        base_old = base_ref[...]
        route, base_new = _route(logits_t, rb_ref[...], base_old, before_ref[...])
        route_ref[...] = route
        base = jnp.where(s >= MERGE_LAG, base_new, base_old)
        base_ref[...] = base
        count_ref[...] = jnp.broadcast_to(base, count_ref.shape)

    def post_norm():
        x1 = _ln(v_prev) * lng_ref[...] + lnb_ref[...]
        x1_ref[...] = x1
        h2 = (_ln(x1) * (1.0 + _mod_part(modt_ref, 4)) + _mod_part(modt_ref, 3)).astype(BF16)
        h2_ref[...] = h2
        h2s_ref[...] = h2

    def relayout():
        n_ch = TM // S5_T
        for t in range(S5_T):
            for g in range(S5_GROUPS):
                y5s_ref[t * n_ch:(t + 1) * n_ch, g * S5_GROUP:(g + 1) * S5_GROUP] = (
                    y5_ref[g, :, t * S5_GROUP:(t + 1) * S5_GROUP])

    side_work = [post_norm, select_experts, relayout, lambda: None]
    acc = jnp.zeros((TM, D_MODEL), F32)
    for n, y in enumerate((ya_ref[...], ys_ref[...], yg_ref[...], y5)):
        gate = _sigmoid(jnp.dot(hb, wg_ref[:, n * D_MODEL:(n + 1) * D_MODEL], preferred_element_type=F32))
        acc = acc + gate * jnp.dot(y, wb_ref[n], preferred_element_type=F32)
        side_work[n]()
    mix = jnp.dot(acc.astype(BF16), wout_ref[...], preferred_element_type=F32)
    v_ref[...] = ALPHA * x_ref[...] + _mod_part(mod_ref, 2) * mix

    y5_pre = jnp.dot(swap_ref[...], y5s_ref[...], preferred_element_type=F32) + s5d_ref[...] * su_ref[...]
    y5p_ref[...] = _gelu_tanh(y5_pre).astype(BF16)
    hbs_ref[...] = (_ln(xn_ref[...]) * (1.0 + _mod_part(modn_ref, 1)) + _mod_part(modn_ref, 0)).astype(BF16)


def _merge(xcat, mods, ya, ys, yg, y5, su, s5_d, wg, wb, wglu, wout, ln_g, ln_b, rw_t, rb):
    n_batch = xcat.shape[0]
    n_tiles = TOK // TM
    n_all = n_batch * n_tiles

    def at(lag):
        def split(s):
            t = jnp.clip(s - lag, 0, n_all - 1)
            return t // n_tiles, t % n_tiles
        return split

    def tile(w, lag):
        return pl.BlockSpec((None, TM, w), lambda s: at(lag)(s) + (0,))

    def mod_spec(lag):
        def index(s):
            b, j = at(lag)(s)
            return jnp.where(j == 0, n_batch, b), 0, 0
        return pl.BlockSpec((None, 1, N_MOD * D_MODEL), index)

    def flat(lag):
        return lambda s: (0, jnp.clip(s - lag, 0, n_all - 1))

    const = lambda shape: pl.BlockSpec(shape, lambda s: tuple(0 for _ in shape))
    before = jnp.asarray(np.triu(np.ones((TM, TM)), 1), BF16)
    return pl.pallas_call(
        _merge_kernel,
        out_shape=[jax.ShapeDtypeStruct((n_batch, TOK, D_MODEL), F32),
                   jax.ShapeDtypeStruct((n_batch, TOK, D_MODEL), BF16),
                   jax.ShapeDtypeStruct((8, n_batch * TOK), F32),
                   jax.ShapeDtypeStruct((N_EXPERTS, 128), F32)],
        grid=(n_all + MERGE_LAG,),
        in_specs=[tile(D_MODEL, 0), mod_spec(0), tile(D_MODEL, 1), mod_spec(1), mod_spec(2),
                  tile(512, 1), tile(512, 1), tile(512, 1),
                  pl.BlockSpec((S5_GROUPS, TM // S5_T, S5_ROWW), lambda s: (0,) + at(0)(s)[::-1]),
                  tile(512, 0), const((1, S5_WIDTH)),
                  const((D_MODEL, N_BRANCH * D_MODEL)), const((N_BRANCH, 512, D_MODEL)),
                  const((S5_WIDTH, 2 * S5_WIDTH)), const((D_MODEL, D_MODEL)),
                  const((1, D_MODEL)), const((1, D_MODEL)),
                  const((N_EXPERTS, D_MODEL)), const((N_EXPERTS, 1)), const((TM, TM)), const((TM, TM))],
        out_specs=[tile(D_MODEL, 2), tile(D_MODEL, 2),
                   pl.BlockSpec((8, TM), flat(MERGE_LAG)),
                   const((N_EXPERTS, 128))],
        scratch_shapes=[pltpu.VMEM((N_EXPERTS, 1), F32), pltpu.VMEM((TM, S5_WIDTH), BF16),
                        pltpu.VMEM((TM, S5_WIDTH), BF16), pltpu.VMEM((TM, D_MODEL), BF16),
                        pltpu.VMEM((TM, D_MODEL), F32), pltpu.VMEM((TM, D_MODEL), BF16)],
        compiler_params=_params(("arbitrary",)),
        name="merge",
    )(xcat, mods, xcat, mods, mods, ya, ys, yg, y5, su, s5_d, wg, wb, wglu, wout, ln_g, ln_b, rw_t, rb, before,
      _chunk_swap())


def _moe_kernel(be_ref, nu_ref, x_ref, wg_ref, wu_ref, wd_ref, y_ref, wgu_s, wd_s):
    i = pl.program_id(0)
    used = i < nu_ref[0]

    @pl.when(used & ((i == 0) | (be_ref[i] != be_ref[jnp.maximum(i - 1, 0)])))
    def _cast():
        wgu_s[:, 0:D_EXPERT] = wg_ref[...].astype(BF16)
        wgu_s[:, D_EXPERT:2 * D_EXPERT] = wu_ref[...].astype(BF16)
        wd_s[...] = wd_ref[...].astype(BF16)

    @pl.when(used)
    def _block():
        gu = jnp.dot(x_ref[...], wgu_s[...], preferred_element_type=F32)
        mid = _silu(gu[:, 0:D_EXPERT]) * gu[:, D_EXPERT:2 * D_EXPERT]
        y_ref[...] = jnp.dot(mid.astype(BF16), wd_s[...], preferred_element_type=F32).astype(y_ref.dtype)


def _moe_experts(block_expert, n_used, xs, w_gate, w_up, w_down, layer):
    n_blocks = block_expert.shape[0]
    return pl.pallas_call(
        _moe_kernel,
        out_shape=jax.ShapeDtypeStruct((n_blocks * MOE_ROWS, D_MODEL), BF16),
        grid_spec=pltpu.PrefetchScalarGridSpec(
            num_scalar_prefetch=2, grid=(n_blocks,),
            in_specs=[pl.BlockSpec((MOE_ROWS, D_MODEL), lambda i, be, nu: (i, 0)),
                      pl.BlockSpec((None, None, D_MODEL, D_EXPERT), lambda i, be, nu: (layer, be[i], 0, 0)),
                      pl.BlockSpec((None, None, D_MODEL, D_EXPERT), lambda i, be, nu: (layer, be[i], 0, 0)),
                      pl.BlockSpec((None, None, D_EXPERT, D_MODEL), lambda i, be, nu: (layer, be[i], 0, 0))],
            out_specs=pl.BlockSpec((MOE_ROWS, D_MODEL), lambda i, be, nu: (i, 0)),
            scratch_shapes=[pltpu.VMEM((D_MODEL, 2 * D_EXPERT), BF16), pltpu.VMEM((D_EXPERT, D_MODEL), BF16)]),
        compiler_params=_params(("arbitrary",)),
        name="moe_experts",
    )(block_expert, n_used, xs, w_gate, w_up, w_down)


def _final_kernel(x_ref, mod_ref, f0_ref, f1_ref, route_ref, lng_ref, lnb_ref, o_ref):
    o_ref[...] = _ffn_residual(x_ref[...], mod_ref, f0_ref, f1_ref, route_ref, lng_ref, lnb_ref)


def _final(x1, mods, f, route, ln_g, ln_b):
    n_batch = x1.shape[0]
    n_tiles = TOK // TM
    skip = CTX_LEN // TM
    const = pl.BlockSpec((1, D_MODEL), lambda b, j: (0, 0))
    return pl.pallas_call(
        _final_kernel,
        out_shape=jax.ShapeDtypeStruct((n_batch, SEQ, D_MODEL), F32),
        grid=(n_batch, n_tiles - skip),
        in_specs=[pl.BlockSpec((None, TM, D_MODEL), lambda b, j: (b, j + skip, 0)),
                  pl.BlockSpec((None, 1, N_MOD * D_MODEL), lambda b, j: (b, 0, 0)),
                  pl.BlockSpec((None, None, TM, D_MODEL), lambda b, j: (0, b, j + skip, 0)),
                  pl.BlockSpec((None, None, TM, D_MODEL), lambda b, j: (1, b, j + skip, 0)),
                  pl.BlockSpec((8, TM), lambda b, j: (0, b * n_tiles + j + skip)),
                  const, const],
        out_specs=pl.BlockSpec((None, TM, D_MODEL), lambda b, j: (b, j, 0)),
        compiler_params=_params(("arbitrary", "arbitrary")),
        name="ffn_residual",
    )(x1, mods, f, f, route, ln_g, ln_b)


def _lookup(table, idx):
    ids = jnp.arange(table.shape[0], dtype=jnp.int32).reshape((-1,) + (1,) * idx.ndim)
    return jnp.sum(jnp.where(idx[None] == ids, table.reshape(ids.shape), 0), axis=0)


def _dispatch(route, counts_rows):
    n_tok = route.shape[1]
    n_assign = n_tok * TOP_K
    n_blocks = -(-(n_assign + N_EXPERTS * (MOE_ROWS - 1)) // MOE_ROWS)
    cap = n_blocks * MOE_ROWS
    counts = counts_rows[:, 0].astype(jnp.int32).reshape(PER_GROUP, N_EXPERT_GROUPS).T.reshape(N_EXPERTS)
    padded = (counts + MOE_ROWS - 1) // MOE_ROWS * MOE_ROWS
    pad_end = jnp.cumsum(padded)
    pad_start = pad_end - padded
    expert = route[0:TOP_K].astype(jnp.int32)
    pos = _lookup(pad_start, expert) + route[2 * TOP_K:3 * TOP_K].astype(jnp.int32)
    block_start = jnp.arange(n_blocks, dtype=jnp.int32) * MOE_ROWS
    block_expert = jnp.minimum(jnp.sum((pad_end[None, :] <= block_start[:, None]).astype(jnp.int32), axis=1),
                               N_EXPERTS - 1)
    n_used = (pad_end[-1:] // MOE_ROWS).astype(jnp.int32)
    gap = padded - counts
    gap_end = jnp.cumsum(gap)
    gap_first = jnp.concatenate([pad_start + counts - (gap_end - gap), pad_end[-1:] - gap_end[-1:]])
    k = jnp.arange(cap - n_assign, dtype=jnp.int32)
    owner = jnp.sum((gap_end[None, :] <= k[:, None]).astype(jnp.int32), axis=1)
    free_slot = _lookup(gap_first, owner) + k
    tok = jnp.broadcast_to(jnp.arange(n_tok, dtype=jnp.int32), (TOP_K, n_tok)).reshape(-1)
    keys = jnp.concatenate([pos.reshape(-1), free_slot])
    vals = jnp.concatenate([tok, k % n_tok])
    _, slot_tok = lax.sort((keys, vals), num_keys=1)
    return slot_tok, pos, block_expert, n_used


def _col(w, part):
    return w[:, _IN_OFF[part]:_IN_OFF[part + 1]]


def _proj_weights(w_in):
    zeros = jnp.zeros((D_MODEL, 128 - SSD_HEADS), w_in.dtype)
    dt = _col(w_in, _SDT)
    parts = [_col(w_in, p) for p in (_AQ, _GQ, _AK, _AV, _GK, _GV, _SX, _SB, _SC, _SZ, _SU)]
    parts += [dt[:, 0:SSD_HEADS], zeros, dt[:, SSD_HEADS:2 * SSD_HEADS], zeros]
    return jnp.concatenate(parts, axis=1).astype(BF16), _col(w_in, _GATE).astype(BF16)


def _rope_tables():
    rows = SEQ // GRID_W
    row = jnp.repeat(jnp.arange(rows, dtype=F32), GRID_W)
    col = jnp.tile(jnp.arange(GRID_W, dtype=F32), rows)
    axis_dim = HEAD_DIM // 2
    inv_freq = ROPE_THETA ** (-jnp.arange(0, axis_dim, 2, dtype=F32) / axis_dim)
    ang_r = row[:, None] * inv_freq
    ang_c = col[:, None] * inv_freq
    ang = jnp.concatenate([ang_r, ang_r, ang_c, ang_c], axis=-1)
    cos = jnp.concatenate([jnp.ones((CTX_LEN, HEAD_DIM), F32), jnp.cos(ang)], axis=0)
    sin = jnp.concatenate([jnp.zeros((CTX_LEN, HEAD_DIM), F32), jnp.sin(ang)], axis=0)
    return jnp.tile(cos, (1, 2)), jnp.tile(sin, (1, 2))


def _pad_lanes(v, width=128):
    return jnp.pad(v, ((0, 0), (0, width - v.shape[-1])))


def kernel(x, c, ctx, c_ctx, mod_w, mod_b, w_in, wa_sink, ga_q_norm, ga_k_norm, ssd_conv_w, ssd_conv_b, ssd_dt_bias, ssd_a_log, ssd_d, ssd_norm_w, s5_a_re, s5_a_im, s5_log_dt, s5_b_re, s5_b_im, s5_c_re, s5_c_im, s5_d, s5_w_glu, w_branch, w_out, ln1_g, ln1_b, ln2_g, ln2_b, router_w, router_bias, moe_w_gate, moe_w_up, moe_w_down):
    n_batch = x.shape[0]
    n_tok = n_batch * TOK
    xcat = jnp.concatenate([ctx, x], axis=1)
    mod_rows = -(-(n_batch + 1) // 8) * 8
    cond = jnp.zeros((mod_rows, D_MODEL), F32).at[:n_batch].set(c).at[n_batch].set(c_ctx)
    mods_all = _modulation(cond, mod_w, mod_b)
    cos, sin = _rope_tables()

    perm = np.array([g * PER_GROUP + j for j in range(PER_GROUP) for g in range(N_EXPERT_GROUPS)])
    rw_t = router_w.T[perm].astype(BF16)
    rb = router_bias.astype(F32)[perm].reshape(N_EXPERTS, 1)

    prev = None
    for layer in range(DEPTH):
        mods = mods_all[layer].reshape(mod_rows, 1, N_MOD * D_MODEL)
        w1, wg = _proj_weights(w_in[layer])
        if prev is None:
            attn, conv_in, sz, su, dt, u_g = _inproj(xcat, mods, w1)
        else:
            xcat, attn, conv_in, sz, su, dt, u_g = _inproj(None, mods, w1, prev)

        sink_rows = jnp.repeat(wa_sink[layer].astype(F32).reshape(N_KV, GRP) * LOG2E, QB,
                               axis=1).reshape(N_KV, 1, GRP * QB)
        qw = jnp.tile(ga_q_norm[layer].astype(F32), N_HEADS).reshape(1, 512)
        kw = jnp.tile(ga_k_norm[layer].astype(F32), N_KV).reshape(1, 128)
        ya, yg = _attention(attn, cos, sin, sink_rows, qw, kw)

        dt_bias = _pad_lanes(ssd_dt_bias[layer].astype(F32).reshape(2, SSD_HEADS)).reshape(2, 1, 128)
        a_log = _pad_lanes(ssd_a_log[layer].astype(F32)).reshape(2, 1, 128)
        d_exp = jnp.repeat(ssd_d[layer].astype(F32), SSD_P).reshape(1, SSD_WIDTH)
        ys = _ssd(conv_in, sz, dt, ssd_conv_w[layer].astype(F32), ssd_conv_b[layer].astype(F32).reshape(1, CONV_W),
                  dt_bias, a_log, d_exp, ssd_norm_w[layer].astype(F32).reshape(1, SSD_WIDTH))

        mats = _s5_matrices(s5_a_re[layer], s5_a_im[layer], s5_log_dt[layer], s5_b_re[layer], s5_b_im[layer],
                            s5_c_re[layer], s5_c_im[layer])
        y5 = _s5(u_g.reshape(S5_GROUPS, S5_NCH * n_batch, S5_ROWW), mats, n_batch)
        y5 = y5.reshape(S5_GROUPS, S5_NCH, n_batch * S5_ROWW)

        x1, h2, route, counts = _merge(
            xcat, mods, ya, ys, yg, y5, su, s5_d[layer].astype(F32).reshape(1, S5_WIDTH), wg,
            w_branch[layer].astype(BF16), s5_w_glu[layer].astype(BF16), w_out[layer].astype(BF16),
            ln1_g[layer].reshape(1, D_MODEL), ln1_b[layer].reshape(1, D_MODEL), rw_t, rb)

        slot_tok, pos, block_expert, n_used = _dispatch(route, counts)
        xs = h2.reshape(n_tok, D_MODEL)[slot_tok]
        y_slots = _moe_experts(block_expert, n_used, xs, moe_w_gate, moe_w_up, moe_w_down, layer)
        f = y_slots[pos.reshape(-1)].reshape(TOP_K, n_batch, TOK, D_MODEL)
        prev = (x1, mods, f, route, ln2_g[layer].reshape(1, D_MODEL), ln2_b[layer].reshape(1, D_MODEL))
    return _final(*prev)
```

```python
import functools
import math

import jax
import jax.numpy as jnp
import numpy as np
from jax import lax
from jax.experimental import pallas as pl
from jax.experimental.pallas import tpu as pltpu

F32 = jnp.float32
BF16 = jnp.bfloat16

D_MODEL = 1024
DEPTH = 4
GRID_W = 64
CTX_LEN = 256
SEQ = 2048
TOK = CTX_LEN + SEQ
N_MOD = 6
HEAD_DIM = 64
N_BRANCH = 4
N_HEADS = 8
N_KV = 2
GRP = N_HEADS // N_KV
WINDOW = 128
QB = 128
ROPE_THETA = 10000.0

SSD_HEADS = 8
SSD_P = 64
SSD_GROUPS = 2
SSD_N = 128
SSD_T = 128
SSD_WIDTH = SSD_HEADS * SSD_P
SSD_CONV_CH = SSD_WIDTH + 2 * SSD_GROUPS * SSD_N
N_CHUNK = TOK // SSD_T

S5_GROUP = 16
S5_WIDTH = 512
S5_GROUPS = S5_WIDTH // S5_GROUP
S5_STATE = 64
S5_T = 16
S5_ROWW = S5_T * S5_GROUP
S5_NCH = TOK // S5_T
S5_CTX_CH = CTX_LEN // S5_T

N_EXPERTS = 32
N_EXPERT_GROUPS = 8
PER_GROUP = N_EXPERTS // N_EXPERT_GROUPS
TOP_K = 2
D_EXPERT = 512
MOE_ROWS = 256

ALPHA = (2 * DEPTH) ** 0.25
NORM_EPS = 1e-6

TM = 256
VMEM_LIMIT = 56 * 1024 * 1024

_IN_SIZES = (512, 128, 128, 512, 128, 128, 512, 512, 256, 256, 16, 512, 4096)
_IN_OFF = np.concatenate([[0], np.cumsum(_IN_SIZES)]).astype(int)
(_AQ, _AK, _AV, _GQ, _GK, _GV, _SX, _SZ, _SB, _SC, _SDT, _SU, _GATE) = range(13)

ATTN_W = 1536
CONV_W = 1024
DT_W = 256
PROJ_W = ATTN_W + CONV_W + 512 + 512 + DT_W


def _params(sem=None):
    return pltpu.CompilerParams(dimension_semantics=sem, vmem_limit_bytes=VMEM_LIMIT)


def _ln(x):
    mu = jnp.mean(x, axis=-1, keepdims=True)
    xc = x - mu
    var = jnp.mean(xc * xc, axis=-1, keepdims=True)
    return xc * lax.rsqrt(var + NORM_EPS)


def _sigmoid(x):
    return 0.5 * jnp.tanh(0.5 * x) + 0.5


def _silu(x):
    return x * _sigmoid(x)


def _bdot(a, b):
    return jnp.dot(a.astype(BF16), b.astype(BF16), preferred_element_type=F32)


def _bdot_nt(a, b):
    return lax.dot_general(a.astype(BF16), b.astype(BF16), (((1,), (1,)), ((), ())),
                           preferred_element_type=F32)


def _split3(v):
    hi = v.astype(BF16)
    r1 = v - hi.astype(F32)
    mid = r1.astype(BF16)
    lo = (r1 - mid.astype(F32)).astype(BF16)
    return hi, mid, lo


def _mod_kernel(c_ref, w_ref, b_ref, o_ref):
    c = c_ref[...]
    o_ref[...] = _bdot(_silu(c), w_ref[...]) + b_ref[...]


def _modulation(cond, mod_w, mod_b):
    n_layer = mod_w.shape[0]
    rows = cond.shape[0]
    return pl.pallas_call(
        _mod_kernel,
        out_shape=jax.ShapeDtypeStruct((n_layer, rows, N_MOD * D_MODEL), F32),
        grid=(n_layer, N_MOD),
        in_specs=[
            pl.BlockSpec((rows, D_MODEL), lambda l, n: (0, 0)),
            pl.BlockSpec((None, D_MODEL, D_MODEL), lambda l, n: (l, 0, n)),
            pl.BlockSpec((None, 1, D_MODEL), lambda l, n: (l, 0, n)),
        ],
        out_specs=pl.BlockSpec((None, rows, D_MODEL), lambda l, n: (l, 0, n)),
        compiler_params=_params(("arbitrary", "arbitrary")),
        name="modulation",
    )(cond, mod_w, mod_b.reshape(n_layer, 1, N_MOD * D_MODEL))


def _mod_part(mod_ref, k):
    return mod_ref[:, k * D_MODEL:(k + 1) * D_MODEL]


def _mod_row(n_batch):
    return lambda b, j: (jnp.where(j == 0, n_batch, b), 0, 0)


def _chunk_swap():
    r = np.arange(TM)
    p = np.zeros((TM, TM), np.float32)
    p[r, (r % S5_T) * (TM // S5_T) + r // S5_T] = 1.0
    return jnp.asarray(p, BF16)


def _ffn_residual(x1, mod_ref, f0_ref, f1_ref, route_ref, lng_ref, lnb_ref):
    r = lax.broadcasted_iota(jnp.int32, (TM, TM), 0)
    c = lax.broadcasted_iota(jnp.int32, (TM, TM), 1)
    w = [jnp.sum(jnp.where(r == c, route_ref[TOP_K + k:TOP_K + k + 1, :], 0.0), axis=1, keepdims=True)
         for k in range(TOP_K)]
    ffn = w[0] * f0_ref[...].astype(F32) + w[1] * f1_ref[...].astype(F32)
    return _ln(ALPHA * x1 + _mod_part(mod_ref, 5) * ffn) * lng_ref[...] + lnb_ref[...]


def _inproj_kernel(*refs, fused):
    if fused:
        (x1_ref, modp_ref, f0_ref, f1_ref, route_ref, lng_ref, lnb_ref, mod_ref, w_ref, swap_ref,
         x_out_ref, attn_ref, conv_ref, sz_ref, su_ref, dt_ref, u_ref, sub_ref) = refs
    else:
        x_ref, mod_ref, w_ref, swap_ref, attn_ref, conv_ref, sz_ref, su_ref, dt_ref, u_ref, sub_ref = refs

    @pl.when(pl.program_id(0) == 0)
    def _reset():
        sub_ref[...] = jnp.zeros_like(sub_ref)

    by_step = jnp.dot(swap_ref[...], sub_ref[...], preferred_element_type=F32)
    n_ch = TM // S5_T
    for t in range(S5_T):
        rows = by_step[t * n_ch:(t + 1) * n_ch]
        for g in range(S5_GROUPS):
            u_ref[g, :, t * S5_GROUP:(t + 1) * S5_GROUP] = rows[:, g * S5_GROUP:(g + 1) * S5_GROUP].astype(BF16)

    if fused:
        x = _ffn_residual(x1_ref[...], modp_ref, f0_ref, f1_ref, route_ref, lng_ref, lnb_ref)
        x_out_ref[...] = x
    else:
        x = x_ref[...]
    h = _ln(x) * (1.0 + _mod_part(mod_ref, 1)) + _mod_part(mod_ref, 0)
    hb = h.astype(BF16)
    off = 0
    for ref in (attn_ref, conv_ref, sz_ref, su_ref, dt_ref):
        width = ref.shape[-1]
        ref[...] = jnp.dot(hb, w_ref[:, off:off + width], preferred_element_type=F32)
        off += width
    sub_ref[...] = su_ref[...].astype(BF16)


def _inproj(xcat, mods, w1, prev=None):
    n_batch = xcat.shape[0] if prev is None else prev[0].shape[0]
    n_tiles = TOK // TM
    n_all = n_batch * n_tiles
    widths = (ATTN_W, CONV_W, 512, 512, DT_W)

    def at(lag):
        def split(s):
            t = jnp.clip(s - lag, 0, n_all - 1)
            return t // n_tiles, t % n_tiles
        return split

    tile = lambda w: pl.BlockSpec((None, TM, w), lambda s: at(0)(s) + (0,))

    def mod_index(s):
        b, j = at(0)(s)
        return jnp.where(j == 0, n_batch, b), 0, 0

    mod_spec = pl.BlockSpec((None, 1, N_MOD * D_MODEL), mod_index)
    const = lambda shape: pl.BlockSpec(shape, lambda s: tuple(0 for _ in shape))
    out_shape = ([jax.ShapeDtypeStruct((n_batch, TOK, w), F32) for w in widths]
                 + [jax.ShapeDtypeStruct((S5_GROUPS, S5_NCH, n_batch * S5_ROWW), BF16)])
    out_specs = ([tile(w) for w in widths]
                 + [pl.BlockSpec((S5_GROUPS, TM // S5_T, S5_ROWW), lambda s: (0,) + at(1)(s)[::-1])])
    own = [mod_spec, const((D_MODEL, PROJ_W)), const((TM, TM))]
    if prev is None:
        in_specs, args = [tile(D_MODEL)] + own, (xcat, mods, w1, _chunk_swap())
    else:
        x1, mods_prev, f, route, ln_g, ln_b = prev
        in_specs = [tile(D_MODEL), mod_spec,
                    pl.BlockSpec((None, None, TM, D_MODEL), lambda s: (0,) + at(0)(s) + (0,)),
                    pl.BlockSpec((None, None, TM, D_MODEL), lambda s: (1,) + at(0)(s) + (0,)),
                    pl.BlockSpec((8, TM), lambda s: (0, jnp.clip(s, 0, n_all - 1))),
                    const((1, D_MODEL)), const((1, D_MODEL))] + own
        args = (x1, mods_prev, f, f, route, ln_g, ln_b, mods, w1, _chunk_swap())
        out_shape = [jax.ShapeDtypeStruct((n_batch, TOK, D_MODEL), F32)] + out_shape
        out_specs = [tile(D_MODEL)] + out_specs
    return pl.pallas_call(
        functools.partial(_inproj_kernel, fused=prev is not None),
        out_shape=out_shape,
        grid=(n_all + 1,),
        in_specs=in_specs,
        out_specs=out_specs,
        scratch_shapes=[pltpu.VMEM((TM, S5_WIDTH), BF16)],
        compiler_params=_params(("arbitrary",)),
        name="inproj",
    )(*args)


def _rope(t, cos, sin):
    width = t.shape[-1]
    lane = lax.broadcasted_iota(jnp.int32, t.shape, t.ndim - 1)
    rot = jnp.where(lane % 32 < 16, -pltpu.roll(t, width - 16, t.ndim - 1), pltpu.roll(t, 16, t.ndim - 1))
    return t * cos + rot * sin


def _tile4(v):
    return jnp.concatenate([v, v, v, v], axis=1)


def _head_inv_rms(t, n_heads):
    t2 = t * t
    lane = lax.broadcasted_iota(jnp.int32, t.shape, 1)
    inv = jnp.zeros_like(t)
    for h in range(n_heads):
        ms = jnp.sum(t2[:, h * HEAD_DIM:(h + 1) * HEAD_DIM], axis=1, keepdims=True) * (1.0 / HEAD_DIM)
        inv = jnp.where(lane // HEAD_DIM == h, lax.rsqrt(ms + NORM_EPS), inv)
    return inv


def _stack_heads(qr, kv):
    parts = [qr[:, (kv * GRP + g) * HEAD_DIM:(kv * GRP + g + 1) * HEAD_DIM] for g in range(GRP)]
    return jnp.concatenate(parts, axis=0).astype(BF16)


QPS = 2
LOG2E = math.log2(math.e)


KEY_BLOCK = 128


def _attend(jobs, sink_ref, out_ref, kb_ref, vt_ref, s_ref, p_ref):
    width_q = GRP * QB
    halves = [slice(0, width_q // 2), slice(width_q // 2, width_q)]
    sub = KEY_BLOCK // 8
    for u, (qr, parts) in enumerate(jobs):
        for kv in range(N_KV):
            qs = _stack_heads(qr, kv)
            row = 0
            for key_idx, width, bias in parts:
                for hc in halves:
                    s = _bdot_nt(kb_ref[kv, key_idx, :], qs[hc])
                    if bias is not None:
                        s = s + jnp.concatenate([bias] * (width_q // 2 // QB), axis=1)
                    s_ref[u, kv, row:row + width, hc] = s
                row += width
    for u, (_, parts) in enumerate(jobs):
        n_keys = sum(width for _, width, _ in parts)
        sink_term = {}
        for kv in range(N_KV):
            for g in range(GRP):
                cols = slice(g * QB, (g + 1) * QB)
                mx = None
                for r0 in range(0, n_keys, KEY_BLOCK):
                    blk = jnp.max(s_ref[u, kv, r0:r0 + KEY_BLOCK, cols].reshape(sub, 8, QB), axis=0)
                    mx = blk if mx is None else jnp.maximum(mx, blk)
                m = jnp.max(mx, axis=0, keepdims=True)
                if sink_ref is not None:
                    m = jnp.maximum(m, sink_ref[kv, :, cols])
                    sink_term[kv, g] = jnp.exp2(sink_ref[kv, :, cols] - m)
                m_b = jnp.broadcast_to(m, (KEY_BLOCK, QB))
                for r0 in range(0, n_keys, KEY_BLOCK):
                    p_ref[u, kv, r0:r0 + KEY_BLOCK, cols] = jnp.exp2(
                        s_ref[u, kv, r0:r0 + KEY_BLOCK, cols] - m_b).astype(BF16)
        o_t = {}
        for kv in range(N_KV):
            for hi, hc in enumerate(halves):
                o = None
                row = 0
                for key_idx, width, _ in parts:
                    part = jnp.dot(vt_ref[kv, :, key_idx], p_ref[u, kv, row:row + width, hc],
                                   preferred_element_type=F32)
                    o = part if o is None else o + part
                    row += width
                o_t[kv, hi] = o
        for g in range(GRP):
            hi, gl = divmod(g, GRP // 2)
            cols = slice(gl * QB, (gl + 1) * QB)
            scaled = []
            for kv in range(N_KV):
                den = o_t[kv, hi][HEAD_DIM:HEAD_DIM + 1, cols]
                if sink_ref is not None:
                    den = den + sink_term[kv, g]
                scaled.append(o_t[kv, hi][0:HEAD_DIM, cols] / den)
            pair = jnp.concatenate(scaled, axis=0).T
            for kv in range(N_KV):
                h = kv * GRP + g
                out_ref[u * QB:(u + 1) * QB, h * HEAD_DIM:(h + 1) * HEAD_DIM] = (
                    pair[:, kv * HEAD_DIM:(kv + 1) * HEAD_DIM].astype(out_ref.dtype))


def _wattn_kernel(q_ref, k_ref, v_ref, cos_ref, sin_ref, sink_ref, bias_ref, out_ref, kb_ref, vt_ref, s_ref, p_ref):
    i = pl.program_id(1)

    @pl.when(i == 0)
    def _prep():
        kr = _rope(k_ref[...], cos_ref[...], sin_ref[...])
        v_t = v_ref[...].T
        for kv in range(N_KV):
            kb_ref[kv] = kr[:, kv * HEAD_DIM:(kv + 1) * HEAD_DIM].astype(BF16)
            vt_ref[kv, 0:HEAD_DIM, :] = v_t[kv * HEAD_DIM:(kv + 1) * HEAD_DIM, :].astype(BF16)
            vt_ref[kv, HEAD_DIM:HEAD_DIM + 16, :] = jnp.ones((16, TOK), BF16)

    row0 = pl.multiple_of(i * (QPS * QB), QPS * QB)
    cos = _tile4(cos_ref[pl.ds(row0, QPS * QB), :])
    sin = _tile4(sin_ref[pl.ds(row0, QPS * QB), :])
    qr = _rope(q_ref[...], cos, sin) * (HEAD_DIM ** -0.5 * LOG2E)
    qrs = [qr[u * QB:(u + 1) * QB] for u in range(QPS)]

    ctx_keys = (slice(0, CTX_LEN), CTX_LEN, None)

    @pl.when(i < CTX_LEN // (QPS * QB))
    def _ctx_queries():
        _attend([(q, [ctx_keys]) for q in qrs], sink_ref, out_ref, kb_ref, vt_ref, s_ref, p_ref)

    @pl.when(i >= CTX_LEN // (QPS * QB))
    def _latent_queries():
        jobs = []
        for u in range(QPS):
            j = i * QPS + u - CTX_LEN // QB
            band = jnp.clip(j - 1, 0, SEQ // QB - 3)
            start = pl.multiple_of(CTX_LEN + band * QB, QB)
            bias = bias_ref[j - band]
            jobs.append((qrs[u], [ctx_keys, (pl.ds(start, 3 * QB), 3 * QB, bias)]))
        _attend(jobs, sink_ref, out_ref, kb_ref, vt_ref, s_ref, p_ref)


def _gattn_kernel(q_ref, k_ref, v_ref, cos_ref, sin_ref, qw_ref, kw_ref, out_ref, kb_ref, vt_ref, s_ref, p_ref):
    i = pl.program_id(1)

    @pl.when(i == 0)
    def _prep():
        k = k_ref[...]
        kn = k * _head_inv_rms(k, N_KV) * kw_ref[...]
        kr = _rope(kn, cos_ref[...], sin_ref[...])
        v_t = v_ref[...].T
        for kv in range(N_KV):
            kb_ref[kv] = kr[:, kv * HEAD_DIM:(kv + 1) * HEAD_DIM].astype(BF16)
            vt_ref[kv, 0:HEAD_DIM, :] = v_t[kv * HEAD_DIM:(kv + 1) * HEAD_DIM, :].astype(BF16)
            vt_ref[kv, HEAD_DIM:HEAD_DIM + 16, :] = jnp.ones((16, TOK), BF16)

    row0 = pl.multiple_of(i * (QPS * QB), QPS * QB)
    cos = _tile4(cos_ref[pl.ds(row0, QPS * QB), :])
    sin = _tile4(sin_ref[pl.ds(row0, QPS * QB), :])
    q = q_ref[...]
    qn = q * _head_inv_rms(q, N_HEADS) * qw_ref[...]
    qr = _rope(qn, cos, sin) * (HEAD_DIM ** -0.5 * LOG2E)
    qrs = [qr[u * QB:(u + 1) * QB] for u in range(QPS)]

    @pl.when(i < CTX_LEN // (QPS * QB))
    def _ctx_queries():
        _attend([(q_u, [(slice(0, CTX_LEN), CTX_LEN, None)]) for q_u in qrs], None, out_ref, kb_ref, vt_ref,
                s_ref, p_ref)

    @pl.when(i >= CTX_LEN // (QPS * QB))
    def _latent_queries():
        _attend([(q_u, [(slice(0, TOK), TOK, None)]) for q_u in qrs], None, out_ref, kb_ref, vt_ref, s_ref, p_ref)


def _window_bias():
    q_pos = np.arange(QB)[None, None, :] + QB * np.arange(3)[:, None, None]
    k_pos = np.arange(3 * QB)[None, :, None]
    return jnp.asarray(np.where(np.abs(q_pos - k_pos) <= WINDOW, 0.0, -np.inf), F32)


def _attention(attn, cos, sin, sink_rows, qw, kw):
    n_batch = attn.shape[0]
    grid = (n_batch, TOK // (QPS * QB))
    q_spec = lambda blk: pl.BlockSpec((None, QPS * QB, 512), lambda b, i: (b, i, blk))
    kv_spec = lambda blk: pl.BlockSpec((None, TOK, 128), lambda b, i: (b, 0, blk))
    tab_spec = pl.BlockSpec((TOK, 128), lambda b, i: (0, 0))
    out_spec = pl.BlockSpec((None, QPS * QB, 512), lambda b, i: (b, i, 0))
    out_shape = jax.ShapeDtypeStruct((n_batch, TOK, 512), BF16)
    kv_scratch = [pltpu.VMEM((N_KV, TOK, HEAD_DIM), BF16), pltpu.VMEM((N_KV, HEAD_DIM + 16, TOK), BF16)]
    w_keys = CTX_LEN + 3 * QB
    ya = pl.pallas_call(
        _wattn_kernel, out_shape=out_shape, grid=grid,
        in_specs=[q_spec(0), kv_spec(8), kv_spec(9), tab_spec, tab_spec,
                  pl.BlockSpec((N_KV, 1, GRP * QB), lambda b, i: (0, 0, 0)),
                  pl.BlockSpec((3, 3 * QB, QB), lambda b, i: (0, 0, 0))],
        out_specs=out_spec,
        scratch_shapes=kv_scratch + [pltpu.VMEM((QPS, N_KV, w_keys, GRP * QB), F32),
                                     pltpu.VMEM((QPS, N_KV, w_keys, GRP * QB), BF16)],
        compiler_params=_params(("arbitrary", "arbitrary")), name="window_attention",
    )(attn, attn, attn, cos, sin, sink_rows, _window_bias())
    yg = pl.pallas_call(
        _gattn_kernel, out_shape=out_shape, grid=grid,
        in_specs=[q_spec(1), kv_spec(10), kv_spec(11), tab_spec, tab_spec,
                  pl.BlockSpec((1, 512), lambda b, i: (0, 0)),
                  pl.BlockSpec((1, 128), lambda b, i: (0, 0))],
        out_specs=out_spec,
        scratch_shapes=kv_scratch + [pltpu.VMEM((QPS, N_KV, TOK, GRP * QB), F32),
                                     pltpu.VMEM((QPS, N_KV, TOK, GRP * QB), BF16)],
        compiler_params=_params(("arbitrary", "arbitrary")), name="global_attention",
    )(attn, attn, attn, cos, sin, qw, kw)
    return ya, yg


def _ssd_chunk(s):
    r = s - N_CHUNK
    back = jnp.where(r == 0, 1, jnp.where(r == 1, 0, N_CHUNK + 1 - r))
    return jnp.where(s < N_CHUNK, s, back)


SSD_NB = 4


def _ssd_kernel(x_ref, prev_ref, next_ref, z_ref, dt_ref, cw_ref, cb_ref, dtb_ref, a_ref, d_ref, nw_ref, e_ref,
                out_ref, yf_ref, state_ref, y_ref):
    s = pl.program_id(1)
    chunk = _ssd_chunk(s)
    backward = s >= N_CHUNK

    @pl.when((s == 0) | (s == N_CHUNK))
    def _reset():
        state_ref[...] = jnp.zeros_like(state_ref)

    first = (chunk == 0) | (chunk == CTX_LEN // SSD_T)
    last = (chunk == CTX_LEN // SSD_T - 1) | (chunk == N_CHUNK - 1)
    row = lax.broadcasted_iota(jnp.int32, (SSD_T, CONV_W), 0)
    tr = lax.broadcasted_iota(jnp.int32, (SSD_T, SSD_T), 0)
    tc = lax.broadcasted_iota(jnp.int32, (SSD_T, SSD_T), 1)
    causal = jnp.where(backward, tc - tr, tr - tc) >= 0
    tri = jnp.where(causal, 1.0, 0.0).astype(BF16)
    neg_a = -jnp.exp(a_ref[...])
    expand = e_ref[...]

    def widen(v):
        return jnp.dot(jnp.concatenate(_split3(v), axis=1), expand, preferred_element_type=F32)

    rows = pl.ds(pl.multiple_of(chunk * SSD_T, SSD_T), SSD_T)

    xs = []
    for u in range(SSD_NB):
        xin = x_ref[u]
        prev = jnp.where(first, 0.0, prev_ref[u, 7:8, :])
        nxt = jnp.where(last, 0.0, next_ref[u, 0:1, :])
        xm1 = jnp.where(row == 0, prev, pltpu.roll(xin, 1, 0))
        xp1 = jnp.where(row == SSD_T - 1, nxt, pltpu.roll(xin, SSD_T - 1, 0))
        conv = cw_ref[0:1, :] * xm1 + cw_ref[1:2, :] * xin + cw_ref[2:3, :] * xp1 + cb_ref[...]
        xbc = _silu(conv)
        x = xbc[:, 0:SSD_WIDTH]
        xs.append(x)
        bmat = [xbc[:, SSD_WIDTH + g * SSD_N:SSD_WIDTH + (g + 1) * SSD_N] for g in range(SSD_GROUPS)]
        cmat = [xbc[:, SSD_WIDTH + (SSD_GROUPS + g) * SSD_N:SSD_WIDTH + (SSD_GROUPS + g + 1) * SSD_N]
                for g in range(SSD_GROUPS)]

        dtv = dt_ref[u] + dtb_ref[...]
        dt = jnp.maximum(dtv, 0.0) + jnp.log(1.0 + jnp.exp(-jnp.abs(dtv)))
        hi, mid, lo = _split3(dt * neg_a)
        cs = (jnp.dot(tri, hi, preferred_element_type=F32) + jnp.dot(tri, mid, preferred_element_type=F32)
              + jnp.dot(tri, lo, preferred_element_type=F32))
        cs_t = cs.T

        dt_x = widen(dt)
        cs_x = widen(cs)
        tot_x = jnp.where(backward, cs_x[0:1, :], cs_x[SSD_T - 1:SSD_T, :])
        ecs_x = jnp.exp(cs_x)
        etot_x = jnp.exp(tot_x)
        xd = x * dt_x
        xd_b = xd.astype(BF16)
        xd_end = (xd * jnp.exp(tot_x - cs_x)).astype(BF16)

        gw = SSD_WIDTH // SSD_GROUPS
        for g in range(SSD_GROUPS):
            glanes = slice(g * gw, (g + 1) * gw)
            cb = _bdot_nt(cmat[g], bmat[g])
            st = state_ref[u, g]
            y_off = ecs_x[:, glanes] * jnp.dot(cmat[g].astype(BF16), st.astype(BF16), preferred_element_type=F32)
            state_ref[u, g] = etot_x[:, glanes] * st + jnp.dot(bmat[g].T.astype(BF16), xd_end[:, glanes],
                                                               preferred_element_type=F32)
            for j in range(SSD_HEADS // SSD_GROUPS):
                h = g * (SSD_HEADS // SSD_GROUPS) + j
                lanes = slice(h * SSD_P, (h + 1) * SSD_P)
                seg = jnp.exp(jnp.where(causal, cs[:, h:h + 1] - cs_t[h:h + 1, :], -jnp.inf))
                y_ref[u, :, lanes] = (jnp.dot((cb * seg).astype(BF16), xd_b[:, lanes], preferred_element_type=F32)
                                      + y_off[:, j * SSD_P:(j + 1) * SSD_P])

    @pl.when(jnp.logical_not(backward))
    def _keep():
        for u in range(SSD_NB):
            yf_ref[u, rows, :] = y_ref[u]

    @pl.when(backward)
    def _finish():
        for u in range(SSD_NB):
            ytot = yf_ref[u, rows, :] + y_ref[u] + d_ref[...] * xs[u]
            gated = ytot * _silu(z_ref[u])
            ms = jnp.mean(gated * gated, axis=1, keepdims=True)
            out_ref[u] = (gated * lax.rsqrt(ms + NORM_EPS) * nw_ref[...]).astype(out_ref.dtype)


def _ssd(conv_in, sz, dt, conv_w, conv_b, dt_bias, a_log, d_exp, norm_w):
    n_batch = conv_in.shape[0]
    assert n_batch % SSD_NB == 0
    halo = SSD_T // 8
    n_halo = TOK // 8

    def chunk_map(b, s):
        return (b, _ssd_chunk(s), 0)

    def out_map(b, s):
        return (b, jnp.where(s < N_CHUNK, 1, _ssd_chunk(s)), 0)

    const = lambda shape: pl.BlockSpec(shape, lambda b, s: tuple(0 for _ in shape))
    dir_spec = pl.BlockSpec((None, 1, 128), lambda b, s: (s // N_CHUNK, 0, 0))
    expand = jnp.asarray(np.tile(np.repeat(np.eye(128, SSD_HEADS), SSD_P, axis=1), (3, 1)), BF16)
    return pl.pallas_call(
        _ssd_kernel,
        out_shape=jax.ShapeDtypeStruct((n_batch, TOK, SSD_WIDTH), BF16),
        grid=(n_batch // SSD_NB, 2 * N_CHUNK),
        in_specs=[
            pl.BlockSpec((SSD_NB, SSD_T, CONV_W), chunk_map),
            pl.BlockSpec((SSD_NB, 8, CONV_W), lambda b, s: (b, jnp.maximum(_ssd_chunk(s) * halo - 1, 0), 0)),
            pl.BlockSpec((SSD_NB, 8, CONV_W), lambda b, s: (b, jnp.minimum((_ssd_chunk(s) + 1) * halo, n_halo - 1), 0)),
            pl.BlockSpec((SSD_NB, SSD_T, SSD_WIDTH), chunk_map),
            pl.BlockSpec((SSD_NB, SSD_T, 128), lambda b, s: (b, _ssd_chunk(s), s // N_CHUNK)),
            const((3, CONV_W)), const((1, CONV_W)), dir_spec, dir_spec,
            const((1, SSD_WIDTH)), const((1, SSD_WIDTH)), const((3 * 128, SSD_WIDTH)),
        ],
        out_specs=pl.BlockSpec((SSD_NB, SSD_T, SSD_WIDTH), out_map),
        scratch_shapes=[pltpu.VMEM((SSD_NB, TOK, SSD_WIDTH), F32),
                        pltpu.VMEM((SSD_NB, SSD_GROUPS, SSD_N, SSD_WIDTH // SSD_GROUPS), F32),
                        pltpu.VMEM((SSD_NB, SSD_T, SSD_WIDTH), F32)],
        compiler_params=_params(("arbitrary", "arbitrary")),
        name="ssd",
    )(conv_in, conv_in, conv_in, sz, dt, conv_w, conv_b, dt_bias, a_log, d_exp, norm_w, expand)


S5_SW = 4 * 128


def _s5_kernel(u_ref, m_ref, h_ref, g_ref, a_ref, y_ref, loc_ref, prev_ref, *, n_batch):
    u = u_ref[...]
    loc_ref[...] = jnp.dot(u, h_ref[...], preferred_element_type=F32)
    a = [a_ref[k:k + 1, :] for k in range(4)]

    def step(k, carry):
        c_f = k
        c_b = jnp.where(k < S5_CTX_CH, S5_CTX_CH - 1 - k, S5_NCH + S5_CTX_CH - 1 - k)
        new = []
        for d, c in enumerate((c_f, c_b)):
            rows = pl.ds(pl.multiple_of(c * n_batch, n_batch), n_batch)
            re_l, im_l = slice(2 * d * 128, (2 * d + 1) * 128), slice((2 * d + 1) * 128, (2 * d + 2) * 128)
            s_re, s_im = carry[2 * d], carry[2 * d + 1]
            prev_ref[rows, re_l] = s_re
            prev_ref[rows, im_l] = s_im
            a_re, a_im = a[2 * d], a[2 * d + 1]
            new += [a_re * s_re - a_im * s_im + loc_ref[rows, re_l], a_re * s_im + a_im * s_re + loc_ref[rows, im_l]]
        return tuple(new)

    zero = jnp.zeros((n_batch, 128), F32)
    lax.fori_loop(0, S5_NCH, step, (zero,) * 4, unroll=2)
    y = jnp.dot(u, m_ref[...], preferred_element_type=F32)
    y = y + jnp.dot(prev_ref[...].astype(BF16), g_ref[...], preferred_element_type=F32)
    y_ref[...] = y.astype(y_ref.dtype)


def _s5(u_g, mats, n_batch):
    m_all, h_all, g_all, a16 = mats
    rows = u_g.shape[1]
    grp = lambda shape: pl.BlockSpec((None,) + shape, lambda g: (g,) + tuple(0 for _ in shape))
    return pl.pallas_call(
        functools.partial(_s5_kernel, n_batch=n_batch),
        out_shape=jax.ShapeDtypeStruct((S5_GROUPS, rows, S5_ROWW), BF16),
        grid=(S5_GROUPS,),
        in_specs=[grp((rows, S5_ROWW)), grp((S5_ROWW, S5_ROWW)), grp((S5_ROWW, S5_SW)), grp((S5_SW, S5_ROWW)),
                  grp((4, 128))],
        out_specs=grp((rows, S5_ROWW)),
        scratch_shapes=[pltpu.VMEM((rows, S5_SW), F32), pltpu.VMEM((rows, S5_SW), F32)],
        compiler_params=_params(("arbitrary",)),
        name="s5",
    )(u_g, m_all, h_all, g_all, a16)


def _cmul(a, b):
    return a[0] * b[0] - a[1] * b[1], a[0] * b[1] + a[1] * b[0]


def _s5_matrices(a_re, a_im, log_dt, b_re, b_im, c_re, c_im):
    hp = lax.Precision.HIGHEST
    t = jnp.arange(S5_T + 1, dtype=F32)
    m_sum = 0.0
    h_all, g_all, a16_all = [], [], []
    c = (c_re.astype(F32), c_im.astype(F32))
    for direction in range(2):
        are = jnp.minimum(a_re[direction].astype(F32), -1e-4)
        aim = a_im[direction].astype(F32)
        dt = jnp.exp(log_dt[direction].astype(F32))[:, None]
        mag = jnp.exp(t[:, None, None] * (are * dt)[None])
        ang = t[:, None, None] * (aim * dt)[None]
        pw = (mag * jnp.cos(ang), mag * jnp.sin(ang))
        num = (pw[0][1] - 1.0, pw[1][1])
        den = are * are + aim * aim
        coef = ((num[0] * are + num[1] * aim) / den, (num[1] * are - num[0] * aim) / den)
        bbar = _cmul((coef[0][..., None], coef[1][..., None]), (b_re.astype(F32), b_im.astype(F32)))
        pb = _cmul((pw[0][:S5_T, :, :, None], pw[1][:S5_T, :, :, None]), (bbar[0][None], bbar[1][None]))
        taps = (jnp.einsum('gon,tgni->tgoi', c[0], pb[0], precision=hp)
                - jnp.einsum('gon,tgni->tgoi', c[1], pb[1], precision=hp))
        ti = jnp.arange(S5_T)
        lag = (ti[None, :] - ti[:, None]) if direction == 0 else (ti[:, None] - ti[None, :])
        k_full = taps[jnp.clip(lag, 0, S5_T - 1)]
        k_full = jnp.where((lag >= 0)[:, :, None, None, None], k_full, 0.0)
        m_sum = m_sum + k_full.transpose(2, 0, 4, 1, 3).reshape(S5_GROUPS, S5_ROWW, S5_ROWW)
        e_in = (S5_T - 1 - ti) if direction == 0 else ti
        hb = _cmul((pw[0][e_in][..., None], pw[1][e_in][..., None]), (bbar[0][None], bbar[1][None]))
        zpad = jnp.zeros_like(hb[0])
        h_mat = jnp.concatenate([hb[0], zpad, hb[1], zpad], axis=2)
        h_all.append(h_mat.transpose(1, 0, 3, 2).reshape(S5_GROUPS, S5_ROWW, 4 * S5_STATE))
        e_out = (ti + 1) if direction == 0 else (S5_T - ti)
        cp = _cmul((c[0][None], c[1][None]),
                   (pw[0][e_out][:, :, None, :], pw[1][e_out][:, :, None, :]))
        zpad = jnp.zeros_like(cp[0])
        g_mat = jnp.concatenate([cp[0], zpad, -cp[1], zpad], axis=3)
        g_all.append(g_mat.transpose(1, 3, 0, 2).reshape(S5_GROUPS, 4 * S5_STATE, S5_ROWW))
        zrow = jnp.zeros_like(pw[0][S5_T])
        a16_all += [jnp.concatenate([pw[0][S5_T], zrow], axis=1), jnp.concatenate([pw[1][S5_T], zrow], axis=1)]
    return (m_sum.astype(BF16), jnp.concatenate(h_all, axis=2).astype(BF16),
            jnp.concatenate(g_all, axis=1).astype(BF16), jnp.stack(a16_all, axis=1))


def _route(logits_t, bias, base, before):
    scores = [_sigmoid(logits_t[j * N_EXPERT_GROUPS:(j + 1) * N_EXPERT_GROUPS]) for j in range(PER_GROUP)]
    sel = [scores[j] + bias[j * N_EXPERT_GROUPS:(j + 1) * N_EXPERT_GROUPS] for j in range(PER_GROUP)]
    hi1, lo1 = jnp.maximum(sel[0], sel[1]), jnp.minimum(sel[0], sel[1])
    hi2, lo2 = jnp.maximum(sel[2], sel[3]), jnp.minimum(sel[2], sel[3])
    group_score = jnp.maximum(hi1, hi2) + jnp.maximum(jnp.minimum(hi1, hi2), jnp.maximum(lo1, lo2))
    gid = lax.broadcasted_iota(jnp.int32, group_score.shape, 0)
    best = jnp.max(group_score, axis=0, keepdims=True)
    grp = jnp.min(jnp.where(group_score == best, gid, N_EXPERT_GROUPS), axis=0, keepdims=True)
    pick = gid == grp
    v = [jnp.sum(jnp.where(pick, sel[j], 0.0), axis=0, keepdims=True) for j in range(PER_GROUP)]
    sc = [jnp.sum(jnp.where(pick, scores[j], 0.0), axis=0, keepdims=True) for j in range(PER_GROUP)]
    rank = []
    for j in range(PER_GROUP):
        r = jnp.zeros_like(grp)
        for i in range(PER_GROUP):
            if i < j:
                r = r + jnp.where(v[i] >= v[j], 1, 0)
            elif i > j:
                r = r + jnp.where(v[i] > v[j], 1, 0)
        rank.append(r)
    e, w, locs = [], [], []
    for k in range(TOP_K):
        loc = sum(jnp.where(rank[j] == k, j, 0) for j in range(PER_GROUP))
        locs.append(loc)
        e.append((grp * PER_GROUP + loc).astype(F32))
        w.append(sum(jnp.where(rank[j] == k, sc[j], 0.0) for j in range(PER_GROUP)))
    wsum = w[0] + w[1]
    chosen = [jnp.where(pick & ((locs[0] == j) | (locs[1] == j)), 1.0, 0.0) for j in range(PER_GROUP)]
    onehot = jnp.concatenate(chosen, axis=0)
    seen = jnp.dot(onehot.astype(BF16), before, preferred_element_type=F32) + base
    pos = []
    for k in range(TOP_K):
        hit = sum(jnp.where(pick & (locs[k] == j), seen[j * N_EXPERT_GROUPS:(j + 1) * N_EXPERT_GROUPS], 0.0)
                  for j in range(PER_GROUP))
        pos.append(jnp.sum(hit, axis=0, keepdims=True))
    rows = [e[0], e[1], w[0] / wsum, w[1] / wsum, pos[0], pos[1]]
    route = jnp.concatenate(rows + [jnp.zeros_like(wsum)] * (8 - len(rows)), axis=0)
    return route, base + jnp.sum(onehot, axis=1, keepdims=True)


def _gelu_tanh(x):
    return 0.5 * x * (1.0 + jnp.tanh(math.sqrt(2.0 / math.pi) * (x + 0.044715 * (x * x * x))))


MERGE_LAG = 3


def _merge_kernel(xn_ref, modn_ref, x_ref, mod_ref, modt_ref, ya_ref, ys_ref, yg_ref, y5_ref, su_ref, s5d_ref,
                  wg_ref, wb_ref, wglu_ref, wout_ref, lng_ref, lnb_ref, rw_ref, rb_ref, before_ref, swap_ref,
                  x1_ref, h2_ref, route_ref, count_ref, base_ref, y5s_ref, y5p_ref, hbs_ref, v_ref, h2s_ref):
    s = pl.program_id(0)

    @pl.when(s == 0)
    def _reset():
        base_ref[...] = jnp.zeros_like(base_ref)
        y5p_ref[...] = jnp.zeros_like(y5p_ref)
        hbs_ref[...] = jnp.zeros_like(hbs_ref)
        v_ref[...] = jnp.zeros_like(v_ref)
        h2s_ref[...] = jnp.zeros_like(h2s_ref)

    logits_t = _bdot_nt(rw_ref[...], h2s_ref[...])
    v_prev = v_ref[...]

    hb = hbs_ref[...]
    glu = jnp.dot(y5p_ref[...], wglu_ref[...], preferred_element_type=F32)
    y5 = (glu[:, 0:S5_WIDTH] * _sigmoid(glu[:, S5_WIDTH:2 * S5_WIDTH])).astype(BF16)
    def select_experts():
        base_old = base_ref[...]
        route, base_new = _route(logits_t, rb_ref[...], base_old, before_ref[...])
        route_ref[...] = route
        base = jnp.where(s >= MERGE_LAG, base_new, base_old)
        base_ref[...] = base
        count_ref[...] = jnp.broadcast_to(base, count_ref.shape)

    def post_norm():
        x1 = _ln(v_prev) * lng_ref[...] + lnb_ref[...]
        x1_ref[...] = x1
        h2 = (_ln(x1) * (1.0 + _mod_part(modt_ref, 4)) + _mod_part(modt_ref, 3)).astype(BF16)
        h2_ref[...] = h2
        h2s_ref[...] = h2

    def relayout():
        n_ch = TM // S5_T
        for t in range(S5_T):
            for g in range(S5_GROUPS):
                y5s_ref[t * n_ch:(t + 1) * n_ch, g * S5_GROUP:(g + 1) * S5_GROUP] = (
                    y5_ref[g, :, t * S5_GROUP:(t + 1) * S5_GROUP])

    side_work = [post_norm, select_experts, relayout, lambda: None]
    acc = jnp.zeros((TM, D_MODEL), F32)
    for n, y in enumerate((ya_ref[...], ys_ref[...], yg_ref[...], y5)):
        gate = _sigmoid(jnp.dot(hb, wg_ref[:, n * D_MODEL:(n + 1) * D_MODEL], preferred_element_type=F32))
        acc = acc + gate * jnp.dot(y, wb_ref[n], preferred_element_type=F32)
        side_work[n]()
    mix = jnp.dot(acc.astype(BF16), wout_ref[...], preferred_element_type=F32)
    v_ref[...] = ALPHA * x_ref[...] + _mod_part(mod_ref, 2) * mix

    y5_pre = jnp.dot(swap_ref[...], y5s_ref[...], preferred_element_type=F32) + s5d_ref[...] * su_ref[...]
    y5p_ref[...] = _gelu_tanh(y5_pre).astype(BF16)
    hbs_ref[...] = (_ln(xn_ref[...]) * (1.0 + _mod_part(modn_ref, 1)) + _mod_part(modn_ref, 0)).astype(BF16)


def _merge(xcat, mods, ya, ys, yg, y5, su, s5_d, wg, wb, wglu, wout, ln_g, ln_b, rw_t, rb):
    n_batch = xcat.shape[0]
    n_tiles = TOK // TM
    n_all = n_batch * n_tiles

    def at(lag):
        def split(s):
            t = jnp.clip(s - lag, 0, n_all - 1)
            return t // n_tiles, t % n_tiles
        return split

    def tile(w, lag):
        return pl.BlockSpec((None, TM, w), lambda s: at(lag)(s) + (0,))

    def mod_spec(lag):
        def index(s):
            b, j = at(lag)(s)
            return jnp.where(j == 0, n_batch, b), 0, 0
        return pl.BlockSpec((None, 1, N_MOD * D_MODEL), index)

    def flat(lag):
        return lambda s: (0, jnp.clip(s - lag, 0, n_all - 1))

    const = lambda shape: pl.BlockSpec(shape, lambda s: tuple(0 for _ in shape))
    before = jnp.asarray(np.triu(np.ones((TM, TM)), 1), BF16)
    return pl.pallas_call(
        _merge_kernel,
        out_shape=[jax.ShapeDtypeStruct((n_batch, TOK, D_MODEL), F32),
                   jax.ShapeDtypeStruct((n_batch, TOK, D_MODEL), BF16),
                   jax.ShapeDtypeStruct((8, n_batch * TOK), F32),
                   jax.ShapeDtypeStruct((N_EXPERTS, 128), F32)],
        grid=(n_all + MERGE_LAG,),
        in_specs=[tile(D_MODEL, 0), mod_spec(0), tile(D_MODEL, 1), mod_spec(1), mod_spec(2),
                  tile(512, 1), tile(512, 1), tile(512, 1),
                  pl.BlockSpec((S5_GROUPS, TM // S5_T, S5_ROWW), lambda s: (0,) + at(0)(s)[::-1]),
                  tile(512, 0), const((1, S5_WIDTH)),
                  const((D_MODEL, N_BRANCH * D_MODEL)), const((N_BRANCH, 512, D_MODEL)),
                  const((S5_WIDTH, 2 * S5_WIDTH)), const((D_MODEL, D_MODEL)),
                  const((1, D_MODEL)), const((1, D_MODEL)),
                  const((N_EXPERTS, D_MODEL)), const((N_EXPERTS, 1)), const((TM, TM)), const((TM, TM))],
        out_specs=[tile(D_MODEL, 2), tile(D_MODEL, 2),
                   pl.BlockSpec((8, TM), flat(MERGE_LAG)),
                   const((N_EXPERTS, 128))],
        scratch_shapes=[pltpu.VMEM((N_EXPERTS, 1), F32), pltpu.VMEM((TM, S5_WIDTH), BF16),
                        pltpu.VMEM((TM, S5_WIDTH), BF16), pltpu.VMEM((TM, D_MODEL), BF16),
                        pltpu.VMEM((TM, D_MODEL), F32), pltpu.VMEM((TM, D_MODEL), BF16)],
        compiler_params=_params(("arbitrary",)),
        name="merge",
    )(xcat, mods, xcat, mods, mods, ya, ys, yg, y5, su, s5_d, wg, wb, wglu, wout, ln_g, ln_b, rw_t, rb, before,
      _chunk_swap())


def _moe_kernel(be_ref, nu_ref, x_ref, wg_ref, wu_ref, wd_ref, y_ref, wgu_s, wd_s):
    i = pl.program_id(0)
    used = i < nu_ref[0]

    @pl.when(used & ((i == 0) | (be_ref[i] != be_ref[jnp.maximum(i - 1, 0)])))
    def _cast():
        wgu_s[:, 0:D_EXPERT] = wg_ref[...].astype(BF16)
        wgu_s[:, D_EXPERT:2 * D_EXPERT] = wu_ref[...].astype(BF16)
        wd_s[...] = wd_ref[...].astype(BF16)

    @pl.when(used)
    def _block():
        gu = jnp.dot(x_ref[...], wgu_s[...], preferred_element_type=F32)
        mid = _silu(gu[:, 0:D_EXPERT]) * gu[:, D_EXPERT:2 * D_EXPERT]
        y_ref[...] = jnp.dot(mid.astype(BF16), wd_s[...], preferred_element_type=F32).astype(y_ref.dtype)


def _moe_experts(block_expert, n_used, xs, w_gate, w_up, w_down, layer):
    n_blocks = block_expert.shape[0]
    return pl.pallas_call(
        _moe_kernel,
        out_shape=jax.ShapeDtypeStruct((n_blocks * MOE_ROWS, D_MODEL), BF16),
        grid_spec=pltpu.PrefetchScalarGridSpec(
            num_scalar_prefetch=2, grid=(n_blocks,),
            in_specs=[pl.BlockSpec((MOE_ROWS, D_MODEL), lambda i, be, nu: (i, 0)),
                      pl.BlockSpec((None, None, D_MODEL, D_EXPERT), lambda i, be, nu: (layer, be[i], 0, 0)),
                      pl.BlockSpec((None, None, D_MODEL, D_EXPERT), lambda i, be, nu: (layer, be[i], 0, 0)),
                      pl.BlockSpec((None, None, D_EXPERT, D_MODEL), lambda i, be, nu: (layer, be[i], 0, 0))],
            out_specs=pl.BlockSpec((MOE_ROWS, D_MODEL), lambda i, be, nu: (i, 0)),
            scratch_shapes=[pltpu.VMEM((D_MODEL, 2 * D_EXPERT), BF16), pltpu.VMEM((D_EXPERT, D_MODEL), BF16)]),
        compiler_params=_params(("arbitrary",)),
        name="moe_experts",
    )(block_expert, n_used, xs, w_gate, w_up, w_down)


def _final_kernel(x_ref, mod_ref, f0_ref, f1_ref, route_ref, lng_ref, lnb_ref, o_ref):
    o_ref[...] = _ffn_residual(x_ref[...], mod_ref, f0_ref, f1_ref, route_ref, lng_ref, lnb_ref)


def _final(x1, mods, f, route, ln_g, ln_b):
    n_batch = x1.shape[0]
    n_tiles = TOK // TM
    skip = CTX_LEN // TM
    const = pl.BlockSpec((1, D_MODEL), lambda b, j: (0, 0))
    return pl.pallas_call(
        _final_kernel,
        out_shape=jax.ShapeDtypeStruct((n_batch, SEQ, D_MODEL), F32),
        grid=(n_batch, n_tiles - skip),
        in_specs=[pl.BlockSpec((None, TM, D_MODEL), lambda b, j: (b, j + skip, 0)),
                  pl.BlockSpec((None, 1, N_MOD * D_MODEL), lambda b, j: (b, 0, 0)),
                  pl.BlockSpec((None, None, TM, D_MODEL), lambda b, j: (0, b, j + skip, 0)),
                  pl.BlockSpec((None, None, TM, D_MODEL), lambda b, j: (1, b, j + skip, 0)),
                  pl.BlockSpec((8, TM), lambda b, j: (0, b * n_tiles + j + skip)),
                  const, const],
        out_specs=pl.BlockSpec((None, TM, D_MODEL), lambda b, j: (b, j, 0)),
        compiler_params=_params(("arbitrary", "arbitrary")),
        name="ffn_residual",
    )(x1, mods, f, f, route, ln_g, ln_b)


def _lookup(table, idx):
    ids = jnp.arange(table.shape[0], dtype=jnp.int32).reshape((-1,) + (1,) * idx.ndim)
    return jnp.sum(jnp.where(idx[None] == ids, table.reshape(ids.shape), 0), axis=0)


def _dispatch(route, counts_rows):
    n_tok = route.shape[1]
    n_assign = n_tok * TOP_K
    n_blocks = -(-(n_assign + N_EXPERTS * (MOE_ROWS - 1)) // MOE_ROWS)
    cap = n_blocks * MOE_ROWS
    counts = counts_rows[:, 0].astype(jnp.int32).reshape(PER_GROUP, N_EXPERT_GROUPS).T.reshape(N_EXPERTS)
    padded = (counts + MOE_ROWS - 1) // MOE_ROWS * MOE_ROWS
    pad_end = jnp.cumsum(padded)
    pad_start = pad_end - padded
    expert = route[0:TOP_K].astype(jnp.int32)
    pos = _lookup(pad_start, expert) + route[2 * TOP_K:3 * TOP_K].astype(jnp.int32)
    block_start = jnp.arange(n_blocks, dtype=jnp.int32) * MOE_ROWS
    block_expert = jnp.minimum(jnp.sum((pad_end[None, :] <= block_start[:, None]).astype(jnp.int32), axis=1),
                               N_EXPERTS - 1)
    n_used = (pad_end[-1:] // MOE_ROWS).astype(jnp.int32)
    gap = padded - counts
    gap_end = jnp.cumsum(gap)
    gap_first = jnp.concatenate([pad_start + counts - (gap_end - gap), pad_end[-1:] - gap_end[-1:]])
    k = jnp.arange(cap - n_assign, dtype=jnp.int32)
    owner = jnp.sum((gap_end[None, :] <= k[:, None]).astype(jnp.int32), axis=1)
    free_slot = _lookup(gap_first, owner) + k
    tok = jnp.broadcast_to(jnp.arange(n_tok, dtype=jnp.int32), (TOP_K, n_tok)).reshape(-1)
    keys = jnp.concatenate([pos.reshape(-1), free_slot])
    vals = jnp.concatenate([tok, k % n_tok])
    _, slot_tok = lax.sort((keys, vals), num_keys=1)
    return slot_tok, pos, block_expert, n_used


def _col(w, part):
    return w[:, _IN_OFF[part]:_IN_OFF[part + 1]]


def _proj_weights(w_in):
    zeros = jnp.zeros((D_MODEL, 128 - SSD_HEADS), w_in.dtype)
    dt = _col(w_in, _SDT)
    parts = [_col(w_in, p) for p in (_AQ, _GQ, _AK, _AV, _GK, _GV, _SX, _SB, _SC, _SZ, _SU)]
    parts += [dt[:, 0:SSD_HEADS], zeros, dt[:, SSD_HEADS:2 * SSD_HEADS], zeros]
    return jnp.concatenate(parts, axis=1).astype(BF16), _col(w_in, _GATE).astype(BF16)


def _rope_tables():
    rows = SEQ // GRID_W
    row = jnp.repeat(jnp.arange(rows, dtype=F32), GRID_W)
    col = jnp.tile(jnp.arange(GRID_W, dtype=F32), rows)
    axis_dim = HEAD_DIM // 2
    inv_freq = ROPE_THETA ** (-jnp.arange(0, axis_dim, 2, dtype=F32) / axis_dim)
    ang_r = row[:, None] * inv_freq
    ang_c = col[:, None] * inv_freq
    ang = jnp.concatenate([ang_r, ang_r, ang_c, ang_c], axis=-1)
    cos = jnp.concatenate([jnp.ones((CTX_LEN, HEAD_DIM), F32), jnp.cos(ang)], axis=0)
    sin = jnp.concatenate([jnp.zeros((CTX_LEN, HEAD_DIM), F32), jnp.sin(ang)], axis=0)
    return jnp.tile(cos, (1, 2)), jnp.tile(sin, (1, 2))


def _pad_lanes(v, width=128):
    return jnp.pad(v, ((0, 0), (0, width - v.shape[-1])))


def kernel(x, c, ctx, c_ctx, mod_w, mod_b, w_in, wa_sink, ga_q_norm, ga_k_norm, ssd_conv_w, ssd_conv_b, ssd_dt_bias, ssd_a_log, ssd_d, ssd_norm_w, s5_a_re, s5_a_im, s5_log_dt, s5_b_re, s5_b_im, s5_c_re, s5_c_im, s5_d, s5_w_glu, w_branch, w_out, ln1_g, ln1_b, ln2_g, ln2_b, router_w, router_bias, moe_w_gate, moe_w_up, moe_w_down):
    n_batch = x.shape[0]
    n_tok = n_batch * TOK
    xcat = jnp.concatenate([ctx, x], axis=1)
    mod_rows = -(-(n_batch + 1) // 8) * 8
    cond = jnp.zeros((mod_rows, D_MODEL), F32).at[:n_batch].set(c).at[n_batch].set(c_ctx)
    mods_all = _modulation(cond, mod_w, mod_b)
    cos, sin = _rope_tables()

    perm = np.array([g * PER_GROUP + j for j in range(PER_GROUP) for g in range(N_EXPERT_GROUPS)])
    rw_t = router_w.T[perm].astype(BF16)
    rb = router_bias.astype(F32)[perm].reshape(N_EXPERTS, 1)

    prev = None
    for layer in range(DEPTH):
        mods = mods_all[layer].reshape(mod_rows, 1, N_MOD * D_MODEL)
        w1, wg = _proj_weights(w_in[layer])
        if prev is None:
            attn, conv_in, sz, su, dt, u_g = _inproj(xcat, mods, w1)
        else:
            xcat, attn, conv_in, sz, su, dt, u_g = _inproj(None, mods, w1, prev)

        sink_rows = jnp.repeat(wa_sink[layer].astype(F32).reshape(N_KV, GRP) * LOG2E, QB,
                               axis=1).reshape(N_KV, 1, GRP * QB)
        qw = jnp.tile(ga_q_norm[layer].astype(F32), N_HEADS).reshape(1, 512)
        kw = jnp.tile(ga_k_norm[layer].astype(F32), N_KV).reshape(1, 128)
        ya, yg = _attention(attn, cos, sin, sink_rows, qw, kw)

        dt_bias = _pad_lanes(ssd_dt_bias[layer].astype(F32).reshape(2, SSD_HEADS)).reshape(2, 1, 128)
        a_log = _pad_lanes(ssd_a_log[layer].astype(F32)).reshape(2, 1, 128)
        d_exp = jnp.repeat(ssd_d[layer].astype(F32), SSD_P).reshape(1, SSD_WIDTH)
        ys = _ssd(conv_in, sz, dt, ssd_conv_w[layer].astype(F32), ssd_conv_b[layer].astype(F32).reshape(1, CONV_W),
                  dt_bias, a_log, d_exp, ssd_norm_w[layer].astype(F32).reshape(1, SSD_WIDTH))

        mats = _s5_matrices(s5_a_re[layer], s5_a_im[layer], s5_log_dt[layer], s5_b_re[layer], s5_b_im[layer],
                            s5_c_re[layer], s5_c_im[layer])
        y5 = _s5(u_g.reshape(S5_GROUPS, S5_NCH * n_batch, S5_ROWW), mats, n_batch)
        y5 = y5.reshape(S5_GROUPS, S5_NCH, n_batch * S5_ROWW)

        x1, h2, route, counts = _merge(
            xcat, mods, ya, ys, yg, y5, su, s5_d[layer].astype(F32).reshape(1, S5_WIDTH), wg,
            w_branch[layer].astype(BF16), s5_w_glu[layer].astype(BF16), w_out[layer].astype(BF16),
            ln1_g[layer].reshape(1, D_MODEL), ln1_b[layer].reshape(1, D_MODEL), rw_t, rb)

        slot_tok, pos, block_expert, n_used = _dispatch(route, counts)
        xs = h2.reshape(n_tok, D_MODEL)[slot_tok]
        y_slots = _moe_experts(block_expert, n_used, xs, moe_w_gate, moe_w_up, moe_w_down, layer)
        f = y_slots[pos.reshape(-1)].reshape(TOP_K, n_batch, TOK, D_MODEL)
        prev = (x1, mods, f, route, ln2_g[layer].reshape(1, D_MODEL), ln2_b[layer].reshape(1, D_MODEL))
    return _final(*prev)
```

```python
import functools
import math

import jax
import jax.numpy as jnp
import numpy as np
from jax import lax
from jax.experimental import pallas as pl
from jax.experimental.pallas import tpu as pltpu

F32 = jnp.float32
BF16 = jnp.bfloat16

D_MODEL = 1024
DEPTH = 4
GRID_W = 64
CTX_LEN = 256
SEQ = 2048
TOK = CTX_LEN + SEQ
N_MOD = 6
HEAD_DIM = 64
N_BRANCH = 4
N_HEADS = 8
N_KV = 2
GRP = N_HEADS // N_KV
WINDOW = 128
QB = 128
ROPE_THETA = 10000.0

SSD_HEADS = 8
SSD_P = 64
SSD_GROUPS = 2
SSD_N = 128
SSD_T = 128
SSD_WIDTH = SSD_HEADS * SSD_P
SSD_CONV_CH = SSD_WIDTH + 2 * SSD_GROUPS * SSD_N
N_CHUNK = TOK // SSD_T

S5_GROUP = 16
S5_WIDTH = 512
S5_GROUPS = S5_WIDTH // S5_GROUP
S5_STATE = 64
S5_T = 16
S5_ROWW = S5_T * S5_GROUP
S5_NCH = TOK // S5_T
S5_CTX_CH = CTX_LEN // S5_T

N_EXPERTS = 32
N_EXPERT_GROUPS = 8
PER_GROUP = N_EXPERTS // N_EXPERT_GROUPS
TOP_K = 2
D_EXPERT = 512
MOE_ROWS = 256

ALPHA = (2 * DEPTH) ** 0.25
NORM_EPS = 1e-6

TM = 256
VMEM_LIMIT = 56 * 1024 * 1024

_IN_SIZES = (512, 128, 128, 512, 128, 128, 512, 512, 256, 256, 16, 512, 4096)
_IN_OFF = np.concatenate([[0], np.cumsum(_IN_SIZES)]).astype(int)
(_AQ, _AK, _AV, _GQ, _GK, _GV, _SX, _SZ, _SB, _SC, _SDT, _SU, _GATE) = range(13)

ATTN_W = 1536
CONV_W = 1024
DT_W = 256
PROJ_W = ATTN_W + CONV_W + 512 + 512 + DT_W


def _params(sem=None):
    return pltpu.CompilerParams(dimension_semantics=sem, vmem_limit_bytes=VMEM_LIMIT)


def _ln(x):
    mu = jnp.mean(x, axis=-1, keepdims=True)
    xc = x - mu
    var = jnp.mean(xc * xc, axis=-1, keepdims=True)
    return xc * lax.rsqrt(var + NORM_EPS)


def _sigmoid(x):
    return 0.5 * jnp.tanh(0.5 * x) + 0.5


def _silu(x):
    return x * _sigmoid(x)


def _bdot(a, b):
    return jnp.dot(a.astype(BF16), b.astype(BF16), preferred_element_type=F32)


def _bdot_nt(a, b):
    return lax.dot_general(a.astype(BF16), b.astype(BF16), (((1,), (1,)), ((), ())),
                           preferred_element_type=F32)


def _split3(v):
    hi = v.astype(BF16)
    r1 = v - hi.astype(F32)
    mid = r1.astype(BF16)
    lo = (r1 - mid.astype(F32)).astype(BF16)
    return hi, mid, lo


def _mod_kernel(c_ref, w_ref, b_ref, o_ref):
    c = c_ref[...]
    o_ref[...] = _bdot(_silu(c), w_ref[...]) + b_ref[...]


def _modulation(cond, mod_w, mod_b):
    n_layer = mod_w.shape[0]
    rows = cond.shape[0]
    return pl.pallas_call(
        _mod_kernel,
        out_shape=jax.ShapeDtypeStruct((n_layer, rows, N_MOD * D_MODEL), F32),
        grid=(n_layer, N_MOD),
        in_specs=[
            pl.BlockSpec((rows, D_MODEL), lambda l, n: (0, 0)),
            pl.BlockSpec((None, D_MODEL, D_MODEL), lambda l, n: (l, 0, n)),
            pl.BlockSpec((None, 1, D_MODEL), lambda l, n: (l, 0, n)),
        ],
        out_specs=pl.BlockSpec((None, rows, D_MODEL), lambda l, n: (l, 0, n)),
        compiler_params=_params(("arbitrary", "arbitrary")),
        name="modulation",
    )(cond, mod_w, mod_b.reshape(n_layer, 1, N_MOD * D_MODEL))


def _mod_part(mod_ref, k):
    return mod_ref[:, k * D_MODEL:(k + 1) * D_MODEL]


def _mod_row(n_batch):
    return lambda b, j: (jnp.where(j == 0, n_batch, b), 0, 0)


def _chunk_swap():
    r = np.arange(TM)
    p = np.zeros((TM, TM), np.float32)
    p[r, (r % S5_T) * (TM // S5_T) + r // S5_T] = 1.0
    return jnp.asarray(p, BF16)


def _ffn_residual(x1, mod_ref, f0_ref, f1_ref, route_ref, lng_ref, lnb_ref):
    r = lax.broadcasted_iota(jnp.int32, (TM, TM), 0)
    c = lax.broadcasted_iota(jnp.int32, (TM, TM), 1)
    w = [jnp.sum(jnp.where(r == c, route_ref[TOP_K + k:TOP_K + k + 1, :], 0.0), axis=1, keepdims=True)
         for k in range(TOP_K)]
    ffn = w[0] * f0_ref[...].astype(F32) + w[1] * f1_ref[...].astype(F32)
    return _ln(ALPHA * x1 + _mod_part(mod_ref, 5) * ffn) * lng_ref[...] + lnb_ref[...]


def _inproj_kernel(*refs, fused):
    if fused:
        (x1_ref, modp_ref, f0_ref, f1_ref, route_ref, lng_ref, lnb_ref, mod_ref, w_ref, swap_ref,
         x_out_ref, attn_ref, conv_ref, sz_ref, su_ref, dt_ref, u_ref, sub_ref) = refs
    else:
        x_ref, mod_ref, w_ref, swap_ref, attn_ref, conv_ref, sz_ref, su_ref, dt_ref, u_ref, sub_ref = refs

    @pl.when(pl.program_id(0) == 0)
    def _reset():
        sub_ref[...] = jnp.zeros_like(sub_ref)

    by_step = jnp.dot(swap_ref[...], sub_ref[...], preferred_element_type=F32)
    n_ch = TM // S5_T
    for t in range(S5_T):
        rows = by_step[t * n_ch:(t + 1) * n_ch]
        for g in range(S5_GROUPS):
            u_ref[g, :, t * S5_GROUP:(t + 1) * S5_GROUP] = rows[:, g * S5_GROUP:(g + 1) * S5_GROUP].astype(BF16)

    if fused:
        x = _ffn_residual(x1_ref[...], modp_ref, f0_ref, f1_ref, route_ref, lng_ref, lnb_ref)
        x_out_ref[...] = x
    else:
        x = x_ref[...]
    h = _ln(x) * (1.0 + _mod_part(mod_ref, 1)) + _mod_part(mod_ref, 0)
    hb = h.astype(BF16)
    off = 0
    for ref in (attn_ref, conv_ref, sz_ref, su_ref, dt_ref):
        width = ref.shape[-1]
        ref[...] = jnp.dot(hb, w_ref[:, off:off + width], preferred_element_type=F32)
        off += width
    sub_ref[...] = su_ref[...].astype(BF16)


def _inproj(xcat, mods, w1, prev=None):
    n_batch = xcat.shape[0] if prev is None else prev[0].shape[0]
    n_tiles = TOK // TM
    n_all = n_batch * n_tiles
    widths = (ATTN_W, CONV_W, 512, 512, DT_W)

    def at(lag):
        def split(s):
            t = jnp.clip(s - lag, 0, n_all - 1)
            return t // n_tiles, t % n_tiles
        return split

    tile = lambda w: pl.BlockSpec((None, TM, w), lambda s: at(0)(s) + (0,))

    def mod_index(s):
        b, j = at(0)(s)
        return jnp.where(j == 0, n_batch, b), 0, 0

    mod_spec = pl.BlockSpec((None, 1, N_MOD * D_MODEL), mod_index)
    const = lambda shape: pl.BlockSpec(shape, lambda s: tuple(0 for _ in shape))
    out_shape = ([jax.ShapeDtypeStruct((n_batch, TOK, w), F32) for w in widths]
                 + [jax.ShapeDtypeStruct((S5_GROUPS, S5_NCH, n_batch * S5_ROWW), BF16)])
    out_specs = ([tile(w) for w in widths]
                 + [pl.BlockSpec((S5_GROUPS, TM // S5_T, S5_ROWW), lambda s: (0,) + at(1)(s)[::-1])])
    own = [mod_spec, const((D_MODEL, PROJ_W)), const((TM, TM))]
    if prev is None:
        in_specs, args = [tile(D_MODEL)] + own, (xcat, mods, w1, _chunk_swap())
    else:
        x1, mods_prev, f, route, ln_g, ln_b = prev
        in_specs = [tile(D_MODEL), mod_spec,
                    pl.BlockSpec((None, None, TM, D_MODEL), lambda s: (0,) + at(0)(s) + (0,)),
                    pl.BlockSpec((None, None, TM, D_MODEL), lambda s: (1,) + at(0)(s) + (0,)),
                    pl.BlockSpec((8, TM), lambda s: (0, jnp.clip(s, 0, n_all - 1))),
                    const((1, D_MODEL)), const((1, D_MODEL))] + own
        args = (x1, mods_prev, f, f, route, ln_g, ln_b, mods, w1, _chunk_swap())
        out_shape = [jax.ShapeDtypeStruct((n_batch, TOK, D_MODEL), F32)] + out_shape
        out_specs = [tile(D_MODEL)] + out_specs
    return pl.pallas_call(
        functools.partial(_inproj_kernel, fused=prev is not None),
        out_shape=out_shape,
        grid=(n_all + 1,),
        in_specs=in_specs,
        out_specs=out_specs,
        scratch_shapes=[pltpu.VMEM((TM, S5_WIDTH), BF16)],
        compiler_params=_params(("arbitrary",)),
        name="inproj",
    )(*args)


def _rope(t, cos, sin):
    width = t.shape[-1]
    lane = lax.broadcasted_iota(jnp.int32, t.shape, t.ndim - 1)
    rot = jnp.where(lane % 32 < 16, -pltpu.roll(t, width - 16, t.ndim - 1), pltpu.roll(t, 16, t.ndim - 1))
    return t * cos + rot * sin


def _tile4(v):
    return jnp.concatenate([v, v, v, v], axis=1)


def _head_inv_rms(t, n_heads):
    t2 = t * t
    lane = lax.broadcasted_iota(jnp.int32, t.shape, 1)
    inv = jnp.zeros_like(t)
    for h in range(n_heads):
        ms = jnp.sum(t2[:, h * HEAD_DIM:(h + 1) * HEAD_DIM], axis=1, keepdims=True) * (1.0 / HEAD_DIM)
        inv = jnp.where(lane // HEAD_DIM == h, lax.rsqrt(ms + NORM_EPS), inv)
    return inv


def _stack_heads(qr, kv):
    parts = [qr[:, (kv * GRP + g) * HEAD_DIM:(kv * GRP + g + 1) * HEAD_DIM] for g in range(GRP)]
    return jnp.concatenate(parts, axis=0).astype(BF16)


QPS = 2
LOG2E = math.log2(math.e)


KEY_BLOCK = 128


def _attend(jobs, sink_ref, out_ref, kb_ref, vt_ref, s_ref, p_ref):
    width_q = GRP * QB
    halves = [slice(0, width_q // 2), slice(width_q // 2, width_q)]
    sub = KEY_BLOCK // 8
    for u, (qr, parts) in enumerate(jobs):
        for kv in range(N_KV):
            qs = _stack_heads(qr, kv)
            row = 0
            for key_idx, width, bias in parts:
                for hc in halves:
                    s = _bdot_nt(kb_ref[kv, key_idx, :], qs[hc])
                    if bias is not None:
                        s = s + jnp.concatenate([bias] * (width_q // 2 // QB), axis=1)
                    s_ref[u, kv, row:row + width, hc] = s
                row += width
    for u, (_, parts) in enumerate(jobs):
        n_keys = sum(width for _, width, _ in parts)
        sink_term = {}
        for kv in range(N_KV):
            for g in range(GRP):
                cols = slice(g * QB, (g + 1) * QB)
                mx = None
                for r0 in range(0, n_keys, KEY_BLOCK):
                    blk = jnp.max(s_ref[u, kv, r0:r0 + KEY_BLOCK, cols].reshape(sub, 8, QB), axis=0)
                    mx = blk if mx is None else jnp.maximum(mx, blk)
                m = jnp.max(mx, axis=0, keepdims=True)
                if sink_ref is not None:
                    m = jnp.maximum(m, sink_ref[kv, :, cols])
                    sink_term[kv, g] = jnp.exp2(sink_ref[kv, :, cols] - m)
                m_b = jnp.broadcast_to(m, (KEY_BLOCK, QB))
                for r0 in range(0, n_keys, KEY_BLOCK):
                    p_ref[u, kv, r0:r0 + KEY_BLOCK, cols] = jnp.exp2(
                        s_ref[u, kv, r0:r0 + KEY_BLOCK, cols] - m_b).astype(BF16)
        o_t = {}
        for kv in range(N_KV):
            for hi, hc in enumerate(halves):
                o = None
                row = 0
                for key_idx, width, _ in parts:
                    part = jnp.dot(vt_ref[kv, :, key_idx], p_ref[u, kv, row:row + width, hc],
                                   preferred_element_type=F32)
                    o = part if o is None else o + part
                    row += width
                o_t[kv, hi] = o
        for g in range(GRP):
            hi, gl = divmod(g, GRP // 2)
            cols = slice(gl * QB, (gl + 1) * QB)
            scaled = []
            for kv in range(N_KV):
                den = o_t[kv, hi][HEAD_DIM:HEAD_DIM + 1, cols]
                if sink_ref is not None:
                    den = den + sink_term[kv, g]
                scaled.append(o_t[kv, hi][0:HEAD_DIM, cols] / den)
            pair = jnp.concatenate(scaled, axis=0).T
            for kv in range(N_KV):
                h = kv * GRP + g
                out_ref[u * QB:(u + 1) * QB, h * HEAD_DIM:(h + 1) * HEAD_DIM] = (
                    pair[:, kv * HEAD_DIM:(kv + 1) * HEAD_DIM].astype(out_ref.dtype))


def _wattn_kernel(q_ref, k_ref, v_ref, cos_ref, sin_ref, sink_ref, bias_ref, out_ref, kb_ref, vt_ref, s_ref, p_ref):
    i = pl.program_id(1)

    @pl.when(i == 0)
    def _prep():
        kr = _rope(k_ref[...], cos_ref[...], sin_ref[...])
        v_t = v_ref[...].T
        for kv in range(N_KV):
            kb_ref[kv] = kr[:, kv * HEAD_DIM:(kv + 1) * HEAD_DIM].astype(BF16)
            vt_ref[kv, 0:HEAD_DIM, :] = v_t[kv * HEAD_DIM:(kv + 1) * HEAD_DIM, :].astype(BF16)
            vt_ref[kv, HEAD_DIM:HEAD_DIM + 16, :] = jnp.ones((16, TOK), BF16)

    row0 = pl.multiple_of(i * (QPS * QB), QPS * QB)
    cos = _tile4(cos_ref[pl.ds(row0, QPS * QB), :])
    sin = _tile4(sin_ref[pl.ds(row0, QPS * QB), :])
    qr = _rope(q_ref[...], cos, sin) * (HEAD_DIM ** -0.5 * LOG2E)
    qrs = [qr[u * QB:(u + 1) * QB] for u in range(QPS)]

    ctx_keys = (slice(0, CTX_LEN), CTX_LEN, None)

    @pl.when(i < CTX_LEN // (QPS * QB))
    def _ctx_queries():
        _attend([(q, [ctx_keys]) for q in qrs], sink_ref, out_ref, kb_ref, vt_ref, s_ref, p_ref)

    @pl.when(i >= CTX_LEN // (QPS * QB))
    def _latent_queries():
        jobs = []
        for u in range(QPS):
            j = i * QPS + u - CTX_LEN // QB
            band = jnp.clip(j - 1, 0, SEQ // QB - 3)
            start = pl.multiple_of(CTX_LEN + band * QB, QB)
            bias = bias_ref[j - band]
            jobs.append((qrs[u], [ctx_keys, (pl.ds(start, 3 * QB), 3 * QB, bias)]))
        _attend(jobs, sink_ref, out_ref, kb_ref, vt_ref, s_ref, p_ref)


def _gattn_kernel(q_ref, k_ref, v_ref, cos_ref, sin_ref, qw_ref, kw_ref, out_ref, kb_ref, vt_ref, s_ref, p_ref):
    i = pl.program_id(1)

    @pl.when(i == 0)
    def _prep():
        k = k_ref[...]
        kn = k * _head_inv_rms(k, N_KV) * kw_ref[...]
        kr = _rope(kn, cos_ref[...], sin_ref[...])
        v_t = v_ref[...].T
        for kv in range(N_KV):
            kb_ref[kv] = kr[:, kv * HEAD_DIM:(kv + 1) * HEAD_DIM].astype(BF16)
            vt_ref[kv, 0:HEAD_DIM, :] = v_t[kv * HEAD_DIM:(kv + 1) * HEAD_DIM, :].astype(BF16)
            vt_ref[kv, HEAD_DIM:HEAD_DIM + 16, :] = jnp.ones((16, TOK), BF16)

    row0 = pl.multiple_of(i * (QPS * QB), QPS * QB)
    cos = _tile4(cos_ref[pl.ds(row0, QPS * QB), :])
    sin = _tile4(sin_ref[pl.ds(row0, QPS * QB), :])
    q = q_ref[...]
    qn = q * _head_inv_rms(q, N_HEADS) * qw_ref[...]
    qr = _rope(qn, cos, sin) * (HEAD_DIM ** -0.5 * LOG2E)
    qrs = [qr[u * QB:(u + 1) * QB] for u in range(QPS)]

    @pl.when(i < CTX_LEN // (QPS * QB))
    def _ctx_queries():
        _attend([(q_u, [(slice(0, CTX_LEN), CTX_LEN, None)]) for q_u in qrs], None, out_ref, kb_ref, vt_ref,
                s_ref, p_ref)

    @pl.when(i >= CTX_LEN // (QPS * QB))
    def _latent_queries():
        _attend([(q_u, [(slice(0, TOK), TOK, None)]) for q_u in qrs], None, out_ref, kb_ref, vt_ref, s_ref, p_ref)


def _window_bias():
    q_pos = np.arange(QB)[None, None, :] + QB * np.arange(3)[:, None, None]
    k_pos = np.arange(3 * QB)[None, :, None]
    return jnp.asarray(np.where(np.abs(q_pos - k_pos) <= WINDOW, 0.0, -np.inf), F32)


def _attention(attn, cos, sin, sink_rows, qw, kw):
    n_batch = attn.shape[0]
    grid = (n_batch, TOK // (QPS * QB))
    q_spec = lambda blk: pl.BlockSpec((None, QPS * QB, 512), lambda b, i: (b, i, blk))
    kv_spec = lambda blk: pl.BlockSpec((None, TOK, 128), lambda b, i: (b, 0, blk))
    tab_spec = pl.BlockSpec((TOK, 128), lambda b, i: (0, 0))
    out_spec = pl.BlockSpec((None, QPS * QB, 512), lambda b, i: (b, i, 0))
    out_shape = jax.ShapeDtypeStruct((n_batch, TOK, 512), BF16)
    kv_scratch = [pltpu.VMEM((N_KV, TOK, HEAD_DIM), BF16), pltpu.VMEM((N_KV, HEAD_DIM + 16, TOK), BF16)]
    w_keys = CTX_LEN + 3 * QB
    ya = pl.pallas_call(
        _wattn_kernel, out_shape=out_shape, grid=grid,
        in_specs=[q_spec(0), kv_spec(8), kv_spec(9), tab_spec, tab_spec,
                  pl.BlockSpec((N_KV, 1, GRP * QB), lambda b, i: (0, 0, 0)),
                  pl.BlockSpec((3, 3 * QB, QB), lambda b, i: (0, 0, 0))],
        out_specs=out_spec,
        scratch_shapes=kv_scratch + [pltpu.VMEM((QPS, N_KV, w_keys, GRP * QB), F32),
                                     pltpu.VMEM((QPS, N_KV, w_keys, GRP * QB), BF16)],
        compiler_params=_params(("arbitrary", "arbitrary")), name="window_attention",
    )(attn, attn, attn, cos, sin, sink_rows, _window_bias())
    yg = pl.pallas_call(
        _gattn_kernel, out_shape=out_shape, grid=grid,
        in_specs=[q_spec(1), kv_spec(10), kv_spec(11), tab_spec, tab_spec,
                  pl.BlockSpec((1, 512), lambda b, i: (0, 0)),
                  pl.BlockSpec((1, 128), lambda b, i: (0, 0))],
        out_specs=out_spec,
        scratch_shapes=kv_scratch + [pltpu.VMEM((QPS, N_KV, TOK, GRP * QB), F32),
                                     pltpu.VMEM((QPS, N_KV, TOK, GRP * QB), BF16)],
        compiler_params=_params(("arbitrary", "arbitrary")), name="global_attention",
    )(attn, attn, attn, cos, sin, qw, kw)
    return ya, yg


def _ssd_chunk(s):
    r = s - N_CHUNK
    back = jnp.where(r == 0, 1, jnp.where(r == 1, 0, N_CHUNK + 1 - r))
    return jnp.where(s < N_CHUNK, s, back)


SSD_NB = 4


def _ssd_kernel(x_ref, prev_ref, next_ref, z_ref, dt_ref, cw_ref, cb_ref, dtb_ref, a_ref, d_ref, nw_ref, e_ref,
                out_ref, yf_ref, state_ref, y_ref):
    s = pl.program_id(1)
    chunk = _ssd_chunk(s)
    backward = s >= N_CHUNK

    @pl.when((s == 0) | (s == N_CHUNK))
    def _reset():
        state_ref[...] = jnp.zeros_like(state_ref)

    first = (chunk == 0) | (chunk == CTX_LEN // SSD_T)
    last = (chunk == CTX_LEN // SSD_T - 1) | (chunk == N_CHUNK - 1)
    row = lax.broadcasted_iota(jnp.int32, (SSD_T, CONV_W), 0)
    tr = lax.broadcasted_iota(jnp.int32, (SSD_T, SSD_T), 0)
    tc = lax.broadcasted_iota(jnp.int32, (SSD_T, SSD_T), 1)
    causal = jnp.where(backward, tc - tr, tr - tc) >= 0
    tri = jnp.where(causal, 1.0, 0.0).astype(BF16)
    neg_a = -jnp.exp(a_ref[...])
    expand = e_ref[...]

    def widen(v):
        return jnp.dot(jnp.concatenate(_split3(v), axis=1), expand, preferred_element_type=F32)

    rows = pl.ds(pl.multiple_of(chunk * SSD_T, SSD_T), SSD_T)

    xs = []
    for u in range(SSD_NB):
        xin = x_ref[u]
        prev = jnp.where(first, 0.0, prev_ref[u, 7:8, :])
        nxt = jnp.where(last, 0.0, next_ref[u, 0:1, :])
        xm1 = jnp.where(row == 0, prev, pltpu.roll(xin, 1, 0))
        xp1 = jnp.where(row == SSD_T - 1, nxt, pltpu.roll(xin, SSD_T - 1, 0))
        conv = cw_ref[0:1, :] * xm1 + cw_ref[1:2, :] * xin + cw_ref[2:3, :] * xp1 + cb_ref[...]
        xbc = _silu(conv)
        x = xbc[:, 0:SSD_WIDTH]
        xs.append(x)
        bmat = [xbc[:, SSD_WIDTH + g * SSD_N:SSD_WIDTH + (g + 1) * SSD_N] for g in range(SSD_GROUPS)]
        cmat = [xbc[:, SSD_WIDTH + (SSD_GROUPS + g) * SSD_N:SSD_WIDTH + (SSD_GROUPS + g + 1) * SSD_N]
                for g in range(SSD_GROUPS)]

        dtv = dt_ref[u] + dtb_ref[...]
        dt = jnp.maximum(dtv, 0.0) + jnp.log(1.0 + jnp.exp(-jnp.abs(dtv)))
        hi, mid, lo = _split3(dt * neg_a)
        cs = (jnp.dot(tri, hi, preferred_element_type=F32) + jnp.dot(tri, mid, preferred_element_type=F32)
              + jnp.dot(tri, lo, preferred_element_type=F32))
        cs_t = cs.T

        dt_x = widen(dt)
        cs_x = widen(cs)
        tot_x = jnp.where(backward, cs_x[0:1, :], cs_x[SSD_T - 1:SSD_T, :])
        ecs_x = jnp.exp(cs_x)
        etot_x = jnp.exp(tot_x)
        xd = x * dt_x
        xd_b = xd.astype(BF16)
        xd_end = (xd * jnp.exp(tot_x - cs_x)).astype(BF16)

        gw = SSD_WIDTH // SSD_GROUPS
        for g in range(SSD_GROUPS):
            glanes = slice(g * gw, (g + 1) * gw)
            cb = _bdot_nt(cmat[g], bmat[g])
            st = state_ref[u, g]
            y_off = ecs_x[:, glanes] * jnp.dot(cmat[g].astype(BF16), st.astype(BF16), preferred_element_type=F32)
            state_ref[u, g] = etot_x[:, glanes] * st + jnp.dot(bmat[g].T.astype(BF16), xd_end[:, glanes],
                                                               preferred_element_type=F32)
            for j in range(SSD_HEADS // SSD_GROUPS):
                h = g * (SSD_HEADS // SSD_GROUPS) + j
                lanes = slice(h * SSD_P, (h + 1) * SSD_P)
                seg = jnp.exp(jnp.where(causal, cs[:, h:h + 1] - cs_t[h:h + 1, :], -jnp.inf))
                y_ref[u, :, lanes] = (jnp.dot((cb * seg).astype(BF16), xd_b[:, lanes], preferred_element_type=F32)
                                      + y_off[:, j * SSD_P:(j + 1) * SSD_P])

    @pl.when(jnp.logical_not(backward))
    def _keep():
        for u in range(SSD_NB):
            yf_ref[u, rows, :] = y_ref[u]

    @pl.when(backward)
    def _finish():
        for u in range(SSD_NB):
            ytot = yf_ref[u, rows, :] + y_ref[u] + d_ref[...] * xs[u]
            gated = ytot * _silu(z_ref[u])
            ms = jnp.mean(gated * gated, axis=1, keepdims=True)
            out_ref[u] = (gated * lax.rsqrt(ms + NORM_EPS) * nw_ref[...]).astype(out_ref.dtype)


def _ssd(conv_in, sz, dt, conv_w, conv_b, dt_bias, a_log, d_exp, norm_w):
    n_batch = conv_in.shape[0]
    assert n_batch % SSD_NB == 0
    halo = SSD_T // 8
    n_halo = TOK // 8

    def chunk_map(b, s):
        return (b, _ssd_chunk(s), 0)

    def out_map(b, s):
        return (b, jnp.where(s < N_CHUNK, 1, _ssd_chunk(s)), 0)

    const = lambda shape: pl.BlockSpec(shape, lambda b, s: tuple(0 for _ in shape))
    dir_spec = pl.BlockSpec((None, 1, 128), lambda b, s: (s // N_CHUNK, 0, 0))
    expand = jnp.asarray(np.tile(np.repeat(np.eye(128, SSD_HEADS), SSD_P, axis=1), (3, 1)), BF16)
    return pl.pallas_call(
        _ssd_kernel,
        out_shape=jax.ShapeDtypeStruct((n_batch, TOK, SSD_WIDTH), BF16),
        grid=(n_batch // SSD_NB, 2 * N_CHUNK),
        in_specs=[
            pl.BlockSpec((SSD_NB, SSD_T, CONV_W), chunk_map),
            pl.BlockSpec((SSD_NB, 8, CONV_W), lambda b, s: (b, jnp.maximum(_ssd_chunk(s) * halo - 1, 0), 0)),
            pl.BlockSpec((SSD_NB, 8, CONV_W), lambda b, s: (b, jnp.minimum((_ssd_chunk(s) + 1) * halo, n_halo - 1), 0)),
            pl.BlockSpec((SSD_NB, SSD_T, SSD_WIDTH), chunk_map),
            pl.BlockSpec((SSD_NB, SSD_T, 128), lambda b, s: (b, _ssd_chunk(s), s // N_CHUNK)),
            const((3, CONV_W)), const((1, CONV_W)), dir_spec, dir_spec,
            const((1, SSD_WIDTH)), const((1, SSD_WIDTH)), const((3 * 128, SSD_WIDTH)),
        ],
        out_specs=pl.BlockSpec((SSD_NB, SSD_T, SSD_WIDTH), out_map),
        scratch_shapes=[pltpu.VMEM((SSD_NB, TOK, SSD_WIDTH), F32),
                        pltpu.VMEM((SSD_NB, SSD_GROUPS, SSD_N, SSD_WIDTH // SSD_GROUPS), F32),
                        pltpu.VMEM((SSD_NB, SSD_T, SSD_WIDTH), F32)],
        compiler_params=_params(("arbitrary", "arbitrary")),
        name="ssd",
    )(conv_in, conv_in, conv_in, sz, dt, conv_w, conv_b, dt_bias, a_log, d_exp, norm_w, expand)


S5_SW = 4 * 128


def _s5_kernel(u_ref, m_ref, h_ref, g_ref, a_ref, y_ref, loc_ref, prev_ref, *, n_batch):
    u = u_ref[...]
    loc_ref[...] = jnp.dot(u, h_ref[...], preferred_element_type=F32)
    a = [a_ref[k:k + 1, :] for k in range(4)]

    def step(k, carry):
        c_f = k
        c_b = jnp.where(k < S5_CTX_CH, S5_CTX_CH - 1 - k, S5_NCH + S5_CTX_CH - 1 - k)
        new = []
        for d, c in enumerate((c_f, c_b)):
            rows = pl.ds(pl.multiple_of(c * n_batch, n_batch), n_batch)
            re_l, im_l = slice(2 * d * 128, (2 * d + 1) * 128), slice((2 * d + 1) * 128, (2 * d + 2) * 128)
            s_re, s_im = carry[2 * d], carry[2 * d + 1]
            prev_ref[rows, re_l] = s_re
            prev_ref[rows, im_l] = s_im
            a_re, a_im = a[2 * d], a[2 * d + 1]
            new += [a_re * s_re - a_im * s_im + loc_ref[rows, re_l], a_re * s_im + a_im * s_re + loc_ref[rows, im_l]]
        return tuple(new)

    zero = jnp.zeros((n_batch, 128), F32)
    lax.fori_loop(0, S5_NCH, step, (zero,) * 4, unroll=2)
    y = jnp.dot(u, m_ref[...], preferred_element_type=F32)
    y = y + jnp.dot(prev_ref[...].astype(BF16), g_ref[...], preferred_element_type=F32)
    y_ref[...] = y.astype(y_ref.dtype)


def _s5(u_g, mats, n_batch):
    m_all, h_all, g_all, a16 = mats
    rows = u_g.shape[1]
    grp = lambda shape: pl.BlockSpec((None,) + shape, lambda g: (g,) + tuple(0 for _ in shape))
    return pl.pallas_call(
        functools.partial(_s5_kernel, n_batch=n_batch),
        out_shape=jax.ShapeDtypeStruct((S5_GROUPS, rows, S5_ROWW), BF16),
        grid=(S5_GROUPS,),
        in_specs=[grp((rows, S5_ROWW)), grp((S5_ROWW, S5_ROWW)), grp((S5_ROWW, S5_SW)), grp((S5_SW, S5_ROWW)),
                  grp((4, 128))],
        out_specs=grp((rows, S5_ROWW)),
        scratch_shapes=[pltpu.VMEM((rows, S5_SW), F32), pltpu.VMEM((rows, S5_SW), F32)],
        compiler_params=_params(("arbitrary",)),
        name="s5",
    )(u_g, m_all, h_all, g_all, a16)


def _cmul(a, b):
    return a[0] * b[0] - a[1] * b[1], a[0] * b[1] + a[1] * b[0]


def _s5_matrices(a_re, a_im, log_dt, b_re, b_im, c_re, c_im):
    hp = lax.Precision.HIGHEST
    t = jnp.arange(S5_T + 1, dtype=F32)
    m_sum = 0.0
    h_all, g_all, a16_all = [], [], []
    c = (c_re.astype(F32), c_im.astype(F32))
    for direction in range(2):
        are = jnp.minimum(a_re[direction].astype(F32), -1e-4)
        aim = a_im[direction].astype(F32)
        dt = jnp.exp(log_dt[direction].astype(F32))[:, None]
        mag = jnp.exp(t[:, None, None] * (are * dt)[None])
        ang = t[:, None, None] * (aim * dt)[None]
        pw = (mag * jnp.cos(ang), mag * jnp.sin(ang))
        num = (pw[0][1] - 1.0, pw[1][1])
        den = are * are + aim * aim
        coef = ((num[0] * are + num[1] * aim) / den, (num[1] * are - num[0] * aim) / den)
        bbar = _cmul((coef[0][..., None], coef[1][..., None]), (b_re.astype(F32), b_im.astype(F32)))
        pb = _cmul((pw[0][:S5_T, :, :, None], pw[1][:S5_T, :, :, None]), (bbar[0][None], bbar[1][None]))
        taps = (jnp.einsum('gon,tgni->tgoi', c[0], pb[0], precision=hp)
                - jnp.einsum('gon,tgni->tgoi', c[1], pb[1], precision=hp))
        ti = jnp.arange(S5_T)
        lag = (ti[None, :] - ti[:, None]) if direction == 0 else (ti[:, None] - ti[None, :])
        pick = (lag[:, :, None] == jnp.arange(S5_T)[None, None, :]).astype(F32)
        k_full = jnp.einsum('abt,tgoi->abgoi', pick, taps, precision=hp)
        m_sum = m_sum + k_full.transpose(2, 0, 4, 1, 3).reshape(S5_GROUPS, S5_ROWW, S5_ROWW)
        e_in = (S5_T - 1 - ti) if direction == 0 else ti
        hb = _cmul((pw[0][e_in][..., None], pw[1][e_in][..., None]), (bbar[0][None], bbar[1][None]))
        zpad = jnp.zeros_like(hb[0])
        h_mat = jnp.concatenate([hb[0], zpad, hb[1], zpad], axis=2)
        h_all.append(h_mat.transpose(1, 0, 3, 2).reshape(S5_GROUPS, S5_ROWW, 4 * S5_STATE))
        e_out = (ti + 1) if direction == 0 else (S5_T - ti)
        cp = _cmul((c[0][None], c[1][None]),
                   (pw[0][e_out][:, :, None, :], pw[1][e_out][:, :, None, :]))
        zpad = jnp.zeros_like(cp[0])
        g_mat = jnp.concatenate([cp[0], zpad, -cp[1], zpad], axis=3)
        g_all.append(g_mat.transpose(1, 3, 0, 2).reshape(S5_GROUPS, 4 * S5_STATE, S5_ROWW))
        zrow = jnp.zeros_like(pw[0][S5_T])
        a16_all += [jnp.concatenate([pw[0][S5_T], zrow], axis=1), jnp.concatenate([pw[1][S5_T], zrow], axis=1)]
    return (m_sum.astype(BF16), jnp.concatenate(h_all, axis=2).astype(BF16),
            jnp.concatenate(g_all, axis=1).astype(BF16), jnp.stack(a16_all, axis=1))


def _route(logits_t, bias, base, before):
    scores = [_sigmoid(logits_t[j * N_EXPERT_GROUPS:(j + 1) * N_EXPERT_GROUPS]) for j in range(PER_GROUP)]
    sel = [scores[j] + bias[j * N_EXPERT_GROUPS:(j + 1) * N_EXPERT_GROUPS] for j in range(PER_GROUP)]
    hi1, lo1 = jnp.maximum(sel[0], sel[1]), jnp.minimum(sel[0], sel[1])
    hi2, lo2 = jnp.maximum(sel[2], sel[3]), jnp.minimum(sel[2], sel[3])
    group_score = jnp.maximum(hi1, hi2) + jnp.maximum(jnp.minimum(hi1, hi2), jnp.maximum(lo1, lo2))
    gid = lax.broadcasted_iota(jnp.int32, group_score.shape, 0)
    best = jnp.max(group_score, axis=0, keepdims=True)
    grp = jnp.min(jnp.where(group_score == best, gid, N_EXPERT_GROUPS), axis=0, keepdims=True)
    pick = gid == grp
    v = [jnp.sum(jnp.where(pick, sel[j], 0.0), axis=0, keepdims=True) for j in range(PER_GROUP)]
    sc = [jnp.sum(jnp.where(pick, scores[j], 0.0), axis=0, keepdims=True) for j in range(PER_GROUP)]
    rank = []
    for j in range(PER_GROUP):
        r = jnp.zeros_like(grp)
        for i in range(PER_GROUP):
            if i < j:
                r = r + jnp.where(v[i] >= v[j], 1, 0)
            elif i > j:
                r = r + jnp.where(v[i] > v[j], 1, 0)
        rank.append(r)
    e, w, locs = [], [], []
    for k in range(TOP_K):
        loc = sum(jnp.where(rank[j] == k, j, 0) for j in range(PER_GROUP))
        locs.append(loc)
        e.append((grp * PER_GROUP + loc).astype(F32))
        w.append(sum(jnp.where(rank[j] == k, sc[j], 0.0) for j in range(PER_GROUP)))
    wsum = w[0] + w[1]
    chosen = [jnp.where(pick & ((locs[0] == j) | (locs[1] == j)), 1.0, 0.0) for j in range(PER_GROUP)]
    onehot = jnp.concatenate(chosen, axis=0)
    seen = jnp.dot(onehot.astype(BF16), before, preferred_element_type=F32) + base
    pos = []
    for k in range(TOP_K):
        hit = sum(jnp.where(pick & (locs[k] == j), seen[j * N_EXPERT_GROUPS:(j + 1) * N_EXPERT_GROUPS], 0.0)
                  for j in range(PER_GROUP))
        pos.append(jnp.sum(hit, axis=0, keepdims=True))
    rows = [e[0], e[1], w[0] / wsum, w[1] / wsum, pos[0], pos[1]]
    route = jnp.concatenate(rows + [jnp.zeros_like(wsum)] * (8 - len(rows)), axis=0)
    return route, base + jnp.sum(onehot, axis=1, keepdims=True)


def _gelu_tanh(x):
    return 0.5 * x * (1.0 + jnp.tanh(math.sqrt(2.0 / math.pi) * (x + 0.044715 * (x * x * x))))


MERGE_LAG = 3


def _merge_kernel(xn_ref, modn_ref, x_ref, mod_ref, modt_ref, ya_ref, ys_ref, yg_ref, y5_ref, su_ref, s5d_ref,
                  wg_ref, wb_ref, wglu_ref, wout_ref, lng_ref, lnb_ref, rw_ref, rb_ref, before_ref, swap_ref,
                  x1_ref, h2_ref, route_ref, count_ref, base_ref, y5s_ref, y5p_ref, hbs_ref, v_ref, h2s_ref):
    s = pl.program_id(0)

    @pl.when(s == 0)
    def _reset():
        base_ref[...] = jnp.zeros_like(base_ref)
        y5p_ref[...] = jnp.zeros_like(y5p_ref)
        hbs_ref[...] = jnp.zeros_like(hbs_ref)
        v_ref[...] = jnp.zeros_like(v_ref)
        h2s_ref[...] = jnp.zeros_like(h2s_ref)

    logits_t = _bdot_nt(rw_ref[...], h2s_ref[...])
    v_prev = v_ref[...]

    hb = hbs_ref[...]
    glu = jnp.dot(y5p_ref[...], wglu_ref[...], preferred_element_type=F32)
    y5 = (glu[:, 0:S5_WIDTH] * _sigmoid(glu[:, S5_WIDTH:2 * S5_WIDTH])).astype(BF16)
    def select_experts():
        base_old = base_ref[...]
        route, base_new = _route(logits_t, rb_ref[...], base_old, before_ref[...])
        route_ref[...] = route
        base = jnp.where(s >= MERGE_LAG, base_new, base_old)
        base_ref[...] = base
        count_ref[...] = jnp.broadcast_to(base, count_ref.shape)

    def post_norm():
        x1 = _ln(v_prev) * lng_ref[...] + lnb_ref[...]
        x1_ref[...] = x1
        h2 = (_ln(x1) * (1.0 + _mod_part(modt_ref, 4)) + _mod_part(modt_ref, 3)).astype(BF16)
        h2_ref[...] = h2
        h2s_ref[...] = h2

    def relayout():
        n_ch = TM // S5_T
        for t in range(S5_T):
            for g in range(S5_GROUPS):
                y5s_ref[t * n_ch:(t + 1) * n_ch, g * S5_GROUP:(g + 1) * S5_GROUP] = (
                    y5_ref[g, :, t * S5_GROUP:(t + 1) * S5_GROUP])

    side_work = [post_norm, select_experts, relayout, lambda: None]
    acc = jnp.zeros((TM, D_MODEL), F32)
    for n, y in enumerate((ya_ref[...], ys_ref[...], yg_ref[...], y5)):
        gate = _sigmoid(jnp.dot(hb, wg_ref[:, n * D_MODEL:(n + 1) * D_MODEL], preferred_element_type=F32))
        acc = acc + gate * jnp.dot(y, wb_ref[n], preferred_element_type=F32)
        side_work[n]()
    mix = jnp.dot(acc.astype(BF16), wout_ref[...], preferred_element_type=F32)
    v_ref[...] = ALPHA * x_ref[...] + _mod_part(mod_ref, 2) * mix

    y5_pre = jnp.dot(swap_ref[...], y5s_ref[...], preferred_element_type=F32) + s5d_ref[...] * su_ref[...]
    y5p_ref[...] = _gelu_tanh(y5_pre).astype(BF16)
    hbs_ref[...] = (_ln(xn_ref[...]) * (1.0 + _mod_part(modn_ref, 1)) + _mod_part(modn_ref, 0)).astype(BF16)


def _merge(xcat, mods, ya, ys, yg, y5, su, s5_d, wg, wb, wglu, wout, ln_g, ln_b, rw_t, rb):
    n_batch = xcat.shape[0]
    n_tiles = TOK // TM
    n_all = n_batch * n_tiles

    def at(lag):
        def split(s):
            t = jnp.clip(s - lag, 0, n_all - 1)
            return t // n_tiles, t % n_tiles
        return split

    def tile(w, lag):
        return pl.BlockSpec((None, TM, w), lambda s: at(lag)(s) + (0,))

    def mod_spec(lag):
        def index(s):
            b, j = at(lag)(s)
            return jnp.where(j == 0, n_batch, b), 0, 0
        return pl.BlockSpec((None, 1, N_MOD * D_MODEL), index)

    def flat(lag):
        return lambda s: (0, jnp.clip(s - lag, 0, n_all - 1))

    const = lambda shape: pl.BlockSpec(shape, lambda s: tuple(0 for _ in shape))
    before = jnp.asarray(np.triu(np.ones((TM, TM)), 1), BF16)
    return pl.pallas_call(
        _merge_kernel,
        out_shape=[jax.ShapeDtypeStruct((n_batch, TOK, D_MODEL), F32),
                   jax.ShapeDtypeStruct((n_batch, TOK, D_MODEL), BF16),
                   jax.ShapeDtypeStruct((8, n_batch * TOK), F32),
                   jax.ShapeDtypeStruct((N_EXPERTS, 128), F32)],
        grid=(n_all + MERGE_LAG,),
        in_specs=[tile(D_MODEL, 0), mod_spec(0), tile(D_MODEL, 1), mod_spec(1), mod_spec(2),
                  tile(512, 1), tile(512, 1), tile(512, 1),
                  pl.BlockSpec((S5_GROUPS, TM // S5_T, S5_ROWW), lambda s: (0,) + at(0)(s)[::-1]),
                  tile(512, 0), const((1, S5_WIDTH)),
                  const((D_MODEL, N_BRANCH * D_MODEL)), const((N_BRANCH, 512, D_MODEL)),
                  const((S5_WIDTH, 2 * S5_WIDTH)), const((D_MODEL, D_MODEL)),
                  const((1, D_MODEL)), const((1, D_MODEL)),
                  const((N_EXPERTS, D_MODEL)), const((N_EXPERTS, 1)), const((TM, TM)), const((TM, TM))],
        out_specs=[tile(D_MODEL, 2), tile(D_MODEL, 2),
                   pl.BlockSpec((8, TM), flat(MERGE_LAG)),
                   const((N_EXPERTS, 128))],
        scratch_shapes=[pltpu.VMEM((N_EXPERTS, 1), F32), pltpu.VMEM((TM, S5_WIDTH), BF16),
                        pltpu.VMEM((TM, S5_WIDTH), BF16), pltpu.VMEM((TM, D_MODEL), BF16),
                        pltpu.VMEM((TM, D_MODEL), F32), pltpu.VMEM((TM, D_MODEL), BF16)],
        compiler_params=_params(("arbitrary",)),
        name="merge",
    )(xcat, mods, xcat, mods, mods, ya, ys, yg, y5, su, s5_d, wg, wb, wglu, wout, ln_g, ln_b, rw_t, rb, before,
      _chunk_swap())


def _moe_kernel(be_ref, nu_ref, nxt_ref, slot_ref, x_ref, wg_hbm, wu_hbm, wd_hbm, y_ref,
                wg_f, wu_f, wd_f, wgu_s, wd_s, sem, *, layer):
    i = pl.program_id(0)
    used = i < nu_ref[0]
    first = used & ((i == 0) | (be_ref[i] != be_ref[jnp.maximum(i - 1, 0)]))

    def copies(expert, buf):
        return [pltpu.make_async_copy(wg_hbm.at[layer, expert], wg_f.at[buf], sem.at[buf, 0]),
                pltpu.make_async_copy(wu_hbm.at[layer, expert], wu_f.at[buf], sem.at[buf, 1]),
                pltpu.make_async_copy(wd_hbm.at[layer, expert], wd_f.at[buf], sem.at[buf, 2])]

    @pl.when(used & (i == 0))
    def _first_fetch():
        for c in copies(be_ref[0], slot_ref[0]):
            c.start()

    @pl.when(first)
    def _next_expert():
        buf = slot_ref[i]
        for c in copies(be_ref[i], buf):
            c.wait()

        @pl.when(nxt_ref[i] >= 0)
        def _prefetch():
            for c in copies(nxt_ref[i], 1 - buf):
                c.start()

        wgu_s[:, 0:D_EXPERT] = wg_f[buf].astype(BF16)
        wgu_s[:, D_EXPERT:2 * D_EXPERT] = wu_f[buf].astype(BF16)
        wd_s[...] = wd_f[buf].astype(BF16)

    @pl.when(used)
    def _block():
        gu = jnp.dot(x_ref[...], wgu_s[...], preferred_element_type=F32)
        mid = _silu(gu[:, 0:D_EXPERT]) * gu[:, D_EXPERT:2 * D_EXPERT]
        y_ref[...] = jnp.dot(mid.astype(BF16), wd_s[...], preferred_element_type=F32).astype(y_ref.dtype)

    @pl.when(jnp.logical_not(used))
    def _unused():
        y_ref[...] = jnp.zeros_like(y_ref)


def _moe_experts(block_expert, n_used, xs, w_gate, w_up, w_down, layer):
    n_blocks = block_expert.shape[0]
    idx = jnp.arange(n_blocks, dtype=jnp.int32)
    starts = jnp.concatenate([jnp.ones((1,), bool), block_expert[1:] != block_expert[:-1]])
    slot = ((jnp.cumsum(starts.astype(jnp.int32)) - 1) % 2).astype(jnp.int32)
    later = jnp.where(starts, idx, n_blocks)
    nxt_idx = jnp.concatenate([lax.cummin(later[::-1])[::-1][1:], jnp.full((1,), n_blocks, jnp.int32)])
    nxt = jnp.where(nxt_idx < n_used[0], block_expert[jnp.minimum(nxt_idx, n_blocks - 1)], -1).astype(jnp.int32)
    hbm = pl.BlockSpec(memory_space=pl.ANY)
    return pl.pallas_call(
        functools.partial(_moe_kernel, layer=layer),
        out_shape=jax.ShapeDtypeStruct((n_blocks * MOE_ROWS, D_MODEL), BF16),
        grid_spec=pltpu.PrefetchScalarGridSpec(
            num_scalar_prefetch=4, grid=(n_blocks,),
            in_specs=[pl.BlockSpec((MOE_ROWS, D_MODEL), lambda i, *_: (i, 0)), hbm, hbm, hbm],
            out_specs=pl.BlockSpec((MOE_ROWS, D_MODEL), lambda i, *_: (i, 0)),
            scratch_shapes=[pltpu.VMEM((2, D_MODEL, D_EXPERT), F32), pltpu.VMEM((2, D_MODEL, D_EXPERT), F32),
                            pltpu.VMEM((2, D_EXPERT, D_MODEL), F32),
                            pltpu.VMEM((D_MODEL, 2 * D_EXPERT), BF16), pltpu.VMEM((D_EXPERT, D_MODEL), BF16),
                            pltpu.SemaphoreType.DMA((2, 3))]),
        compiler_params=_params(("arbitrary",)),
        name="moe_experts",
    )(block_expert, n_used, nxt, slot, xs, w_gate, w_up, w_down)


def _final_kernel(x_ref, mod_ref, f0_ref, f1_ref, route_ref, lng_ref, lnb_ref, o_ref):
    o_ref[...] = _ffn_residual(x_ref[...], mod_ref, f0_ref, f1_ref, route_ref, lng_ref, lnb_ref)


def _final(x1, mods, f, route, ln_g, ln_b):
    n_batch = x1.shape[0]
    n_tiles = TOK // TM
    skip = CTX_LEN // TM
    const = pl.BlockSpec((1, D_MODEL), lambda b, j: (0, 0))
    return pl.pallas_call(
        _final_kernel,
        out_shape=jax.ShapeDtypeStruct((n_batch, SEQ, D_MODEL), F32),
        grid=(n_batch, n_tiles - skip),
        in_specs=[pl.BlockSpec((None, TM, D_MODEL), lambda b, j: (b, j + skip, 0)),
                  pl.BlockSpec((None, 1, N_MOD * D_MODEL), lambda b, j: (b, 0, 0)),
                  pl.BlockSpec((None, None, TM, D_MODEL), lambda b, j: (0, b, j + skip, 0)),
                  pl.BlockSpec((None, None, TM, D_MODEL), lambda b, j: (1, b, j + skip, 0)),
                  pl.BlockSpec((8, TM), lambda b, j: (0, b * n_tiles + j + skip)),
                  const, const],
        out_specs=pl.BlockSpec((None, TM, D_MODEL), lambda b, j: (b, j, 0)),
        compiler_params=_params(("arbitrary", "arbitrary")),
        name="ffn_residual",
    )(x1, mods, f, f, route, ln_g, ln_b)


def _lookup(table, idx):
    ids = jnp.arange(table.shape[0], dtype=jnp.int32).reshape((-1,) + (1,) * idx.ndim)
    return jnp.sum(jnp.where(idx[None] == ids, table.reshape(ids.shape), 0), axis=0)


def _dispatch(route, counts_rows):
    n_tok = route.shape[1]
    n_assign = n_tok * TOP_K
    n_blocks = -(-(n_assign + N_EXPERTS * (MOE_ROWS - 1)) // MOE_ROWS)
    cap = n_blocks * MOE_ROWS
    counts = counts_rows[:, 0].astype(jnp.int32).reshape(PER_GROUP, N_EXPERT_GROUPS).T.reshape(N_EXPERTS)
    padded = (counts + MOE_ROWS - 1) // MOE_ROWS * MOE_ROWS
    pad_end = jnp.cumsum(padded)
    pad_start = pad_end - padded
    expert = route[0:TOP_K].astype(jnp.int32)
    pos = _lookup(pad_start, expert) + route[2 * TOP_K:3 * TOP_K].astype(jnp.int32)
    block_start = jnp.arange(n_blocks, dtype=jnp.int32) * MOE_ROWS
    block_expert = jnp.minimum(jnp.sum((pad_end[None, :] <= block_start[:, None]).astype(jnp.int32), axis=1),
                               N_EXPERTS - 1)
    n_used = (pad_end[-1:] // MOE_ROWS).astype(jnp.int32)
    gap = padded - counts
    gap_end = jnp.cumsum(gap)
    gap_first = jnp.concatenate([pad_start + counts - (gap_end - gap), pad_end[-1:] - gap_end[-1:]])
    k = jnp.arange(cap - n_assign, dtype=jnp.int32)
    owner = jnp.sum((gap_end[None, :] <= k[:, None]).astype(jnp.int32), axis=1)
    free_slot = _lookup(gap_first, owner) + k
    tok = jnp.broadcast_to(jnp.arange(n_tok, dtype=jnp.int32), (TOP_K, n_tok)).reshape(-1)
    keys = jnp.concatenate([pos.reshape(-1), free_slot])
    vals = jnp.concatenate([tok, k % n_tok])
    _, slot_tok = lax.sort((keys, vals), num_keys=1)
    return slot_tok, pos, block_expert, n_used


def _col(w, part):
    return w[:, _IN_OFF[part]:_IN_OFF[part + 1]]


def _proj_weights(w_in):
    zeros = jnp.zeros((D_MODEL, 128 - SSD_HEADS), w_in.dtype)
    dt = _col(w_in, _SDT)
    parts = [_col(w_in, p) for p in (_AQ, _GQ, _AK, _AV, _GK, _GV, _SX, _SB, _SC, _SZ, _SU)]
    parts += [dt[:, 0:SSD_HEADS], zeros, dt[:, SSD_HEADS:2 * SSD_HEADS], zeros]
    return jnp.concatenate(parts, axis=1).astype(BF16), _col(w_in, _GATE).astype(BF16)


def _rope_tables():
    rows = SEQ // GRID_W
    row = jnp.repeat(jnp.arange(rows, dtype=F32), GRID_W)
    col = jnp.tile(jnp.arange(GRID_W, dtype=F32), rows)
    axis_dim = HEAD_DIM // 2
    inv_freq = ROPE_THETA ** (-jnp.arange(0, axis_dim, 2, dtype=F32) / axis_dim)
    ang_r = row[:, None] * inv_freq
    ang_c = col[:, None] * inv_freq
    ang = jnp.concatenate([ang_r, ang_r, ang_c, ang_c], axis=-1)
    cos = jnp.concatenate([jnp.ones((CTX_LEN, HEAD_DIM), F32), jnp.cos(ang)], axis=0)
    sin = jnp.concatenate([jnp.zeros((CTX_LEN, HEAD_DIM), F32), jnp.sin(ang)], axis=0)
    return jnp.tile(cos, (1, 2)), jnp.tile(sin, (1, 2))


def _pad_lanes(v, width=128):
    return jnp.pad(v, ((0, 0), (0, width - v.shape[-1])))


def kernel(x, c, ctx, c_ctx, mod_w, mod_b, w_in, wa_sink, ga_q_norm, ga_k_norm, ssd_conv_w, ssd_conv_b, ssd_dt_bias, ssd_a_log, ssd_d, ssd_norm_w, s5_a_re, s5_a_im, s5_log_dt, s5_b_re, s5_b_im, s5_c_re, s5_c_im, s5_d, s5_w_glu, w_branch, w_out, ln1_g, ln1_b, ln2_g, ln2_b, router_w, router_bias, moe_w_gate, moe_w_up, moe_w_down):
    n_batch = x.shape[0]
    n_tok = n_batch * TOK
    xcat = jnp.concatenate([ctx, x], axis=1)
    mod_rows = -(-(n_batch + 1) // 8) * 8
    cond = jnp.zeros((mod_rows, D_MODEL), F32).at[:n_batch].set(c).at[n_batch].set(c_ctx)
    mods_all = _modulation(cond, mod_w, mod_b)
    cos, sin = _rope_tables()

    perm = np.array([g * PER_GROUP + j for j in range(PER_GROUP) for g in range(N_EXPERT_GROUPS)])
    rw_t = router_w.T[perm].astype(BF16)
    rb = router_bias.astype(F32)[perm].reshape(N_EXPERTS, 1)

    prev = None
    for layer in range(DEPTH):
        mods = mods_all[layer].reshape(mod_rows, 1, N_MOD * D_MODEL)
        w1, wg = _proj_weights(w_in[layer])
        if prev is None:
            attn, conv_in, sz, su, dt, u_g = _inproj(xcat, mods, w1)
        else:
            xcat, attn, conv_in, sz, su, dt, u_g = _inproj(None, mods, w1, prev)

        sink_rows = jnp.repeat(wa_sink[layer].astype(F32).reshape(N_KV, GRP) * LOG2E, QB,
                               axis=1).reshape(N_KV, 1, GRP * QB)
        qw = jnp.tile(ga_q_norm[layer].astype(F32), N_HEADS).reshape(1, 512)
        kw = jnp.tile(ga_k_norm[layer].astype(F32), N_KV).reshape(1, 128)
        ya, yg = _attention(attn, cos, sin, sink_rows, qw, kw)

        dt_bias = _pad_lanes(ssd_dt_bias[layer].astype(F32).reshape(2, SSD_HEADS)).reshape(2, 1, 128)
        a_log = _pad_lanes(ssd_a_log[layer].astype(F32)).reshape(2, 1, 128)
        d_exp = jnp.repeat(ssd_d[layer].astype(F32), SSD_P).reshape(1, SSD_WIDTH)
        ys = _ssd(conv_in, sz, dt, ssd_conv_w[layer].astype(F32), ssd_conv_b[layer].astype(F32).reshape(1, CONV_W),
                  dt_bias, a_log, d_exp, ssd_norm_w[layer].astype(F32).reshape(1, SSD_WIDTH))

        mats = _s5_matrices(s5_a_re[layer], s5_a_im[layer], s5_log_dt[layer], s5_b_re[layer], s5_b_im[layer],
                            s5_c_re[layer], s5_c_im[layer])
        y5 = _s5(u_g.reshape(S5_GROUPS, S5_NCH * n_batch, S5_ROWW), mats, n_batch)
        y5 = y5.reshape(S5_GROUPS, S5_NCH, n_batch * S5_ROWW)

        x1, h2, route, counts = _merge(
            xcat, mods, ya, ys, yg, y5, su, s5_d[layer].astype(F32).reshape(1, S5_WIDTH), wg,
            w_branch[layer].astype(BF16), s5_w_glu[layer].astype(BF16), w_out[layer].astype(BF16),
            ln1_g[layer].reshape(1, D_MODEL), ln1_b[layer].reshape(1, D_MODEL), rw_t, rb)

        slot_tok, pos, block_expert, n_used = _dispatch(route, counts)
        xs = h2.reshape(n_tok, D_MODEL)[slot_tok]
        y_slots = _moe_experts(block_expert, n_used, xs, moe_w_gate, moe_w_up, moe_w_down, layer)
        f = y_slots[pos.reshape(-1)].reshape(TOP_K, n_batch, TOK, D_MODEL)
        prev = (x1, mods, f, route, ln2_g[layer].reshape(1, D_MODEL), ln2_b[layer].reshape(1, D_MODEL))
    return _final(*prev)
```

```python
import functools
import math

import jax
import jax.numpy as jnp
import numpy as np
from jax import lax
from jax.experimental import pallas as pl
from jax.experimental.pallas import tpu as pltpu

F32 = jnp.float32
BF16 = jnp.bfloat16

D_MODEL = 1024
DEPTH = 4
GRID_W = 64
CTX_LEN = 256
SEQ = 2048
TOK = CTX_LEN + SEQ
N_MOD = 6
HEAD_DIM = 64
N_BRANCH = 4
N_HEADS = 8
N_KV = 2
GRP = N_HEADS // N_KV
WINDOW = 128
QB = 128
ROPE_THETA = 10000.0

SSD_HEADS = 8
SSD_P = 64
SSD_GROUPS = 2
SSD_N = 128
SSD_T = 128
SSD_WIDTH = SSD_HEADS * SSD_P
SSD_CONV_CH = SSD_WIDTH + 2 * SSD_GROUPS * SSD_N
N_CHUNK = TOK // SSD_T

S5_GROUP = 16
S5_WIDTH = 512
S5_GROUPS = S5_WIDTH // S5_GROUP
S5_STATE = 64
S5_T = 16
S5_ROWW = S5_T * S5_GROUP
S5_NCH = TOK // S5_T
S5_CTX_CH = CTX_LEN // S5_T

N_EXPERTS = 32
N_EXPERT_GROUPS = 8
PER_GROUP = N_EXPERTS // N_EXPERT_GROUPS
TOP_K = 2
D_EXPERT = 512
MOE_ROWS = 256

ALPHA = (2 * DEPTH) ** 0.25
NORM_EPS = 1e-6

TM = 256
VMEM_LIMIT = 56 * 1024 * 1024

_IN_SIZES = (512, 128, 128, 512, 128, 128, 512, 512, 256, 256, 16, 512, 4096)
_IN_OFF = np.concatenate([[0], np.cumsum(_IN_SIZES)]).astype(int)
(_AQ, _AK, _AV, _GQ, _GK, _GV, _SX, _SZ, _SB, _SC, _SDT, _SU, _GATE) = range(13)

ATTN_W = 1536
CONV_W = 1024
DT_W = 256
PROJ_W = ATTN_W + CONV_W + 512 + 512 + DT_W


def _params(sem=None):
    return pltpu.CompilerParams(dimension_semantics=sem, vmem_limit_bytes=VMEM_LIMIT)


def _ln(x):
    mu = jnp.mean(x, axis=-1, keepdims=True)
    xc = x - mu
    var = jnp.mean(xc * xc, axis=-1, keepdims=True)
    return xc * lax.rsqrt(var + NORM_EPS)


def _sigmoid(x):
    return 0.5 * jnp.tanh(0.5 * x) + 0.5


def _silu(x):
    return x * _sigmoid(x)


def _bdot(a, b):
    return jnp.dot(a.astype(BF16), b.astype(BF16), preferred_element_type=F32)


def _bdot_nt(a, b):
    return lax.dot_general(a.astype(BF16), b.astype(BF16), (((1,), (1,)), ((), ())),
                           preferred_element_type=F32)


def _split3(v):
    hi = v.astype(BF16)
    r1 = v - hi.astype(F32)
    mid = r1.astype(BF16)
    lo = (r1 - mid.astype(F32)).astype(BF16)
    return hi, mid, lo


def _mod_kernel(c_ref, w_ref, b_ref, o_ref):
    c = c_ref[...]
    o_ref[...] = _bdot(_silu(c), w_ref[...]) + b_ref[...]


def _modulation(cond, mod_w, mod_b):
    n_layer = mod_w.shape[0]
    rows = cond.shape[0]
    return pl.pallas_call(
        _mod_kernel,
        out_shape=jax.ShapeDtypeStruct((n_layer, rows, N_MOD * D_MODEL), F32),
        grid=(n_layer, N_MOD),
        in_specs=[
            pl.BlockSpec((rows, D_MODEL), lambda l, n: (0, 0)),
            pl.BlockSpec((None, D_MODEL, D_MODEL), lambda l, n: (l, 0, n)),
            pl.BlockSpec((None, 1, D_MODEL), lambda l, n: (l, 0, n)),
        ],
        out_specs=pl.BlockSpec((None, rows, D_MODEL), lambda l, n: (l, 0, n)),
        compiler_params=_params(("arbitrary", "arbitrary")),
        name="modulation",
    )(cond, mod_w, mod_b.reshape(n_layer, 1, N_MOD * D_MODEL))


def _mod_part(mod_ref, k):
    return mod_ref[:, k * D_MODEL:(k + 1) * D_MODEL]


def _mod_row(n_batch):
    return lambda b, j: (jnp.where(j == 0, n_batch, b), 0, 0)


def _chunk_swap():
    r = np.arange(TM)
    p = np.zeros((TM, TM), np.float32)
    p[r, (r % S5_T) * (TM // S5_T) + r // S5_T] = 1.0
    return jnp.asarray(p, BF16)


def _ffn_residual(x1, mod_ref, f0_ref, f1_ref, route_ref, lng_ref, lnb_ref):
    r = lax.broadcasted_iota(jnp.int32, (TM, TM), 0)
    c = lax.broadcasted_iota(jnp.int32, (TM, TM), 1)
    w = [jnp.sum(jnp.where(r == c, route_ref[TOP_K + k:TOP_K + k + 1, :], 0.0), axis=1, keepdims=True)
         for k in range(TOP_K)]
    ffn = w[0] * f0_ref[...].astype(F32) + w[1] * f1_ref[...].astype(F32)
    return _ln(ALPHA * x1 + _mod_part(mod_ref, 5) * ffn) * lng_ref[...] + lnb_ref[...]


def _inproj_kernel(*refs, fused):
    if fused:
        (x1_ref, modp_ref, f0_ref, f1_ref, route_ref, lng_ref, lnb_ref, mod_ref, w_ref, swap_ref,
         x_out_ref, attn_ref, conv_ref, sz_ref, su_ref, dt_ref, u_ref, sub_ref) = refs
    else:
        x_ref, mod_ref, w_ref, swap_ref, attn_ref, conv_ref, sz_ref, su_ref, dt_ref, u_ref, sub_ref = refs

    @pl.when(pl.program_id(0) == 0)
    def _reset():
        sub_ref[...] = jnp.zeros_like(sub_ref)

    by_step = jnp.dot(swap_ref[...], sub_ref[...], preferred_element_type=F32)
    n_ch = TM // S5_T
    for t in range(S5_T):
        rows = by_step[t * n_ch:(t + 1) * n_ch]
        for g in range(S5_GROUPS):
            u_ref[g, :, t * S5_GROUP:(t + 1) * S5_GROUP] = rows[:, g * S5_GROUP:(g + 1) * S5_GROUP].astype(BF16)

    if fused:
        x = _ffn_residual(x1_ref[...], modp_ref, f0_ref, f1_ref, route_ref, lng_ref, lnb_ref)
        x_out_ref[...] = x
    else:
        x = x_ref[...]
    h = _ln(x) * (1.0 + _mod_part(mod_ref, 1)) + _mod_part(mod_ref, 0)
    hb = h.astype(BF16)
    off = 0
    for ref in (attn_ref, conv_ref, sz_ref, su_ref, dt_ref):
        width = ref.shape[-1]
        ref[...] = jnp.dot(hb, w_ref[:, off:off + width], preferred_element_type=F32)
        off += width
    sub_ref[...] = su_ref[...].astype(BF16)


def _inproj(xcat, mods, w1, prev=None):
    n_batch = xcat.shape[0] if prev is None else prev[0].shape[0]
    n_tiles = TOK // TM
    n_all = n_batch * n_tiles
    widths = (ATTN_W, CONV_W, 512, 512, DT_W)

    def at(lag):
        def split(s):
            t = jnp.clip(s - lag, 0, n_all - 1)
            return t // n_tiles, t % n_tiles
        return split

    tile = lambda w: pl.BlockSpec((None, TM, w), lambda s: at(0)(s) + (0,))

    def mod_index(s):
        b, j = at(0)(s)
        return jnp.where(j == 0, n_batch, b), 0, 0

    mod_spec = pl.BlockSpec((None, 1, N_MOD * D_MODEL), mod_index)
    const = lambda shape: pl.BlockSpec(shape, lambda s: tuple(0 for _ in shape))
    out_shape = ([jax.ShapeDtypeStruct((n_batch, TOK, w), F32) for w in widths]
                 + [jax.ShapeDtypeStruct((S5_GROUPS, S5_NCH, n_batch * S5_ROWW), BF16)])
    out_specs = ([tile(w) for w in widths]
                 + [pl.BlockSpec((S5_GROUPS, TM // S5_T, S5_ROWW), lambda s: (0,) + at(1)(s)[::-1])])
    own = [mod_spec, const((D_MODEL, PROJ_W)), const((TM, TM))]
    if prev is None:
        in_specs, args = [tile(D_MODEL)] + own, (xcat, mods, w1, _chunk_swap())
    else:
        x1, mods_prev, f, route, ln_g, ln_b = prev
        in_specs = [tile(D_MODEL), mod_spec,
                    pl.BlockSpec((None, None, TM, D_MODEL), lambda s: (0,) + at(0)(s) + (0,)),
                    pl.BlockSpec((None, None, TM, D_MODEL), lambda s: (1,) + at(0)(s) + (0,)),
                    pl.BlockSpec((8, TM), lambda s: (0, jnp.clip(s, 0, n_all - 1))),
                    const((1, D_MODEL)), const((1, D_MODEL))] + own
        args = (x1, mods_prev, f, f, route, ln_g, ln_b, mods, w1, _chunk_swap())
        out_shape = [jax.ShapeDtypeStruct((n_batch, TOK, D_MODEL), F32)] + out_shape
        out_specs = [tile(D_MODEL)] + out_specs
    return pl.pallas_call(
        functools.partial(_inproj_kernel, fused=prev is not None),
        out_shape=out_shape,
        grid=(n_all + 1,),
        in_specs=in_specs,
        out_specs=out_specs,
        scratch_shapes=[pltpu.VMEM((TM, S5_WIDTH), BF16)],
        compiler_params=_params(("arbitrary",)),
        name="inproj",
    )(*args)


def _rope(t, cos, sin):
    width = t.shape[-1]
    lane = lax.broadcasted_iota(jnp.int32, t.shape, t.ndim - 1)
    rot = jnp.where(lane % 32 < 16, -pltpu.roll(t, width - 16, t.ndim - 1), pltpu.roll(t, 16, t.ndim - 1))
    return t * cos + rot * sin


def _tile4(v):
    return jnp.concatenate([v, v, v, v], axis=1)


def _head_inv_rms(t, n_heads):
    t2 = t * t
    lane = lax.broadcasted_iota(jnp.int32, t.shape, 1)
    inv = jnp.zeros_like(t)
    for h in range(n_heads):
        ms = jnp.sum(t2[:, h * HEAD_DIM:(h + 1) * HEAD_DIM], axis=1, keepdims=True) * (1.0 / HEAD_DIM)
        inv = jnp.where(lane // HEAD_DIM == h, lax.rsqrt(ms + NORM_EPS), inv)
    return inv


def _stack_heads(qr, kv):
    parts = [qr[:, (kv * GRP + g) * HEAD_DIM:(kv * GRP + g + 1) * HEAD_DIM] for g in range(GRP)]
    return jnp.concatenate(parts, axis=0).astype(BF16)


QPS = 2
LOG2E = math.log2(math.e)


KEY_BLOCK = 128


def _attend(jobs, sink_ref, out_ref, kb_ref, vt_ref, s_ref, p_ref):
    width_q = GRP * QB
    halves = [slice(0, width_q // 2), slice(width_q // 2, width_q)]
    sub = KEY_BLOCK // 8
    for u, (qr, parts) in enumerate(jobs):
        for kv in range(N_KV):
            qs = _stack_heads(qr, kv)
            row = 0
            for key_idx, width, bias in parts:
                for hc in halves:
                    s = _bdot_nt(kb_ref[kv, key_idx, :], qs[hc])
                    if bias is not None:
                        s = s + jnp.concatenate([bias] * (width_q // 2 // QB), axis=1)
                    s_ref[u, kv, row:row + width, hc] = s
                row += width
    for u, (_, parts) in enumerate(jobs):
        n_keys = sum(width for _, width, _ in parts)
        sink_term = {}
        for kv in range(N_KV):
            for g in range(GRP):
                cols = slice(g * QB, (g + 1) * QB)
                mx = None
                for r0 in range(0, n_keys, KEY_BLOCK):
                    blk = jnp.max(s_ref[u, kv, r0:r0 + KEY_BLOCK, cols].reshape(sub, 8, QB), axis=0)
                    mx = blk if mx is None else jnp.maximum(mx, blk)
                m = jnp.max(mx, axis=0, keepdims=True)
                if sink_ref is not None:
                    m = jnp.maximum(m, sink_ref[kv, :, cols])
                    sink_term[kv, g] = jnp.exp2(sink_ref[kv, :, cols] - m)
                m_b = jnp.broadcast_to(m, (KEY_BLOCK, QB))
                for r0 in range(0, n_keys, KEY_BLOCK):
                    p_ref[u, kv, r0:r0 + KEY_BLOCK, cols] = jnp.exp2(
                        s_ref[u, kv, r0:r0 + KEY_BLOCK, cols] - m_b).astype(BF16)
        o_t = {}
        for kv in range(N_KV):
            for hi, hc in enumerate(halves):
                o = None
                row = 0
                for key_idx, width, _ in parts:
                    part = jnp.dot(vt_ref[kv, :, key_idx], p_ref[u, kv, row:row + width, hc],
                                   preferred_element_type=F32)
                    o = part if o is None else o + part
                    row += width
                o_t[kv, hi] = o
        for g in range(GRP):
            hi, gl = divmod(g, GRP // 2)
            cols = slice(gl * QB, (gl + 1) * QB)
            scaled = []
            for kv in range(N_KV):
                den = o_t[kv, hi][HEAD_DIM:HEAD_DIM + 1, cols]
                if sink_ref is not None:
                    den = den + sink_term[kv, g]
                scaled.append(o_t[kv, hi][0:HEAD_DIM, cols] / den)
            pair = jnp.concatenate(scaled, axis=0).T
            for kv in range(N_KV):
                h = kv * GRP + g
                out_ref[u * QB:(u + 1) * QB, h * HEAD_DIM:(h + 1) * HEAD_DIM] = (
                    pair[:, kv * HEAD_DIM:(kv + 1) * HEAD_DIM].astype(out_ref.dtype))


def _wattn_kernel(q_ref, k_ref, v_ref, cos_ref, sin_ref, sink_ref, bias_ref, out_ref, kb_ref, vt_ref, s_ref, p_ref):
    i = pl.program_id(1)

    @pl.when(i == 0)
    def _prep():
        kr = _rope(k_ref[...], cos_ref[...], sin_ref[...])
        v_t = v_ref[...].T
        for kv in range(N_KV):
            kb_ref[kv] = kr[:, kv * HEAD_DIM:(kv + 1) * HEAD_DIM].astype(BF16)
            vt_ref[kv, 0:HEAD_DIM, :] = v_t[kv * HEAD_DIM:(kv + 1) * HEAD_DIM, :].astype(BF16)
            vt_ref[kv, HEAD_DIM:HEAD_DIM + 16, :] = jnp.ones((16, TOK), BF16)

    row0 = pl.multiple_of(i * (QPS * QB), QPS * QB)
    cos = _tile4(cos_ref[pl.ds(row0, QPS * QB), :])
    sin = _tile4(sin_ref[pl.ds(row0, QPS * QB), :])
    qr = _rope(q_ref[...], cos, sin) * (HEAD_DIM ** -0.5 * LOG2E)
    qrs = [qr[u * QB:(u + 1) * QB] for u in range(QPS)]

    ctx_keys = (slice(0, CTX_LEN), CTX_LEN, None)

    @pl.when(i < CTX_LEN // (QPS * QB))
    def _ctx_queries():
        _attend([(q, [ctx_keys]) for q in qrs], sink_ref, out_ref, kb_ref, vt_ref, s_ref, p_ref)

    @pl.when(i >= CTX_LEN // (QPS * QB))
    def _latent_queries():
        jobs = []
        for u in range(QPS):
            j = i * QPS + u - CTX_LEN // QB
            band = jnp.clip(j - 1, 0, SEQ // QB - 3)
            start = pl.multiple_of(CTX_LEN + band * QB, QB)
            bias = bias_ref[j - band]
            jobs.append((qrs[u], [ctx_keys, (pl.ds(start, 3 * QB), 3 * QB, bias)]))
        _attend(jobs, sink_ref, out_ref, kb_ref, vt_ref, s_ref, p_ref)


def _gattn_kernel(q_ref, k_ref, v_ref, cos_ref, sin_ref, qw_ref, kw_ref, out_ref, kb_ref, vt_ref, s_ref, p_ref):
    i = pl.program_id(1)

    @pl.when(i == 0)
    def _prep():
        k = k_ref[...]
        kn = k * _head_inv_rms(k, N_KV) * kw_ref[...]
        kr = _rope(kn, cos_ref[...], sin_ref[...])
        v_t = v_ref[...].T
        for kv in range(N_KV):
            kb_ref[kv] = kr[:, kv * HEAD_DIM:(kv + 1) * HEAD_DIM].astype(BF16)
            vt_ref[kv, 0:HEAD_DIM, :] = v_t[kv * HEAD_DIM:(kv + 1) * HEAD_DIM, :].astype(BF16)
            vt_ref[kv, HEAD_DIM:HEAD_DIM + 16, :] = jnp.ones((16, TOK), BF16)

    row0 = pl.multiple_of(i * (QPS * QB), QPS * QB)
    cos = _tile4(cos_ref[pl.ds(row0, QPS * QB), :])
    sin = _tile4(sin_ref[pl.ds(row0, QPS * QB), :])
    q = q_ref[...]
    qn = q * _head_inv_rms(q, N_HEADS) * qw_ref[...]
    qr = _rope(qn, cos, sin) * (HEAD_DIM ** -0.5 * LOG2E)
    qrs = [qr[u * QB:(u + 1) * QB] for u in range(QPS)]

    @pl.when(i < CTX_LEN // (QPS * QB))
    def _ctx_queries():
        _attend([(q_u, [(slice(0, CTX_LEN), CTX_LEN, None)]) for q_u in qrs], None, out_ref, kb_ref, vt_ref,
                s_ref, p_ref)

    @pl.when(i >= CTX_LEN // (QPS * QB))
    def _latent_queries():
        _attend([(q_u, [(slice(0, TOK), TOK, None)]) for q_u in qrs], None, out_ref, kb_ref, vt_ref, s_ref, p_ref)


def _window_bias():
    q_pos = np.arange(QB)[None, None, :] + QB * np.arange(3)[:, None, None]
    k_pos = np.arange(3 * QB)[None, :, None]
    return jnp.asarray(np.where(np.abs(q_pos - k_pos) <= WINDOW, 0.0, -np.inf), F32)


def _attention(attn, cos, sin, sink_rows, qw, kw):
    n_batch = attn.shape[0]
    grid = (n_batch, TOK // (QPS * QB))
    q_spec = lambda blk: pl.BlockSpec((None, QPS * QB, 512), lambda b, i: (b, i, blk))
    kv_spec = lambda blk: pl.BlockSpec((None, TOK, 128), lambda b, i: (b, 0, blk))
    tab_spec = pl.BlockSpec((TOK, 128), lambda b, i: (0, 0))
    out_spec = pl.BlockSpec((None, QPS * QB, 512), lambda b, i: (b, i, 0))
    out_shape = jax.ShapeDtypeStruct((n_batch, TOK, 512), BF16)
    kv_scratch = [pltpu.VMEM((N_KV, TOK, HEAD_DIM), BF16), pltpu.VMEM((N_KV, HEAD_DIM + 16, TOK), BF16)]
    w_keys = CTX_LEN + 3 * QB
    ya = pl.pallas_call(
        _wattn_kernel, out_shape=out_shape, grid=grid,
        in_specs=[q_spec(0), kv_spec(8), kv_spec(9), tab_spec, tab_spec,
                  pl.BlockSpec((N_KV, 1, GRP * QB), lambda b, i: (0, 0, 0)),
                  pl.BlockSpec((3, 3 * QB, QB), lambda b, i: (0, 0, 0))],
        out_specs=out_spec,
        scratch_shapes=kv_scratch + [pltpu.VMEM((QPS, N_KV, w_keys, GRP * QB), F32),
                                     pltpu.VMEM((QPS, N_KV, w_keys, GRP * QB), BF16)],
        compiler_params=_params(("arbitrary", "arbitrary")), name="window_attention",
    )(attn, attn, attn, cos, sin, sink_rows, _window_bias())
    yg = pl.pallas_call(
        _gattn_kernel, out_shape=out_shape, grid=grid,
        in_specs=[q_spec(1), kv_spec(10), kv_spec(11), tab_spec, tab_spec,
                  pl.BlockSpec((1, 512), lambda b, i: (0, 0)),
                  pl.BlockSpec((1, 128), lambda b, i: (0, 0))],
        out_specs=out_spec,
        scratch_shapes=kv_scratch + [pltpu.VMEM((QPS, N_KV, TOK, GRP * QB), F32),
                                     pltpu.VMEM((QPS, N_KV, TOK, GRP * QB), BF16)],
        compiler_params=_params(("arbitrary", "arbitrary")), name="global_attention",
    )(attn, attn, attn, cos, sin, qw, kw)
    return ya, yg


def _ssd_chunk(s):
    r = s - N_CHUNK
    back = jnp.where(r == 0, 1, jnp.where(r == 1, 0, N_CHUNK + 1 - r))
    return jnp.where(s < N_CHUNK, s, back)


SSD_NB = 4


def _ssd_kernel(x_ref, prev_ref, next_ref, z_ref, dt_ref, cw_ref, cb_ref, dtb_ref, a_ref, d_ref, nw_ref, e_ref,
                out_ref, yf_ref, state_ref, y_ref):
    s = pl.program_id(1)
    chunk = _ssd_chunk(s)
    backward = s >= N_CHUNK

    @pl.when((s == 0) | (s == N_CHUNK))
    def _reset():
        state_ref[...] = jnp.zeros_like(state_ref)

    first = (chunk == 0) | (chunk == CTX_LEN // SSD_T)
    last = (chunk == CTX_LEN // SSD_T - 1) | (chunk == N_CHUNK - 1)
    row = lax.broadcasted_iota(jnp.int32, (SSD_T, CONV_W), 0)
    tr = lax.broadcasted_iota(jnp.int32, (SSD_T, SSD_T), 0)
    tc = lax.broadcasted_iota(jnp.int32, (SSD_T, SSD_T), 1)
    causal = jnp.where(backward, tc - tr, tr - tc) >= 0
    tri = jnp.where(causal, 1.0, 0.0).astype(BF16)
    neg_a = -jnp.exp(a_ref[...])
    expand = e_ref[...]

    def widen(v):
        return jnp.dot(jnp.concatenate(_split3(v), axis=1), expand, preferred_element_type=F32)

    rows = pl.ds(pl.multiple_of(chunk * SSD_T, SSD_T), SSD_T)

    xs = []
    for u in range(SSD_NB):
        xin = x_ref[u]
        prev = jnp.where(first, 0.0, prev_ref[u, 7:8, :])
        nxt = jnp.where(last, 0.0, next_ref[u, 0:1, :])
        xm1 = jnp.where(row == 0, prev, pltpu.roll(xin, 1, 0))
        xp1 = jnp.where(row == SSD_T - 1, nxt, pltpu.roll(xin, SSD_T - 1, 0))
        conv = cw_ref[0:1, :] * xm1 + cw_ref[1:2, :] * xin + cw_ref[2:3, :] * xp1 + cb_ref[...]
        xbc = _silu(conv)
        x = xbc[:, 0:SSD_WIDTH]
        xs.append(x)
        bmat = [xbc[:, SSD_WIDTH + g * SSD_N:SSD_WIDTH + (g + 1) * SSD_N] for g in range(SSD_GROUPS)]
        cmat = [xbc[:, SSD_WIDTH + (SSD_GROUPS + g) * SSD_N:SSD_WIDTH + (SSD_GROUPS + g + 1) * SSD_N]
                for g in range(SSD_GROUPS)]

        dtv = dt_ref[u] + dtb_ref[...]
        dt = jnp.maximum(dtv, 0.0) + jnp.log(1.0 + jnp.exp(-jnp.abs(dtv)))
        hi, mid, lo = _split3(dt * neg_a)
        cs = (jnp.dot(tri, hi, preferred_element_type=F32) + jnp.dot(tri, mid, preferred_element_type=F32)
              + jnp.dot(tri, lo, preferred_element_type=F32))
        cs_t = cs.T

        dt_x = widen(dt)
        cs_x = widen(cs)
        tot_x = jnp.where(backward, cs_x[0:1, :], cs_x[SSD_T - 1:SSD_T, :])
        ecs_x = jnp.exp(cs_x)
        etot_x = jnp.exp(tot_x)
        xd = x * dt_x
        xd_b = xd.astype(BF16)
        xd_end = (xd * jnp.exp(tot_x - cs_x)).astype(BF16)

        gw = SSD_WIDTH // SSD_GROUPS
        for g in range(SSD_GROUPS):
            glanes = slice(g * gw, (g + 1) * gw)
            cb = _bdot_nt(cmat[g], bmat[g])
            st = state_ref[u, g]
            y_off = ecs_x[:, glanes] * jnp.dot(cmat[g].astype(BF16), st.astype(BF16), preferred_element_type=F32)
            state_ref[u, g] = etot_x[:, glanes] * st + jnp.dot(bmat[g].T.astype(BF16), xd_end[:, glanes],
                                                               preferred_element_type=F32)
            for j in range(SSD_HEADS // SSD_GROUPS):
                h = g * (SSD_HEADS // SSD_GROUPS) + j
                lanes = slice(h * SSD_P, (h + 1) * SSD_P)
                seg = jnp.exp(jnp.where(causal, cs[:, h:h + 1] - cs_t[h:h + 1, :], -jnp.inf))
                y_ref[u, :, lanes] = (jnp.dot((cb * seg).astype(BF16), xd_b[:, lanes], preferred_element_type=F32)
                                      + y_off[:, j * SSD_P:(j + 1) * SSD_P])

    @pl.when(jnp.logical_not(backward))
    def _keep():
        for u in range(SSD_NB):
            yf_ref[u, rows, :] = y_ref[u]

    @pl.when(backward)
    def _finish():
        for u in range(SSD_NB):
            ytot = yf_ref[u, rows, :] + y_ref[u] + d_ref[...] * xs[u]
            gated = ytot * _silu(z_ref[u])
            ms = jnp.mean(gated * gated, axis=1, keepdims=True)
            out_ref[u] = (gated * lax.rsqrt(ms + NORM_EPS) * nw_ref[...]).astype(out_ref.dtype)


def _ssd(conv_in, sz, dt, conv_w, conv_b, dt_bias, a_log, d_exp, norm_w):
    n_batch = conv_in.shape[0]
    assert n_batch % SSD_NB == 0
    halo = SSD_T // 8
    n_halo = TOK // 8

    def chunk_map(b, s):
        return (b, _ssd_chunk(s), 0)

    def out_map(b, s):
        return (b, jnp.where(s < N_CHUNK, 1, _ssd_chunk(s)), 0)

    const = lambda shape: pl.BlockSpec(shape, lambda b, s: tuple(0 for _ in shape))
    dir_spec = pl.BlockSpec((None, 1, 128), lambda b, s: (s // N_CHUNK, 0, 0))
    expand = jnp.asarray(np.tile(np.repeat(np.eye(128, SSD_HEADS), SSD_P, axis=1), (3, 1)), BF16)
    return pl.pallas_call(
        _ssd_kernel,
        out_shape=jax.ShapeDtypeStruct((n_batch, TOK, SSD_WIDTH), BF16),
        grid=(n_batch // SSD_NB, 2 * N_CHUNK),
        in_specs=[
            pl.BlockSpec((SSD_NB, SSD_T, CONV_W), chunk_map),
            pl.BlockSpec((SSD_NB, 8, CONV_W), lambda b, s: (b, jnp.maximum(_ssd_chunk(s) * halo - 1, 0), 0)),
            pl.BlockSpec((SSD_NB, 8, CONV_W), lambda b, s: (b, jnp.minimum((_ssd_chunk(s) + 1) * halo, n_halo - 1), 0)),
            pl.BlockSpec((SSD_NB, SSD_T, SSD_WIDTH), chunk_map),
            pl.BlockSpec((SSD_NB, SSD_T, 128), lambda b, s: (b, _ssd_chunk(s), s // N_CHUNK)),
            const((3, CONV_W)), const((1, CONV_W)), dir_spec, dir_spec,
            const((1, SSD_WIDTH)), const((1, SSD_WIDTH)), const((3 * 128, SSD_WIDTH)),
        ],
        out_specs=pl.BlockSpec((SSD_NB, SSD_T, SSD_WIDTH), out_map),
        scratch_shapes=[pltpu.VMEM((SSD_NB, TOK, SSD_WIDTH), F32),
                        pltpu.VMEM((SSD_NB, SSD_GROUPS, SSD_N, SSD_WIDTH // SSD_GROUPS), F32),
                        pltpu.VMEM((SSD_NB, SSD_T, SSD_WIDTH), F32)],
        compiler_params=_params(("arbitrary", "arbitrary")),
        name="ssd",
    )(conv_in, conv_in, conv_in, sz, dt, conv_w, conv_b, dt_bias, a_log, d_exp, norm_w, expand)


S5_SW = 4 * 128


S5_NG = 2


def _s5_kernel(u_ref, m_ref, h_ref, g_ref, a_ref, y_ref, loc_ref, prev_ref, *, n_batch):
    for q in range(S5_NG):
        loc_ref[q] = jnp.dot(u_ref[q], h_ref[q], preferred_element_type=F32)
    a = [[a_ref[q, k:k + 1, :] for k in range(4)] for q in range(S5_NG)]

    def step(k, carry):
        c_f = k
        c_b = jnp.where(k < S5_CTX_CH, S5_CTX_CH - 1 - k, S5_NCH + S5_CTX_CH - 1 - k)
        new = []
        for q in range(S5_NG):
            for d, c in enumerate((c_f, c_b)):
                rows = pl.ds(pl.multiple_of(c * n_batch, n_batch), n_batch)
                re_l, im_l = slice(2 * d * 128, (2 * d + 1) * 128), slice((2 * d + 1) * 128, (2 * d + 2) * 128)
                s_re, s_im = carry[4 * q + 2 * d], carry[4 * q + 2 * d + 1]
                prev_ref[q, rows, re_l] = s_re
                prev_ref[q, rows, im_l] = s_im
                a_re, a_im = a[q][2 * d], a[q][2 * d + 1]
                new += [a_re * s_re - a_im * s_im + loc_ref[q, rows, re_l],
                        a_re * s_im + a_im * s_re + loc_ref[q, rows, im_l]]
        return tuple(new)

    zero = jnp.zeros((n_batch, 128), F32)
    lax.fori_loop(0, S5_NCH, step, (zero,) * (4 * S5_NG), unroll=2)
    for q in range(S5_NG):
        y = jnp.dot(u_ref[q], m_ref[q], preferred_element_type=F32)
        y = y + jnp.dot(prev_ref[q].astype(BF16), g_ref[q], preferred_element_type=F32)
        y_ref[q] = y.astype(y_ref.dtype)


def _s5(u_g, mats, n_batch):
    m_all, h_all, g_all, a16 = mats
    rows = u_g.shape[1]
    grp = lambda shape: pl.BlockSpec((S5_NG,) + shape, lambda g: (g,) + tuple(0 for _ in shape))
    return pl.pallas_call(
        functools.partial(_s5_kernel, n_batch=n_batch),
        out_shape=jax.ShapeDtypeStruct((S5_GROUPS, rows, S5_ROWW), BF16),
        grid=(S5_GROUPS // S5_NG,),
        in_specs=[grp((rows, S5_ROWW)), grp((S5_ROWW, S5_ROWW)), grp((S5_ROWW, S5_SW)), grp((S5_SW, S5_ROWW)),
                  grp((4, 128))],
        out_specs=grp((rows, S5_ROWW)),
        scratch_shapes=[pltpu.VMEM((S5_NG, rows, S5_SW), F32), pltpu.VMEM((S5_NG, rows, S5_SW), F32)],
        compiler_params=_params(("arbitrary",)),
        name="s5",
    )(u_g, m_all, h_all, g_all, a16)


def _cmul(a, b):
    return a[0] * b[0] - a[1] * b[1], a[0] * b[1] + a[1] * b[0]


def _s5_matrices(a_re, a_im, log_dt, b_re, b_im, c_re, c_im):
    hp = lax.Precision.HIGHEST
    t = jnp.arange(S5_T + 1, dtype=F32)
    m_sum = 0.0
    h_all, g_all, a16_all = [], [], []
    c = (c_re.astype(F32), c_im.astype(F32))
    for direction in range(2):
        are = jnp.minimum(a_re[direction].astype(F32), -1e-4)
        aim = a_im[direction].astype(F32)
        dt = jnp.exp(log_dt[direction].astype(F32))[:, None]
        mag = jnp.exp(t[:, None, None] * (are * dt)[None])
        ang = t[:, None, None] * (aim * dt)[None]
        pw = (mag * jnp.cos(ang), mag * jnp.sin(ang))
        num = (pw[0][1] - 1.0, pw[1][1])
        den = are * are + aim * aim
        coef = ((num[0] * are + num[1] * aim) / den, (num[1] * are - num[0] * aim) / den)
        bbar = _cmul((coef[0][..., None], coef[1][..., None]), (b_re.astype(F32), b_im.astype(F32)))
        pb = _cmul((pw[0][:S5_T, :, :, None], pw[1][:S5_T, :, :, None]), (bbar[0][None], bbar[1][None]))
        taps = (jnp.einsum('gon,tgni->tgoi', c[0], pb[0], precision=hp)
                - jnp.einsum('gon,tgni->tgoi', c[1], pb[1], precision=hp))
        ti = jnp.arange(S5_T)
        lag = (ti[None, :] - ti[:, None]) if direction == 0 else (ti[:, None] - ti[None, :])
        pick = (lag[:, :, None] == jnp.arange(S5_T)[None, None, :]).astype(F32)
        k_full = jnp.einsum('abt,tgoi->abgoi', pick, taps, precision=hp)
        m_sum = m_sum + k_full.transpose(2, 0, 4, 1, 3).reshape(S5_GROUPS, S5_ROWW, S5_ROWW)
        e_in = (S5_T - 1 - ti) if direction == 0 else ti
        hb = _cmul((pw[0][e_in][..., None], pw[1][e_in][..., None]), (bbar[0][None], bbar[1][None]))
        zpad = jnp.zeros_like(hb[0])
        h_mat = jnp.concatenate([hb[0], zpad, hb[1], zpad], axis=2)
        h_all.append(h_mat.transpose(1, 0, 3, 2).reshape(S5_GROUPS, S5_ROWW, 4 * S5_STATE))
        e_out = (ti + 1) if direction == 0 else (S5_T - ti)
        cp = _cmul((c[0][None], c[1][None]),
                   (pw[0][e_out][:, :, None, :], pw[1][e_out][:, :, None, :]))
        zpad = jnp.zeros_like(cp[0])
        g_mat = jnp.concatenate([cp[0], zpad, -cp[1], zpad], axis=3)
        g_all.append(g_mat.transpose(1, 3, 0, 2).reshape(S5_GROUPS, 4 * S5_STATE, S5_ROWW))
        zrow = jnp.zeros_like(pw[0][S5_T])
        a16_all += [jnp.concatenate([pw[0][S5_T], zrow], axis=1), jnp.concatenate([pw[1][S5_T], zrow], axis=1)]
    return (m_sum.astype(BF16), jnp.concatenate(h_all, axis=2).astype(BF16),
            jnp.concatenate(g_all, axis=1).astype(BF16), jnp.stack(a16_all, axis=1))


def _route(logits_t, bias, base, before):
    scores = [_sigmoid(logits_t[j * N_EXPERT_GROUPS:(j + 1) * N_EXPERT_GROUPS]) for j in range(PER_GROUP)]
    sel = [scores[j] + bias[j * N_EXPERT_GROUPS:(j + 1) * N_EXPERT_GROUPS] for j in range(PER_GROUP)]
    hi1, lo1 = jnp.maximum(sel[0], sel[1]), jnp.minimum(sel[0], sel[1])
    hi2, lo2 = jnp.maximum(sel[2], sel[3]), jnp.minimum(sel[2], sel[3])
    group_score = jnp.maximum(hi1, hi2) + jnp.maximum(jnp.minimum(hi1, hi2), jnp.maximum(lo1, lo2))
    gid = lax.broadcasted_iota(jnp.int32, group_score.shape, 0)
    best = jnp.max(group_score, axis=0, keepdims=True)
    grp = jnp.min(jnp.where(group_score == best, gid, N_EXPERT_GROUPS), axis=0, keepdims=True)
    pick = gid == grp
    v = [jnp.sum(jnp.where(pick, sel[j], 0.0), axis=0, keepdims=True) for j in range(PER_GROUP)]
    sc = [jnp.sum(jnp.where(pick, scores[j], 0.0), axis=0, keepdims=True) for j in range(PER_GROUP)]
    rank = []
    for j in range(PER_GROUP):
        r = jnp.zeros_like(grp)
        for i in range(PER_GROUP):
            if i < j:
                r = r + jnp.where(v[i] >= v[j], 1, 0)
            elif i > j:
                r = r + jnp.where(v[i] > v[j], 1, 0)
        rank.append(r)
    e, w, locs = [], [], []
    for k in range(TOP_K):
        loc = sum(jnp.where(rank[j] == k, j, 0) for j in range(PER_GROUP))
        locs.append(loc)
        e.append((grp * PER_GROUP + loc).astype(F32))
        w.append(sum(jnp.where(rank[j] == k, sc[j], 0.0) for j in range(PER_GROUP)))
    wsum = w[0] + w[1]
    chosen = [jnp.where(pick & ((locs[0] == j) | (locs[1] == j)), 1.0, 0.0) for j in range(PER_GROUP)]
    onehot = jnp.concatenate(chosen, axis=0)
    seen = jnp.dot(onehot.astype(BF16), before, preferred_element_type=F32) + base
    pos = []
    for k in range(TOP_K):
        hit = sum(jnp.where(pick & (locs[k] == j), seen[j * N_EXPERT_GROUPS:(j + 1) * N_EXPERT_GROUPS], 0.0)
                  for j in range(PER_GROUP))
        pos.append(jnp.sum(hit, axis=0, keepdims=True))
    rows = [e[0], e[1], w[0] / wsum, w[1] / wsum, pos[0], pos[1]]
    route = jnp.concatenate(rows + [jnp.zeros_like(wsum)] * (8 - len(rows)), axis=0)
    return route, base + jnp.sum(onehot, axis=1, keepdims=True)


def _gelu_tanh(x):
    return 0.5 * x * (1.0 + jnp.tanh(math.sqrt(2.0 / math.pi) * (x + 0.044715 * (x * x * x))))


MERGE_LAG = 3


def _merge_kernel(xn_ref, modn_ref, x_ref, mod_ref, modt_ref, ya_ref, ys_ref, yg_ref, y5_ref, su_ref, s5d_ref,
                  wg_ref, wb_ref, wglu_ref, wout_ref, lng_ref, lnb_ref, rw_ref, rb_ref, before_ref, swap_ref,
                  x1_ref, h2_ref, route_ref, count_ref, base_ref, y5s_ref, y5p_ref, hbs_ref, v_ref, h2s_ref):
    s = pl.program_id(0)

    @pl.when(s == 0)
    def _reset():
        base_ref[...] = jnp.zeros_like(base_ref)
        y5p_ref[...] = jnp.zeros_like(y5p_ref)
        hbs_ref[...] = jnp.zeros_like(hbs_ref)
        v_ref[...] = jnp.zeros_like(v_ref)
        h2s_ref[...] = jnp.zeros_like(h2s_ref)

    logits_t = _bdot_nt(rw_ref[...], h2s_ref[...])
    v_prev = v_ref[...]

    hb = hbs_ref[...]
    glu = jnp.dot(y5p_ref[...], wglu_ref[...], preferred_element_type=F32)
    y5 = (glu[:, 0:S5_WIDTH] * _sigmoid(glu[:, S5_WIDTH:2 * S5_WIDTH])).astype(BF16)
    def select_experts():
        base_old = base_ref[...]
        route, base_new = _route(logits_t, rb_ref[...], base_old, before_ref[...])
        route_ref[...] = route
        base = jnp.where(s >= MERGE_LAG, base_new, base_old)
        base_ref[...] = base
        count_ref[...] = jnp.broadcast_to(base, count_ref.shape)

    def post_norm():
        x1 = _ln(v_prev) * lng_ref[...] + lnb_ref[...]
        x1_ref[...] = x1
        h2 = (_ln(x1) * (1.0 + _mod_part(modt_ref, 4)) + _mod_part(modt_ref, 3)).astype(BF16)
        h2_ref[...] = h2
        h2s_ref[...] = h2

    def relayout():
        n_ch = TM // S5_T
        for t in range(S5_T):
            for g in range(S5_GROUPS):
                y5s_ref[t * n_ch:(t + 1) * n_ch, g * S5_GROUP:(g + 1) * S5_GROUP] = (
                    y5_ref[g, :, t * S5_GROUP:(t + 1) * S5_GROUP])

    side_work = [post_norm, select_experts, relayout, lambda: None]
    acc = jnp.zeros((TM, D_MODEL), F32)
    for n, y in enumerate((ya_ref[...], ys_ref[...], yg_ref[...], y5)):
        gate = _sigmoid(jnp.dot(hb, wg_ref[:, n * D_MODEL:(n + 1) * D_MODEL], preferred_element_type=F32))
        acc = acc + gate * jnp.dot(y, wb_ref[n], preferred_element_type=F32)
        side_work[n]()
    mix = jnp.dot(acc.astype(BF16), wout_ref[...], preferred_element_type=F32)
    v_ref[...] = ALPHA * x_ref[...] + _mod_part(mod_ref, 2) * mix

    y5_pre = jnp.dot(swap_ref[...], y5s_ref[...], preferred_element_type=F32) + s5d_ref[...] * su_ref[...]
    y5p_ref[...] = _gelu_tanh(y5_pre).astype(BF16)
    hbs_ref[...] = (_ln(xn_ref[...]) * (1.0 + _mod_part(modn_ref, 1)) + _mod_part(modn_ref, 0)).astype(BF16)


def _merge(xcat, mods, ya, ys, yg, y5, su, s5_d, wg, wb, wglu, wout, ln_g, ln_b, rw_t, rb):
    n_batch = xcat.shape[0]
    n_tiles = TOK // TM
    n_all = n_batch * n_tiles

    def at(lag):
        def split(s):
            t = jnp.clip(s - lag, 0, n_all - 1)
            return t // n_tiles, t % n_tiles
        return split

    def tile(w, lag):
        return pl.BlockSpec((None, TM, w), lambda s: at(lag)(s) + (0,))

    def mod_spec(lag):
        def index(s):
            b, j = at(lag)(s)
            return jnp.where(j == 0, n_batch, b), 0, 0
        return pl.BlockSpec((None, 1, N_MOD * D_MODEL), index)

    def flat(lag):
        return lambda s: (0, jnp.clip(s - lag, 0, n_all - 1))

    const = lambda shape: pl.BlockSpec(shape, lambda s: tuple(0 for _ in shape))
    before = jnp.asarray(np.triu(np.ones((TM, TM)), 1), BF16)
    return pl.pallas_call(
        _merge_kernel,
        out_shape=[jax.ShapeDtypeStruct((n_batch, TOK, D_MODEL), F32),
                   jax.ShapeDtypeStruct((n_batch, TOK, D_MODEL), BF16),
                   jax.ShapeDtypeStruct((8, n_batch * TOK), F32),
                   jax.ShapeDtypeStruct((N_EXPERTS, 128), F32)],
        grid=(n_all + MERGE_LAG,),
        in_specs=[tile(D_MODEL, 0), mod_spec(0), tile(D_MODEL, 1), mod_spec(1), mod_spec(2),
                  tile(512, 1), tile(512, 1), tile(512, 1),
                  pl.BlockSpec((S5_GROUPS, TM // S5_T, S5_ROWW), lambda s: (0,) + at(0)(s)[::-1]),
                  tile(512, 0), const((1, S5_WIDTH)),
                  const((D_MODEL, N_BRANCH * D_MODEL)), const((N_BRANCH, 512, D_MODEL)),
                  const((S5_WIDTH, 2 * S5_WIDTH)), const((D_MODEL, D_MODEL)),
                  const((1, D_MODEL)), const((1, D_MODEL)),
                  const((N_EXPERTS, D_MODEL)), const((N_EXPERTS, 1)), const((TM, TM)), const((TM, TM))],
        out_specs=[tile(D_MODEL, 2), tile(D_MODEL, 2),
                   pl.BlockSpec((8, TM), flat(MERGE_LAG)),
                   const((N_EXPERTS, 128))],
        scratch_shapes=[pltpu.VMEM((N_EXPERTS, 1), F32), pltpu.VMEM((TM, S5_WIDTH), BF16),
                        pltpu.VMEM((TM, S5_WIDTH), BF16), pltpu.VMEM((TM, D_MODEL), BF16),
                        pltpu.VMEM((TM, D_MODEL), F32), pltpu.VMEM((TM, D_MODEL), BF16)],
        compiler_params=_params(("arbitrary",)),
        name="merge",
    )(xcat, mods, xcat, mods, mods, ya, ys, yg, y5, su, s5_d, wg, wb, wglu, wout, ln_g, ln_b, rw_t, rb, before,
      _chunk_swap())


def _moe_kernel(be_ref, nu_ref, nxt_ref, slot_ref, x_ref, wg_hbm, wu_hbm, wd_hbm, y_ref,
                wg_f, wu_f, wd_f, wgu_s, wd_s, sem, *, layer):
    i = pl.program_id(0)
    used = i < nu_ref[0]
    first = used & ((i == 0) | (be_ref[i] != be_ref[jnp.maximum(i - 1, 0)]))

    def copies(expert, buf):
        return [pltpu.make_async_copy(wg_hbm.at[layer, expert], wg_f.at[buf], sem.at[buf, 0]),
                pltpu.make_async_copy(wu_hbm.at[layer, expert], wu_f.at[buf], sem.at[buf, 1]),
                pltpu.make_async_copy(wd_hbm.at[layer, expert], wd_f.at[buf], sem.at[buf, 2])]

    @pl.when(used & (i == 0))
    def _first_fetch():
        for c in copies(be_ref[0], slot_ref[0]):
            c.start()

    @pl.when(first)
    def _next_expert():
        buf = slot_ref[i]
        for c in copies(be_ref[i], buf):
            c.wait()

        @pl.when(nxt_ref[i] >= 0)
        def _prefetch():
            for c in copies(nxt_ref[i], 1 - buf):
                c.start()

        wgu_s[:, 0:D_EXPERT] = wg_f[buf].astype(BF16)
        wgu_s[:, D_EXPERT:2 * D_EXPERT] = wu_f[buf].astype(BF16)
        wd_s[...] = wd_f[buf].astype(BF16)

    @pl.when(used)
    def _block():
        gu = jnp.dot(x_ref[...], wgu_s[...], preferred_element_type=F32)
        mid = _silu(gu[:, 0:D_EXPERT]) * gu[:, D_EXPERT:2 * D_EXPERT]
        y_ref[...] = jnp.dot(mid.astype(BF16), wd_s[...], preferred_element_type=F32).astype(y_ref.dtype)

    @pl.when(jnp.logical_not(used))
    def _unused():
        y_ref[...] = jnp.zeros_like(y_ref)


def _moe_experts(block_expert, n_used, xs, w_gate, w_up, w_down, layer):
    n_blocks = block_expert.shape[0]
    idx = jnp.arange(n_blocks, dtype=jnp.int32)
    starts = jnp.concatenate([jnp.ones((1,), bool), block_expert[1:] != block_expert[:-1]])
    slot = ((jnp.cumsum(starts.astype(jnp.int32)) - 1) % 2).astype(jnp.int32)
    later = jnp.where(starts, idx, n_blocks)
    nxt_idx = jnp.concatenate([lax.cummin(later[::-1])[::-1][1:], jnp.full((1,), n_blocks, jnp.int32)])
    nxt = jnp.where(nxt_idx < n_used[0], block_expert[jnp.minimum(nxt_idx, n_blocks - 1)], -1).astype(jnp.int32)
    hbm = pl.BlockSpec(memory_space=pl.ANY)
    return pl.pallas_call(
        functools.partial(_moe_kernel, layer=layer),
        out_shape=jax.ShapeDtypeStruct((n_blocks * MOE_ROWS, D_MODEL), BF16),
        grid_spec=pltpu.PrefetchScalarGridSpec(
            num_scalar_prefetch=4, grid=(n_blocks,),
            in_specs=[pl.BlockSpec((MOE_ROWS, D_MODEL), lambda i, *_: (i, 0)), hbm, hbm, hbm],
            out_specs=pl.BlockSpec((MOE_ROWS, D_MODEL), lambda i, *_: (i, 0)),
            scratch_shapes=[pltpu.VMEM((2, D_MODEL, D_EXPERT), F32), pltpu.VMEM((2, D_MODEL, D_EXPERT), F32),
                            pltpu.VMEM((2, D_EXPERT, D_MODEL), F32),
                            pltpu.VMEM((D_MODEL, 2 * D_EXPERT), BF16), pltpu.VMEM((D_EXPERT, D_MODEL), BF16),
                            pltpu.SemaphoreType.DMA((2, 3))]),
        compiler_params=_params(("arbitrary",)),
        name="moe_experts",
    )(block_expert, n_used, nxt, slot, xs, w_gate, w_up, w_down)


def _final_kernel(x_ref, mod_ref, f0_ref, f1_ref, route_ref, lng_ref, lnb_ref, o_ref):
    o_ref[...] = _ffn_residual(x_ref[...], mod_ref, f0_ref, f1_ref, route_ref, lng_ref, lnb_ref)


def _final(x1, mods, f, route, ln_g, ln_b):
    n_batch = x1.shape[0]
    n_tiles = TOK // TM
    skip = CTX_LEN // TM
    const = pl.BlockSpec((1, D_MODEL), lambda b, j: (0, 0))
    return pl.pallas_call(
        _final_kernel,
        out_shape=jax.ShapeDtypeStruct((n_batch, SEQ, D_MODEL), F32),
        grid=(n_batch, n_tiles - skip),
        in_specs=[pl.BlockSpec((None, TM, D_MODEL), lambda b, j: (b, j + skip, 0)),
                  pl.BlockSpec((None, 1, N_MOD * D_MODEL), lambda b, j: (b, 0, 0)),
                  pl.BlockSpec((None, None, TM, D_MODEL), lambda b, j: (0, b, j + skip, 0)),
                  pl.BlockSpec((None, None, TM, D_MODEL), lambda b, j: (1, b, j + skip, 0)),
                  pl.BlockSpec((8, TM), lambda b, j: (0, b * n_tiles + j + skip)),
                  const, const],
        out_specs=pl.BlockSpec((None, TM, D_MODEL), lambda b, j: (b, j, 0)),
        compiler_params=_params(("arbitrary", "arbitrary")),
        name="ffn_residual",
    )(x1, mods, f, f, route, ln_g, ln_b)


def _lookup(table, idx):
    ids = jnp.arange(table.shape[0], dtype=jnp.int32).reshape((-1,) + (1,) * idx.ndim)
    return jnp.sum(jnp.where(idx[None] == ids, table.reshape(ids.shape), 0), axis=0)


def _dispatch(route, counts_rows):
    n_tok = route.shape[1]
    n_assign = n_tok * TOP_K
    n_blocks = -(-(n_assign + N_EXPERTS * (MOE_ROWS - 1)) // MOE_ROWS)
    cap = n_blocks * MOE_ROWS
    counts = counts_rows[:, 0].astype(jnp.int32).reshape(PER_GROUP, N_EXPERT_GROUPS).T.reshape(N_EXPERTS)
    padded = (counts + MOE_ROWS - 1) // MOE_ROWS * MOE_ROWS
    pad_end = jnp.cumsum(padded)
    pad_start = pad_end - padded
    expert = route[0:TOP_K].astype(jnp.int32)
    pos = _lookup(pad_start, expert) + route[2 * TOP_K:3 * TOP_K].astype(jnp.int32)
    block_start = jnp.arange(n_blocks, dtype=jnp.int32) * MOE_ROWS
    block_expert = jnp.minimum(jnp.sum((pad_end[None, :] <= block_start[:, None]).astype(jnp.int32), axis=1),
                               N_EXPERTS - 1)
    n_used = (pad_end[-1:] // MOE_ROWS).astype(jnp.int32)
    gap = padded - counts
    gap_end = jnp.cumsum(gap)
    gap_first = jnp.concatenate([pad_start + counts - (gap_end - gap), pad_end[-1:] - gap_end[-1:]])
    k = jnp.arange(cap - n_assign, dtype=jnp.int32)
    owner = jnp.sum((gap_end[None, :] <= k[:, None]).astype(jnp.int32), axis=1)
    free_slot = _lookup(gap_first, owner) + k
    tok = jnp.broadcast_to(jnp.arange(n_tok, dtype=jnp.int32), (TOP_K, n_tok)).reshape(-1)
    keys = jnp.concatenate([pos.reshape(-1), free_slot])
    vals = jnp.concatenate([tok, k % n_tok])
    _, slot_tok = lax.sort((keys, vals), num_keys=1)
    return slot_tok, pos, block_expert, n_used


def _col(w, part):
    return w[:, _IN_OFF[part]:_IN_OFF[part + 1]]


def _proj_weights(w_in):
    zeros = jnp.zeros((D_MODEL, 128 - SSD_HEADS), w_in.dtype)
    dt = _col(w_in, _SDT)
    parts = [_col(w_in, p) for p in (_AQ, _GQ, _AK, _AV, _GK, _GV, _SX, _SB, _SC, _SZ, _SU)]
    parts += [dt[:, 0:SSD_HEADS], zeros, dt[:, SSD_HEADS:2 * SSD_HEADS], zeros]
    return jnp.concatenate(parts, axis=1).astype(BF16), _col(w_in, _GATE).astype(BF16)


def _rope_tables():
    rows = SEQ // GRID_W
    row = jnp.repeat(jnp.arange(rows, dtype=F32), GRID_W)
    col = jnp.tile(jnp.arange(GRID_W, dtype=F32), rows)
    axis_dim = HEAD_DIM // 2
    inv_freq = ROPE_THETA ** (-jnp.arange(0, axis_dim, 2, dtype=F32) / axis_dim)
    ang_r = row[:, None] * inv_freq
    ang_c = col[:, None] * inv_freq
    ang = jnp.concatenate([ang_r, ang_r, ang_c, ang_c], axis=-1)
    cos = jnp.concatenate([jnp.ones((CTX_LEN, HEAD_DIM), F32), jnp.cos(ang)], axis=0)
    sin = jnp.concatenate([jnp.zeros((CTX_LEN, HEAD_DIM), F32), jnp.sin(ang)], axis=0)
    return jnp.tile(cos, (1, 2)), jnp.tile(sin, (1, 2))


def _pad_lanes(v, width=128):
    return jnp.pad(v, ((0, 0), (0, width - v.shape[-1])))


def kernel(x, c, ctx, c_ctx, mod_w, mod_b, w_in, wa_sink, ga_q_norm, ga_k_norm, ssd_conv_w, ssd_conv_b, ssd_dt_bias, ssd_a_log, ssd_d, ssd_norm_w, s5_a_re, s5_a_im, s5_log_dt, s5_b_re, s5_b_im, s5_c_re, s5_c_im, s5_d, s5_w_glu, w_branch, w_out, ln1_g, ln1_b, ln2_g, ln2_b, router_w, router_bias, moe_w_gate, moe_w_up, moe_w_down):
    n_batch = x.shape[0]
    n_tok = n_batch * TOK
    xcat = jnp.concatenate([ctx, x], axis=1)
    mod_rows = -(-(n_batch + 1) // 8) * 8
    cond = jnp.zeros((mod_rows, D_MODEL), F32).at[:n_batch].set(c).at[n_batch].set(c_ctx)
    mods_all = _modulation(cond, mod_w, mod_b)
    cos, sin = _rope_tables()

    perm = np.array([g * PER_GROUP + j for j in range(PER_GROUP) for g in range(N_EXPERT_GROUPS)])
    rw_t = router_w.T[perm].astype(BF16)
    rb = router_bias.astype(F32)[perm].reshape(N_EXPERTS, 1)

    prev = None
    for layer in range(DEPTH):
        mods = mods_all[layer].reshape(mod_rows, 1, N_MOD * D_MODEL)
        w1, wg = _proj_weights(w_in[layer])
        if prev is None:
            attn, conv_in, sz, su, dt, u_g = _inproj(xcat, mods, w1)
        else:
            xcat, attn, conv_in, sz, su, dt, u_g = _inproj(None, mods, w1, prev)

        sink_rows = jnp.repeat(wa_sink[layer].astype(F32).reshape(N_KV, GRP) * LOG2E, QB,
                               axis=1).reshape(N_KV, 1, GRP * QB)
        qw = jnp.tile(ga_q_norm[layer].astype(F32), N_HEADS).reshape(1, 512)
        kw = jnp.tile(ga_k_norm[layer].astype(F32), N_KV).reshape(1, 128)
        ya, yg = _attention(attn, cos, sin, sink_rows, qw, kw)

        dt_bias = _pad_lanes(ssd_dt_bias[layer].astype(F32).reshape(2, SSD_HEADS)).reshape(2, 1, 128)
        a_log = _pad_lanes(ssd_a_log[layer].astype(F32)).reshape(2, 1, 128)
        d_exp = jnp.repeat(ssd_d[layer].astype(F32), SSD_P).reshape(1, SSD_WIDTH)
        ys = _ssd(conv_in, sz, dt, ssd_conv_w[layer].astype(F32), ssd_conv_b[layer].astype(F32).reshape(1, CONV_W),
                  dt_bias, a_log, d_exp, ssd_norm_w[layer].astype(F32).reshape(1, SSD_WIDTH))

        mats = _s5_matrices(s5_a_re[layer], s5_a_im[layer], s5_log_dt[layer], s5_b_re[layer], s5_b_im[layer],
                            s5_c_re[layer], s5_c_im[layer])
        y5 = _s5(u_g.reshape(S5_GROUPS, S5_NCH * n_batch, S5_ROWW), mats, n_batch)
        y5 = y5.reshape(S5_GROUPS, S5_NCH, n_batch * S5_ROWW)

        x1, h2, route, counts = _merge(
            xcat, mods, ya, ys, yg, y5, su, s5_d[layer].astype(F32).reshape(1, S5_WIDTH), wg,
            w_branch[layer].astype(BF16), s5_w_glu[layer].astype(BF16), w_out[layer].astype(BF16),
            ln1_g[layer].reshape(1, D_MODEL), ln1_b[layer].reshape(1, D_MODEL), rw_t, rb)

        slot_tok, pos, block_expert, n_used = _dispatch(route, counts)
        xs = h2.reshape(n_tok, D_MODEL)[slot_tok]
        y_slots = _moe_experts(block_expert, n_used, xs, moe_w_gate, moe_w_up, moe_w_down, layer)
        f = y_slots[pos.reshape(-1)].reshape(TOP_K, n_batch, TOK, D_MODEL)
        prev = (x1, mods, f, route, ln2_g[layer].reshape(1, D_MODEL), ln2_b[layer].reshape(1, D_MODEL))
    return _final(*prev)
```
